```python
import math, functools
import jax, jax.numpy as jnp
from jax import lax
import numpy as np

D_MODEL = 1024
BATCH = 4
SEQ = 4096
DEPTH = 2
DEC_BATCH = 8
DEC_SEQ = 16
PAST_LEN = 1024

CHUNK = 64
HEAD_DIM = 64
N_FOX_HEADS = 8
FOX_WIDTH = N_FOX_HEADS * HEAD_DIM
SSM_CH_PER_GROUP = 16
SSM_GROUPS = 16
SSM_WIDTH = SSM_GROUPS * SSM_CH_PER_GROUP
SSM_STATE = 64
N_MEM_HEADS = 4
MEM_WIDTH = N_MEM_HEADS * HEAD_DIM
N_MEM = 256
MIX_WIDTH = FOX_WIDTH + SSM_WIDTH + MEM_WIDTH
Q_BLOCK = 128
EPS = 1e-6
IN_SIZES = (FOX_WIDTH, FOX_WIDTH, FOX_WIDTH, FOX_WIDTH, N_FOX_HEADS, SSM_WIDTH, SSM_WIDTH, MEM_WIDTH, MEM_WIDTH)
IN_COLS = 4 * FOX_WIDTH + N_FOX_HEADS + 2 * SSM_WIDTH + 2 * MEM_WIDTH

kernel_name = "hybrid_fox_s5_memory_stream_step"


def rms_norm(x, g):
    xf = x.astype(jnp.float32)
    y = xf * lax.rsqrt(jnp.mean(xf * xf, axis=-1, keepdims=True) + EPS)
    return (y * g.astype(jnp.float32)).astype(x.dtype)


def split_columns(z):
    offs = []
    o = 0
    for s in IN_SIZES[:-1]:
        o += s
        offs.append(o)
    return jnp.split(z, offs, axis=-1)


def fox_block(q, k, v, cq, ck, q_pos, k_pos):
    s = jnp.einsum('bqhd,bkhd->bhqk', q, k).astype(jnp.float32) * (HEAD_DIM ** -0.5)
    s = s + jnp.swapaxes(cq, 1, 2)[..., :, None] - jnp.swapaxes(ck, 1, 2)[..., None, :]
    mask = k_pos[None, :] <= q_pos[:, None]
    p = jax.nn.softmax(jnp.where(mask, s, -jnp.inf), axis=-1)
    return jnp.einsum('bhqk,bkhd->bqhd', p.astype(v.dtype), v)


def fox_attention_prompt(q, k, v, logf):
    B, T, H, Dh = q.shape
    nb = T // Q_BLOCK
    c = jnp.cumsum(logf, axis=1)
    qb = jnp.swapaxes(q.reshape(B, nb, Q_BLOCK, H, Dh), 0, 1)
    cb = jnp.swapaxes(c.reshape(B, nb, Q_BLOCK, H), 0, 1)
    pos = jnp.arange(T)
    pb = pos.reshape(nb, Q_BLOCK)
    out = lax.map(lambda a: fox_block(a[0], k, v, a[1], c, a[2], pos), (qb, cb, pb))
    return jnp.swapaxes(out, 0, 1).reshape(B, T, H, Dh)


def fox_attention_sample(q, k, v, logf, cache_k, cache_v, cache_logf):
    P = cache_k.shape[1]
    T = q.shape[1]
    k_all = jnp.concatenate([cache_k.astype(k.dtype), k], axis=1)
    v_all = jnp.concatenate([cache_v.astype(v.dtype), v], axis=1)
    c_all = jnp.cumsum(jnp.concatenate([cache_logf.astype(jnp.float32), logf], axis=1), axis=1)
    return fox_block(q, k_all, v_all, c_all[:, P:], c_all, P + jnp.arange(T), jnp.arange(P + T))


def _complex_affine_combine(e1, e2):
    a1r, a1i, b1r, b1i = e1
    a2r, a2i, b2r, b2i = e2
    ar = a2r * a1r - a2i * a1i
    ai = a2r * a1i + a2i * a1r
    br = a2r * b1r - a2i * b1i + b2r
    bi = a2r * b1i + a2i * b1r + b2i
    return (ar, ai, br, bi)


def s5_mixer(u, h0_re, h0_im, a_re, a_im, log_dt, b_re, b_im, c_re, c_im, d, w_glu, b_glu):
    f32 = jnp.float32
    B, T, _ = u.shape
    uf = u.astype(f32).reshape(B, T, SSM_GROUPS, SSM_CH_PER_GROUP)
    dt = jnp.exp(log_dt.astype(f32))[:, None]
    ar, ai = a_re.astype(f32), a_im.astype(f32)
    mag = jnp.exp(dt * ar)
    abar_re = mag * jnp.cos(dt * ai)
    abar_im = mag * jnp.sin(dt * ai)
    den = ar * ar + ai * ai
    zr = ((abar_re - 1.0) * ar + abar_im * ai) / den
    zi = (abar_im * ar - (abar_re - 1.0) * ai) / den
    br, bi = b_re.astype(f32), b_im.astype(f32)
    bbar_re = zr[..., None] * br - zi[..., None] * bi
    bbar_im = zr[..., None] * bi + zi[..., None] * br
    bu_re = jnp.einsum('btgp,gnp->btgn', uf, bbar_re)
    bu_im = jnp.einsum('btgp,gnp->btgn', uf, bbar_im)
    a_seq_re = jnp.broadcast_to(abar_re, bu_re.shape)
    a_seq_im = jnp.broadcast_to(abar_im, bu_im.shape)
    acum_re, acum_im, h_re, h_im = lax.associative_scan(
        _complex_affine_combine, (a_seq_re, a_seq_im, bu_re, bu_im), axis=1)
    h0r = h0_re.astype(f32)[:, None]
    h0i = h0_im.astype(f32)[:, None]
    h_re = h_re + acum_re * h0r - acum_im * h0i
    h_im = h_im + acum_re * h0i + acum_im * h0r
    y = (jnp.einsum('gpn,btgn->btgp', c_re.astype(f32), h_re)
         - jnp.einsum('gpn,btgn->btgp', c_im.astype(f32), h_im)
         + d.astype(f32) * uf).reshape(B, T, SSM_WIDTH)
    g = jax.nn.gelu(y)
    out = g * jax.nn.sigmoid(g @ w_glu.astype(f32) + b_glu.astype(f32))
    return out.astype(u.dtype), h_re[:, -1], h_im[:, -1]


def memory_attention(q, mk, mv):
    s = jnp.einsum('bqhd,bkhd->bhqk', q, mk.astype(q.dtype)).astype(jnp.float32) * (HEAD_DIM ** -0.5)
    p = jax.nn.softmax(s, axis=-1)
    return jnp.einsum('bhqk,bkhd->bqhd', p.astype(q.dtype), mv.astype(q.dtype))


def memory_kv(mem, mem_norm, w_mem_kv, mem_k_norm):
    B, M, _ = mem.shape
    kv = rms_norm(mem, mem_norm) @ w_mem_kv
    mk, mv = jnp.split(kv, 2, axis=-1)
    mk = rms_norm(mk.reshape(B, M, N_MEM_HEADS, HEAD_DIM), mem_k_norm)
    return mk, mv.reshape(B, M, N_MEM_HEADS, HEAD_DIM)


def mixer_layer(x, w, fox_attend, h0_re, h0_im, mem_k, mem_v):
    (norm_g, w_in, b_forget, fox_q_norm, fox_k_norm, a_re, a_im, log_dt, b_re, b_im,
     c_re, c_im, d, w_glu, b_glu, mem_q_norm, w_out) = w
    B, T, _ = x.shape
    h = rms_norm(x, norm_g)
    fq, fk, fv, fg, ff, su, sg, mq, mg = split_columns(h @ w_in)
    fq = rms_norm(fq.reshape(B, T, N_FOX_HEADS, HEAD_DIM), fox_q_norm)
    fk = rms_norm(fk.reshape(B, T, N_FOX_HEADS, HEAD_DIM), fox_k_norm)
    fv = fv.reshape(B, T, N_FOX_HEADS, HEAD_DIM)
    logf = jax.nn.log_sigmoid(ff.astype(jnp.float32) + b_forget.astype(jnp.float32))
    fox_out = fox_attend(fq, fk, fv, logf).reshape(B, T, FOX_WIDTH) * jax.nn.silu(fg)
    ssm_y, h_re, h_im = s5_mixer(su, h0_re, h0_im, a_re, a_im, log_dt, b_re, b_im, c_re, c_im, d, w_glu, b_glu)
    ssm_out = ssm_y * jax.nn.silu(sg)
    mq = rms_norm(mq.reshape(B, T, N_MEM_HEADS, HEAD_DIM), mem_q_norm)
    mem_out = memory_attention(mq, mem_k, mem_v).reshape(B, T, MEM_WIDTH) * jax.nn.silu(mg)
    mix = jnp.concatenate([fox_out, ssm_out, mem_out], axis=-1) @ w_out
    return x + mix, fk, fv, logf, h_re, h_im


def setup_inputs(seed: int = 0) -> dict:
    key = jax.random.key(seed)
    ks = jax.random.split(key, 40)
    f32 = jnp.float32

    def nrm(k, shape, scale):
        return scale * jax.random.normal(k, shape, f32)

    inp = {}
    inp["x_prompt"] = nrm(ks[0], (BATCH, SEQ, D_MODEL), 1.0)
    inp["x_sample"] = nrm(ks[1], (DEC_BATCH, DEC_SEQ, D_MODEL), 1.0)
    inp["mem_prompt"] = nrm(ks[2], (BATCH, N_MEM, D_MODEL), 1.0)
    inp["cache_fox_k"] = nrm(ks[3], (DEPTH, DEC_BATCH, PAST_LEN, N_FOX_HEADS, HEAD_DIM), 1.0)
    inp["cache_fox_v"] = nrm(ks[4], (DEPTH, DEC_BATCH, PAST_LEN, N_FOX_HEADS, HEAD_DIM), 1.0)
    inp["cache_fox_logf"] = jax.nn.log_sigmoid(3.0 + nrm(ks[5], (DEPTH, DEC_BATCH, PAST_LEN, N_FOX_HEADS), 1.0))
    inp["state_ssm_re"] = nrm(ks[6], (DEPTH, DEC_BATCH, SSM_GROUPS, SSM_STATE), 0.5)
    inp["state_ssm_im"] = nrm(ks[7], (DEPTH, DEC_BATCH, SSM_GROUPS, SSM_STATE), 0.5)
    inp["cache_mem_k"] = nrm(ks[8], (DEPTH, DEC_BATCH, N_MEM, N_MEM_HEADS, HEAD_DIM), 1.0)
    inp["cache_mem_v"] = nrm(ks[9], (DEPTH, DEC_BATCH, N_MEM, N_MEM_HEADS, HEAD_DIM), 1.0)
    inp["norm_g"] = 1.0 + nrm(ks[10], (DEPTH, D_MODEL), 0.02)
    inp["w_in"] = nrm(ks[11], (DEPTH, D_MODEL, IN_COLS), D_MODEL ** -0.5)
    inp["b_forget"] = jnp.linspace(1.0, 6.0, N_FOX_HEADS, dtype=f32)[None, :] + nrm(ks[12], (DEPTH, N_FOX_HEADS), 0.1)
    inp["fox_q_norm"] = 1.0 + nrm(ks[13], (DEPTH, HEAD_DIM), 0.02)
    inp["fox_k_norm"] = 1.0 + nrm(ks[14], (DEPTH, HEAD_DIM), 0.02)
    inp["ssm_a_re"] = -0.5 + nrm(ks[15], (DEPTH, SSM_GROUPS, SSM_STATE), 0.01)
    inp["ssm_a_im"] = (math.pi * jnp.arange(SSM_STATE, dtype=f32))[None, None, :] + nrm(ks[16], (DEPTH, SSM_GROUPS, SSM_STATE), 0.01)
    inp["ssm_log_dt"] = jax.random.uniform(ks[17], (DEPTH, SSM_GROUPS), f32, math.log(1e-3), math.log(1e-1))
    inp["ssm_b_re"] = nrm(ks[18], (DEPTH, SSM_GROUPS, SSM_STATE, SSM_CH_PER_GROUP), (2 * SSM_CH_PER_GROUP) ** -0.5)
    inp["ssm_b_im"] = nrm(ks[19], (DEPTH, SSM_GROUPS, SSM_STATE, SSM_CH_PER_GROUP), (2 * SSM_CH_PER_GROUP) ** -0.5)
    inp["ssm_c_re"] = nrm(ks[20], (DEPTH, SSM_GROUPS, SSM_CH_PER_GROUP, SSM_STATE), SSM_STATE ** -0.5)
    inp["ssm_c_im"] = nrm(ks[21], (DEPTH, SSM_GROUPS, SSM_CH_PER_GROUP, SSM_STATE), SSM_STATE ** -0.5)
    inp["ssm_d"] = nrm(ks[22], (DEPTH, SSM_GROUPS, SSM_CH_PER_GROUP), 0.5)
    inp["w_glu"] = nrm(ks[23], (DEPTH, SSM_WIDTH, SSM_WIDTH), SSM_WIDTH ** -0.5)
    inp["b_glu"] = nrm(ks[24], (DEPTH, SSM_WIDTH), 0.01)
    inp["mem_norm"] = 1.0 + nrm(ks[25], (DEPTH, D_MODEL), 0.02)
    inp["w_mem_kv"] = nrm(ks[26], (DEPTH, D_MODEL, 2 * MEM_WIDTH), D_MODEL ** -0.5)
    inp["mem_q_norm"] = 1.0 + nrm(ks[27], (DEPTH, HEAD_DIM), 0.02)
    inp["mem_k_norm"] = 1.0 + nrm(ks[28], (DEPTH, HEAD_DIM), 0.02)
    inp["w_out"] = nrm(ks[29], (DEPTH, MIX_WIDTH, D_MODEL), MIX_WIDTH ** -0.5)
    return inp


def reference(x_prompt, x_sample, mem_prompt, cache_fox_k, cache_fox_v, cache_fox_logf,
              state_ssm_re, state_ssm_im, cache_mem_k, cache_mem_v,
              norm_g, w_in, b_forget, fox_q_norm, fox_k_norm, ssm_a_re, ssm_a_im, ssm_log_dt,
              ssm_b_re, ssm_b_im, ssm_c_re, ssm_c_im, ssm_d, w_glu, b_glu,
              mem_norm, w_mem_kv, mem_q_norm, mem_k_norm, w_out):
    xp, xs = x_prompt, x_sample
    pk, pv, pf, pre, pim, pmk, pmv = [], [], [], [], [], [], []
    sk, sv, sf, sre, sim = [], [], [], [], []
    for l in range(DEPTH):
        w = (norm_g[l], w_in[l], b_forget[l], fox_q_norm[l], fox_k_norm[l], ssm_a_re[l], ssm_a_im[l],
             ssm_log_dt[l], ssm_b_re[l], ssm_b_im[l], ssm_c_re[l], ssm_c_im[l], ssm_d[l], w_glu[l], b_glu[l],
             mem_q_norm[l], w_out[l])
        mk, mv = memory_kv(mem_prompt, mem_norm[l], w_mem_kv[l], mem_k_norm[l])
        h0 = jnp.zeros((xp.shape[0], SSM_GROUPS, SSM_STATE), jnp.float32)
        xp, k, v, lf, hr, hi = mixer_layer(xp, w, fox_attention_prompt, h0, h0, mk, mv)
        pk.append(k); pv.append(v); pf.append(lf); pre.append(hr); pim.append(hi)
        pmk.append(mk); pmv.append(mv)
        fox_sample = functools.partial(fox_attention_sample, cache_k=cache_fox_k[l],
                                       cache_v=cache_fox_v[l], cache_logf=cache_fox_logf[l])
        xs, k, v, lf, hr, hi = mixer_layer(xs, w, fox_sample, state_ssm_re[l], state_ssm_im[l],
                                           cache_mem_k[l], cache_mem_v[l])
        sk.append(k); sv.append(v); sf.append(lf); sre.append(hr); sim.append(hi)
    y_prompt, y_sample = xp, xs
    fox_k_prompt, fox_v_prompt, fox_logf_prompt = jnp.stack(pk), jnp.stack(pv), jnp.stack(pf)
    ssm_re_prompt, ssm_im_prompt = jnp.stack(pre), jnp.stack(pim)
    mem_k_prompt, mem_v_prompt = jnp.stack(pmk), jnp.stack(pmv)
    fox_k_sample, fox_v_sample, fox_logf_sample = jnp.stack(sk), jnp.stack(sv), jnp.stack(sf)
    ssm_re_sample, ssm_im_sample = jnp.stack(sre), jnp.stack(sim)
    return (y_prompt, y_sample, fox_k_prompt, fox_v_prompt, fox_logf_prompt, ssm_re_prompt, ssm_im_prompt,
            mem_k_prompt, mem_v_prompt, fox_k_sample, fox_v_sample, fox_logf_sample, ssm_re_sample, ssm_im_sample)
```

```python
import functools
import math

import numpy as np
import jax
import jax.numpy as jnp
from jax import lax
from jax.experimental import pallas as pl
from jax.experimental.pallas import tpu as pltpu

f32 = jnp.float32
bf16 = jnp.bfloat16

HEAD_DIM = 64
N_FOX_HEADS = 8
FOX_WIDTH = N_FOX_HEADS * HEAD_DIM
SSM_GROUPS = 16
SSM_CH = 16
SSM_STATE = 64
SSM_WIDTH = SSM_GROUPS * SSM_CH
SSM_MODES = SSM_GROUPS * SSM_STATE
N_MEM_HEADS = 4
MEM_WIDTH = N_MEM_HEADS * HEAD_DIM
EPS = 1e-6
QK_SCALE = HEAD_DIM ** -0.5

LANES = 128
S5_CHUNK = 8
ROW_TILE = 512
ATTN_TQ = 256
ATTN_TK = 256
VMEM_LIMIT = 56 * 1024 * 1024

_C_Q, _C_K, _C_V, _C_G = 0, 512, 1024, 1536
_C_SU, _C_SG, _C_MQ, _C_MG, _C_F = 2048, 2304, 2560, 2816, 3072
_W_COLS = 3200
_N_PIECES = 3


def _cparams(sem):
    return pltpu.CompilerParams(dimension_semantics=sem, vmem_limit_bytes=VMEM_LIMIT)


def _silu(x):
    return x * jax.nn.sigmoid(x)


def _log_sigmoid(x):
    return -(jnp.maximum(-x, 0.0) + jnp.log1p(jnp.exp(-jnp.abs(x))))


def _gelu_tanh(x):
    return 0.5 * x * (1.0 + jnp.tanh(math.sqrt(2.0 / math.pi) * (x + 0.044715 * (x * x * x))))


def _split3(x):
    hi = x.astype(bf16).astype(f32)
    r = x - hi
    mid = r.astype(bf16).astype(f32)
    lo = (r - mid).astype(bf16).astype(f32)
    return hi, mid, lo


def _pack3(x):
    hi, mid, lo = _split3(x)
    return (hi + pltpu.roll(mid, 8, 1) + pltpu.roll(lo, 16, 1)).astype(bf16)


def _unpack3(y):
    return y + pltpu.roll(y, LANES - 8, 1) + pltpu.roll(y, LANES - 16, 1)


def _lane_iota(shape):
    return lax.broadcasted_iota(jnp.int32, shape, len(shape) - 1)


def _row_iota(shape):
    return lax.broadcasted_iota(jnp.int32, shape, len(shape) - 2)


def _head_norm(z, hm_ref, g_ref):
    ms = jnp.dot((z * z).astype(bf16), hm_ref[...], preferred_element_type=f32)
    return z * lax.rsqrt(ms + EPS) * g_ref[...]


def _in_proj_kernel(*refs, prompt, tiles_per_batch):
    if prompt:
        (x_ref, ng_ref, w_ref, bf_ref, qg_ref, kg_ref, mqg_ref, hm_ref, tril_ref, eq_ref, ek_ref,
         k_ref, v_ref, logf_ref, gf_ref, u_ref, gs_ref, mq_ref, gm_ref, qa_ref, ka_ref, va_ref,
         carry_ref) = refs
    else:
        (x_ref, ng_ref, w_ref, bf_ref, qg_ref, kg_ref, mqg_ref, hm_ref,
         k_ref, v_ref, logf_ref, gf_ref, u_ref, gs_ref, mq_ref, gm_ref, q_ref) = refs

    x = x_ref[...]
    ms = jnp.mean(x * x, axis=-1, keepdims=True)
    h = (x * lax.rsqrt(ms + EPS) * ng_ref[...]).astype(bf16)

    def seg(lo, width):
        return jnp.dot(h, w_ref[:, lo:lo + width], preferred_element_type=f32)

    q = _head_norm(seg(_C_Q, FOX_WIDTH), hm_ref, qg_ref) * QK_SCALE
    k = _head_norm(seg(_C_K, FOX_WIDTH), hm_ref, kg_ref)
    v = seg(_C_V, FOX_WIDTH)
    k_ref[...] = k
    v_ref[...] = v
    gf_ref[...] = _silu(seg(_C_G, FOX_WIDTH))
    u_ref[...] = seg(_C_SU, SSM_WIDTH)
    gs_ref[...] = _silu(seg(_C_SG, SSM_WIDTH))
    mq = _head_norm(seg(_C_MQ, MEM_WIDTH), hm_ref.at[0:MEM_WIDTH, 0:MEM_WIDTH], mqg_ref) * QK_SCALE
    mq_ref[...] = mq.astype(bf16)
    gm_ref[...] = _silu(seg(_C_MG, MEM_WIDTH))

    logf = _log_sigmoid(seg(_C_F, LANES) + bf_ref[...])
    logf_ref[...] = logf[:, :N_FOX_HEADS]

    if not prompt:
        q_ref[...] = q.astype(bf16)
        return

    tm = x.shape[0]
    lane = _lane_iota((tm, LANES))
    i = pl.program_id(0)

    @pl.when(i % tiles_per_batch == 0)
    def _():
        carry_ref[...] = jnp.zeros_like(carry_ref)

    lf = jnp.where(lane < N_FOX_HEADS, logf, 0.0)
    c = _unpack3(jnp.dot(tril_ref[...], _pack3(lf), preferred_element_type=f32))
    c = jnp.where(lane < N_FOX_HEADS, c + carry_ref[0:1, :], 0.0)
    carry_ref[0:1, :] = c[tm - 1:tm, :]

    pieces = (_pack3(c).astype(f32) + jnp.where(lane == 3 * N_FOX_HEADS, 1.0, 0.0)).astype(bf16)
    qaug = jnp.dot(pieces, eq_ref[...], preferred_element_type=f32)
    kaug = jnp.dot(pieces, ek_ref[...], preferred_element_type=f32)
    for hd in range(N_FOX_HEADS):
        pair = (hd // 2) * LANES
        own = (lane < HEAD_DIM) if hd % 2 == 0 else (lane >= HEAD_DIM)
        ones_lane = HEAD_DIM if hd % 2 == 0 else 0
        blk = slice(hd * LANES, (hd + 1) * LANES)
        qa_ref[0, hd] = jnp.where(own, q[:, pair:pair + LANES], qaug[:, blk]).astype(bf16)
        ka_ref[0, hd] = jnp.where(own, k[:, pair:pair + LANES], kaug[:, blk]).astype(bf16)
        va_ref[0, hd] = jnp.where(own, v[:, pair:pair + LANES],
                                  jnp.where(lane == ones_lane, 1.0, 0.0)).astype(bf16)


def _bias_selectors():
    eq = np.zeros((LANES, N_FOX_HEADS * LANES), np.float32)
    ek = np.zeros((LANES, N_FOX_HEADS * LANES), np.float32)
    one = 3 * N_FOX_HEADS
    for hd in range(N_FOX_HEADS):
        base = hd * LANES + (HEAD_DIM if hd % 2 == 0 else 0)
        for p in range(_N_PIECES):
            eq[p * N_FOX_HEADS + hd, base + p] = 1.0
            eq[one, base + _N_PIECES + p] = 1.0
            ek[one, base + p] = 1.0
            ek[p * N_FOX_HEADS + hd, base + _N_PIECES + p] = -1.0
    return jnp.asarray(eq, bf16), jnp.asarray(ek, bf16)


def _head_mean_matrix():
    idx = np.arange(FOX_WIDTH) // HEAD_DIM
    return jnp.asarray((idx[:, None] == idx[None, :]).astype(np.float32) / HEAD_DIM, bf16)


def _in_proj(x, lw, *, batch, prompt):
    n, d = x.shape
    t = n // batch
    tm = ROW_TILE if prompt else n
    assert n % tm == 0 and (t % tm == 0 or not prompt)
    tiles_per_batch = t // tm if prompt else 1
    grid = (n // tm,)
    row = lambda w: pl.BlockSpec((tm, w), lambda i: (i, 0))
    full = lambda a: pl.BlockSpec(a.shape, lambda i: (0,) * a.ndim)

    ins = [x, lw["norm_g"], lw["w_in"], lw["b_forget"], lw["fox_q_norm"], lw["fox_k_norm"],
           lw["mem_q_norm"], lw["head_mean"]]
    in_specs = [row(d)] + [full(a) for a in ins[1:]]
    outs = [((n, FOX_WIDTH), f32), ((n, FOX_WIDTH), f32), ((n, N_FOX_HEADS), f32), ((n, FOX_WIDTH), f32),
            ((n, SSM_WIDTH), f32), ((n, SSM_WIDTH), f32), ((n, MEM_WIDTH), bf16), ((n, MEM_WIDTH), f32)]
    out_specs = [row(s[1]) for s, _ in outs]
    scratch = []
    if prompt:
        tril = jnp.asarray(np.tril(np.ones((tm, tm), np.float32)), bf16)
        extra = [tril, lw["bias_eq"], lw["bias_ek"]]
        ins += extra
        in_specs += [full(a) for a in extra]
        head_shape = (batch, N_FOX_HEADS, t, LANES)
        head_spec = pl.BlockSpec((1, N_FOX_HEADS, tm, LANES),
                                 lambda i: (i // tiles_per_batch, 0, i % tiles_per_batch, 0))
        outs += [(head_shape, bf16)] * 3
        out_specs += [head_spec] * 3
        scratch = [pltpu.VMEM((8, LANES), f32)]
    else:
        outs += [((n, FOX_WIDTH), bf16)]
        out_specs += [row(FOX_WIDTH)]

    res = pl.pallas_call(
        functools.partial(_in_proj_kernel, prompt=prompt, tiles_per_batch=tiles_per_batch),
        grid=grid, in_specs=in_specs, out_specs=out_specs,
        out_shape=[jax.ShapeDtypeStruct(s, dt) for s, dt in outs],
        scratch_shapes=scratch,
        compiler_params=_cparams(("arbitrary",)),
        name="in_proj_prompt" if prompt else "in_proj_sample",
    )(*ins)
    names = ["k", "v", "logf", "gf", "u", "gs", "mq", "gm"] + (["qa", "ka", "va"] if prompt else ["q"])
    return dict(zip(names, res))


def _fox_prompt_kernel(qa_ref, ka_ref, va_ref, o_ref):
    tq, tk = ATTN_TQ, ATTN_TK
    qb = pl.program_id(2)
    lane = _lane_iota((tq, LANES))
    heads = []
    for j in range(2):
        q = qa_ref[0, j]

        def update(off, carry, masked):
            m, acc = carry
            kk = ka_ref[0, j, pl.ds(off, tk), :]
            s = lax.dot_general(q, kk, (((1,), (1,)), ((), ())), preferred_element_type=f32)
            if masked:
                s = jnp.where(_row_iota((tq, tk)) >= _lane_iota((tq, tk)), s, -jnp.inf)
            m_new = jnp.maximum(m, jnp.max(s, axis=-1, keepdims=True))
            p = jnp.exp(s - m_new)
            vv = va_ref[0, j, pl.ds(off, tk), :]
            acc = jnp.exp(m - m_new) * acc + jnp.dot(p.astype(bf16), vv, preferred_element_type=f32)
            return m_new, acc

        carry = (jnp.full((tq, 1), -jnp.inf, f32), jnp.zeros((tq, LANES), f32))
        carry = lax.fori_loop(
            0, qb, lambda kb, c: update(pl.multiple_of(kb * tk, tk), c, False), carry)
        _, acc = update(pl.multiple_of(qb * tk, tk), carry, True)
        sum_lane = HEAD_DIM if j == 0 else 0
        heads.append(acc / acc[:, sum_lane:sum_lane + 1])
    o_ref[0] = jnp.where(lane < HEAD_DIM, heads[0], heads[1])


def _fox_prompt(qa, ka, va):
    b, nh, t, _ = qa.shape
    assert ATTN_TQ == ATTN_TK and t % ATTN_TQ == 0
    return pl.pallas_call(
        _fox_prompt_kernel,
        grid=(b, nh // 2, t // ATTN_TQ),
        in_specs=[pl.BlockSpec((1, 2, ATTN_TQ, LANES), lambda bi, hp, qi: (bi, hp, qi, 0)),
                  pl.BlockSpec((1, 2, t, LANES), lambda bi, hp, qi: (bi, hp, 0, 0)),
                  pl.BlockSpec((1, 2, t, LANES), lambda bi, hp, qi: (bi, hp, 0, 0))],
        out_specs=pl.BlockSpec((1, ATTN_TQ, LANES), lambda bi, hp, qi: (bi, qi, hp)),
        out_shape=jax.ShapeDtypeStruct((b, t, FOX_WIDTH), f32),
        compiler_params=_cparams(("arbitrary", "arbitrary", "arbitrary")),
        name="fox_prompt",
    )(qa, ka, va)


def _fox_sample_kernel(q_ref, kn_ref, vn_ref, kc_ref, vc_ref, lall_ref, tril_ref, ex_ref, o_ref,
                       kall_ref, vall_ref, *, past, tnew):
    nkeys = kall_ref.shape[0]
    kall_ref[0:past, :] = kc_ref[0].astype(bf16)
    vall_ref[0:past, :] = vc_ref[0].astype(bf16)
    pad = nkeys - past
    kall_ref[past:nkeys, :] = jnp.concatenate(
        [kn_ref[...].astype(bf16), jnp.zeros((pad - tnew, FOX_WIDTH), bf16)], axis=0)
    vall_ref[past:nkeys, :] = jnp.concatenate(
        [vn_ref[...].astype(bf16), jnp.zeros((pad - tnew, FOX_WIDTH), bf16)], axis=0)

    nq = N_FOX_HEADS * tnew
    lane = _lane_iota((nkeys, LANES))
    sub = _row_iota((nkeys, LANES))
    lf = jnp.where(lane < N_FOX_HEADS, lall_ref[0], 0.0)
    c = _unpack3(jnp.dot(tril_ref[...], _pack3(lf), preferred_element_type=f32))
    c = jnp.where(lane < N_FOX_HEADS, c, 0.0)
    ckey = jnp.dot(_pack3(c), ex_ref[...], preferred_element_type=f32)
    cnew = ckey[past:past + tnew, :]
    own_row = _row_iota((tnew, LANES)) == _lane_iota((tnew, LANES)) % tnew
    cq = jnp.sum(jnp.where(own_row, cnew, 0.0), axis=0, keepdims=True)

    tile = own_row.astype(bf16)
    qrep = lax.dot_general(q_ref[...], tile, (((0,), (0,)), ((), ())), preferred_element_type=f32)
    dl = _lane_iota((FOX_WIDTH, LANES))
    dr = _row_iota((FOX_WIDTH, LANES))
    qbd = jnp.where((dr // HEAD_DIM == dl // tnew) & (dl < nq), qrep, 0.0).astype(bf16)

    s = jnp.dot(kall_ref[...], qbd, preferred_element_type=f32) + cq - ckey
    s = jnp.where(sub <= past + lane % tnew, s, -jnp.inf)
    m = jnp.max(s, axis=0, keepdims=True)
    p = jnp.exp(s - m).astype(bf16)
    tn = (((0,), (0,)), ((), ()))
    o2 = lax.dot_general(p, vall_ref[...], tn, preferred_element_type=f32)
    l2 = lax.dot_general(p, jnp.ones((nkeys, LANES), bf16), tn, preferred_element_type=f32)
    o2 = o2 / l2[:, 0:1]
    olane = _lane_iota((tnew, FOX_WIDTH))
    out = jnp.zeros((tnew, FOX_WIDTH), f32)
    for hd in range(N_FOX_HEADS):
        out = out + jnp.where(olane // HEAD_DIM == hd, o2[hd * tnew:(hd + 1) * tnew, :], 0.0)
    o_ref[...] = out


def _fox_sample(q, k_new, v_new, cache_k, cache_v, l_all, *, batch):
    n = q.shape[0]
    tnew = n // batch
    past = cache_k.shape[1]
    nkeys = l_all.shape[1]
    assert N_FOX_HEADS * tnew <= LANES and tnew % 8 == 0 and past % 8 == 0
    tril = jnp.asarray(np.tril(np.ones((nkeys, nkeys), np.float32)), bf16)
    ex = np.zeros((LANES, LANES), np.float32)
    for p in range(_N_PIECES):
        for hd in range(N_FOX_HEADS):
            ex[p * N_FOX_HEADS + hd, hd * tnew:(hd + 1) * tnew] = 1.0
    ex = jnp.asarray(ex, bf16)
    rows = lambda w: pl.BlockSpec((tnew, w), lambda bi: (bi, 0))
    return pl.pallas_call(
        functools.partial(_fox_sample_kernel, past=past, tnew=tnew),
        grid=(batch,),
        in_specs=[rows(FOX_WIDTH), rows(FOX_WIDTH), rows(FOX_WIDTH),
                  pl.BlockSpec((1, past, FOX_WIDTH), lambda bi: (bi, 0, 0)),
                  pl.BlockSpec((1, past, FOX_WIDTH), lambda bi: (bi, 0, 0)),
                  pl.BlockSpec((1, nkeys, LANES), lambda bi: (bi, 0, 0)),
                  pl.BlockSpec((nkeys, nkeys), lambda bi: (0, 0)),
                  pl.BlockSpec((LANES, LANES), lambda bi: (0, 0))],
        out_specs=rows(FOX_WIDTH),
        out_shape=jax.ShapeDtypeStruct((n, FOX_WIDTH), f32),
        scratch_shapes=[pltpu.VMEM((nkeys, FOX_WIDTH), bf16), pltpu.VMEM((nkeys, FOX_WIDTH), bf16)],
        compiler_params=_cparams(("arbitrary",)),
        name="fox_sample",
    )(q, k_new, v_new, cache_k, cache_v, l_all, tril, ex)


def _s5_prep_kernel(are_r, aim_r, ldt_r, are_c, aim_c, ldt_c, bxr_ref, bxi_ref, cxr_ref, cxi_ref,
                    bst_ref, cst_ref, kd_ref, apow_ref):
    L, W, M = S5_CHUNK, SSM_WIDTH, SSM_MODES

    def lam(are, aim, ldt):
        dt = jnp.exp(ldt[...])
        return dt * are[...], dt * aim[...]

    def power(lr, li, k):
        mag = jnp.exp(float(k) * lr)
        return mag * jnp.cos(float(k) * li), mag * jnp.sin(float(k) * li)

    lr, li = lam(are_r, aim_r, ldt_r)
    lrc, lic = lam(are_c, aim_c, ldt_c)
    ar, ai = are_r[...], aim_r[...]
    abr, abi = power(lr, li, 1)
    den = ar * ar + ai * ai
    zr = ((abr - 1.0) * ar + abi * ai) / den
    zi = (abi * ar - (abr - 1.0) * ai) / den
    bbr = zr * bxr_ref[...] - zi * bxi_ref[...]
    bbi = zr * bxi_ref[...] + zi * bxr_ref[...]

    def b_pow(k):
        pr, pi = power(lr, li, k)
        return jnp.concatenate([pr * bbr - pi * bbi, pr * bbi + pi * bbr], axis=1)

    def c_pow(k):
        pr, pi = power(lrc, lic, k)
        cr, ci = cxr_ref[...], cxi_ref[...]
        return jnp.concatenate([cr * pr - ci * pi, -(cr * pi + ci * pr)], axis=0)

    b0 = b_pow(0)
    for tl in range(L):
        bst_ref[tl * W:(tl + 1) * W, :] = b_pow(L - 1 - tl).astype(bf16)
        cst_ref[:, tl * W:(tl + 1) * W] = c_pow(tl + 1).astype(bf16)
        kd = jnp.dot(b0, c_pow(L - 1 - tl), preferred_element_type=f32, precision=lax.Precision.HIGHEST)
        kd_ref[tl * W:(tl + 1) * W, :] = kd.astype(bf16)
    pr, pi = power(lr, li, L)
    apow_ref[...] = jnp.concatenate([pr, pi], axis=1)


def _s5_prep(a_re, a_im, log_dt, b_re, b_im, c_re, c_im):
    G, N, P, L = SSM_GROUPS, SSM_STATE, SSM_CH, S5_CHUNK
    W, M = SSM_WIDTH, SSM_MODES
    eye = jnp.eye(G, dtype=f32)
    bx = lambda b: (jnp.swapaxes(b, 1, 2)[:, :, None, :] * eye[:, None, :, None]).reshape(W, M)
    cx = lambda c: (jnp.swapaxes(c, 1, 2)[:, :, None, :] * eye[:, None, :, None]).reshape(M, W)
    ldt = jnp.repeat(log_dt, N)
    ins = [a_re.reshape(1, M), a_im.reshape(1, M), ldt.reshape(1, M),
           a_re.reshape(M, 1), a_im.reshape(M, 1), ldt.reshape(M, 1),
           bx(b_re), bx(b_im), cx(c_re), cx(c_im)]
    return pl.pallas_call(
        _s5_prep_kernel,
        out_shape=[jax.ShapeDtypeStruct((L * W, 2 * M), bf16), jax.ShapeDtypeStruct((2 * M, L * W), bf16),
                   jax.ShapeDtypeStruct((L * W, W), bf16), jax.ShapeDtypeStruct((1, 2 * M), f32)],
        compiler_params=pltpu.CompilerParams(vmem_limit_bytes=VMEM_LIMIT),
        name="s5_prep",
    )(*ins)


def _s5_kernel(u_ref, gs_ref, h0_ref, bst_ref, cst_ref, kd_ref, apow_ref, d_ref, wg_ref, bg_ref,
               o_ref, hT_ref, ds_ref, sp_ref, carry_ref, *, nb, rows):
    L, W, M = S5_CHUNK, SSM_WIDTH, SSM_MODES
    ti = pl.program_id(1)

    @pl.when(ti == 0)
    def _():
        for b in range(nb):
            carry_ref[b:b + 1, :] = h0_ref[b]

    ub = u_ref[...].astype(bf16)
    ds_ref[...] = jnp.dot(ub, bst_ref[...], preferred_element_type=f32)
    ar, ai = apow_ref[:, :M], apow_ref[:, M:]
    for b in range(nb):
        def step(c, carry):
            sr, si = carry
            r = b * rows + c
            sp_ref[pl.ds(r, 1), :] = jnp.concatenate([sr, si], axis=1)
            d = ds_ref[pl.ds(r, 1), :]
            return ar * sr - ai * si + d[:, :M], ar * si + ai * sr + d[:, M:]

        s0 = carry_ref[b:b + 1, :]
        sr, si = lax.fori_loop(0, rows, step, (s0[:, :M], s0[:, M:]))
        carry_ref[b:b + 1, :] = jnp.concatenate([sr, si], axis=1)
        hT_ref[b] = jnp.concatenate([sr, si], axis=1)

    spb = sp_ref[...].astype(bf16)
    for tl in range(L):
        cols = slice(tl * W, (tl + 1) * W)
        y = jnp.dot(ub[:, :(tl + 1) * W], kd_ref[(L - 1 - tl) * W:, :], preferred_element_type=f32)
        y = y + jnp.dot(spb, cst_ref[:, cols], preferred_element_type=f32)
        y = y + d_ref[...] * u_ref[:, cols]
        g = _gelu_tanh(y)
        gate = jax.nn.sigmoid(jnp.dot(g.astype(bf16), wg_ref[...], preferred_element_type=f32) + bg_ref[...])
        o_ref[:, cols] = (g * gate * gs_ref[:, cols]).astype(bf16)


def _s5(u, gs, h0, ops, d_row, w_glu, b_glu, *, batch, rows_per_step):
    L, W, M = S5_CHUNK, SSM_WIDTH, SSM_MODES
    n = u.shape[0]
    t = n // batch
    assert t % L == 0
    chunks = t // L
    bst, cst, kd, apow = ops
    u2 = u.reshape(n // L, L * W)
    gs2 = gs.reshape(n // L, L * W)
    if rows_per_step is None:
        nb, rows, grid = batch, chunks, (1, 1)
    else:
        assert chunks % rows_per_step == 0
        nb, rows, grid = 1, rows_per_step, (batch, chunks // rows_per_step)
    tiles = grid[1]
    blk = pl.BlockSpec((nb * rows, L * W), lambda bi, ti: (bi * tiles + ti, 0))
    full = lambda a: pl.BlockSpec(a.shape, lambda bi, ti: (0,) * a.ndim)
    state = pl.BlockSpec((nb, 1, 2 * M), lambda bi, ti: (bi, 0, 0))
    out, hT = pl.pallas_call(
        functools.partial(_s5_kernel, nb=nb, rows=rows),
        grid=grid,
        in_specs=[blk, blk, state, full(bst), full(cst), full(kd), full(apow), full(d_row),
                  full(w_glu), full(b_glu)],
        out_specs=[blk, state],
        out_shape=[jax.ShapeDtypeStruct((n // L, L * W), bf16), jax.ShapeDtypeStruct((batch, 1, 2 * M), f32)],
        scratch_shapes=[pltpu.VMEM((nb * rows, 2 * M), f32), pltpu.VMEM((nb * rows, 2 * M), f32),
                        pltpu.VMEM((max(nb, 8), 2 * M), f32)],
        compiler_params=_cparams(("arbitrary", "arbitrary")),
        name="s5_prompt" if rows_per_step is not None else "s5_sample",
    )(u2, gs2, h0, bst, cst, kd, apow, d_row, w_glu, b_glu)
    return out.reshape(n, W), hT


def _memory_kv_kernel(mem_ref, g_ref, w_ref, kg_ref, hm_ref, mk_ref, mv_ref):
    x = mem_ref[...]
    ms = jnp.mean(x * x, axis=-1, keepdims=True)
    h = (x * lax.rsqrt(ms + EPS) * g_ref[...]).astype(bf16)
    mk = jnp.dot(h, w_ref[:, 0:MEM_WIDTH], preferred_element_type=f32)
    mk_ref[...] = _head_norm(mk, hm_ref.at[0:MEM_WIDTH, 0:MEM_WIDTH], kg_ref)
    mv_ref[...] = jnp.dot(h, w_ref[:, MEM_WIDTH:2 * MEM_WIDTH], preferred_element_type=f32)


def _memory_kv(mem, lw):
    n = mem.shape[0]
    return pl.pallas_call(
        _memory_kv_kernel,
        out_shape=[jax.ShapeDtypeStruct((n, MEM_WIDTH), f32)] * 2,
        compiler_params=pltpu.CompilerParams(vmem_limit_bytes=VMEM_LIMIT),
        name="memory_kv",
    )(mem, lw["mem_norm"], lw["w_mem_kv"], lw["mem_k_norm"], lw["head_mean"])


def _mix_out_kernel(x_ref, fox_ref, gf_ref, ssm_ref, mq_ref, gm_ref, mk_ref, mv_ref, w_ref, o_ref):
    tm = x_ref.shape[0]
    nmem = mk_ref.shape[1]
    mq = mq_ref[...]
    mk = mk_ref[0]
    mv = mv_ref[0]
    qlane = _lane_iota((tm, MEM_WIDTH)) // HEAD_DIM
    vlane = _lane_iota((nmem, MEM_WIDTH)) // HEAD_DIM
    mem = jnp.zeros((tm, MEM_WIDTH), f32)
    for hd in range(N_MEM_HEADS):
        qh = jnp.where(qlane == hd, mq, jnp.zeros_like(mq))
        s = lax.dot_general(qh, mk, (((1,), (1,)), ((), ())), preferred_element_type=f32)
        p = jnp.exp(s - jnp.max(s, axis=-1, keepdims=True))
        p = p / jnp.sum(p, axis=-1, keepdims=True)
        vh = jnp.where(vlane == hd, mv, jnp.zeros_like(mv))
        mem = mem + jnp.dot(p.astype(bf16), vh, preferred_element_type=f32)
    fox = (fox_ref[...] * gf_ref[...]).astype(bf16)
    memg = (mem * gm_ref[...]).astype(bf16)
    y = x_ref[...]
    y = y + jnp.dot(fox, w_ref[0:FOX_WIDTH, :], preferred_element_type=f32)
    y = y + jnp.dot(ssm_ref[...], w_ref[FOX_WIDTH:FOX_WIDTH + SSM_WIDTH, :], preferred_element_type=f32)
    y = y + jnp.dot(memg, w_ref[FOX_WIDTH + SSM_WIDTH:, :], preferred_element_type=f32)
    o_ref[...] = y


def _mix_out(x, fox, gf, ssm, mq, gm, mk, mv, w_out, *, batch, tm):
    n, d = x.shape
    t = n // batch
    assert t % tm == 0
    tiles_per_batch = t // tm
    row = lambda w: pl.BlockSpec((tm, w), lambda i: (i, 0))
    memb = pl.BlockSpec((1,) + mk.shape[1:], lambda i: (i // tiles_per_batch, 0, 0))
    return pl.pallas_call(
        _mix_out_kernel,
        grid=(n // tm,),
        in_specs=[row(d), row(FOX_WIDTH), row(FOX_WIDTH), row(SSM_WIDTH), row(MEM_WIDTH), row(MEM_WIDTH),
                  memb, memb, pl.BlockSpec(w_out.shape, lambda i: (0, 0))],
        out_specs=row(d),
        out_shape=jax.ShapeDtypeStruct((n, d), f32),
        compiler_params=_cparams(("arbitrary",)),
        name="mix_out_prompt" if tiles_per_batch > 1 else "mix_out_sample",
    )(x, fox, gf, ssm, mq, gm, mk, mv, w_out)


def _layer_weights(l, norm_g, w_in, b_forget, fox_q_norm, fox_k_norm, mem_q_norm, mem_norm, w_mem_kv,
                   mem_k_norm, w_out, w_glu, b_glu, ssm_d):
    w = w_in[l]
    nf = 4 * FOX_WIDTH
    d = w.shape[0]
    w_packed = jnp.concatenate(
        [w[:, :nf], w[:, nf + N_FOX_HEADS:], w[:, nf:nf + N_FOX_HEADS],
         jnp.zeros((d, _W_COLS - w.shape[1]), w.dtype)], axis=1).astype(bf16)
    eq, ek = _bias_selectors()
    return dict(
        norm_g=norm_g[l].reshape(1, d), w_in=w_packed,
        b_forget=jnp.pad(b_forget[l], (0, LANES - N_FOX_HEADS)).reshape(1, LANES),
        fox_q_norm=jnp.tile(fox_q_norm[l], N_FOX_HEADS).reshape(1, FOX_WIDTH),
        fox_k_norm=jnp.tile(fox_k_norm[l], N_FOX_HEADS).reshape(1, FOX_WIDTH),
        mem_q_norm=jnp.tile(mem_q_norm[l], N_MEM_HEADS).reshape(1, MEM_WIDTH),
        mem_k_norm=jnp.tile(mem_k_norm[l], N_MEM_HEADS).reshape(1, MEM_WIDTH),
        mem_norm=mem_norm[l].reshape(1, d), w_mem_kv=w_mem_kv[l].astype(bf16),
        head_mean=_head_mean_matrix(), bias_eq=eq, bias_ek=ek,
        w_out=w_out[l].astype(bf16), w_glu=w_glu[l].astype(bf16), b_glu=b_glu[l].reshape(1, SSM_WIDTH),
        ssm_d=ssm_d[l].reshape(1, SSM_WIDTH))


def _split_state(hT, batch):
    hT = hT.reshape(batch, 2, SSM_GROUPS, SSM_STATE)
    return hT[:, 0], hT[:, 1]


def kernel(x_prompt, x_sample, mem_prompt, cache_fox_k, cache_fox_v, cache_fox_logf, state_ssm_re, state_ssm_im, cache_mem_k, cache_mem_v, norm_g, w_in, b_forget, fox_q_norm, fox_k_norm, ssm_a_re, ssm_a_im, ssm_log_dt, ssm_b_re, ssm_b_im, ssm_c_re, ssm_c_im, ssm_d, w_glu, b_glu, mem_norm, w_mem_kv, mem_q_norm, mem_k_norm, w_out):
    B, T, D = x_prompt.shape
    Bs, Ts, _ = x_sample.shape
    depth = w_in.shape[0]
    past = cache_fox_k.shape[2]
    nmem = mem_prompt.shape[1]
    nkeys = -(-(past + Ts) // LANES) * LANES
    M = SSM_MODES

    xp = x_prompt.reshape(B * T, D)
    xs = x_sample.reshape(Bs * Ts, D)
    mem = mem_prompt.reshape(B * nmem, D)
    outs = {k: [] for k in ("pk", "pv", "pf", "pre", "pim", "pmk", "pmv", "sk", "sv", "sf", "sre", "sim")}
    for l in range(depth):
        lw = _layer_weights(l, norm_g, w_in, b_forget, fox_q_norm, fox_k_norm, mem_q_norm, mem_norm,
                            w_mem_kv, mem_k_norm, w_out, w_glu, b_glu, ssm_d)
        ops = _s5_prep(ssm_a_re[l], ssm_a_im[l], ssm_log_dt[l], ssm_b_re[l], ssm_b_im[l],
                       ssm_c_re[l], ssm_c_im[l])
        mk, mv = _memory_kv(mem, lw)

        pr = _in_proj(xp, lw, batch=B, prompt=True)
        fox = _fox_prompt(pr["qa"], pr["ka"], pr["va"]).reshape(B * T, FOX_WIDTH)
        ssm, hT = _s5(pr["u"], pr["gs"], jnp.zeros((B, 1, 2 * M), f32), ops, lw["ssm_d"], lw["w_glu"],
                      lw["b_glu"], batch=B, rows_per_step=128)
        xp = _mix_out(xp, fox, pr["gf"], ssm, pr["mq"], pr["gm"],
                      mk.reshape(B, nmem, MEM_WIDTH).astype(bf16), mv.reshape(B, nmem, MEM_WIDTH).astype(bf16),
                      lw["w_out"], batch=B, tm=ROW_TILE)
        hre, him = _split_state(hT, B)
        outs["pk"].append(pr["k"].reshape(B, T, N_FOX_HEADS, HEAD_DIM))
        outs["pv"].append(pr["v"].reshape(B, T, N_FOX_HEADS, HEAD_DIM))
        outs["pf"].append(pr["logf"].reshape(B, T, N_FOX_HEADS))
        outs["pre"].append(hre)
        outs["pim"].append(him)
        outs["pmk"].append(mk.reshape(B, nmem, N_MEM_HEADS, HEAD_DIM))
        outs["pmv"].append(mv.reshape(B, nmem, N_MEM_HEADS, HEAD_DIM))

        sr = _in_proj(xs, lw, batch=Bs, prompt=False)
        l_all = jnp.concatenate(
            [cache_fox_logf[l].astype(f32), sr["logf"].reshape(Bs, Ts, N_FOX_HEADS),
             jnp.zeros((Bs, nkeys - past - Ts, N_FOX_HEADS), f32)], axis=1)
        l_all = jnp.pad(l_all, ((0, 0), (0, 0), (0, LANES - N_FOX_HEADS)))
        fox_s = _fox_sample(sr["q"], sr["k"], sr["v"], cache_fox_k[l].reshape(Bs, past, FOX_WIDTH),
                            cache_fox_v[l].reshape(Bs, past, FOX_WIDTH), l_all, batch=Bs)
        h0 = jnp.concatenate([state_ssm_re[l].reshape(Bs, 1, M), state_ssm_im[l].reshape(Bs, 1, M)],
                             axis=-1).astype(f32)
        ssm_s, hT_s = _s5(sr["u"], sr["gs"], h0, ops, lw["ssm_d"], lw["w_glu"], lw["b_glu"], batch=Bs,
                          rows_per_step=None)
        xs = _mix_out(xs, fox_s, sr["gf"], ssm_s, sr["mq"], sr["gm"],
                      cache_mem_k[l].reshape(Bs, nmem, MEM_WIDTH).astype(bf16),
                      cache_mem_v[l].reshape(Bs, nmem, MEM_WIDTH).astype(bf16),
                      lw["w_out"], batch=Bs, tm=Ts)
        sre, sim = _split_state(hT_s, Bs)
        outs["sk"].append(sr["k"].reshape(Bs, Ts, N_FOX_HEADS, HEAD_DIM))
        outs["sv"].append(sr["v"].reshape(Bs, Ts, N_FOX_HEADS, HEAD_DIM))
        outs["sf"].append(sr["logf"].reshape(Bs, Ts, N_FOX_HEADS))
        outs["sre"].append(sre)
        outs["sim"].append(sim)

    st = {k: jnp.stack(v) for k, v in outs.items()}
    return (xp.reshape(B, T, D), xs.reshape(Bs, Ts, D), st["pk"], st["pv"], st["pf"], st["pre"], st["pim"],
            st["pmk"], st["pmv"], st["sk"], st["sv"], st["sf"], st["sre"], st["sim"])
```

```python
import functools
import math

import numpy as np
import jax
import jax.numpy as jnp
from jax import lax
from jax.experimental import pallas as pl
from jax.experimental.pallas import tpu as pltpu

f32 = jnp.float32
bf16 = jnp.bfloat16

HEAD_DIM = 64
N_FOX_HEADS = 8
FOX_WIDTH = N_FOX_HEADS * HEAD_DIM
SSM_GROUPS = 16
SSM_CH = 16
SSM_STATE = 64
SSM_WIDTH = SSM_GROUPS * SSM_CH
SSM_MODES = SSM_GROUPS * SSM_STATE
N_MEM_HEADS = 4
MEM_WIDTH = N_MEM_HEADS * HEAD_DIM
EPS = 1e-6
QK_SCALE = HEAD_DIM ** -0.5

LANES = 128
S5_CHUNK = 8
ROW_TILE = 512
ATTN_TQ = 512
ATTN_TK = 512
VMEM_LIMIT = 56 * 1024 * 1024

_C_Q, _C_K, _C_V, _C_G = 0, 512, 1024, 1536
_C_SU, _C_SG, _C_MQ, _C_MG, _C_F = 2048, 2304, 2560, 2816, 3072
_W_COLS = 3200
_N_PIECES = 3


def _cparams(sem):
    return pltpu.CompilerParams(dimension_semantics=sem, vmem_limit_bytes=VMEM_LIMIT)


def _silu(x):
    return x * jax.nn.sigmoid(x)


def _log_sigmoid(x):
    return -(jnp.maximum(-x, 0.0) + jnp.log1p(jnp.exp(-jnp.abs(x))))


def _gelu_tanh(x):
    return 0.5 * x * (1.0 + jnp.tanh(math.sqrt(2.0 / math.pi) * (x + 0.044715 * (x * x * x))))


def _split3(x):
    hi = x.astype(bf16).astype(f32)
    r = x - hi
    mid = r.astype(bf16).astype(f32)
    lo = (r - mid).astype(bf16).astype(f32)
    return hi, mid, lo


def _pack3(x):
    hi, mid, lo = _split3(x)
    return (hi + pltpu.roll(mid, 8, 1) + pltpu.roll(lo, 16, 1)).astype(bf16)


def _unpack3(y):
    return y + pltpu.roll(y, LANES - 8, 1) + pltpu.roll(y, LANES - 16, 1)


def _lane_iota(shape):
    return lax.broadcasted_iota(jnp.int32, shape, len(shape) - 1)


def _row_iota(shape):
    return lax.broadcasted_iota(jnp.int32, shape, len(shape) - 2)


def _head_norm(z, hm_ref, g_ref):
    ms = jnp.dot((z * z).astype(bf16), hm_ref[...], preferred_element_type=f32)
    return z * lax.rsqrt(ms + EPS) * g_ref[...]


def _in_proj_kernel(*refs, prompt, tiles_per_batch):
    if prompt:
        (x_ref, ng_ref, w_ref, bf_ref, qg_ref, kg_ref, mqg_ref, hm_ref, tril_ref, eq_ref, ek_ref,
         k_ref, v_ref, logf_ref, gf_ref, u_ref, gs_ref, mq_ref, gm_ref, qa_ref, ka_ref, va_ref,
         carry_ref) = refs
    else:
        (x_ref, ng_ref, w_ref, bf_ref, qg_ref, kg_ref, mqg_ref, hm_ref,
         k_ref, v_ref, logf_ref, gf_ref, u_ref, gs_ref, mq_ref, gm_ref, q_ref) = refs

    x = x_ref[...]
    ms = jnp.mean(x * x, axis=-1, keepdims=True)
    h = (x * lax.rsqrt(ms + EPS) * ng_ref[...]).astype(bf16)

    def seg(lo, width):
        return jnp.dot(h, w_ref[:, lo:lo + width], preferred_element_type=f32)

    q = _head_norm(seg(_C_Q, FOX_WIDTH), hm_ref, qg_ref) * QK_SCALE
    k = _head_norm(seg(_C_K, FOX_WIDTH), hm_ref, kg_ref)
    v = seg(_C_V, FOX_WIDTH)
    k_ref[...] = k
    v_ref[...] = v
    gf_ref[...] = _silu(seg(_C_G, FOX_WIDTH))
    u_ref[...] = seg(_C_SU, SSM_WIDTH)
    gs_ref[...] = _silu(seg(_C_SG, SSM_WIDTH))
    mq = _head_norm(seg(_C_MQ, MEM_WIDTH), hm_ref.at[0:MEM_WIDTH, 0:MEM_WIDTH], mqg_ref) * QK_SCALE
    mq_ref[...] = mq.astype(bf16)
    gm_ref[...] = _silu(seg(_C_MG, MEM_WIDTH))

    logf = _log_sigmoid(seg(_C_F, LANES) + bf_ref[...])
    logf_ref[...] = logf[:, :N_FOX_HEADS]

    if not prompt:
        q_ref[...] = q.astype(bf16)
        return

    tm = x.shape[0]
    lane = _lane_iota((tm, LANES))
    i = pl.program_id(0)

    @pl.when(i % tiles_per_batch == 0)
    def _():
        carry_ref[...] = jnp.zeros_like(carry_ref)

    lf = jnp.where(lane < N_FOX_HEADS, logf, 0.0)
    c = _unpack3(jnp.dot(tril_ref[...], _pack3(lf), preferred_element_type=f32))
    c = jnp.where(lane < N_FOX_HEADS, c + carry_ref[0:1, :], 0.0)
    carry_ref[0:1, :] = c[tm - 1:tm, :]

    pieces = (_pack3(c).astype(f32) + jnp.where(lane == 3 * N_FOX_HEADS, 1.0, 0.0)).astype(bf16)
    qaug = jnp.dot(pieces, eq_ref[...], preferred_element_type=f32)
    kaug = jnp.dot(pieces, ek_ref[...], preferred_element_type=f32)
    for hd in range(N_FOX_HEADS):
        pair = (hd // 2) * LANES
        own = (lane < HEAD_DIM) if hd % 2 == 0 else (lane >= HEAD_DIM)
        ones_lane = HEAD_DIM if hd % 2 == 0 else 0
        blk = slice(hd * LANES, (hd + 1) * LANES)
        qa_ref[0, hd] = jnp.where(own, q[:, pair:pair + LANES], qaug[:, blk]).astype(bf16)
        ka_ref[0, hd] = jnp.where(own, k[:, pair:pair + LANES], kaug[:, blk]).astype(bf16)
        va_ref[0, hd] = jnp.where(own, v[:, pair:pair + LANES],
                                  jnp.where(lane == ones_lane, 1.0, 0.0)).astype(bf16)


def _bias_selectors():
    eq = np.zeros((LANES, N_FOX_HEADS * LANES), np.float32)
    ek = np.zeros((LANES, N_FOX_HEADS * LANES), np.float32)
    one = 3 * N_FOX_HEADS
    for hd in range(N_FOX_HEADS):
        base = hd * LANES + (HEAD_DIM if hd % 2 == 0 else 0)
        for p in range(_N_PIECES):
            eq[p * N_FOX_HEADS + hd, base + p] = 1.0
            eq[one, base + _N_PIECES + p] = 1.0
            ek[one, base + p] = 1.0
            ek[p * N_FOX_HEADS + hd, base + _N_PIECES + p] = -1.0
    return jnp.asarray(eq, bf16), jnp.asarray(ek, bf16)


def _head_mean_matrix():
    idx = np.arange(FOX_WIDTH) // HEAD_DIM
    return jnp.asarray((idx[:, None] == idx[None, :]).astype(np.float32) / HEAD_DIM, bf16)


def _in_proj(x, lw, *, batch, prompt):
    n, d = x.shape
    t = n // batch
    tm = ROW_TILE if prompt else n
    assert n % tm == 0 and (t % tm == 0 or not prompt)
    tiles_per_batch = t // tm if prompt else 1
    grid = (n // tm,)
    row = lambda w: pl.BlockSpec((tm, w), lambda i: (i, 0))
    full = lambda a: pl.BlockSpec(a.shape, lambda i: (0,) * a.ndim)

    ins = [x, lw["norm_g"], lw["w_in"], lw["b_forget"], lw["fox_q_norm"], lw["fox_k_norm"],
           lw["mem_q_norm"], lw["head_mean"]]
    in_specs = [row(d)] + [full(a) for a in ins[1:]]
    outs = [((n, FOX_WIDTH), f32), ((n, FOX_WIDTH), f32), ((n, N_FOX_HEADS), f32), ((n, FOX_WIDTH), f32),
            ((n, SSM_WIDTH), f32), ((n, SSM_WIDTH), f32), ((n, MEM_WIDTH), bf16), ((n, MEM_WIDTH), f32)]
    out_specs = [row(s[1]) for s, _ in outs]
    scratch = []
    if prompt:
        tril = jnp.asarray(np.tril(np.ones((tm, tm), np.float32)), bf16)
        extra = [tril, lw["bias_eq"], lw["bias_ek"]]
        ins += extra
        in_specs += [full(a) for a in extra]
        head_shape = (batch, N_FOX_HEADS, t, LANES)
        head_spec = pl.BlockSpec((1, N_FOX_HEADS, tm, LANES),
                                 lambda i: (i // tiles_per_batch, 0, i % tiles_per_batch, 0))
        outs += [(head_shape, bf16)] * 3
        out_specs += [head_spec] * 3
        scratch = [pltpu.VMEM((8, LANES), f32)]
    else:
        outs += [((n, FOX_WIDTH), bf16)]
        out_specs += [row(FOX_WIDTH)]

    res = pl.pallas_call(
        functools.partial(_in_proj_kernel, prompt=prompt, tiles_per_batch=tiles_per_batch),
        grid=grid, in_specs=in_specs, out_specs=out_specs,
        out_shape=[jax.ShapeDtypeStruct(s, dt) for s, dt in outs],
        scratch_shapes=scratch,
        compiler_params=_cparams(("arbitrary",)),
        name="in_proj_prompt" if prompt else "in_proj_sample",
    )(*ins)
    names = ["k", "v", "logf", "gf", "u", "gs", "mq", "gm"] + (["qa", "ka", "va"] if prompt else ["q"])
    return dict(zip(names, res))


def _fox_prompt_kernel(qa_ref, ka_ref, va_ref, o_ref, s0_ref, s1_ref, m_ref, acc_ref):
    tq, tk = ATTN_TQ, ATTN_TK
    qb = pl.program_id(2)
    s_refs = (s0_ref, s1_ref)
    m_ref[...] = jnp.full(m_ref.shape, -jnp.inf, f32)
    acc_ref[...] = jnp.zeros(acc_ref.shape, f32)

    def scores(kb, slot):
        off = pl.multiple_of(kb * tk, tk)
        for j in range(2):
            kk = ka_ref[0, j, pl.ds(off, tk), :]
            s_refs[slot][j] = lax.dot_general(qa_ref[0, j], kk, (((1,), (1,)), ((), ())),
                                              preferred_element_type=f32)

    def softmax_pv(kb, slot, masked):
        off = pl.multiple_of(kb * tk, tk)
        for j in range(2):
            s = s_refs[slot][j]
            if masked:
                s = jnp.where(_row_iota((tq, tk)) >= _lane_iota((tq, tk)), s, -jnp.inf)
            m = m_ref[j]
            m_new = jnp.maximum(m, jnp.max(s, axis=-1, keepdims=True))
            p = jnp.exp(s - pltpu.repeat(m_new, tk // LANES, axis=1))
            vv = va_ref[0, j, pl.ds(off, tk), :]
            acc_ref[j] = jnp.exp(m - m_new) * acc_ref[j] + jnp.dot(p.astype(bf16), vv, preferred_element_type=f32)
            m_ref[j] = m_new

    scores(0, 0)

    def body(i, carry):
        scores(2 * i + 1, 1)
        softmax_pv(2 * i, 0, False)
        scores(2 * i + 2, 0)
        softmax_pv(2 * i + 1, 1, False)
        return carry

    lax.fori_loop(0, qb // 2, body, 0)

    @pl.when(qb % 2 == 0)
    def _():
        softmax_pv(qb, 0, True)

    @pl.when(qb % 2 == 1)
    def _():
        scores(qb, 1)
        softmax_pv(qb - 1, 0, False)
        softmax_pv(qb, 1, True)
    h0 = acc_ref[0] / acc_ref[0, :, HEAD_DIM:HEAD_DIM + 1]
    h1 = acc_ref[1] / acc_ref[1, :, 0:1]
    o_ref[0] = jnp.where(_lane_iota((tq, LANES)) < HEAD_DIM, h0, h1)


def _fox_prompt(qa, ka, va):
    b, nh, t, _ = qa.shape
    assert ATTN_TQ == ATTN_TK and t % ATTN_TQ == 0
    return pl.pallas_call(
        _fox_prompt_kernel,
        grid=(b, nh // 2, t // ATTN_TQ),
        in_specs=[pl.BlockSpec((1, 2, ATTN_TQ, LANES), lambda bi, hp, qi: (bi, hp, qi, 0)),
                  pl.BlockSpec((1, 2, t, LANES), lambda bi, hp, qi: (bi, hp, 0, 0)),
                  pl.BlockSpec((1, 2, t, LANES), lambda bi, hp, qi: (bi, hp, 0, 0))],
        out_specs=pl.BlockSpec((1, ATTN_TQ, LANES), lambda bi, hp, qi: (bi, qi, hp)),
        out_shape=jax.ShapeDtypeStruct((b, t, FOX_WIDTH), f32),
        scratch_shapes=[pltpu.VMEM((2, ATTN_TQ, ATTN_TK), f32), pltpu.VMEM((2, ATTN_TQ, ATTN_TK), f32),
                        pltpu.VMEM((2, ATTN_TQ, LANES), f32), pltpu.VMEM((2, ATTN_TQ, LANES), f32)],
        compiler_params=_cparams(("arbitrary", "arbitrary", "arbitrary")),
        name="fox_prompt",
    )(qa, ka, va)


def _fox_sample_kernel(q_ref, kn_ref, vn_ref, kc_ref, vc_ref, lall_ref, tril_ref, ex_ref, o_ref,
                       kall_ref, vall_ref, *, past, tnew):
    nkeys = kall_ref.shape[0]
    kall_ref[0:past, :] = kc_ref[0].astype(bf16)
    vall_ref[0:past, :] = vc_ref[0].astype(bf16)
    pad = nkeys - past
    kall_ref[past:nkeys, :] = jnp.concatenate(
        [kn_ref[...].astype(bf16), jnp.zeros((pad - tnew, FOX_WIDTH), bf16)], axis=0)
    vall_ref[past:nkeys, :] = jnp.concatenate(
        [vn_ref[...].astype(bf16), jnp.zeros((pad - tnew, FOX_WIDTH), bf16)], axis=0)

    nq = N_FOX_HEADS * tnew
    lane = _lane_iota((nkeys, LANES))
    sub = _row_iota((nkeys, LANES))
    lf = jnp.where(lane < N_FOX_HEADS, lall_ref[0], 0.0)
    c = _unpack3(jnp.dot(tril_ref[...], _pack3(lf), preferred_element_type=f32))
    c = jnp.where(lane < N_FOX_HEADS, c, 0.0)
    ckey = jnp.dot(_pack3(c), ex_ref[...], preferred_element_type=f32)
    cnew = ckey[past:past + tnew, :]
    own_row = _row_iota((tnew, LANES)) == _lane_iota((tnew, LANES)) % tnew
    cq = jnp.sum(jnp.where(own_row, cnew, 0.0), axis=0, keepdims=True)

    tile = own_row.astype(bf16)
    qrep = lax.dot_general(q_ref[...], tile, (((0,), (0,)), ((), ())), preferred_element_type=f32)
    dl = _lane_iota((FOX_WIDTH, LANES))
    dr = _row_iota((FOX_WIDTH, LANES))
    qbd = jnp.where((dr // HEAD_DIM == dl // tnew) & (dl < nq), qrep, 0.0).astype(bf16)

    s = jnp.dot(kall_ref[...], qbd, preferred_element_type=f32) + cq - ckey
    s = jnp.where(sub <= past + lane % tnew, s, -jnp.inf)
    m = jnp.max(s, axis=0, keepdims=True)
    p = jnp.exp(s - m).astype(bf16)
    tn = (((0,), (0,)), ((), ()))
    o2 = lax.dot_general(p, vall_ref[...], tn, preferred_element_type=f32)
    l2 = lax.dot_general(p, jnp.ones((nkeys, LANES), bf16), tn, preferred_element_type=f32)
    o2 = o2 / l2[:, 0:1]
    olane = _lane_iota((tnew, FOX_WIDTH))
    out = jnp.zeros((tnew, FOX_WIDTH), f32)
    for hd in range(N_FOX_HEADS):
        out = out + jnp.where(olane // HEAD_DIM == hd, o2[hd * tnew:(hd + 1) * tnew, :], 0.0)
    o_ref[...] = out


def _fox_sample(q, k_new, v_new, cache_k, cache_v, l_all, *, batch):
    n = q.shape[0]
    tnew = n // batch
    past = cache_k.shape[1]
    nkeys = l_all.shape[1]
    assert N_FOX_HEADS * tnew <= LANES and tnew % 8 == 0 and past % 8 == 0
    tril = jnp.asarray(np.tril(np.ones((nkeys, nkeys), np.float32)), bf16)
    ex = np.zeros((LANES, LANES), np.float32)
    for p in range(_N_PIECES):
        for hd in range(N_FOX_HEADS):
            ex[p * N_FOX_HEADS + hd, hd * tnew:(hd + 1) * tnew] = 1.0
    ex = jnp.asarray(ex, bf16)
    rows = lambda w: pl.BlockSpec((tnew, w), lambda bi: (bi, 0))
    return pl.pallas_call(
        functools.partial(_fox_sample_kernel, past=past, tnew=tnew),
        grid=(batch,),
        in_specs=[rows(FOX_WIDTH), rows(FOX_WIDTH), rows(FOX_WIDTH),
                  pl.BlockSpec((1, past, FOX_WIDTH), lambda bi: (bi, 0, 0)),
                  pl.BlockSpec((1, past, FOX_WIDTH), lambda bi: (bi, 0, 0)),
                  pl.BlockSpec((1, nkeys, LANES), lambda bi: (bi, 0, 0)),
                  pl.BlockSpec((nkeys, nkeys), lambda bi: (0, 0)),
                  pl.BlockSpec((LANES, LANES), lambda bi: (0, 0))],
        out_specs=rows(FOX_WIDTH),
        out_shape=jax.ShapeDtypeStruct((n, FOX_WIDTH), f32),
        scratch_shapes=[pltpu.VMEM((nkeys, FOX_WIDTH), bf16), pltpu.VMEM((nkeys, FOX_WIDTH), bf16)],
        compiler_params=_cparams(("arbitrary",)),
        name="fox_sample",
    )(q, k_new, v_new, cache_k, cache_v, l_all, tril, ex)


def _s5_prep_kernel(are_r, aim_r, ldt_r, are_c, aim_c, ldt_c, bxr_ref, bxi_ref, cxr_ref, cxi_ref,
                    bst_ref, cst_ref, kd_ref, apow_ref):
    L, W, M = S5_CHUNK, SSM_WIDTH, SSM_MODES

    def lam(are, aim, ldt):
        dt = jnp.exp(ldt[...])
        return dt * are[...], dt * aim[...]

    def power(lr, li, k):
        mag = jnp.exp(float(k) * lr)
        return mag * jnp.cos(float(k) * li), mag * jnp.sin(float(k) * li)

    lr, li = lam(are_r, aim_r, ldt_r)
    lrc, lic = lam(are_c, aim_c, ldt_c)
    ar, ai = are_r[...], aim_r[...]
    abr, abi = power(lr, li, 1)
    den = ar * ar + ai * ai
    zr = ((abr - 1.0) * ar + abi * ai) / den
    zi = (abi * ar - (abr - 1.0) * ai) / den
    bbr = zr * bxr_ref[...] - zi * bxi_ref[...]
    bbi = zr * bxi_ref[...] + zi * bxr_ref[...]

    def b_pow(k):
        pr, pi = power(lr, li, k)
        return jnp.concatenate([pr * bbr - pi * bbi, pr * bbi + pi * bbr], axis=1)

    def c_pow(k):
        pr, pi = power(lrc, lic, k)
        cr, ci = cxr_ref[...], cxi_ref[...]
        return jnp.concatenate([cr * pr - ci * pi, -(cr * pi + ci * pr)], axis=0)

    b0 = b_pow(0)
    for tl in range(L):
        bst_ref[tl * W:(tl + 1) * W, :] = b_pow(L - 1 - tl).astype(bf16)
        cst_ref[:, tl * W:(tl + 1) * W] = c_pow(tl + 1).astype(bf16)
        kd = jnp.dot(b0, c_pow(L - 1 - tl), preferred_element_type=f32, precision=lax.Precision.HIGHEST)
        kd_ref[tl * W:(tl + 1) * W, :] = kd.astype(bf16)
    pr, pi = power(lr, li, L)
    apow_ref[...] = jnp.concatenate([pr, pi], axis=1)


def _s5_prep(a_re, a_im, log_dt, b_re, b_im, c_re, c_im):
    G, N, P, L = SSM_GROUPS, SSM_STATE, SSM_CH, S5_CHUNK
    W, M = SSM_WIDTH, SSM_MODES
    eye = jnp.eye(G, dtype=f32)
    bx = lambda b: (jnp.swapaxes(b, 1, 2)[:, :, None, :] * eye[:, None, :, None]).reshape(W, M)
    cx = lambda c: (jnp.swapaxes(c, 1, 2)[:, :, None, :] * eye[:, None, :, None]).reshape(M, W)
    ldt = jnp.repeat(log_dt, N)
    ins = [a_re.reshape(1, M), a_im.reshape(1, M), ldt.reshape(1, M),
           a_re.reshape(M, 1), a_im.reshape(M, 1), ldt.reshape(M, 1),
           bx(b_re), bx(b_im), cx(c_re), cx(c_im)]
    return pl.pallas_call(
        _s5_prep_kernel,
        out_shape=[jax.ShapeDtypeStruct((L * W, 2 * M), bf16), jax.ShapeDtypeStruct((2 * M, L * W), bf16),
                   jax.ShapeDtypeStruct((L * W, W), bf16), jax.ShapeDtypeStruct((1, 2 * M), f32)],
        compiler_params=pltpu.CompilerParams(vmem_limit_bytes=VMEM_LIMIT),
        name="s5_prep",
    )(*ins)


def _s5_kernel(u_ref, gs_ref, h0_ref, bst_ref, cst_ref, kd_ref, apow_ref, d_ref, wg_ref, bg_ref,
               o_ref, hT_ref, ds_ref, sp_ref, carry_ref, *, nb, rows):
    L, W, M = S5_CHUNK, SSM_WIDTH, SSM_MODES
    ti = pl.program_id(1)

    @pl.when(ti == 0)
    def _():
        for b in range(nb):
            carry_ref[b:b + 1, :] = h0_ref[b]

    ub = u_ref[...].astype(bf16)
    ds_ref[...] = jnp.dot(ub, bst_ref[...], preferred_element_type=f32)
    ar, ai = apow_ref[:, :M], apow_ref[:, M:]
    for b in range(nb):
        def step(c, carry):
            sr, si = carry
            r = b * rows + c
            sp_ref[pl.ds(r, 1), :] = jnp.concatenate([sr, si], axis=1)
            d = ds_ref[pl.ds(r, 1), :]
            return ar * sr - ai * si + d[:, :M], ar * si + ai * sr + d[:, M:]

        s0 = carry_ref[b:b + 1, :]
        sr, si = lax.fori_loop(0, rows, step, (s0[:, :M], s0[:, M:]))
        carry_ref[b:b + 1, :] = jnp.concatenate([sr, si], axis=1)
        hT_ref[b] = jnp.concatenate([sr, si], axis=1)

    spb = sp_ref[...].astype(bf16)
    for tl in range(L):
        cols = slice(tl * W, (tl + 1) * W)
        y = jnp.dot(ub[:, :(tl + 1) * W], kd_ref[(L - 1 - tl) * W:, :], preferred_element_type=f32)
        y = y + jnp.dot(spb, cst_ref[:, cols], preferred_element_type=f32)
        y = y + d_ref[...] * u_ref[:, cols]
        g = _gelu_tanh(y)
        gate = jax.nn.sigmoid(jnp.dot(g.astype(bf16), wg_ref[...], preferred_element_type=f32) + bg_ref[...])
        o_ref[:, cols] = (g * gate * gs_ref[:, cols]).astype(bf16)


def _s5(u, gs, h0, ops, d_row, w_glu, b_glu, *, batch, rows_per_step):
    L, W, M = S5_CHUNK, SSM_WIDTH, SSM_MODES
    n = u.shape[0]
    t = n // batch
    assert t % L == 0
    chunks = t // L
    bst, cst, kd, apow = ops
    u2 = u.reshape(n // L, L * W)
    gs2 = gs.reshape(n // L, L * W)
    if rows_per_step is None:
        nb, rows, grid = batch, chunks, (1, 1)
    else:
        assert chunks % rows_per_step == 0
        nb, rows, grid = 1, rows_per_step, (batch, chunks // rows_per_step)
    tiles = grid[1]
    blk = pl.BlockSpec((nb * rows, L * W), lambda bi, ti: (bi * tiles + ti, 0))
    full = lambda a: pl.BlockSpec(a.shape, lambda bi, ti: (0,) * a.ndim)
    state = pl.BlockSpec((nb, 1, 2 * M), lambda bi, ti: (bi, 0, 0))
    out, hT = pl.pallas_call(
        functools.partial(_s5_kernel, nb=nb, rows=rows),
        grid=grid,
        in_specs=[blk, blk, state, full(bst), full(cst), full(kd), full(apow), full(d_row),
                  full(w_glu), full(b_glu)],
        out_specs=[blk, state],
        out_shape=[jax.ShapeDtypeStruct((n // L, L * W), bf16), jax.ShapeDtypeStruct((batch, 1, 2 * M), f32)],
        scratch_shapes=[pltpu.VMEM((nb * rows, 2 * M), f32), pltpu.VMEM((nb * rows, 2 * M), f32),
                        pltpu.VMEM((max(nb, 8), 2 * M), f32)],
        compiler_params=_cparams(("arbitrary", "arbitrary")),
        name="s5_prompt" if rows_per_step is not None else "s5_sample",
    )(u2, gs2, h0, bst, cst, kd, apow, d_row, w_glu, b_glu)
    return out.reshape(n, W), hT


def _memory_kv_kernel(mem_ref, g_ref, w_ref, kg_ref, hm_ref, mk_ref, mv_ref):
    x = mem_ref[...]
    ms = jnp.mean(x * x, axis=-1, keepdims=True)
    h = (x * lax.rsqrt(ms + EPS) * g_ref[...]).astype(bf16)
    mk = jnp.dot(h, w_ref[:, 0:MEM_WIDTH], preferred_element_type=f32)
    mk_ref[...] = _head_norm(mk, hm_ref.at[0:MEM_WIDTH, 0:MEM_WIDTH], kg_ref)
    mv_ref[...] = jnp.dot(h, w_ref[:, MEM_WIDTH:2 * MEM_WIDTH], preferred_element_type=f32)


def _memory_kv(mem, lw):
    n = mem.shape[0]
    return pl.pallas_call(
        _memory_kv_kernel,
        out_shape=[jax.ShapeDtypeStruct((n, MEM_WIDTH), f32)] * 2,
        compiler_params=pltpu.CompilerParams(vmem_limit_bytes=VMEM_LIMIT),
        name="memory_kv",
    )(mem, lw["mem_norm"], lw["w_mem_kv"], lw["mem_k_norm"], lw["head_mean"])


def _mix_out_kernel(x_ref, fox_ref, gf_ref, ssm_ref, mq_ref, gm_ref, mk_ref, mv_ref, w_ref, o_ref):
    tm = x_ref.shape[0]
    nmem = mk_ref.shape[1]
    mq = mq_ref[...]
    mk = mk_ref[0]
    mv = mv_ref[0]
    qlane = _lane_iota((tm, MEM_WIDTH)) // HEAD_DIM
    vlane = _lane_iota((nmem, MEM_WIDTH)) // HEAD_DIM
    mem = jnp.zeros((tm, MEM_WIDTH), f32)
    for hd in range(N_MEM_HEADS):
        qh = jnp.where(qlane == hd, mq, jnp.zeros_like(mq))
        s = lax.dot_general(qh, mk, (((1,), (1,)), ((), ())), preferred_element_type=f32)
        p = jnp.exp(s - jnp.max(s, axis=-1, keepdims=True))
        p = p / jnp.sum(p, axis=-1, keepdims=True)
        vh = jnp.where(vlane == hd, mv, jnp.zeros_like(mv))
        mem = mem + jnp.dot(p.astype(bf16), vh, preferred_element_type=f32)
    fox = (fox_ref[...] * gf_ref[...]).astype(bf16)
    memg = (mem * gm_ref[...]).astype(bf16)
    y = x_ref[...]
    y = y + jnp.dot(fox, w_ref[0:FOX_WIDTH, :], preferred_element_type=f32)
    y = y + jnp.dot(ssm_ref[...], w_ref[FOX_WIDTH:FOX_WIDTH + SSM_WIDTH, :], preferred_element_type=f32)
    y = y + jnp.dot(memg, w_ref[FOX_WIDTH + SSM_WIDTH:, :], preferred_element_type=f32)
    o_ref[...] = y


def _mix_out(x, fox, gf, ssm, mq, gm, mk, mv, w_out, *, batch, tm):
    n, d = x.shape
    t = n // batch
    assert t % tm == 0
    tiles_per_batch = t // tm
    row = lambda w: pl.BlockSpec((tm, w), lambda i: (i, 0))
    memb = pl.BlockSpec((1,) + mk.shape[1:], lambda i: (i // tiles_per_batch, 0, 0))
    return pl.pallas_call(
        _mix_out_kernel,
        grid=(n // tm,),
        in_specs=[row(d), row(FOX_WIDTH), row(FOX_WIDTH), row(SSM_WIDTH), row(MEM_WIDTH), row(MEM_WIDTH),
                  memb, memb, pl.BlockSpec(w_out.shape, lambda i: (0, 0))],
        out_specs=row(d),
        out_shape=jax.ShapeDtypeStruct((n, d), f32),
        compiler_params=_cparams(("arbitrary",)),
        name="mix_out_prompt" if tiles_per_batch > 1 else "mix_out_sample",
    )(x, fox, gf, ssm, mq, gm, mk, mv, w_out)


def _layer_weights(l, norm_g, w_in, b_forget, fox_q_norm, fox_k_norm, mem_q_norm, mem_norm, w_mem_kv,
                   mem_k_norm, w_out, w_glu, b_glu, ssm_d):
    w = w_in[l]
    nf = 4 * FOX_WIDTH
    d = w.shape[0]
    w_packed = jnp.concatenate(
        [w[:, :nf], w[:, nf + N_FOX_HEADS:], w[:, nf:nf + N_FOX_HEADS],
         jnp.zeros((d, _W_COLS - w.shape[1]), w.dtype)], axis=1).astype(bf16)
    eq, ek = _bias_selectors()
    return dict(
        norm_g=norm_g[l].reshape(1, d), w_in=w_packed,
        b_forget=jnp.pad(b_forget[l], (0, LANES - N_FOX_HEADS)).reshape(1, LANES),
        fox_q_norm=jnp.tile(fox_q_norm[l], N_FOX_HEADS).reshape(1, FOX_WIDTH),
        fox_k_norm=jnp.tile(fox_k_norm[l], N_FOX_HEADS).reshape(1, FOX_WIDTH),
        mem_q_norm=jnp.tile(mem_q_norm[l], N_MEM_HEADS).reshape(1, MEM_WIDTH),
        mem_k_norm=jnp.tile(mem_k_norm[l], N_MEM_HEADS).reshape(1, MEM_WIDTH),
        mem_norm=mem_norm[l].reshape(1, d), w_mem_kv=w_mem_kv[l].astype(bf16),
        head_mean=_head_mean_matrix(), bias_eq=eq, bias_ek=ek,
        w_out=w_out[l].astype(bf16), w_glu=w_glu[l].astype(bf16), b_glu=b_glu[l].reshape(1, SSM_WIDTH),
        ssm_d=ssm_d[l].reshape(1, SSM_WIDTH))


def _split_state(hT, batch):
    hT = hT.reshape(batch, 2, SSM_GROUPS, SSM_STATE)
    return hT[:, 0], hT[:, 1]


def kernel(x_prompt, x_sample, mem_prompt, cache_fox_k, cache_fox_v, cache_fox_logf, state_ssm_re, state_ssm_im, cache_mem_k, cache_mem_v, norm_g, w_in, b_forget, fox_q_norm, fox_k_norm, ssm_a_re, ssm_a_im, ssm_log_dt, ssm_b_re, ssm_b_im, ssm_c_re, ssm_c_im, ssm_d, w_glu, b_glu, mem_norm, w_mem_kv, mem_q_norm, mem_k_norm, w_out):
    B, T, D = x_prompt.shape
    Bs, Ts, _ = x_sample.shape
    depth = w_in.shape[0]
    past = cache_fox_k.shape[2]
    nmem = mem_prompt.shape[1]
    nkeys = -(-(past + Ts) // LANES) * LANES
    M = SSM_MODES

    xp = x_prompt.reshape(B * T, D)
    xs = x_sample.reshape(Bs * Ts, D)
    mem = mem_prompt.reshape(B * nmem, D)
    outs = {k: [] for k in ("pk", "pv", "pf", "pre", "pim", "pmk", "pmv", "sk", "sv", "sf", "sre", "sim")}
    for l in range(depth):
        lw = _layer_weights(l, norm_g, w_in, b_forget, fox_q_norm, fox_k_norm, mem_q_norm, mem_norm,
                            w_mem_kv, mem_k_norm, w_out, w_glu, b_glu, ssm_d)
        ops = _s5_prep(ssm_a_re[l], ssm_a_im[l], ssm_log_dt[l], ssm_b_re[l], ssm_b_im[l],
                       ssm_c_re[l], ssm_c_im[l])
        mk, mv = _memory_kv(mem, lw)

        pr = _in_proj(xp, lw, batch=B, prompt=True)
        fox = _fox_prompt(pr["qa"], pr["ka"], pr["va"]).reshape(B * T, FOX_WIDTH)
        ssm, hT = _s5(pr["u"], pr["gs"], jnp.zeros((B, 1, 2 * M), f32), ops, lw["ssm_d"], lw["w_glu"],
                      lw["b_glu"], batch=B, rows_per_step=128)
        xp = _mix_out(xp, fox, pr["gf"], ssm, pr["mq"], pr["gm"],
                      mk.reshape(B, nmem, MEM_WIDTH).astype(bf16), mv.reshape(B, nmem, MEM_WIDTH).astype(bf16),
                      lw["w_out"], batch=B, tm=ROW_TILE)
        hre, him = _split_state(hT, B)
        outs["pk"].append(pr["k"].reshape(B, T, N_FOX_HEADS, HEAD_DIM))
        outs["pv"].append(pr["v"].reshape(B, T, N_FOX_HEADS, HEAD_DIM))
        outs["pf"].append(pr["logf"].reshape(B, T, N_FOX_HEADS))
        outs["pre"].append(hre)
        outs["pim"].append(him)
        outs["pmk"].append(mk.reshape(B, nmem, N_MEM_HEADS, HEAD_DIM))
        outs["pmv"].append(mv.reshape(B, nmem, N_MEM_HEADS, HEAD_DIM))

        sr = _in_proj(xs, lw, batch=Bs, prompt=False)
        l_all = jnp.concatenate(
            [cache_fox_logf[l].astype(f32), sr["logf"].reshape(Bs, Ts, N_FOX_HEADS),
             jnp.zeros((Bs, nkeys - past - Ts, N_FOX_HEADS), f32)], axis=1)
        l_all = jnp.pad(l_all, ((0, 0), (0, 0), (0, LANES - N_FOX_HEADS)))
        fox_s = _fox_sample(sr["q"], sr["k"], sr["v"], cache_fox_k[l].reshape(Bs, past, FOX_WIDTH),
                            cache_fox_v[l].reshape(Bs, past, FOX_WIDTH), l_all, batch=Bs)
        h0 = jnp.concatenate([state_ssm_re[l].reshape(Bs, 1, M), state_ssm_im[l].reshape(Bs, 1, M)],
                             axis=-1).astype(f32)
        ssm_s, hT_s = _s5(sr["u"], sr["gs"], h0, ops, lw["ssm_d"], lw["w_glu"], lw["b_glu"], batch=Bs,
                          rows_per_step=None)
        xs = _mix_out(xs, fox_s, sr["gf"], ssm_s, sr["mq"], sr["gm"],
                      cache_mem_k[l].reshape(Bs, nmem, MEM_WIDTH).astype(bf16),
                      cache_mem_v[l].reshape(Bs, nmem, MEM_WIDTH).astype(bf16),
                      lw["w_out"], batch=Bs, tm=Ts)
        sre, sim = _split_state(hT_s, Bs)
        outs["sk"].append(sr["k"].reshape(Bs, Ts, N_FOX_HEADS, HEAD_DIM))
        outs["sv"].append(sr["v"].reshape(Bs, Ts, N_FOX_HEADS, HEAD_DIM))
        outs["sf"].append(sr["logf"].reshape(Bs, Ts, N_FOX_HEADS))
        outs["sre"].append(sre)
        outs["sim"].append(sim)

    st = {k: jnp.stack(v) for k, v in outs.items()}
    return (xp.reshape(B, T, D), xs.reshape(Bs, Ts, D), st["pk"], st["pv"], st["pf"], st["pre"], st["pim"],
            st["pmk"], st["pmv"], st["sk"], st["sv"], st["sf"], st["sre"], st["sim"])
```

```python
import functools
import math

import numpy as np
import jax
import jax.numpy as jnp
from jax import lax
from jax.experimental import pallas as pl
from jax.experimental.pallas import tpu as pltpu

f32 = jnp.float32
bf16 = jnp.bfloat16

HEAD_DIM = 64
N_FOX_HEADS = 8
FOX_WIDTH = N_FOX_HEADS * HEAD_DIM
SSM_GROUPS = 16
SSM_CH = 16
SSM_STATE = 64
SSM_WIDTH = SSM_GROUPS * SSM_CH
SSM_MODES = SSM_GROUPS * SSM_STATE
N_MEM_HEADS = 4
MEM_WIDTH = N_MEM_HEADS * HEAD_DIM
EPS = 1e-6
QK_SCALE = HEAD_DIM ** -0.5

LANES = 128
S5_CHUNK = 8
ROW_TILE = 512
ATTN_TQ = 512
ATTN_TK = 512
VMEM_LIMIT = 56 * 1024 * 1024

_C_Q, _C_K, _C_V, _C_G = 0, 512, 1024, 1536
_C_SU, _C_SG, _C_MQ, _C_MG, _C_F = 2048, 2304, 2560, 2816, 3072
_W_COLS = 3200
_N_PIECES = 3


def _cparams(sem):
    return pltpu.CompilerParams(dimension_semantics=sem, vmem_limit_bytes=VMEM_LIMIT)


def _silu(x):
    return x * jax.nn.sigmoid(x)


def _log_sigmoid(x):
    return -(jnp.maximum(-x, 0.0) + jnp.log1p(jnp.exp(-jnp.abs(x))))


def _gelu_tanh(x):
    return 0.5 * x * (1.0 + jnp.tanh(math.sqrt(2.0 / math.pi) * (x + 0.044715 * (x * x * x))))


def _split3(x):
    hi = x.astype(bf16).astype(f32)
    r = x - hi
    mid = r.astype(bf16).astype(f32)
    lo = (r - mid).astype(bf16).astype(f32)
    return hi, mid, lo


def _pack3(x):
    hi, mid, lo = _split3(x)
    return (hi + pltpu.roll(mid, 8, 1) + pltpu.roll(lo, 16, 1)).astype(bf16)


def _unpack3(y):
    return y + pltpu.roll(y, LANES - 8, 1) + pltpu.roll(y, LANES - 16, 1)


def _lane_iota(shape):
    return lax.broadcasted_iota(jnp.int32, shape, len(shape) - 1)


def _row_iota(shape):
    return lax.broadcasted_iota(jnp.int32, shape, len(shape) - 2)


def _head_norm(z, hm_ref, g_ref):
    ms = jnp.dot((z * z).astype(bf16), hm_ref[...], preferred_element_type=f32)
    return z * lax.rsqrt(ms + EPS) * g_ref[...]


def _in_proj_kernel(*refs, prompt, tiles_per_batch):
    if prompt:
        (x_ref, ng_ref, w_ref, bf_ref, qg_ref, kg_ref, mqg_ref, hm_ref, tril_ref, eq_ref, ek_ref,
         kt_ref, vt_ref, logft_ref, gf_ref, u_ref, gs_ref, mq_ref, gm_ref, qa_ref, ka_ref, va_ref,
         carry_ref) = refs
    else:
        (x_ref, ng_ref, w_ref, bf_ref, qg_ref, kg_ref, mqg_ref, hm_ref,
         k_ref, v_ref, logf_ref, gf_ref, u_ref, gs_ref, mq_ref, gm_ref, q_ref) = refs

    x = x_ref[...]
    tm = x.shape[0]
    lane = _lane_iota((tm, LANES))
    ms = jnp.mean(x * x, axis=-1, keepdims=True)
    h = (x * lax.rsqrt(ms + EPS) * ng_ref[...]).astype(bf16)

    def seg(lo, width):
        return jnp.dot(h, w_ref[:, lo:lo + width], preferred_element_type=f32)

    def halves(ref, z):
        ref[0] = z[:, :LANES]
        ref[1] = z[:, LANES:]

    q = _head_norm(seg(_C_Q, FOX_WIDTH), hm_ref, qg_ref) * QK_SCALE
    k = _head_norm(seg(_C_K, FOX_WIDTH), hm_ref, kg_ref)
    v = seg(_C_V, FOX_WIDTH)
    gf_ref[...] = _silu(seg(_C_G, FOX_WIDTH))
    halves(u_ref, seg(_C_SU, SSM_WIDTH))
    halves(gs_ref, _silu(seg(_C_SG, SSM_WIDTH)))
    mq = _head_norm(seg(_C_MQ, MEM_WIDTH), hm_ref.at[0:MEM_WIDTH, 0:MEM_WIDTH], mqg_ref) * QK_SCALE
    mq_ref[...] = mq.astype(bf16)
    gm_ref[...] = _silu(seg(_C_MG, MEM_WIDTH))

    logf = _log_sigmoid(seg(_C_F, LANES) + bf_ref[...])
    lf = jnp.where(lane < N_FOX_HEADS, logf, 0.0)

    if not prompt:
        k_ref[...] = k
        v_ref[...] = v
        logf_ref[...] = lf
        q_ref[...] = q.astype(bf16)
        return

    kt = k.T
    vt = v.T
    for hd in range(N_FOX_HEADS):
        kt_ref[0, hd] = kt[hd * HEAD_DIM:(hd + 1) * HEAD_DIM, :]
        vt_ref[0, hd] = vt[hd * HEAD_DIM:(hd + 1) * HEAD_DIM, :]
    logft_ref[0] = lf.T[0:N_FOX_HEADS, :]

    i = pl.program_id(0)

    @pl.when(i % tiles_per_batch == 0)
    def _():
        carry_ref[...] = jnp.zeros_like(carry_ref)

    c = _unpack3(jnp.dot(tril_ref[...], _pack3(lf), preferred_element_type=f32))
    c = jnp.where(lane < N_FOX_HEADS, c + carry_ref[0:1, :], 0.0)
    carry_ref[0:1, :] = c[tm - 1:tm, :]

    pieces = (_pack3(c).astype(f32) + jnp.where(lane == 3 * N_FOX_HEADS, 1.0, 0.0)).astype(bf16)
    qaug = jnp.dot(pieces, eq_ref[...], preferred_element_type=f32)
    kaug = jnp.dot(pieces, ek_ref[...], preferred_element_type=f32)
    for hd in range(N_FOX_HEADS):
        pair = (hd // 2) * LANES
        own = (lane < HEAD_DIM) if hd % 2 == 0 else (lane >= HEAD_DIM)
        ones_lane = HEAD_DIM if hd % 2 == 0 else 0
        blk = slice(hd * LANES, (hd + 1) * LANES)
        qa_ref[0, hd] = jnp.where(own, q[:, pair:pair + LANES], qaug[:, blk]).astype(bf16)
        ka_ref[0, hd] = jnp.where(own, k[:, pair:pair + LANES], kaug[:, blk]).astype(bf16)
        va_ref[0, hd] = jnp.where(own, v[:, pair:pair + LANES],
                                  jnp.where(lane == ones_lane, 1.0, 0.0)).astype(bf16)


def _bias_selectors():
    eq = np.zeros((LANES, N_FOX_HEADS * LANES), np.float32)
    ek = np.zeros((LANES, N_FOX_HEADS * LANES), np.float32)
    one = 3 * N_FOX_HEADS
    for hd in range(N_FOX_HEADS):
        base = hd * LANES + (HEAD_DIM if hd % 2 == 0 else 0)
        for p in range(_N_PIECES):
            eq[p * N_FOX_HEADS + hd, base + p] = 1.0
            eq[one, base + _N_PIECES + p] = 1.0
            ek[one, base + p] = 1.0
            ek[p * N_FOX_HEADS + hd, base + _N_PIECES + p] = -1.0
    return jnp.asarray(eq, bf16), jnp.asarray(ek, bf16)


def _head_mean_matrix():
    idx = np.arange(FOX_WIDTH) // HEAD_DIM
    return jnp.asarray((idx[:, None] == idx[None, :]).astype(np.float32) / HEAD_DIM, bf16)


def _in_proj(x, lw, *, batch, prompt):
    n, d = x.shape
    t = n // batch
    tm = ROW_TILE if prompt else n
    assert n % tm == 0 and (t % tm == 0 or not prompt)
    tiles_per_batch = t // tm if prompt else 1
    grid = (n // tm,)
    row = lambda w: pl.BlockSpec((tm, w), lambda i: (i, 0))
    full = lambda a: pl.BlockSpec(a.shape, lambda i: (0,) * a.ndim)
    split = pl.BlockSpec((2, tm, LANES), lambda i: (0, i, 0))
    by_batch = lambda i: (i // tiles_per_batch, 0, i % tiles_per_batch, 0)

    ins = [x, lw["norm_g"], lw["w_in"], lw["b_forget"], lw["fox_q_norm"], lw["fox_k_norm"],
           lw["mem_q_norm"], lw["head_mean"]]
    in_specs = [row(d)] + [full(a) for a in ins[1:]]
    common = [((n, FOX_WIDTH), f32, row(FOX_WIDTH)), ((2, n, LANES), f32, split), ((2, n, LANES), f32, split),
              ((n, MEM_WIDTH), bf16, row(MEM_WIDTH)), ((n, MEM_WIDTH), f32, row(MEM_WIDTH))]
    scratch = []
    if prompt:
        tril = jnp.asarray(np.tril(np.ones((tm, tm), np.float32)), bf16)
        extra = [tril, lw["bias_eq"], lw["bias_ek"]]
        ins += extra
        in_specs += [full(a) for a in extra]
        tspec = pl.BlockSpec((1, N_FOX_HEADS, HEAD_DIM, tm),
                             lambda i: (i // tiles_per_batch, 0, 0, i % tiles_per_batch))
        lspec = pl.BlockSpec((1, N_FOX_HEADS, tm), lambda i: (i // tiles_per_batch, 0, i % tiles_per_batch))
        hspec = pl.BlockSpec((1, N_FOX_HEADS, tm, LANES), by_batch)
        tshape = (batch, N_FOX_HEADS, HEAD_DIM, t)
        outs = ([(tshape, f32, tspec), (tshape, f32, tspec), ((batch, N_FOX_HEADS, t), f32, lspec)] + common
                + [((batch, N_FOX_HEADS, t, LANES), bf16, hspec)] * 3)
        names = ["kt", "vt", "logft", "gf", "u", "gs", "mq", "gm", "qa", "ka", "va"]
        scratch = [pltpu.VMEM((8, LANES), f32)]
    else:
        outs = ([((n, FOX_WIDTH), f32, row(FOX_WIDTH)), ((n, FOX_WIDTH), f32, row(FOX_WIDTH)),
                 ((n, LANES), f32, row(LANES))] + common + [((n, FOX_WIDTH), bf16, row(FOX_WIDTH))])
        names = ["k", "v", "logf", "gf", "u", "gs", "mq", "gm", "q"]

    res = pl.pallas_call(
        functools.partial(_in_proj_kernel, prompt=prompt, tiles_per_batch=tiles_per_batch),
        grid=grid, in_specs=in_specs, out_specs=[o[2] for o in outs],
        out_shape=[jax.ShapeDtypeStruct(o[0], o[1]) for o in outs],
        scratch_shapes=scratch,
        compiler_params=_cparams(("arbitrary",)),
        name="in_proj_prompt" if prompt else "in_proj_sample",
    )(*ins)
    return dict(zip(names, res))


def _fox_prompt_kernel(qa_ref, ka_ref, va_ref, o_ref, s0_ref, s1_ref, m_ref, acc_ref):
    tq, tk = ATTN_TQ, ATTN_TK
    qb = pl.program_id(2)
    s_refs = (s0_ref, s1_ref)
    m_ref[...] = jnp.full(m_ref.shape, -jnp.inf, f32)
    acc_ref[...] = jnp.zeros(acc_ref.shape, f32)

    def scores(kb, slot):
        off = pl.multiple_of(kb * tk, tk)
        for j in range(2):
            kk = ka_ref[0, j, pl.ds(off, tk), :]
            s_refs[slot][j] = lax.dot_general(qa_ref[0, j], kk, (((1,), (1,)), ((), ())),
                                              preferred_element_type=f32)

    def softmax_pv(kb, slot, masked):
        off = pl.multiple_of(kb * tk, tk)
        for j in range(2):
            s = s_refs[slot][j]
            if masked:
                s = jnp.where(_row_iota((tq, tk)) >= _lane_iota((tq, tk)), s, -jnp.inf)
            m = m_ref[j]
            m_new = jnp.maximum(m, jnp.max(s, axis=-1, keepdims=True))
            p = jnp.exp(s - jnp.concatenate([m_new] * (tk // LANES), axis=1))
            vv = va_ref[0, j, pl.ds(off, tk), :]
            acc_ref[j] = jnp.exp(m - m_new) * acc_ref[j] + jnp.dot(p.astype(bf16), vv, preferred_element_type=f32)
            m_ref[j] = m_new

    scores(0, 0)

    def body(i, carry):
        scores(2 * i + 1, 1)
        softmax_pv(2 * i, 0, False)
        scores(2 * i + 2, 0)
        softmax_pv(2 * i + 1, 1, False)
        return carry

    lax.fori_loop(0, qb // 2, body, 0)

    @pl.when(qb % 2 == 0)
    def _():
        softmax_pv(qb, 0, True)

    @pl.when(qb % 2 == 1)
    def _():
        scores(qb, 1)
        softmax_pv(qb - 1, 0, False)
        softmax_pv(qb, 1, True)
    h0 = acc_ref[0] / acc_ref[0, :, HEAD_DIM:HEAD_DIM + 1]
    h1 = acc_ref[1] / acc_ref[1, :, 0:1]
    o_ref[0] = jnp.where(_lane_iota((tq, LANES)) < HEAD_DIM, h0, h1)


def _fox_prompt(qa, ka, va):
    b, nh, t, _ = qa.shape
    assert ATTN_TQ == ATTN_TK and t % ATTN_TQ == 0
    return pl.pallas_call(
        _fox_prompt_kernel,
        grid=(b, nh // 2, t // ATTN_TQ),
        in_specs=[pl.BlockSpec((1, 2, ATTN_TQ, LANES), lambda bi, hp, qi: (bi, hp, qi, 0)),
                  pl.BlockSpec((1, 2, t, LANES), lambda bi, hp, qi: (bi, hp, 0, 0)),
                  pl.BlockSpec((1, 2, t, LANES), lambda bi, hp, qi: (bi, hp, 0, 0))],
        out_specs=pl.BlockSpec((1, ATTN_TQ, LANES), lambda bi, hp, qi: (bi, qi, hp)),
        out_shape=jax.ShapeDtypeStruct((b, t, FOX_WIDTH), f32),
        scratch_shapes=[pltpu.VMEM((2, ATTN_TQ, ATTN_TK), f32), pltpu.VMEM((2, ATTN_TQ, ATTN_TK), f32),
                        pltpu.VMEM((2, ATTN_TQ, LANES), f32), pltpu.VMEM((2, ATTN_TQ, LANES), f32)],
        compiler_params=_cparams(("arbitrary", "arbitrary", "arbitrary")),
        name="fox_prompt",
    )(qa, ka, va)


def _fox_sample_kernel(q_ref, kn_ref, vn_ref, ln_ref, kc_ref, vc_ref, lc_ref, triu_ref, o_ref,
                       kall_ref, vall_ref, *, past, tnew):
    nkeys = kall_ref.shape[1]
    nq = N_FOX_HEADS * tnew

    def new_cols(ref):
        z = jnp.concatenate([ref[...], jnp.zeros((LANES - tnew, ref.shape[1]), f32)], axis=0)
        return z.T

    kall_ref[:, 0:past] = kc_ref[0, 0].astype(bf16)
    vall_ref[:, 0:past] = vc_ref[0, 0].astype(bf16)
    kall_ref[:, past:nkeys] = new_cols(kn_ref).astype(bf16)
    vall_ref[:, past:nkeys] = new_cols(vn_ref).astype(bf16)

    l_all = jnp.concatenate([lc_ref[0, 0], new_cols(ln_ref)[0:N_FOX_HEADS, :]], axis=1)
    pieces = jnp.concatenate(_split3(l_all), axis=0).astype(bf16)
    c3 = jnp.dot(pieces, triu_ref[...], preferred_element_type=f32)
    c = c3[0:8] + c3[8:16] + c3[16:24]
    crow = jnp.concatenate([jnp.broadcast_to(c[hd:hd + 1, :], (tnew, nkeys)) for hd in range(N_FOX_HEADS)],
                           axis=0)
    rq = _row_iota((nq, LANES))
    cq = jnp.sum(jnp.where(_lane_iota((nq, LANES)) == rq % tnew, crow[:, past:nkeys], 0.0),
                 axis=1, keepdims=True)

    qrep = jnp.concatenate([q_ref[...]] * N_FOX_HEADS, axis=0)
    own = _row_iota((nq, FOX_WIDTH)) // tnew == _lane_iota((nq, FOX_WIDTH)) // HEAD_DIM
    qbd = jnp.where(own, qrep, jnp.zeros_like(qrep))
    s = jnp.dot(qbd, kall_ref[...], preferred_element_type=f32) + cq - crow
    s = jnp.where(_lane_iota((nq, nkeys)) <= past + _row_iota((nq, nkeys)) % tnew, s, -jnp.inf)
    p = jnp.exp(s - jnp.max(s, axis=1, keepdims=True))
    o2 = lax.dot_general(p.astype(bf16), vall_ref[...], (((1,), (1,)), ((), ())), preferred_element_type=f32)
    o2 = o2 / jnp.sum(p, axis=1, keepdims=True)
    olane = _lane_iota((tnew, FOX_WIDTH))
    out = jnp.zeros((tnew, FOX_WIDTH), f32)
    for hd in range(N_FOX_HEADS):
        out = out + jnp.where(olane // HEAD_DIM == hd, o2[hd * tnew:(hd + 1) * tnew, :], 0.0)
    o_ref[...] = out


def _fox_sample(q, k_new, v_new, logf_new, cache_kt, cache_vt, cache_lt, layer, *, batch):
    n = q.shape[0]
    tnew = n // batch
    past = cache_kt.shape[3]
    nkeys = past + LANES
    assert N_FOX_HEADS * tnew == LANES and past % LANES == 0
    triu = jnp.asarray(np.triu(np.ones((nkeys, nkeys), np.float32)), bf16)
    rows = lambda w: pl.BlockSpec((tnew, w), lambda bi: (bi, 0))
    cache = lambda h: pl.BlockSpec((1, 1, h, past), lambda bi: (layer, bi, 0, 0))
    return pl.pallas_call(
        functools.partial(_fox_sample_kernel, past=past, tnew=tnew),
        grid=(batch,),
        in_specs=[rows(FOX_WIDTH), rows(FOX_WIDTH), rows(FOX_WIDTH), rows(LANES),
                  cache(FOX_WIDTH), cache(FOX_WIDTH), cache(N_FOX_HEADS),
                  pl.BlockSpec((nkeys, nkeys), lambda bi: (0, 0))],
        out_specs=rows(FOX_WIDTH),
        out_shape=jax.ShapeDtypeStruct((n, FOX_WIDTH), f32),
        scratch_shapes=[pltpu.VMEM((FOX_WIDTH, nkeys), bf16), pltpu.VMEM((FOX_WIDTH, nkeys), bf16)],
        compiler_params=_cparams(("arbitrary",)),
        name="fox_sample",
    )(q, k_new, v_new, logf_new, cache_kt, cache_vt, cache_lt, triu)


def _s5_prep_kernel(are_r, aim_r, ldt_r, are_c, aim_c, ldt_c, bxr_ref, bxi_ref, cxr_ref, cxi_ref,
                    bst_ref, cst_ref, kd_ref, apow_ref):
    L, W, M = S5_CHUNK, SSM_WIDTH, SSM_MODES

    def lam(are, aim, ldt):
        dt = jnp.exp(ldt[...])
        return dt * are[...], dt * aim[...]

    def power(lr, li, k):
        mag = jnp.exp(float(k) * lr)
        return mag * jnp.cos(float(k) * li), mag * jnp.sin(float(k) * li)

    lr, li = lam(are_r, aim_r, ldt_r)
    lrc, lic = lam(are_c, aim_c, ldt_c)
    ar, ai = are_r[...], aim_r[...]
    abr, abi = power(lr, li, 1)
    den = ar * ar + ai * ai
    zr = ((abr - 1.0) * ar + abi * ai) / den
    zi = (abi * ar - (abr - 1.0) * ai) / den
    bbr = zr * bxr_ref[...] - zi * bxi_ref[...]
    bbi = zr * bxi_ref[...] + zi * bxr_ref[...]

    def b_pow(k):
        pr, pi = power(lr, li, k)
        return jnp.concatenate([pr * bbr - pi * bbi, pr * bbi + pi * bbr], axis=1)

    def c_pow(k):
        pr, pi = power(lrc, lic, k)
        cr, ci = cxr_ref[...], cxi_ref[...]
        return jnp.concatenate([cr * pr - ci * pi, -(cr * pi + ci * pr)], axis=0)

    b0 = b_pow(0)
    for tl in range(L):
        bst_ref[tl * W:(tl + 1) * W, :] = b_pow(L - 1 - tl).astype(bf16)
        cst_ref[:, tl * W:(tl + 1) * W] = c_pow(tl + 1).astype(bf16)
        kd = jnp.dot(b0, c_pow(L - 1 - tl), preferred_element_type=f32, precision=lax.Precision.HIGHEST)
        kd_ref[tl * W:(tl + 1) * W, :] = kd.astype(bf16)
    pr, pi = power(lr, li, L)
    apow_ref[...] = jnp.concatenate([pr, pi], axis=1)


def _s5_prep(a_re, a_im, log_dt, b_re, b_im, c_re, c_im):
    G, N, P, L = SSM_GROUPS, SSM_STATE, SSM_CH, S5_CHUNK
    W, M = SSM_WIDTH, SSM_MODES
    eye = jnp.eye(G, dtype=f32)
    bx = lambda b: (jnp.swapaxes(b, 1, 2)[:, :, None, :] * eye[:, None, :, None]).reshape(W, M)
    cx = lambda c: (jnp.swapaxes(c, 1, 2)[:, :, None, :] * eye[:, None, :, None]).reshape(M, W)
    ldt = jnp.repeat(log_dt, N)
    ins = [a_re.reshape(1, M), a_im.reshape(1, M), ldt.reshape(1, M),
           a_re.reshape(M, 1), a_im.reshape(M, 1), ldt.reshape(M, 1),
           bx(b_re), bx(b_im), cx(c_re), cx(c_im)]
    return pl.pallas_call(
        _s5_prep_kernel,
        out_shape=[jax.ShapeDtypeStruct((L * W, 2 * M), bf16), jax.ShapeDtypeStruct((2 * M, L * W), bf16),
                   jax.ShapeDtypeStruct((L * W, W), bf16), jax.ShapeDtypeStruct((1, 2 * M), f32)],
        compiler_params=pltpu.CompilerParams(vmem_limit_bytes=VMEM_LIMIT),
        name="s5_prep",
    )(*ins)


def _s5_kernel(u_ref, gs_ref, h0_ref, bst_ref, cst_ref, kd_ref, apow_ref, d_ref, wg_ref, bg_ref,
               o_ref, hT_ref, ds_ref, sp_ref, carry_ref, *, nb, rows):
    L, W, M = S5_CHUNK, SSM_WIDTH, SSM_MODES
    nr = nb * rows
    ti = pl.program_id(1)

    @pl.when(ti == 0)
    def _():
        for b in range(nb):
            carry_ref[b:b + 1, :] = h0_ref[b]

    def chunked(ref, tl):
        return jnp.concatenate([ref[0, pl.ds(tl, nr, stride=L), :], ref[1, pl.ds(tl, nr, stride=L), :]], axis=1)

    uf = [chunked(u_ref, tl) for tl in range(L)]
    ub = jnp.concatenate(uf, axis=1).astype(bf16)
    ds_ref[...] = jnp.dot(ub, bst_ref[...], preferred_element_type=f32)
    ar, ai = apow_ref[:, :M], apow_ref[:, M:]
    for b in range(nb):
        def step(c, carry):
            sr, si = carry
            r = b * rows + c
            sp_ref[pl.ds(r, 1), :] = jnp.concatenate([sr, si], axis=1)
            d = ds_ref[pl.ds(r, 1), :]
            return ar * sr - ai * si + d[:, :M], ar * si + ai * sr + d[:, M:]

        s0 = carry_ref[b:b + 1, :]
        sr, si = lax.fori_loop(0, rows, step, (s0[:, :M], s0[:, M:]))
        carry_ref[b:b + 1, :] = jnp.concatenate([sr, si], axis=1)
        hT_ref[b] = jnp.concatenate([sr, si], axis=1)

    spb = sp_ref[...].astype(bf16)
    for tl in range(L):
        cols = slice(tl * W, (tl + 1) * W)
        y = jnp.dot(ub[:, :(tl + 1) * W], kd_ref[(L - 1 - tl) * W:, :], preferred_element_type=f32)
        y = y + jnp.dot(spb, cst_ref[:, cols], preferred_element_type=f32)
        y = y + d_ref[...] * uf[tl]
        g = _gelu_tanh(y)
        gate = jax.nn.sigmoid(jnp.dot(g.astype(bf16), wg_ref[...], preferred_element_type=f32) + bg_ref[...])
        out = g * gate * chunked(gs_ref, tl)
        o_ref[0, pl.ds(tl, nr, stride=L), :] = out[:, :LANES]
        o_ref[1, pl.ds(tl, nr, stride=L), :] = out[:, LANES:]


def _s5(u, gs, h0, ops, d_row, w_glu, b_glu, *, batch, rows_per_step):
    L, W, M = S5_CHUNK, SSM_WIDTH, SSM_MODES
    n = u.shape[1]
    t = n // batch
    assert t % L == 0
    chunks = t // L
    bst, cst, kd, apow = ops
    if rows_per_step is None:
        nb, rows, grid = batch, chunks, (1, 1)
    else:
        assert chunks % rows_per_step == 0
        nb, rows, grid = 1, rows_per_step, (batch, chunks // rows_per_step)
    tiles = grid[1]
    blk = pl.BlockSpec((2, nb * rows * L, LANES), lambda bi, ti: (0, bi * tiles + ti, 0))
    full = lambda a: pl.BlockSpec(a.shape, lambda bi, ti: (0,) * a.ndim)
    state = pl.BlockSpec((nb, 1, 2 * M), lambda bi, ti: (bi, 0, 0))
    return pl.pallas_call(
        functools.partial(_s5_kernel, nb=nb, rows=rows),
        grid=grid,
        in_specs=[blk, blk, state, full(bst), full(cst), full(kd), full(apow), full(d_row),
                  full(w_glu), full(b_glu)],
        out_specs=[blk, state],
        out_shape=[jax.ShapeDtypeStruct((2, n, LANES), f32), jax.ShapeDtypeStruct((batch, 1, 2 * M), f32)],
        scratch_shapes=[pltpu.VMEM((nb * rows, 2 * M), f32), pltpu.VMEM((nb * rows, 2 * M), f32),
                        pltpu.VMEM((max(nb, 8), 2 * M), f32)],
        compiler_params=_cparams(("arbitrary", "arbitrary")),
        name="s5_prompt" if rows_per_step is not None else "s5_sample",
    )(u, gs, h0, bst, cst, kd, apow, d_row, w_glu, b_glu)


def _memory_kv_kernel(mem_ref, g_ref, w_ref, kg_ref, hm_ref, mkt_ref, mvt_ref):
    batch, _, nmem = mkt_ref.shape
    x = mem_ref[...]
    ms = jnp.mean(x * x, axis=-1, keepdims=True)
    h = (x * lax.rsqrt(ms + EPS) * g_ref[...]).astype(bf16)
    mk = jnp.dot(h, w_ref[:, 0:MEM_WIDTH], preferred_element_type=f32)
    mk = _head_norm(mk, hm_ref.at[0:MEM_WIDTH, 0:MEM_WIDTH], kg_ref)
    mv = jnp.dot(h, w_ref[:, MEM_WIDTH:2 * MEM_WIDTH], preferred_element_type=f32)
    for b in range(batch):
        mkt_ref[b] = mk[b * nmem:(b + 1) * nmem, :].T
        mvt_ref[b] = mv[b * nmem:(b + 1) * nmem, :].T


def _memory_kv(mem, lw, *, batch):
    n = mem.shape[0]
    return pl.pallas_call(
        _memory_kv_kernel,
        out_shape=[jax.ShapeDtypeStruct((batch, MEM_WIDTH, n // batch), f32)] * 2,
        compiler_params=pltpu.CompilerParams(vmem_limit_bytes=VMEM_LIMIT),
        name="memory_kv",
    )(mem, lw["mem_norm"], lw["w_mem_kv"], lw["mem_k_norm"], lw["head_mean"])


def _mix_out_kernel(x_ref, fox_ref, gf_ref, ssm_ref, mq_ref, gm_ref, mkt_ref, mvt_ref, w_ref, o_ref):
    tm = x_ref.shape[0]
    nmem = mkt_ref.shape[2]
    mq = mq_ref[...]
    mkt = mkt_ref[0]
    mvt = mvt_ref[0]
    qhead = _lane_iota((tm, MEM_WIDTH)) // HEAD_DIM
    vhead = _row_iota((MEM_WIDTH, nmem)) // HEAD_DIM
    mem = jnp.zeros((tm, MEM_WIDTH), f32)
    for hd in range(N_MEM_HEADS):
        qh = jnp.where(qhead == hd, mq, jnp.zeros_like(mq))
        s = jnp.dot(qh, mkt, preferred_element_type=f32)
        p = jnp.exp(s - jnp.max(s, axis=-1, keepdims=True))
        p = p / jnp.sum(p, axis=-1, keepdims=True)
        vh = jnp.where(vhead == hd, mvt, jnp.zeros_like(mvt))
        mem = mem + lax.dot_general(p.astype(bf16), vh, (((1,), (1,)), ((), ())), preferred_element_type=f32)
    fox = (fox_ref[...] * gf_ref[...]).astype(bf16)
    ssm = jnp.concatenate([ssm_ref[0], ssm_ref[1]], axis=1).astype(bf16)
    memg = (mem * gm_ref[...]).astype(bf16)
    y = x_ref[...]
    y = y + jnp.dot(fox, w_ref[0:FOX_WIDTH, :], preferred_element_type=f32)
    y = y + jnp.dot(ssm, w_ref[FOX_WIDTH:FOX_WIDTH + SSM_WIDTH, :], preferred_element_type=f32)
    y = y + jnp.dot(memg, w_ref[FOX_WIDTH + SSM_WIDTH:, :], preferred_element_type=f32)
    o_ref[...] = y


def _mix_out(x, fox, gf, ssm, mq, gm, mkt, mvt, w_out, *, batch, tm):
    n, d = x.shape
    t = n // batch
    assert t % tm == 0
    tiles_per_batch = t // tm
    row = lambda w: pl.BlockSpec((tm, w), lambda i: (i, 0))
    memb = pl.BlockSpec((1,) + mkt.shape[1:], lambda i: (i // tiles_per_batch, 0, 0))
    return pl.pallas_call(
        _mix_out_kernel,
        grid=(n // tm,),
        in_specs=[row(d), row(FOX_WIDTH), row(FOX_WIDTH), pl.BlockSpec((2, tm, LANES), lambda i: (0, i, 0)),
                  row(MEM_WIDTH), row(MEM_WIDTH), memb, memb, pl.BlockSpec(w_out.shape, lambda i: (0, 0))],
        out_specs=row(d),
        out_shape=jax.ShapeDtypeStruct((n, d), f32),
        compiler_params=_cparams(("arbitrary",)),
        name="mix_out_prompt" if tiles_per_batch > 1 else "mix_out_sample",
    )(x, fox, gf, ssm, mq, gm, mkt, mvt, w_out)


def _layer_weights(l, norm_g, w_in, b_forget, fox_q_norm, fox_k_norm, mem_q_norm, mem_norm, w_mem_kv,
                   mem_k_norm, w_out, w_glu, b_glu, ssm_d):
    w = w_in[l]
    nf = 4 * FOX_WIDTH
    d = w.shape[0]
    w_packed = jnp.concatenate(
        [w[:, :nf], w[:, nf + N_FOX_HEADS:], w[:, nf:nf + N_FOX_HEADS],
         jnp.zeros((d, _W_COLS - w.shape[1]), w.dtype)], axis=1).astype(bf16)
    eq, ek = _bias_selectors()
    return dict(
        norm_g=norm_g[l].reshape(1, d), w_in=w_packed,
        b_forget=jnp.pad(b_forget[l], (0, LANES - N_FOX_HEADS)).reshape(1, LANES),
        fox_q_norm=jnp.tile(fox_q_norm[l], N_FOX_HEADS).reshape(1, FOX_WIDTH),
        fox_k_norm=jnp.tile(fox_k_norm[l], N_FOX_HEADS).reshape(1, FOX_WIDTH),
        mem_q_norm=jnp.tile(mem_q_norm[l], N_MEM_HEADS).reshape(1, MEM_WIDTH),
        mem_k_norm=jnp.tile(mem_k_norm[l], N_MEM_HEADS).reshape(1, MEM_WIDTH),
        mem_norm=mem_norm[l].reshape(1, d), w_mem_kv=w_mem_kv[l].astype(bf16),
        head_mean=_head_mean_matrix(), bias_eq=eq, bias_ek=ek,
        w_out=w_out[l].astype(bf16), w_glu=w_glu[l].astype(bf16), b_glu=b_glu[l].reshape(1, SSM_WIDTH),
        ssm_d=ssm_d[l].reshape(1, SSM_WIDTH))


def _split_state(hT, batch):
    hT = hT.reshape(batch, 2, SSM_GROUPS, SSM_STATE)
    return hT[:, 0], hT[:, 1]


def kernel(x_prompt, x_sample, mem_prompt, cache_fox_k, cache_fox_v, cache_fox_logf, state_ssm_re, state_ssm_im, cache_mem_k, cache_mem_v, norm_g, w_in, b_forget, fox_q_norm, fox_k_norm, ssm_a_re, ssm_a_im, ssm_log_dt, ssm_b_re, ssm_b_im, ssm_c_re, ssm_c_im, ssm_d, w_glu, b_glu, mem_norm, w_mem_kv, mem_q_norm, mem_k_norm, w_out):
    B, T, D = x_prompt.shape
    Bs, Ts, _ = x_sample.shape
    depth = w_in.shape[0]
    past = cache_fox_k.shape[2]
    nmem = mem_prompt.shape[1]
    M = SSM_MODES

    xp = x_prompt.reshape(B * T, D)
    xs = x_sample.reshape(Bs * Ts, D)
    mem = mem_prompt.reshape(B * nmem, D)
    cache_kt = jnp.transpose(cache_fox_k, (0, 1, 3, 4, 2)).reshape(depth, Bs, FOX_WIDTH, past)
    cache_vt = jnp.transpose(cache_fox_v, (0, 1, 3, 4, 2)).reshape(depth, Bs, FOX_WIDTH, past)
    cache_lt = jnp.transpose(cache_fox_logf, (0, 1, 3, 2)).astype(f32)
    cache_mkt = jnp.transpose(cache_mem_k, (0, 1, 3, 4, 2)).reshape(depth, Bs, MEM_WIDTH, nmem).astype(bf16)
    cache_mvt = jnp.transpose(cache_mem_v, (0, 1, 3, 4, 2)).reshape(depth, Bs, MEM_WIDTH, nmem).astype(bf16)

    outs = {k: [] for k in ("pk", "pv", "pf", "pre", "pim", "pmk", "pmv", "sk", "sv", "sf", "sre", "sim")}
    for l in range(depth):
        lw = _layer_weights(l, norm_g, w_in, b_forget, fox_q_norm, fox_k_norm, mem_q_norm, mem_norm,
                            w_mem_kv, mem_k_norm, w_out, w_glu, b_glu, ssm_d)
        ops = _s5_prep(ssm_a_re[l], ssm_a_im[l], ssm_log_dt[l], ssm_b_re[l], ssm_b_im[l],
                       ssm_c_re[l], ssm_c_im[l])
        mkt, mvt = _memory_kv(mem, lw, batch=B)

        pr = _in_proj(xp, lw, batch=B, prompt=True)
        fox = _fox_prompt(pr["qa"], pr["ka"], pr["va"]).reshape(B * T, FOX_WIDTH)
        ssm, hT = _s5(pr["u"], pr["gs"], jnp.zeros((B, 1, 2 * M), f32), ops, lw["ssm_d"], lw["w_glu"],
                      lw["b_glu"], batch=B, rows_per_step=128)
        xp = _mix_out(xp, fox, pr["gf"], ssm, pr["mq"], pr["gm"], mkt.astype(bf16), mvt.astype(bf16),
                      lw["w_out"], batch=B, tm=ROW_TILE)
        hre, him = _split_state(hT, B)
        outs["pk"].append(pr["kt"])
        outs["pv"].append(pr["vt"])
        outs["pf"].append(pr["logft"])
        outs["pre"].append(hre)
        outs["pim"].append(him)
        outs["pmk"].append(mkt.reshape(B, N_MEM_HEADS, HEAD_DIM, nmem))
        outs["pmv"].append(mvt.reshape(B, N_MEM_HEADS, HEAD_DIM, nmem))

        sr = _in_proj(xs, lw, batch=Bs, prompt=False)
        fox_s = _fox_sample(sr["q"], sr["k"], sr["v"], sr["logf"], cache_kt, cache_vt, cache_lt, l, batch=Bs)
        h0 = jnp.concatenate([state_ssm_re[l].reshape(Bs, 1, M), state_ssm_im[l].reshape(Bs, 1, M)],
                             axis=-1).astype(f32)
        ssm_s, hT_s = _s5(sr["u"], sr["gs"], h0, ops, lw["ssm_d"], lw["w_glu"], lw["b_glu"], batch=Bs,
                          rows_per_step=None)
        xs = _mix_out(xs, fox_s, sr["gf"], ssm_s, sr["mq"], sr["gm"], cache_mkt[l], cache_mvt[l],
                      lw["w_out"], batch=Bs, tm=Ts)
        sre, sim = _split_state(hT_s, Bs)
        outs["sk"].append(sr["k"].reshape(Bs, Ts, N_FOX_HEADS, HEAD_DIM))
        outs["sv"].append(sr["v"].reshape(Bs, Ts, N_FOX_HEADS, HEAD_DIM))
        outs["sf"].append(sr["logf"][:, :N_FOX_HEADS].reshape(Bs, Ts, N_FOX_HEADS))
        outs["sre"].append(sre)
        outs["sim"].append(sim)

    st = {k: jnp.stack(v) for k, v in outs.items()}
    fox_k_prompt = jnp.transpose(st["pk"], (0, 1, 4, 2, 3))
    fox_v_prompt = jnp.transpose(st["pv"], (0, 1, 4, 2, 3))
    fox_logf_prompt = jnp.transpose(st["pf"], (0, 1, 3, 2))
    mem_k_prompt = jnp.transpose(st["pmk"], (0, 1, 4, 2, 3))
    mem_v_prompt = jnp.transpose(st["pmv"], (0, 1, 4, 2, 3))
    return (xp.reshape(B, T, D), xs.reshape(Bs, Ts, D), fox_k_prompt, fox_v_prompt, fox_logf_prompt,
            st["pre"], st["pim"], mem_k_prompt, mem_v_prompt, st["sk"], st["sv"], st["sf"], st["sre"], st["sim"])
```

```python
import functools
import math

import numpy as np
import jax
import jax.numpy as jnp
from jax import lax
from jax.experimental import pallas as pl
from jax.experimental.pallas import tpu as pltpu

f32 = jnp.float32
bf16 = jnp.bfloat16

HEAD_DIM = 64
N_FOX_HEADS = 8
FOX_WIDTH = N_FOX_HEADS * HEAD_DIM
SSM_GROUPS = 16
SSM_CH = 16
SSM_STATE = 64
SSM_WIDTH = SSM_GROUPS * SSM_CH
SSM_MODES = SSM_GROUPS * SSM_STATE
N_MEM_HEADS = 4
MEM_WIDTH = N_MEM_HEADS * HEAD_DIM
EPS = 1e-6
QK_SCALE = HEAD_DIM ** -0.5
LOG2E = math.log2(math.e)

LANES = 128
S5_CHUNK = 8
ROW_TILE = 512
ATTN_TQ = 512
VMEM_LIMIT = 56 * 1024 * 1024

_C_Q, _C_K, _C_V, _C_G = 0, 512, 1024, 1536
_C_SU, _C_SG, _C_MQ, _C_MG, _C_F = 2048, 2304, 2560, 2816, 3072
_W_COLS = 3200
_N_PIECES = 3


def _cparams(sem):
    return pltpu.CompilerParams(dimension_semantics=sem, vmem_limit_bytes=VMEM_LIMIT)


def _silu(x):
    return x * jax.nn.sigmoid(x)


def _log_sigmoid(x):
    return -(jnp.maximum(-x, 0.0) + jnp.log1p(jnp.exp(-jnp.abs(x))))


def _gelu_tanh(x):
    return 0.5 * x * (1.0 + jnp.tanh(math.sqrt(2.0 / math.pi) * (x + 0.044715 * (x * x * x))))


def _split3(x):
    hi = x.astype(bf16).astype(f32)
    r = x - hi
    mid = r.astype(bf16).astype(f32)
    lo = (r - mid).astype(bf16).astype(f32)
    return hi, mid, lo


def _pack3(x):
    hi, mid, lo = _split3(x)
    return (hi + pltpu.roll(mid, 8, 1) + pltpu.roll(lo, 16, 1)).astype(bf16)


def _unpack3(y):
    return y + pltpu.roll(y, LANES - 8, 1) + pltpu.roll(y, LANES - 16, 1)


def _lane_iota(shape):
    return lax.broadcasted_iota(jnp.int32, shape, len(shape) - 1)


def _row_iota(shape):
    return lax.broadcasted_iota(jnp.int32, shape, len(shape) - 2)


def _head_norm(z, hm_ref, g_ref):
    ms = jnp.dot((z * z).astype(bf16), hm_ref[...], preferred_element_type=f32)
    return z * lax.rsqrt(ms + EPS) * g_ref[...]


def _in_proj_kernel(*refs, prompt, tiles_per_batch):
    if prompt:
        (x_ref, ng_ref, w_ref, bf_ref, qg_ref, kg_ref, mqg_ref, hm_ref, tril_ref, eq_ref, ek_ref,
         kt_ref, vt_ref, logft_ref, gf_ref, u_ref, gs_ref, mq_ref, gm_ref, qa_ref, ka_ref, va_ref,
         carry_ref) = refs
    else:
        (x_ref, ng_ref, w_ref, bf_ref, qg_ref, kg_ref, mqg_ref, hm_ref,
         k_ref, v_ref, logf_ref, gf_ref, u_ref, gs_ref, mq_ref, gm_ref, q_ref) = refs

    x = x_ref[...]
    tm = x.shape[0]
    lane = _lane_iota((tm, LANES))
    ms = jnp.mean(x * x, axis=-1, keepdims=True)
    h = (x * lax.rsqrt(ms + EPS) * ng_ref[...]).astype(bf16)

    def seg(lo, width):
        return jnp.dot(h, w_ref[:, lo:lo + width], preferred_element_type=f32)

    def halves(ref, z):
        ref[0] = z[:, :LANES]
        ref[1] = z[:, LANES:]

    q = _head_norm(seg(_C_Q, FOX_WIDTH), hm_ref, qg_ref) * QK_SCALE
    k = _head_norm(seg(_C_K, FOX_WIDTH), hm_ref, kg_ref)
    v = seg(_C_V, FOX_WIDTH)
    gf_ref[...] = _silu(seg(_C_G, FOX_WIDTH))
    halves(u_ref, seg(_C_SU, SSM_WIDTH))
    halves(gs_ref, _silu(seg(_C_SG, SSM_WIDTH)))
    mq = _head_norm(seg(_C_MQ, MEM_WIDTH), hm_ref.at[0:MEM_WIDTH, 0:MEM_WIDTH], mqg_ref) * QK_SCALE
    mq_ref[...] = mq.astype(bf16)
    gm_ref[...] = _silu(seg(_C_MG, MEM_WIDTH))

    logf = _log_sigmoid(seg(_C_F, LANES) + bf_ref[...])
    lf = jnp.where(lane < N_FOX_HEADS, logf, 0.0)

    if not prompt:
        k_ref[...] = k
        v_ref[...] = v
        logf_ref[...] = lf
        q_ref[...] = q.astype(bf16)
        return

    kt = k.T
    vt = v.T
    for hd in range(N_FOX_HEADS):
        kt_ref[0, hd] = kt[hd * HEAD_DIM:(hd + 1) * HEAD_DIM, :]
        vt_ref[0, hd] = vt[hd * HEAD_DIM:(hd + 1) * HEAD_DIM, :]
    logft_ref[0] = lf.T[0:N_FOX_HEADS, :]

    i = pl.program_id(0)

    @pl.when(i % tiles_per_batch == 0)
    def _():
        carry_ref[...] = jnp.zeros_like(carry_ref)

    c = _unpack3(jnp.dot(tril_ref[...], _pack3(lf), preferred_element_type=f32))
    c = jnp.where(lane < N_FOX_HEADS, c + carry_ref[0:1, :], 0.0)
    carry_ref[0:1, :] = c[tm - 1:tm, :]

    q = q * LOG2E
    pieces = (_pack3(c * LOG2E).astype(f32) + jnp.where(lane == 3 * N_FOX_HEADS, 1.0, 0.0)).astype(bf16)
    qaug = jnp.dot(pieces, eq_ref[...], preferred_element_type=f32)
    kaug = jnp.dot(pieces, ek_ref[...], preferred_element_type=f32)
    for hd in range(N_FOX_HEADS):
        pair = (hd // 2) * LANES
        own = (lane < HEAD_DIM) if hd % 2 == 0 else (lane >= HEAD_DIM)
        ones_lane = HEAD_DIM if hd % 2 == 0 else 0
        blk = slice(hd * LANES, (hd + 1) * LANES)
        qa_ref[0, hd] = jnp.where(own, q[:, pair:pair + LANES], qaug[:, blk]).astype(bf16)
        ka_ref[0, hd] = jnp.where(own, k[:, pair:pair + LANES], kaug[:, blk]).astype(bf16)
        va_ref[0, hd] = jnp.where(own, v[:, pair:pair + LANES],
                                  jnp.where(lane == ones_lane, 1.0, 0.0)).astype(bf16)


def _bias_selectors():
    eq = np.zeros((LANES, N_FOX_HEADS * LANES), np.float32)
    ek = np.zeros((LANES, N_FOX_HEADS * LANES), np.float32)
    one = 3 * N_FOX_HEADS
    for hd in range(N_FOX_HEADS):
        base = hd * LANES + (HEAD_DIM if hd % 2 == 0 else 0)
        for p in range(_N_PIECES):
            eq[p * N_FOX_HEADS + hd, base + p] = 1.0
            eq[one, base + _N_PIECES + p] = 1.0
            ek[one, base + p] = 1.0
            ek[p * N_FOX_HEADS + hd, base + _N_PIECES + p] = -1.0
    return jnp.asarray(eq, bf16), jnp.asarray(ek, bf16)


def _head_mean_matrix():
    idx = np.arange(FOX_WIDTH) // HEAD_DIM
    return jnp.asarray((idx[:, None] == idx[None, :]).astype(np.float32) / HEAD_DIM, bf16)


def _in_proj(x, lw, *, batch, prompt):
    n, d = x.shape
    t = n // batch
    tm = ROW_TILE if prompt else n
    assert n % tm == 0 and (t % tm == 0 or not prompt)
    tiles_per_batch = t // tm if prompt else 1
    grid = (n // tm,)
    row = lambda w: pl.BlockSpec((tm, w), lambda i: (i, 0))
    full = lambda a: pl.BlockSpec(a.shape, lambda i: (0,) * a.ndim)
    split = pl.BlockSpec((2, tm, LANES), lambda i: (0, i, 0))
    by_batch = lambda i: (i // tiles_per_batch, 0, i % tiles_per_batch, 0)

    ins = [x, lw["norm_g"], lw["w_in"], lw["b_forget"], lw["fox_q_norm"], lw["fox_k_norm"],
           lw["mem_q_norm"], lw["head_mean"]]
    in_specs = [row(d)] + [full(a) for a in ins[1:]]
    common = [((n, FOX_WIDTH), f32, row(FOX_WIDTH)), ((2, n, LANES), f32, split), ((2, n, LANES), f32, split),
              ((n, MEM_WIDTH), bf16, row(MEM_WIDTH)), ((n, MEM_WIDTH), f32, row(MEM_WIDTH))]
    scratch = []
    if prompt:
        tril = jnp.asarray(np.tril(np.ones((tm, tm), np.float32)), bf16)
        extra = [tril, lw["bias_eq"], lw["bias_ek"]]
        ins += extra
        in_specs += [full(a) for a in extra]
        tspec = pl.BlockSpec((1, N_FOX_HEADS, HEAD_DIM, tm),
                             lambda i: (i // tiles_per_batch, 0, 0, i % tiles_per_batch))
        lspec = pl.BlockSpec((1, N_FOX_HEADS, tm), lambda i: (i // tiles_per_batch, 0, i % tiles_per_batch))
        hspec = pl.BlockSpec((1, N_FOX_HEADS, tm, LANES), by_batch)
        tshape = (batch, N_FOX_HEADS, HEAD_DIM, t)
        outs = ([(tshape, f32, tspec), (tshape, f32, tspec), ((batch, N_FOX_HEADS, t), f32, lspec)] + common
                + [((batch, N_FOX_HEADS, t, LANES), bf16, hspec)] * 3)
        names = ["kt", "vt", "logft", "gf", "u", "gs", "mq", "gm", "qa", "ka", "va"]
        scratch = [pltpu.VMEM((8, LANES), f32)]
    else:
        outs = ([((n, FOX_WIDTH), f32, row(FOX_WIDTH)), ((n, FOX_WIDTH), f32, row(FOX_WIDTH)),
                 ((n, LANES), f32, row(LANES))] + common + [((n, FOX_WIDTH), bf16, row(FOX_WIDTH))])
        names = ["k", "v", "logf", "gf", "u", "gs", "mq", "gm", "q"]

    res = pl.pallas_call(
        functools.partial(_in_proj_kernel, prompt=prompt, tiles_per_batch=tiles_per_batch),
        grid=grid, in_specs=in_specs, out_specs=[o[2] for o in outs],
        out_shape=[jax.ShapeDtypeStruct(o[0], o[1]) for o in outs],
        scratch_shapes=scratch,
        compiler_params=_cparams(("arbitrary",)),
        name="in_proj_prompt" if prompt else "in_proj_sample",
    )(*ins)
    return dict(zip(names, res))


def _fox_prompt_kernel(qa_ref, ka_ref, va_ref, o_ref, s0_ref, s1_ref, m_ref, acc_ref, *, nq):
    tq = tk = ATTN_TQ
    s_refs = (s0_ref, s1_ref)
    m_ref[...] = jnp.full(m_ref.shape, -jnp.inf, f32)
    acc_ref[...] = jnp.zeros(acc_ref.shape, f32)

    def scores(qb, kb, slot):
        qoff = pl.multiple_of(qb * tq, tq)
        koff = pl.multiple_of(kb * tk, tk)
        for j in range(2):
            s_refs[slot][j] = lax.dot_general(qa_ref[0, j, pl.ds(qoff, tq), :], ka_ref[0, j, pl.ds(koff, tk), :],
                                              (((1,), (1,)), ((), ())), preferred_element_type=f32)

    def softmax_pv(qb, kb, slot, masked):
        koff = pl.multiple_of(kb * tk, tk)
        for j in range(2):
            s = s_refs[slot][j]
            if masked:
                s = jnp.where(_row_iota((tq, tk)) >= _lane_iota((tq, tk)), s, -jnp.inf)
            m = m_ref[qb, j]
            m_new = jnp.maximum(m, jnp.max(s, axis=-1, keepdims=True))
            p = jnp.exp2(s - jnp.concatenate([m_new] * (tk // LANES), axis=1))
            vv = va_ref[0, j, pl.ds(koff, tk), :]
            acc_ref[qb, j] = (jnp.exp2(m - m_new) * acc_ref[qb, j]
                              + jnp.dot(p.astype(bf16), vv, preferred_element_type=f32))
            m_ref[qb, j] = m_new

    def finalize(qb):
        h0 = acc_ref[qb, 0] / acc_ref[qb, 0, :, HEAD_DIM:HEAD_DIM + 1]
        h1 = acc_ref[qb, 1] / acc_ref[qb, 1, :, 0:1]
        o_ref[0, pl.ds(pl.multiple_of(qb * tq, tq), tq), :] = jnp.where(
            _lane_iota((tq, LANES)) < HEAD_DIM, h0, h1)

    def after(qb, kb):
        row_end = kb + 1 >= qb
        last = jnp.logical_and(row_end, qb + 1 >= nq)
        nqb = jnp.where(last, 0, jnp.where(row_end, qb + 1, qb))
        nkb = jnp.where(row_end, 0, kb + 1)
        return nqb, nkb

    def lower_step(qb, kb, slot):
        nqb, nkb = after(qb, kb)
        scores(nqb, nkb, 1 - slot)
        softmax_pv(qb, kb, slot, False)
        return nqb, nkb

    def diag_step(qb, slot):
        nxt = jnp.minimum(qb + 1, nq - 1)
        scores(nxt, nxt, 1 - slot)
        softmax_pv(qb, qb, slot, True)
        finalize(qb)

    n_lower = nq * (nq - 1) // 2
    if n_lower:
        scores(1, 0, 0)

        def lower_body(i, carry):
            qb, kb = lower_step(*carry, 0)
            return lower_step(qb, kb, 1)

        lax.fori_loop(0, n_lower // 2, lower_body, (jnp.int32(1), jnp.int32(0)))
    else:
        scores(0, 0, 0)

    def diag_body(i, carry):
        diag_step(2 * i, 0)
        diag_step(2 * i + 1, 1)
        return carry

    lax.fori_loop(0, nq // 2, diag_body, 0)


def _fox_prompt(qa, ka, va):
    b, nh, t, _ = qa.shape
    nq = t // ATTN_TQ
    assert t % ATTN_TQ == 0 and nq % 2 == 0 and (nq * (nq - 1) // 2) % 2 == 0
    whole = pl.BlockSpec((1, 2, t, LANES), lambda bi, hp: (bi, hp, 0, 0))
    return pl.pallas_call(
        functools.partial(_fox_prompt_kernel, nq=nq),
        grid=(b, nh // 2),
        in_specs=[whole, whole, whole],
        out_specs=pl.BlockSpec((1, t, LANES), lambda bi, hp: (bi, 0, hp)),
        out_shape=jax.ShapeDtypeStruct((b, t, FOX_WIDTH), f32),
        scratch_shapes=[pltpu.VMEM((2, ATTN_TQ, ATTN_TQ), f32), pltpu.VMEM((2, ATTN_TQ, ATTN_TQ), f32),
                        pltpu.VMEM((nq, 2, ATTN_TQ, LANES), f32), pltpu.VMEM((nq, 2, ATTN_TQ, LANES), f32)],
        compiler_params=_cparams(("arbitrary", "arbitrary")),
        name="fox_prompt",
    )(qa, ka, va)


def _fox_sample_kernel(q_ref, kn_ref, vn_ref, ln_ref, kc_ref, vc_ref, lc_ref, triu_ref, o_ref,
                       kall_ref, vall_ref, *, past, tnew):
    nkeys = kall_ref.shape[1]
    nq = N_FOX_HEADS * tnew

    def new_cols(ref):
        z = jnp.concatenate([ref[...], jnp.zeros((LANES - tnew, ref.shape[1]), f32)], axis=0)
        return z.T

    kall_ref[:, 0:past] = kc_ref[0, 0].astype(bf16)
    vall_ref[:, 0:past] = vc_ref[0, 0].astype(bf16)
    kall_ref[:, past:nkeys] = new_cols(kn_ref).astype(bf16)
    vall_ref[:, past:nkeys] = new_cols(vn_ref).astype(bf16)

    l_all = jnp.concatenate([lc_ref[0, 0], new_cols(ln_ref)[0:N_FOX_HEADS, :]], axis=1)
    pieces = jnp.concatenate(_split3(l_all), axis=0).astype(bf16)
    c3 = jnp.dot(pieces, triu_ref[...], preferred_element_type=f32)
    c = c3[0:8] + c3[8:16] + c3[16:24]
    crow = jnp.concatenate([jnp.broadcast_to(c[hd:hd + 1, :], (tnew, nkeys)) for hd in range(N_FOX_HEADS)],
                           axis=0)
    rq = _row_iota((nq, LANES))
    cq = jnp.sum(jnp.where(_lane_iota((nq, LANES)) == rq % tnew, crow[:, past:nkeys], 0.0),
                 axis=1, keepdims=True)

    qrep = jnp.concatenate([q_ref[...]] * N_FOX_HEADS, axis=0)
    own = _row_iota((nq, FOX_WIDTH)) // tnew == _lane_iota((nq, FOX_WIDTH)) // HEAD_DIM
    qbd = jnp.where(own, qrep, jnp.zeros_like(qrep))
    s = jnp.dot(qbd, kall_ref[...], preferred_element_type=f32) + cq - crow
    s = jnp.where(_lane_iota((nq, nkeys)) <= past + _row_iota((nq, nkeys)) % tnew, s, -jnp.inf)
    p = jnp.exp(s - jnp.max(s, axis=1, keepdims=True))
    o2 = lax.dot_general(p.astype(bf16), vall_ref[...], (((1,), (1,)), ((), ())), preferred_element_type=f32)
    o2 = o2 / jnp.sum(p, axis=1, keepdims=True)
    olane = _lane_iota((tnew, FOX_WIDTH))
    out = jnp.zeros((tnew, FOX_WIDTH), f32)
    for hd in range(N_FOX_HEADS):
        out = out + jnp.where(olane // HEAD_DIM == hd, o2[hd * tnew:(hd + 1) * tnew, :], 0.0)
    o_ref[...] = out


def _fox_sample(q, k_new, v_new, logf_new, cache_kt, cache_vt, cache_lt, layer, *, batch):
    n = q.shape[0]
    tnew = n // batch
    past = cache_kt.shape[3]
    nkeys = past + LANES
    assert N_FOX_HEADS * tnew == LANES and past % LANES == 0
    triu = jnp.asarray(np.triu(np.ones((nkeys, nkeys), np.float32)), bf16)
    rows = lambda w: pl.BlockSpec((tnew, w), lambda bi: (bi, 0))
    cache = lambda h: pl.BlockSpec((1, 1, h, past), lambda bi: (layer, bi, 0, 0))
    return pl.pallas_call(
        functools.partial(_fox_sample_kernel, past=past, tnew=tnew),
        grid=(batch,),
        in_specs=[rows(FOX_WIDTH), rows(FOX_WIDTH), rows(FOX_WIDTH), rows(LANES),
                  cache(FOX_WIDTH), cache(FOX_WIDTH), cache(N_FOX_HEADS),
                  pl.BlockSpec((nkeys, nkeys), lambda bi: (0, 0))],
        out_specs=rows(FOX_WIDTH),
        out_shape=jax.ShapeDtypeStruct((n, FOX_WIDTH), f32),
        scratch_shapes=[pltpu.VMEM((FOX_WIDTH, nkeys), bf16), pltpu.VMEM((FOX_WIDTH, nkeys), bf16)],
        compiler_params=_cparams(("arbitrary",)),
        name="fox_sample",
    )(q, k_new, v_new, logf_new, cache_kt, cache_vt, cache_lt, triu)


def _s5_prep_kernel(are_r, aim_r, ldt_r, are_c, aim_c, ldt_c, bxr_ref, bxi_ref, cxr_ref, cxi_ref,
                    bst_ref, cst_ref, kd_ref, apow_ref):
    L, W, M = S5_CHUNK, SSM_WIDTH, SSM_MODES

    def lam(are, aim, ldt):
        dt = jnp.exp(ldt[...])
        return dt * are[...], dt * aim[...]

    def power(lr, li, k):
        mag = jnp.exp(float(k) * lr)
        return mag * jnp.cos(float(k) * li), mag * jnp.sin(float(k) * li)

    lr, li = lam(are_r, aim_r, ldt_r)
    lrc, lic = lam(are_c, aim_c, ldt_c)
    ar, ai = are_r[...], aim_r[...]
    abr, abi = power(lr, li, 1)
    den = ar * ar + ai * ai
    zr = ((abr - 1.0) * ar + abi * ai) / den
    zi = (abi * ar - (abr - 1.0) * ai) / den
    bbr = zr * bxr_ref[...] - zi * bxi_ref[...]
    bbi = zr * bxi_ref[...] + zi * bxr_ref[...]

    def b_pow(k):
        pr, pi = power(lr, li, k)
        return jnp.concatenate([pr * bbr - pi * bbi, pr * bbi + pi * bbr], axis=1)

    def c_pow(k):
        pr, pi = power(lrc, lic, k)
        cr, ci = cxr_ref[...], cxi_ref[...]
        return jnp.concatenate([cr * pr - ci * pi, -(cr * pi + ci * pr)], axis=0)

    b0 = b_pow(0)
    for tl in range(L):
        bst_ref[tl * W:(tl + 1) * W, :] = b_pow(L - 1 - tl).astype(bf16)
        cst_ref[:, tl * W:(tl + 1) * W] = c_pow(tl + 1).astype(bf16)
        kd = jnp.dot(b0, c_pow(L - 1 - tl), preferred_element_type=f32, precision=lax.Precision.HIGHEST)
        kd_ref[tl * W:(tl + 1) * W, :] = kd.astype(bf16)
    pr, pi = power(lr, li, L)
    apow_ref[...] = jnp.concatenate([pr, pi], axis=1)


def _s5_prep(a_re, a_im, log_dt, b_re, b_im, c_re, c_im):
    G, N, P, L = SSM_GROUPS, SSM_STATE, SSM_CH, S5_CHUNK
    W, M = SSM_WIDTH, SSM_MODES
    eye = jnp.eye(G, dtype=f32)
    bx = lambda b: (jnp.swapaxes(b, 1, 2)[:, :, None, :] * eye[:, None, :, None]).reshape(W, M)
    cx = lambda c: (jnp.swapaxes(c, 1, 2)[:, :, None, :] * eye[:, None, :, None]).reshape(M, W)
    ldt = jnp.repeat(log_dt, N)
    ins = [a_re.reshape(1, M), a_im.reshape(1, M), ldt.reshape(1, M),
           a_re.reshape(M, 1), a_im.reshape(M, 1), ldt.reshape(M, 1),
           bx(b_re), bx(b_im), cx(c_re), cx(c_im)]
    return pl.pallas_call(
        _s5_prep_kernel,
        out_shape=[jax.ShapeDtypeStruct((L * W, 2 * M), bf16), jax.ShapeDtypeStruct((2 * M, L * W), bf16),
                   jax.ShapeDtypeStruct((L * W, W), bf16), jax.ShapeDtypeStruct((1, 2 * M), f32)],
        compiler_params=pltpu.CompilerParams(vmem_limit_bytes=VMEM_LIMIT),
        name="s5_prep",
    )(*ins)


def _s5_kernel(u_ref, gs_ref, h0_ref, bst_ref, cst_ref, kd_ref, apow_ref, d_ref, wg_ref, bg_ref,
               o_ref, hT_ref, ds_ref, sp_ref, carry_ref, *, nb, rows):
    L, W, M = S5_CHUNK, SSM_WIDTH, SSM_MODES
    nr = nb * rows
    ti = pl.program_id(1)

    @pl.when(ti == 0)
    def _():
        for b in range(nb):
            carry_ref[b:b + 1, :] = h0_ref[b]

    def chunked(ref, tl):
        return jnp.concatenate([ref[0, pl.ds(tl, nr, stride=L), :], ref[1, pl.ds(tl, nr, stride=L), :]], axis=1)

    uf = [chunked(u_ref, tl) for tl in range(L)]
    ub = jnp.concatenate(uf, axis=1).astype(bf16)
    ds_ref[...] = jnp.dot(ub, bst_ref[...], preferred_element_type=f32)
    ar, ai = apow_ref[:, :M], apow_ref[:, M:]
    for b in range(nb):
        def step(c, carry):
            sr, si = carry
            r = b * rows + c
            sp_ref[pl.ds(r, 1), :] = jnp.concatenate([sr, si], axis=1)
            d = ds_ref[pl.ds(r, 1), :]
            return ar * sr - ai * si + d[:, :M], ar * si + ai * sr + d[:, M:]

        s0 = carry_ref[b:b + 1, :]
        sr, si = lax.fori_loop(0, rows, step, (s0[:, :M], s0[:, M:]))
        carry_ref[b:b + 1, :] = jnp.concatenate([sr, si], axis=1)
        hT_ref[b] = jnp.concatenate([sr, si], axis=1)

    spb = sp_ref[...].astype(bf16)
    for tl in range(L):
        cols = slice(tl * W, (tl + 1) * W)
        y = jnp.dot(ub[:, :(tl + 1) * W], kd_ref[(L - 1 - tl) * W:, :], preferred_element_type=f32)
        y = y + jnp.dot(spb, cst_ref[:, cols], preferred_element_type=f32)
        y = y + d_ref[...] * uf[tl]
        g = _gelu_tanh(y)
        gate = jax.nn.sigmoid(jnp.dot(g.astype(bf16), wg_ref[...], preferred_element_type=f32) + bg_ref[...])
        out = g * gate * chunked(gs_ref, tl)
        o_ref[0, pl.ds(tl, nr, stride=L), :] = out[:, :LANES]
        o_ref[1, pl.ds(tl, nr, stride=L), :] = out[:, LANES:]


def _s5(u, gs, h0, ops, d_row, w_glu, b_glu, *, batch, rows_per_step):
    L, W, M = S5_CHUNK, SSM_WIDTH, SSM_MODES
    n = u.shape[1]
    t = n // batch
    assert t % L == 0
    chunks = t // L
    bst, cst, kd, apow = ops
    if rows_per_step is None:
        nb, rows, grid = batch, chunks, (1, 1)
    else:
        assert chunks % rows_per_step == 0
        nb, rows, grid = 1, rows_per_step, (batch, chunks // rows_per_step)
    tiles = grid[1]
    blk = pl.BlockSpec((2, nb * rows * L, LANES), lambda bi, ti: (0, bi * tiles + ti, 0))
    full = lambda a: pl.BlockSpec(a.shape, lambda bi, ti: (0,) * a.ndim)
    state = pl.BlockSpec((nb, 1, 2 * M), lambda bi, ti: (bi, 0, 0))
    return pl.pallas_call(
        functools.partial(_s5_kernel, nb=nb, rows=rows),
        grid=grid,
        in_specs=[blk, blk, state, full(bst), full(cst), full(kd), full(apow), full(d_row),
                  full(w_glu), full(b_glu)],
        out_specs=[blk, state],
        out_shape=[jax.ShapeDtypeStruct((2, n, LANES), f32), jax.ShapeDtypeStruct((batch, 1, 2 * M), f32)],
        scratch_shapes=[pltpu.VMEM((nb * rows, 2 * M), f32), pltpu.VMEM((nb * rows, 2 * M), f32),
                        pltpu.VMEM((max(nb, 8), 2 * M), f32)],
        compiler_params=_cparams(("arbitrary", "arbitrary")),
        name="s5_prompt" if rows_per_step is not None else "s5_sample",
    )(u, gs, h0, bst, cst, kd, apow, d_row, w_glu, b_glu)


def _memory_kv_kernel(mem_ref, g_ref, w_ref, kg_ref, hm_ref, mkt_ref, mvt_ref):
    batch, _, nmem = mkt_ref.shape
    x = mem_ref[...]
    ms = jnp.mean(x * x, axis=-1, keepdims=True)
    h = (x * lax.rsqrt(ms + EPS) * g_ref[...]).astype(bf16)
    mk = jnp.dot(h, w_ref[:, 0:MEM_WIDTH], preferred_element_type=f32)
    mk = _head_norm(mk, hm_ref.at[0:MEM_WIDTH, 0:MEM_WIDTH], kg_ref)
    mv = jnp.dot(h, w_ref[:, MEM_WIDTH:2 * MEM_WIDTH], preferred_element_type=f32)
    for b in range(batch):
        mkt_ref[b] = mk[b * nmem:(b + 1) * nmem, :].T
        mvt_ref[b] = mv[b * nmem:(b + 1) * nmem, :].T


def _memory_kv(mem, lw, *, batch):
    n = mem.shape[0]
    return pl.pallas_call(
        _memory_kv_kernel,
        out_shape=[jax.ShapeDtypeStruct((batch, MEM_WIDTH, n // batch), f32)] * 2,
        compiler_params=pltpu.CompilerParams(vmem_limit_bytes=VMEM_LIMIT),
        name="memory_kv",
    )(mem, lw["mem_norm"], lw["w_mem_kv"], lw["mem_k_norm"], lw["head_mean"])


def _mix_out_kernel(x_ref, fox_ref, gf_ref, ssm_ref, mq_ref, gm_ref, mkt_ref, mvt_ref, w_ref, o_ref):
    tm = x_ref.shape[0]
    nmem = mkt_ref.shape[2]
    mq = mq_ref[...]
    mkt = mkt_ref[0]
    mvt = mvt_ref[0]
    qhead = _lane_iota((tm, MEM_WIDTH)) // HEAD_DIM
    vhead = _row_iota((MEM_WIDTH, nmem)) // HEAD_DIM
    mem = jnp.zeros((tm, MEM_WIDTH), f32)
    for hd in range(N_MEM_HEADS):
        qh = jnp.where(qhead == hd, mq, jnp.zeros_like(mq))
        s = jnp.dot(qh, mkt, preferred_element_type=f32)
        p = jnp.exp(s - jnp.max(s, axis=-1, keepdims=True))
        p = p / jnp.sum(p, axis=-1, keepdims=True)
        vh = jnp.where(vhead == hd, mvt, jnp.zeros_like(mvt))
        mem = mem + lax.dot_general(p.astype(bf16), vh, (((1,), (1,)), ((), ())), preferred_element_type=f32)
    fox = (fox_ref[...] * gf_ref[...]).astype(bf16)
    ssm = jnp.concatenate([ssm_ref[0], ssm_ref[1]], axis=1).astype(bf16)
    memg = (mem * gm_ref[...]).astype(bf16)
    y = x_ref[...]
    y = y + jnp.dot(fox, w_ref[0:FOX_WIDTH, :], preferred_element_type=f32)
    y = y + jnp.dot(ssm, w_ref[FOX_WIDTH:FOX_WIDTH + SSM_WIDTH, :], preferred_element_type=f32)
    y = y + jnp.dot(memg, w_ref[FOX_WIDTH + SSM_WIDTH:, :], preferred_element_type=f32)
    o_ref[...] = y


def _mix_out(x, fox, gf, ssm, mq, gm, mkt, mvt, w_out, *, batch, tm):
    n, d = x.shape
    t = n // batch
    assert t % tm == 0
    tiles_per_batch = t // tm
    row = lambda w: pl.BlockSpec((tm, w), lambda i: (i, 0))
    memb = pl.BlockSpec((1,) + mkt.shape[1:], lambda i: (i // tiles_per_batch, 0, 0))
    return pl.pallas_call(
        _mix_out_kernel,
        grid=(n // tm,),
        in_specs=[row(d), row(FOX_WIDTH), row(FOX_WIDTH), pl.BlockSpec((2, tm, LANES), lambda i: (0, i, 0)),
                  row(MEM_WIDTH), row(MEM_WIDTH), memb, memb, pl.BlockSpec(w_out.shape, lambda i: (0, 0))],
        out_specs=row(d),
        out_shape=jax.ShapeDtypeStruct((n, d), f32),
        compiler_params=_cparams(("arbitrary",)),
        name="mix_out_prompt" if tiles_per_batch > 1 else "mix_out_sample",
    )(x, fox, gf, ssm, mq, gm, mkt, mvt, w_out)


def _layer_weights(l, norm_g, w_in, b_forget, fox_q_norm, fox_k_norm, mem_q_norm, mem_norm, w_mem_kv,
                   mem_k_norm, w_out, w_glu, b_glu, ssm_d):
    w = w_in[l]
    nf = 4 * FOX_WIDTH
    d = w.shape[0]
    w_packed = jnp.concatenate(
        [w[:, :nf], w[:, nf + N_FOX_HEADS:], w[:, nf:nf + N_FOX_HEADS],
         jnp.zeros((d, _W_COLS - w.shape[1]), w.dtype)], axis=1).astype(bf16)
    eq, ek = _bias_selectors()
    return dict(
        norm_g=norm_g[l].reshape(1, d), w_in=w_packed,
        b_forget=jnp.pad(b_forget[l], (0, LANES - N_FOX_HEADS)).reshape(1, LANES),
        fox_q_norm=jnp.tile(fox_q_norm[l], N_FOX_HEADS).reshape(1, FOX_WIDTH),
        fox_k_norm=jnp.tile(fox_k_norm[l], N_FOX_HEADS).reshape(1, FOX_WIDTH),
        mem_q_norm=jnp.tile(mem_q_norm[l], N_MEM_HEADS).reshape(1, MEM_WIDTH),
        mem_k_norm=jnp.tile(mem_k_norm[l], N_MEM_HEADS).reshape(1, MEM_WIDTH),
        mem_norm=mem_norm[l].reshape(1, d), w_mem_kv=w_mem_kv[l].astype(bf16),
        head_mean=_head_mean_matrix(), bias_eq=eq, bias_ek=ek,
        w_out=w_out[l].astype(bf16), w_glu=w_glu[l].astype(bf16), b_glu=b_glu[l].reshape(1, SSM_WIDTH),
        ssm_d=ssm_d[l].reshape(1, SSM_WIDTH))


def _split_state(hT, batch):
    hT = hT.reshape(batch, 2, SSM_GROUPS, SSM_STATE)
    return hT[:, 0], hT[:, 1]


def kernel(x_prompt, x_sample, mem_prompt, cache_fox_k, cache_fox_v, cache_fox_logf, state_ssm_re, state_ssm_im, cache_mem_k, cache_mem_v, norm_g, w_in, b_forget, fox_q_norm, fox_k_norm, ssm_a_re, ssm_a_im, ssm_log_dt, ssm_b_re, ssm_b_im, ssm_c_re, ssm_c_im, ssm_d, w_glu, b_glu, mem_norm, w_mem_kv, mem_q_norm, mem_k_norm, w_out):
    B, T, D = x_prompt.shape
    Bs, Ts, _ = x_sample.shape
    depth = w_in.shape[0]
    past = cache_fox_k.shape[2]
    nmem = mem_prompt.shape[1]
    M = SSM_MODES

    xp = x_prompt.reshape(B * T, D)
    xs = x_sample.reshape(Bs * Ts, D)
    mem = mem_prompt.reshape(B * nmem, D)
    cache_kt = jnp.transpose(cache_fox_k, (0, 1, 3, 4, 2)).reshape(depth, Bs, FOX_WIDTH, past)
    cache_vt = jnp.transpose(cache_fox_v, (0, 1, 3, 4, 2)).reshape(depth, Bs, FOX_WIDTH, past)
    cache_lt = jnp.transpose(cache_fox_logf, (0, 1, 3, 2)).astype(f32)
    cache_mkt = jnp.transpose(cache_mem_k, (0, 1, 3, 4, 2)).reshape(depth, Bs, MEM_WIDTH, nmem).astype(bf16)
    cache_mvt = jnp.transpose(cache_mem_v, (0, 1, 3, 4, 2)).reshape(depth, Bs, MEM_WIDTH, nmem).astype(bf16)

    outs = {k: [] for k in ("pk", "pv", "pf", "pre", "pim", "pmk", "pmv", "sk", "sv", "sf", "sre", "sim")}
    for l in range(depth):
        lw = _layer_weights(l, norm_g, w_in, b_forget, fox_q_norm, fox_k_norm, mem_q_norm, mem_norm,
                            w_mem_kv, mem_k_norm, w_out, w_glu, b_glu, ssm_d)
        ops = _s5_prep(ssm_a_re[l], ssm_a_im[l], ssm_log_dt[l], ssm_b_re[l], ssm_b_im[l],
                       ssm_c_re[l], ssm_c_im[l])
        mkt, mvt = _memory_kv(mem, lw, batch=B)

        pr = _in_proj(xp, lw, batch=B, prompt=True)
        fox = _fox_prompt(pr["qa"], pr["ka"], pr["va"]).reshape(B * T, FOX_WIDTH)
        ssm, hT = _s5(pr["u"], pr["gs"], jnp.zeros((B, 1, 2 * M), f32), ops, lw["ssm_d"], lw["w_glu"],
                      lw["b_glu"], batch=B, rows_per_step=128)
        xp = _mix_out(xp, fox, pr["gf"], ssm, pr["mq"], pr["gm"], mkt.astype(bf16), mvt.astype(bf16),
                      lw["w_out"], batch=B, tm=ROW_TILE)
        hre, him = _split_state(hT, B)
        outs["pk"].append(pr["kt"])
        outs["pv"].append(pr["vt"])
        outs["pf"].append(pr["logft"])
        outs["pre"].append(hre)
        outs["pim"].append(him)
        outs["pmk"].append(mkt.reshape(B, N_MEM_HEADS, HEAD_DIM, nmem))
        outs["pmv"].append(mvt.reshape(B, N_MEM_HEADS, HEAD_DIM, nmem))

        sr = _in_proj(xs, lw, batch=Bs, prompt=False)
        fox_s = _fox_sample(sr["q"], sr["k"], sr["v"], sr["logf"], cache_kt, cache_vt, cache_lt, l, batch=Bs)
        h0 = jnp.concatenate([state_ssm_re[l].reshape(Bs, 1, M), state_ssm_im[l].reshape(Bs, 1, M)],
                             axis=-1).astype(f32)
        ssm_s, hT_s = _s5(sr["u"], sr["gs"], h0, ops, lw["ssm_d"], lw["w_glu"], lw["b_glu"], batch=Bs,
                          rows_per_step=None)
        xs = _mix_out(xs, fox_s, sr["gf"], ssm_s, sr["mq"], sr["gm"], cache_mkt[l], cache_mvt[l],
                      lw["w_out"], batch=Bs, tm=Ts)
        sre, sim = _split_state(hT_s, Bs)
        outs["sk"].append(sr["k"].reshape(Bs, Ts, N_FOX_HEADS, HEAD_DIM))
        outs["sv"].append(sr["v"].reshape(Bs, Ts, N_FOX_HEADS, HEAD_DIM))
        outs["sf"].append(sr["logf"][:, :N_FOX_HEADS].reshape(Bs, Ts, N_FOX_HEADS))
        outs["sre"].append(sre)
        outs["sim"].append(sim)

    st = {k: jnp.stack(v) for k, v in outs.items()}
    fox_k_prompt = jnp.transpose(st["pk"], (0, 1, 4, 2, 3))
    fox_v_prompt = jnp.transpose(st["pv"], (0, 1, 4, 2, 3))
    fox_logf_prompt = jnp.transpose(st["pf"], (0, 1, 3, 2))
    mem_k_prompt = jnp.transpose(st["pmk"], (0, 1, 4, 2, 3))
    mem_v_prompt = jnp.transpose(st["pmv"], (0, 1, 4, 2, 3))
    return (xp.reshape(B, T, D), xs.reshape(Bs, Ts, D), fox_k_prompt, fox_v_prompt, fox_logf_prompt,
            st["pre"], st["pim"], mem_k_prompt, mem_v_prompt, st["sk"], st["sv"], st["sf"], st["sre"], st["sim"])
```

```python
import functools
import math

import numpy as np
import jax
import jax.numpy as jnp
from jax import lax
from jax.experimental import pallas as pl
from jax.experimental.pallas import tpu as pltpu

f32 = jnp.float32
bf16 = jnp.bfloat16

HEAD_DIM = 64
N_FOX_HEADS = 8
FOX_WIDTH = N_FOX_HEADS * HEAD_DIM
SSM_GROUPS = 16
SSM_CH = 16
SSM_STATE = 64
SSM_WIDTH = SSM_GROUPS * SSM_CH
SSM_MODES = SSM_GROUPS * SSM_STATE
N_MEM_HEADS = 4
MEM_WIDTH = N_MEM_HEADS * HEAD_DIM
EPS = 1e-6
QK_SCALE = HEAD_DIM ** -0.5
LOG2E = math.log2(math.e)

LANES = 128
S5_CHUNK = 8
S5_SEG = 16
ROW_TILE = 512
ATTN_TQ = 512
VMEM_LIMIT = 56 * 1024 * 1024

_C_Q, _C_K, _C_V, _C_G = 0, 512, 1024, 1536
_C_SU, _C_SG, _C_MQ, _C_MG, _C_F = 2048, 2304, 2560, 2816, 3072
_W_COLS = 3200
_N_PIECES = 3


def _cparams(sem):
    return pltpu.CompilerParams(dimension_semantics=sem, vmem_limit_bytes=VMEM_LIMIT)


def _silu(x):
    return x * jax.nn.sigmoid(x)


def _log_sigmoid(x):
    return -(jnp.maximum(-x, 0.0) + jnp.log1p(jnp.exp(-jnp.abs(x))))


def _gelu_tanh(x):
    return 0.5 * x * (1.0 + jnp.tanh(math.sqrt(2.0 / math.pi) * (x + 0.044715 * (x * x * x))))


def _split3(x):
    hi = x.astype(bf16).astype(f32)
    r = x - hi
    mid = r.astype(bf16).astype(f32)
    lo = (r - mid).astype(bf16).astype(f32)
    return hi, mid, lo


def _pack3(x):
    hi, mid, lo = _split3(x)
    return (hi + pltpu.roll(mid, 8, 1) + pltpu.roll(lo, 16, 1)).astype(bf16)


def _unpack3(y):
    return y + pltpu.roll(y, LANES - 8, 1) + pltpu.roll(y, LANES - 16, 1)


def _lane_iota(shape):
    return lax.broadcasted_iota(jnp.int32, shape, len(shape) - 1)


def _row_iota(shape):
    return lax.broadcasted_iota(jnp.int32, shape, len(shape) - 2)


def _head_norm(z, hm_ref, g_ref):
    ms = jnp.dot((z * z).astype(bf16), hm_ref[...], preferred_element_type=f32)
    return z * lax.rsqrt(ms + EPS) * g_ref[...]


def _in_proj_kernel(*refs, prompt, tiles_per_batch):
    if prompt:
        (x_ref, ng_ref, w_ref, bf_ref, qg_ref, kg_ref, mqg_ref, hm_ref, tril_ref, eq_ref, ek_ref,
         kt_ref, vt_ref, logft_ref, gf_ref, u_ref, gs_ref, mq_ref, gm_ref, qa_ref, ka_ref, va_ref,
         carry_ref) = refs
    else:
        (x_ref, ng_ref, w_ref, bf_ref, qg_ref, kg_ref, mqg_ref, hm_ref,
         k_ref, v_ref, logf_ref, gf_ref, u_ref, gs_ref, mq_ref, gm_ref, q_ref) = refs

    x = x_ref[...]
    tm = x.shape[0]
    lane = _lane_iota((tm, LANES))
    ms = jnp.mean(x * x, axis=-1, keepdims=True)
    h = (x * lax.rsqrt(ms + EPS) * ng_ref[...]).astype(bf16)

    def seg(lo, width):
        return jnp.dot(h, w_ref[:, lo:lo + width], preferred_element_type=f32)

    def halves(ref, z):
        ref[0] = z[:, :LANES]
        ref[1] = z[:, LANES:]

    q = _head_norm(seg(_C_Q, FOX_WIDTH), hm_ref, qg_ref) * QK_SCALE
    k = _head_norm(seg(_C_K, FOX_WIDTH), hm_ref, kg_ref)
    v = seg(_C_V, FOX_WIDTH)
    gf_ref[...] = _silu(seg(_C_G, FOX_WIDTH))
    halves(u_ref, seg(_C_SU, SSM_WIDTH))
    halves(gs_ref, _silu(seg(_C_SG, SSM_WIDTH)))
    mq = _head_norm(seg(_C_MQ, MEM_WIDTH), hm_ref.at[0:MEM_WIDTH, 0:MEM_WIDTH], mqg_ref) * QK_SCALE
    mq_ref[...] = mq.astype(bf16)
    gm_ref[...] = _silu(seg(_C_MG, MEM_WIDTH))

    logf = _log_sigmoid(seg(_C_F, LANES) + bf_ref[...])
    lf = jnp.where(lane < N_FOX_HEADS, logf, 0.0)

    if not prompt:
        k_ref[...] = k
        v_ref[...] = v
        logf_ref[...] = lf
        q_ref[...] = q.astype(bf16)
        return

    kt = k.T
    vt = v.T
    for hd in range(N_FOX_HEADS):
        kt_ref[0, hd] = kt[hd * HEAD_DIM:(hd + 1) * HEAD_DIM, :]
        vt_ref[0, hd] = vt[hd * HEAD_DIM:(hd + 1) * HEAD_DIM, :]
    logft_ref[0] = lf.T[0:N_FOX_HEADS, :]

    i = pl.program_id(0)

    @pl.when(i % tiles_per_batch == 0)
    def _():
        carry_ref[...] = jnp.zeros_like(carry_ref)

    c = _unpack3(jnp.dot(tril_ref[...], _pack3(lf), preferred_element_type=f32))
    c = jnp.where(lane < N_FOX_HEADS, c + carry_ref[0:1, :], 0.0)
    carry_ref[0:1, :] = c[tm - 1:tm, :]

    q = q * LOG2E
    pieces = (_pack3(c * LOG2E).astype(f32) + jnp.where(lane == 3 * N_FOX_HEADS, 1.0, 0.0)).astype(bf16)
    qaug = jnp.dot(pieces, eq_ref[...], preferred_element_type=f32)
    kaug = jnp.dot(pieces, ek_ref[...], preferred_element_type=f32)
    for hd in range(N_FOX_HEADS):
        pair = (hd // 2) * LANES
        own = (lane < HEAD_DIM) if hd % 2 == 0 else (lane >= HEAD_DIM)
        ones_lane = HEAD_DIM if hd % 2 == 0 else 0
        blk = slice(hd * LANES, (hd + 1) * LANES)
        qa_ref[0, hd] = jnp.where(own, q[:, pair:pair + LANES], qaug[:, blk]).astype(bf16)
        ka_ref[0, hd] = jnp.where(own, k[:, pair:pair + LANES], kaug[:, blk]).astype(bf16)
        va_ref[0, hd] = jnp.where(own, v[:, pair:pair + LANES],
                                  jnp.where(lane == ones_lane, 1.0, 0.0)).astype(bf16)


def _bias_selectors():
    eq = np.zeros((LANES, N_FOX_HEADS * LANES), np.float32)
    ek = np.zeros((LANES, N_FOX_HEADS * LANES), np.float32)
    one = 3 * N_FOX_HEADS
    for hd in range(N_FOX_HEADS):
        base = hd * LANES + (HEAD_DIM if hd % 2 == 0 else 0)
        for p in range(_N_PIECES):
            eq[p * N_FOX_HEADS + hd, base + p] = 1.0
            eq[one, base + _N_PIECES + p] = 1.0
            ek[one, base + p] = 1.0
            ek[p * N_FOX_HEADS + hd, base + _N_PIECES + p] = -1.0
    return jnp.asarray(eq, bf16), jnp.asarray(ek, bf16)


def _head_mean_matrix():
    idx = np.arange(FOX_WIDTH) // HEAD_DIM
    return jnp.asarray((idx[:, None] == idx[None, :]).astype(np.float32) / HEAD_DIM, bf16)


def _in_proj(x, lw, *, batch, prompt):
    n, d = x.shape
    t = n // batch
    tm = ROW_TILE if prompt else n
    assert n % tm == 0 and (t % tm == 0 or not prompt)
    tiles_per_batch = t // tm if prompt else 1
    grid = (n // tm,)
    row = lambda w: pl.BlockSpec((tm, w), lambda i: (i, 0))
    full = lambda a: pl.BlockSpec(a.shape, lambda i: (0,) * a.ndim)
    split = pl.BlockSpec((2, tm, LANES), lambda i: (0, i, 0))
    by_batch = lambda i: (i // tiles_per_batch, 0, i % tiles_per_batch, 0)

    ins = [x, lw["norm_g"], lw["w_in"], lw["b_forget"], lw["fox_q_norm"], lw["fox_k_norm"],
           lw["mem_q_norm"], lw["head_mean"]]
    in_specs = [row(d)] + [full(a) for a in ins[1:]]
    common = [((n, FOX_WIDTH), f32, row(FOX_WIDTH)), ((2, n, LANES), f32, split), ((2, n, LANES), f32, split),
              ((n, MEM_WIDTH), bf16, row(MEM_WIDTH)), ((n, MEM_WIDTH), f32, row(MEM_WIDTH))]
    scratch = []
    if prompt:
        tril = jnp.asarray(np.tril(np.ones((tm, tm), np.float32)), bf16)
        extra = [tril, lw["bias_eq"], lw["bias_ek"]]
        ins += extra
        in_specs += [full(a) for a in extra]
        tspec = pl.BlockSpec((1, N_FOX_HEADS, HEAD_DIM, tm),
                             lambda i: (i // tiles_per_batch, 0, 0, i % tiles_per_batch))
        lspec = pl.BlockSpec((1, N_FOX_HEADS, tm), lambda i: (i // tiles_per_batch, 0, i % tiles_per_batch))
        hspec = pl.BlockSpec((1, N_FOX_HEADS, tm, LANES), by_batch)
        tshape = (batch, N_FOX_HEADS, HEAD_DIM, t)
        outs = ([(tshape, f32, tspec), (tshape, f32, tspec), ((batch, N_FOX_HEADS, t), f32, lspec)] + common
                + [((batch, N_FOX_HEADS, t, LANES), bf16, hspec)] * 3)
        names = ["kt", "vt", "logft", "gf", "u", "gs", "mq", "gm", "qa", "ka", "va"]
        scratch = [pltpu.VMEM((8, LANES), f32)]
    else:
        outs = ([((n, FOX_WIDTH), f32, row(FOX_WIDTH)), ((n, FOX_WIDTH), f32, row(FOX_WIDTH)),
                 ((n, LANES), f32, row(LANES))] + common + [((n, FOX_WIDTH), bf16, row(FOX_WIDTH))])
        names = ["k", "v", "logf", "gf", "u", "gs", "mq", "gm", "q"]

    res = pl.pallas_call(
        functools.partial(_in_proj_kernel, prompt=prompt, tiles_per_batch=tiles_per_batch),
        grid=grid, in_specs=in_specs, out_specs=[o[2] for o in outs],
        out_shape=[jax.ShapeDtypeStruct(o[0], o[1]) for o in outs],
        scratch_shapes=scratch,
        compiler_params=_cparams(("arbitrary",)),
        name="in_proj_prompt" if prompt else "in_proj_sample",
    )(*ins)
    return dict(zip(names, res))


def _fox_prompt_kernel(qa_ref, ka_ref, va_ref, o_ref, s0_ref, s1_ref, m_ref, acc_ref, *, nq):
    tq = tk = ATTN_TQ
    s_refs = (s0_ref, s1_ref)
    m_ref[...] = jnp.full(m_ref.shape, -jnp.inf, f32)
    acc_ref[...] = jnp.zeros(acc_ref.shape, f32)

    def scores(qb, kb, slot):
        qoff = pl.multiple_of(qb * tq, tq)
        koff = pl.multiple_of(kb * tk, tk)
        for j in range(2):
            s_refs[slot][j] = lax.dot_general(qa_ref[0, j, pl.ds(qoff, tq), :], ka_ref[0, j, pl.ds(koff, tk), :],
                                              (((1,), (1,)), ((), ())), preferred_element_type=f32)

    def softmax_pv(qb, kb, slot, masked):
        koff = pl.multiple_of(kb * tk, tk)
        for j in range(2):
            s = s_refs[slot][j]
            if masked:
                s = jnp.where(_row_iota((tq, tk)) >= _lane_iota((tq, tk)), s, -jnp.inf)
            m = m_ref[qb, j]
            m_new = jnp.maximum(m, jnp.max(s, axis=-1, keepdims=True))
            p = jnp.exp2(s - jnp.concatenate([m_new] * (tk // LANES), axis=1))
            vv = va_ref[0, j, pl.ds(koff, tk), :]
            acc_ref[qb, j] = (jnp.exp2(m - m_new) * acc_ref[qb, j]
                              + jnp.dot(p.astype(bf16), vv, preferred_element_type=f32))
            m_ref[qb, j] = m_new

    def finalize(qb):
        h0 = acc_ref[qb, 0] / acc_ref[qb, 0, :, HEAD_DIM:HEAD_DIM + 1]
        h1 = acc_ref[qb, 1] / acc_ref[qb, 1, :, 0:1]
        o_ref[0, pl.ds(pl.multiple_of(qb * tq, tq), tq), :] = jnp.where(
            _lane_iota((tq, LANES)) < HEAD_DIM, h0, h1)

    def after(qb, kb):
        row_end = kb + 1 >= qb
        last = jnp.logical_and(row_end, qb + 1 >= nq)
        nqb = jnp.where(last, 0, jnp.where(row_end, qb + 1, qb))
        nkb = jnp.where(row_end, 0, kb + 1)
        return nqb, nkb

    def lower_step(qb, kb, slot):
        nqb, nkb = after(qb, kb)
        scores(nqb, nkb, 1 - slot)
        softmax_pv(qb, kb, slot, False)
        return nqb, nkb

    def diag_step(qb, slot):
        nxt = jnp.minimum(qb + 1, nq - 1)
        scores(nxt, nxt, 1 - slot)
        softmax_pv(qb, qb, slot, True)
        finalize(qb)

    n_lower = nq * (nq - 1) // 2
    if n_lower:
        scores(1, 0, 0)

        def lower_body(i, carry):
            qb, kb = lower_step(*carry, 0)
            return lower_step(qb, kb, 1)

        lax.fori_loop(0, n_lower // 2, lower_body, (jnp.int32(1), jnp.int32(0)))
    else:
        scores(0, 0, 0)

    def diag_body(i, carry):
        diag_step(2 * i, 0)
        diag_step(2 * i + 1, 1)
        return carry

    lax.fori_loop(0, nq // 2, diag_body, 0)


def _fox_prompt(qa, ka, va):
    b, nh, t, _ = qa.shape
    nq = t // ATTN_TQ
    assert t % ATTN_TQ == 0 and nq % 2 == 0 and (nq * (nq - 1) // 2) % 2 == 0
    whole = pl.BlockSpec((1, 2, t, LANES), lambda bi, hp: (bi, hp, 0, 0))
    return pl.pallas_call(
        functools.partial(_fox_prompt_kernel, nq=nq),
        grid=(b, nh // 2),
        in_specs=[whole, whole, whole],
        out_specs=pl.BlockSpec((1, t, LANES), lambda bi, hp: (bi, 0, hp)),
        out_shape=jax.ShapeDtypeStruct((b, t, FOX_WIDTH), f32),
        scratch_shapes=[pltpu.VMEM((2, ATTN_TQ, ATTN_TQ), f32), pltpu.VMEM((2, ATTN_TQ, ATTN_TQ), f32),
                        pltpu.VMEM((nq, 2, ATTN_TQ, LANES), f32), pltpu.VMEM((nq, 2, ATTN_TQ, LANES), f32)],
        compiler_params=_cparams(("arbitrary", "arbitrary")),
        name="fox_prompt",
    )(qa, ka, va)


def _fox_sample_kernel(q_ref, kn_ref, vn_ref, ln_ref, kc_ref, vc_ref, lc_ref, triu_ref, o_ref,
                       kall_ref, vall_ref, *, past, tnew):
    nkeys = kall_ref.shape[1]
    nq = N_FOX_HEADS * tnew

    def new_cols(ref):
        z = jnp.concatenate([ref[...], jnp.zeros((LANES - tnew, ref.shape[1]), f32)], axis=0)
        return z.T

    kall_ref[:, 0:past] = kc_ref[0, 0].astype(bf16)
    vall_ref[:, 0:past] = vc_ref[0, 0].astype(bf16)
    kall_ref[:, past:nkeys] = new_cols(kn_ref).astype(bf16)
    vall_ref[:, past:nkeys] = new_cols(vn_ref).astype(bf16)

    l_all = jnp.concatenate([lc_ref[0, 0], new_cols(ln_ref)[0:N_FOX_HEADS, :]], axis=1)
    pieces = jnp.concatenate(_split3(l_all), axis=0).astype(bf16)
    c3 = jnp.dot(pieces, triu_ref[...], preferred_element_type=f32)
    c = c3[0:8] + c3[8:16] + c3[16:24]
    crow = jnp.concatenate([jnp.broadcast_to(c[hd:hd + 1, :], (tnew, nkeys)) for hd in range(N_FOX_HEADS)],
                           axis=0)
    rq = _row_iota((nq, LANES))
    cq = jnp.sum(jnp.where(_lane_iota((nq, LANES)) == rq % tnew, crow[:, past:nkeys], 0.0),
                 axis=1, keepdims=True)

    qrep = jnp.concatenate([q_ref[...]] * N_FOX_HEADS, axis=0)
    own = _row_iota((nq, FOX_WIDTH)) // tnew == _lane_iota((nq, FOX_WIDTH)) // HEAD_DIM
    qbd = jnp.where(own, qrep, jnp.zeros_like(qrep))
    s = jnp.dot(qbd, kall_ref[...], preferred_element_type=f32) + cq - crow
    s = jnp.where(_lane_iota((nq, nkeys)) <= past + _row_iota((nq, nkeys)) % tnew, s, -jnp.inf)
    p = jnp.exp(s - jnp.max(s, axis=1, keepdims=True))
    o2 = lax.dot_general(p.astype(bf16), vall_ref[...], (((1,), (1,)), ((), ())), preferred_element_type=f32)
    o2 = o2 / jnp.sum(p, axis=1, keepdims=True)
    olane = _lane_iota((tnew, FOX_WIDTH))
    out = jnp.zeros((tnew, FOX_WIDTH), f32)
    for hd in range(N_FOX_HEADS):
        out = out + jnp.where(olane // HEAD_DIM == hd, o2[hd * tnew:(hd + 1) * tnew, :], 0.0)
    o_ref[...] = out


def _fox_sample(q, k_new, v_new, logf_new, cache_kt, cache_vt, cache_lt, layer, *, batch):
    n = q.shape[0]
    tnew = n // batch
    past = cache_kt.shape[3]
    nkeys = past + LANES
    assert N_FOX_HEADS * tnew == LANES and past % LANES == 0
    triu = jnp.asarray(np.triu(np.ones((nkeys, nkeys), np.float32)), bf16)
    rows = lambda w: pl.BlockSpec((tnew, w), lambda bi: (bi, 0))
    cache = lambda h: pl.BlockSpec((1, 1, h, past), lambda bi: (layer, bi, 0, 0))
    return pl.pallas_call(
        functools.partial(_fox_sample_kernel, past=past, tnew=tnew),
        grid=(batch,),
        in_specs=[rows(FOX_WIDTH), rows(FOX_WIDTH), rows(FOX_WIDTH), rows(LANES),
                  cache(FOX_WIDTH), cache(FOX_WIDTH), cache(N_FOX_HEADS),
                  pl.BlockSpec((nkeys, nkeys), lambda bi: (0, 0))],
        out_specs=rows(FOX_WIDTH),
        out_shape=jax.ShapeDtypeStruct((n, FOX_WIDTH), f32),
        scratch_shapes=[pltpu.VMEM((FOX_WIDTH, nkeys), bf16), pltpu.VMEM((FOX_WIDTH, nkeys), bf16)],
        compiler_params=_cparams(("arbitrary",)),
        name="fox_sample",
    )(q, k_new, v_new, logf_new, cache_kt, cache_vt, cache_lt, triu)


def _s5_prep_kernel(are_r, aim_r, ldt_r, are_c, aim_c, ldt_c, bxr_ref, bxi_ref, cxr_ref, cxi_ref,
                    bst_ref, cst_ref, kt_ref, apow_ref):
    L, W, M = S5_CHUNK, SSM_WIDTH, SSM_MODES

    def cmul(xr, xi, yr, yi):
        return xr * yr - xi * yi, xr * yi + xi * yr

    def powers(are, aim, ldt, n):
        dt = jnp.exp(ldt[...])
        mag = jnp.exp(dt * are[...])
        p1 = (mag * jnp.cos(dt * aim[...]), mag * jnp.sin(dt * aim[...]))
        out = [(jnp.ones_like(mag), jnp.zeros_like(mag)), p1]
        for _ in range(n - 1):
            out.append(cmul(*out[-1], *p1))
        return out

    prow = powers(are_r, aim_r, ldt_r, L)
    pcol = powers(are_c, aim_c, ldt_c, L)
    ar, ai = are_r[...], aim_r[...]
    abr, abi = prow[1]
    den = ar * ar + ai * ai
    zr = ((abr - 1.0) * ar + abi * ai) / den
    zi = (abi * ar - (abr - 1.0) * ai) / den
    bbr, bbi = cmul(zr, zi, bxr_ref[...], bxi_ref[...])

    def b_pow(k):
        return jnp.concatenate(cmul(*prow[k], bbr, bbi), axis=1)

    def c_pow(k):
        pr, pi = pcol[k]
        cr, ci = cxr_ref[...], cxi_ref[...]
        return jnp.concatenate([cr * pr - ci * pi, -(cr * pi + ci * pr)], axis=0)

    cpow = [c_pow(k) for k in range(L + 1)]
    for tl in range(L):
        bst_ref[tl * W:(tl + 1) * W, :] = b_pow(L - 1 - tl).astype(bf16)
        cst_ref[:, tl * W:(tl + 1) * W] = cpow[tl + 1].astype(bf16)

    def hi_lo(x):
        hi = x.astype(bf16)
        return hi, (x - hi.astype(f32)).astype(bf16)

    bh, bl = hi_lo(b_pow(0))
    ch, cl = hi_lo(jnp.concatenate(cpow[:L], axis=1))
    kd = (jnp.dot(bh, ch, preferred_element_type=f32) + jnp.dot(bh, cl, preferred_element_type=f32)
          + jnp.dot(bl, ch, preferred_element_type=f32)).astype(bf16)
    for tl in range(L):
        if tl:
            kt_ref[tl * W:(tl + 1) * W, 0:tl * W] = jnp.zeros((W, tl * W), bf16)
        kt_ref[tl * W:(tl + 1) * W, tl * W:] = kd[:, :(L - tl) * W]

    step = prow[L]
    seg = step
    for _ in range(int(math.log2(S5_SEG))):
        seg = cmul(*seg, *seg)
    apow_ref[...] = jnp.concatenate(
        [jnp.concatenate(step, axis=1), jnp.concatenate(seg, axis=1), jnp.zeros((6, 2 * M), f32)], axis=0)


def _s5_prep(a_re, a_im, log_dt, b_re, b_im, c_re, c_im):
    G, N, P, L = SSM_GROUPS, SSM_STATE, SSM_CH, S5_CHUNK
    W, M = SSM_WIDTH, SSM_MODES
    eye = jnp.eye(G, dtype=f32)
    bx = lambda b: (jnp.swapaxes(b, 1, 2)[:, :, None, :] * eye[:, None, :, None]).reshape(W, M)
    cx = lambda c: (jnp.swapaxes(c, 1, 2)[:, :, None, :] * eye[:, None, :, None]).reshape(M, W)
    ldt = jnp.repeat(log_dt, N)
    ins = [a_re.reshape(1, M), a_im.reshape(1, M), ldt.reshape(1, M),
           a_re.reshape(M, 1), a_im.reshape(M, 1), ldt.reshape(M, 1),
           bx(b_re), bx(b_im), cx(c_re), cx(c_im)]
    return pl.pallas_call(
        _s5_prep_kernel,
        out_shape=[jax.ShapeDtypeStruct((L * W, 2 * M), bf16), jax.ShapeDtypeStruct((2 * M, L * W), bf16),
                   jax.ShapeDtypeStruct((L * W, L * W), bf16), jax.ShapeDtypeStruct((8, 2 * M), f32)],
        compiler_params=pltpu.CompilerParams(vmem_limit_bytes=VMEM_LIMIT),
        name="s5_prep",
    )(*ins)


def _s5_kernel(u_ref, gs_ref, h0_ref, bst_ref, cst_ref, kt_ref, apow_ref, d_ref, wg_ref, bg_ref,
               o_ref, hT_ref, ds_ref, sp_ref, y_ref, carry_ref, *, chained, seglen):
    L, W, M = S5_CHUNK, SSM_WIDTH, SSM_MODES
    nr = 8 * seglen
    nlb = M // LANES

    def chunked(ref, tl):
        return jnp.concatenate([ref[0, pl.ds(tl, nr, stride=L), :], ref[1, pl.ds(tl, nr, stride=L), :]], axis=1)

    uf = [chunked(u_ref, tl) for tl in range(L)]
    ub = jnp.concatenate(uf, axis=1).astype(bf16)
    ds = jnp.dot(ub, bst_ref[...], preferred_element_type=f32)
    for lb in range(2 * nlb):
        ds_ref[lb] = ds[:, lb * LANES:(lb + 1) * LANES]
    for pair in range(L // 2):
        kk = (2 * pair + 2) * W
        cols = slice(2 * pair * W, kk)
        y_ref[:, cols] = jnp.dot(ub[:, :kk], kt_ref[0:kk, cols], preferred_element_type=f32)

    def blocks(row):
        return ([row[:, lb * LANES:(lb + 1) * LANES] for lb in range(nlb)],
                [row[:, M + lb * LANES:M + (lb + 1) * LANES] for lb in range(nlb)])

    def bcast8(parts):
        return [jnp.broadcast_to(p, (8, LANES)) for p in parts]

    ar, ai = (bcast8(p) for p in blocks(apow_ref[0:1, :]))

    def scan(vr, vi, store):
        for i in range(seglen):
            rows = pl.ds(i, 8, stride=seglen)
            for lb in range(nlb):
                if store:
                    sp_ref[lb, rows, :] = vr[lb]
                    sp_ref[nlb + lb, rows, :] = vi[lb]
                dr, di = ds_ref[lb, rows, :], ds_ref[nlb + lb, rows, :]
                vr[lb], vi[lb] = (ar[lb] * vr[lb] - ai[lb] * vi[lb] + dr,
                                  ar[lb] * vi[lb] + ai[lb] * vr[lb] + di)
        return vr, vi

    if chained:
        ti = pl.program_id(1)

        @pl.when(ti == 0)
        def _():
            carry_ref[0:1, :] = h0_ref[0]

        zero = [jnp.zeros((8, LANES), f32) for _ in range(nlb)]
        er, ei = scan(list(zero), list(zero), store=False)
        pr, pi = blocks(apow_ref[1:2, :])
        cr, ci = blocks(carry_ref[0:1, :])
        sr, si = [[] for _ in range(nlb)], [[] for _ in range(nlb)]
        for j in range(8):
            for lb in range(nlb):
                sr[lb].append(cr[lb])
                si[lb].append(ci[lb])
                cr[lb], ci[lb] = (pr[lb] * cr[lb] - pi[lb] * ci[lb] + er[lb][j:j + 1, :],
                                  pr[lb] * ci[lb] + pi[lb] * cr[lb] + ei[lb][j:j + 1, :])
        final = jnp.concatenate(cr + ci, axis=1)
        carry_ref[0:1, :] = final
        hT_ref[0] = final
        scan([jnp.concatenate(x, axis=0) for x in sr], [jnp.concatenate(x, axis=0) for x in si], store=True)
    else:
        vr, vi = blocks(h0_ref[...])
        vr, vi = scan(vr, vi, store=True)
        hT_ref[...] = jnp.concatenate(vr + vi, axis=1)

    spb = jnp.concatenate([sp_ref[lb] for lb in range(2 * nlb)], axis=1).astype(bf16)
    for pair in range(L // 2):
        cols = slice(2 * pair * W, (2 * pair + 2) * W)
        ypair = y_ref[:, cols] + jnp.dot(spb, cst_ref[:, cols], preferred_element_type=f32)
        for tl in (2 * pair, 2 * pair + 1):
            y = ypair[:, (tl % 2) * W:(tl % 2 + 1) * W] + d_ref[...] * uf[tl]
            g = _gelu_tanh(y)
            gate = jax.nn.sigmoid(jnp.dot(g.astype(bf16), wg_ref[...], preferred_element_type=f32) + bg_ref[...])
            out = g * gate * chunked(gs_ref, tl)
            o_ref[0, pl.ds(tl, nr, stride=L), :] = out[:, :LANES]
            o_ref[1, pl.ds(tl, nr, stride=L), :] = out[:, LANES:]


def _s5(u, gs, h0, ops, d_row, w_glu, b_glu, *, batch, chained):
    L, W, M = S5_CHUNK, SSM_WIDTH, SSM_MODES
    n = u.shape[1]
    t = n // batch
    assert t % L == 0
    chunks = t // L
    bst, cst, kt, apow = ops
    if chained:
        seglen = S5_SEG
        assert chunks % (8 * seglen) == 0
        grid = (batch, chunks // (8 * seglen))
        state = pl.BlockSpec((1, 1, 2 * M), lambda bi, ti: (bi, 0, 0))
    else:
        seglen = chunks
        assert batch == 8
        grid = (1, 1)
        state = pl.BlockSpec((8, 2 * M), lambda bi, ti: (0, 0))
    tiles = grid[1]
    nr = 8 * seglen
    blk = pl.BlockSpec((2, nr * L, LANES), lambda bi, ti: (0, bi * tiles + ti, 0))
    full = lambda a: pl.BlockSpec(a.shape, lambda bi, ti: (0,) * a.ndim)
    once = lambda a: pl.BlockSpec(a.shape, lambda bi, ti: (0,) * a.ndim, pipeline_mode=pl.Buffered(1))
    return pl.pallas_call(
        functools.partial(_s5_kernel, chained=chained, seglen=seglen),
        grid=grid,
        in_specs=[blk, blk, state, once(bst), once(cst), once(kt), full(apow), full(d_row),
                  full(w_glu), full(b_glu)],
        out_specs=[blk, state],
        out_shape=[jax.ShapeDtypeStruct((2, n, LANES), f32), jax.ShapeDtypeStruct(h0.shape, f32)],
        scratch_shapes=[pltpu.VMEM((2 * M // LANES, nr, LANES), f32), pltpu.VMEM((2 * M // LANES, nr, LANES), f32),
                        pltpu.VMEM((nr, L * W), f32), pltpu.VMEM((8, 2 * M), f32)],
        compiler_params=_cparams(("arbitrary", "arbitrary")),
        name="s5_prompt" if chained else "s5_sample",
    )(u, gs, h0, bst, cst, kt, apow, d_row, w_glu, b_glu)


def _memory_kv_kernel(mem_ref, g_ref, w_ref, kg_ref, hm_ref, mkt_ref, mvt_ref):
    batch, _, nmem = mkt_ref.shape
    x = mem_ref[...]
    ms = jnp.mean(x * x, axis=-1, keepdims=True)
    h = (x * lax.rsqrt(ms + EPS) * g_ref[...]).astype(bf16)
    mk = jnp.dot(h, w_ref[:, 0:MEM_WIDTH], preferred_element_type=f32)
    mk = _head_norm(mk, hm_ref.at[0:MEM_WIDTH, 0:MEM_WIDTH], kg_ref)
    mv = jnp.dot(h, w_ref[:, MEM_WIDTH:2 * MEM_WIDTH], preferred_element_type=f32)
    for b in range(batch):
        mkt_ref[b] = mk[b * nmem:(b + 1) * nmem, :].T
        mvt_ref[b] = mv[b * nmem:(b + 1) * nmem, :].T


def _memory_kv(mem, lw, *, batch):
    n = mem.shape[0]
    return pl.pallas_call(
        _memory_kv_kernel,
        out_shape=[jax.ShapeDtypeStruct((batch, MEM_WIDTH, n // batch), f32)] * 2,
        compiler_params=pltpu.CompilerParams(vmem_limit_bytes=VMEM_LIMIT),
        name="memory_kv",
    )(mem, lw["mem_norm"], lw["w_mem_kv"], lw["mem_k_norm"], lw["head_mean"])


def _mix_out_kernel(x_ref, fox_ref, gf_ref, ssm_ref, mq_ref, gm_ref, mkt_ref, mvt_ref, w_ref, o_ref):
    tm = x_ref.shape[0]
    nmem = mkt_ref.shape[2]
    mq = mq_ref[...]
    mkt = mkt_ref[0]
    mvt = mvt_ref[0]
    qhead = _lane_iota((tm, MEM_WIDTH)) // HEAD_DIM
    vhead = _row_iota((MEM_WIDTH, nmem)) // HEAD_DIM
    mem = jnp.zeros((tm, MEM_WIDTH), f32)
    for hd in range(N_MEM_HEADS):
        qh = jnp.where(qhead == hd, mq, jnp.zeros_like(mq))
        s = jnp.dot(qh, mkt, preferred_element_type=f32)
        p = jnp.exp(s - jnp.max(s, axis=-1, keepdims=True))
        p = p / jnp.sum(p, axis=-1, keepdims=True)
        vh = jnp.where(vhead == hd, mvt, jnp.zeros_like(mvt))
        mem = mem + lax.dot_general(p.astype(bf16), vh, (((1,), (1,)), ((), ())), preferred_element_type=f32)
    fox = (fox_ref[...] * gf_ref[...]).astype(bf16)
    ssm = jnp.concatenate([ssm_ref[0], ssm_ref[1]], axis=1).astype(bf16)
    memg = (mem * gm_ref[...]).astype(bf16)
    y = x_ref[...]
    y = y + jnp.dot(fox, w_ref[0:FOX_WIDTH, :], preferred_element_type=f32)
    y = y + jnp.dot(ssm, w_ref[FOX_WIDTH:FOX_WIDTH + SSM_WIDTH, :], preferred_element_type=f32)
    y = y + jnp.dot(memg, w_ref[FOX_WIDTH + SSM_WIDTH:, :], preferred_element_type=f32)
    o_ref[...] = y


def _mix_out(x, fox, gf, ssm, mq, gm, mkt, mvt, w_out, *, batch, tm):
    n, d = x.shape
    t = n // batch
    assert t % tm == 0
    tiles_per_batch = t // tm
    row = lambda w: pl.BlockSpec((tm, w), lambda i: (i, 0))
    memb = pl.BlockSpec((1,) + mkt.shape[1:], lambda i: (i // tiles_per_batch, 0, 0))
    return pl.pallas_call(
        _mix_out_kernel,
        grid=(n // tm,),
        in_specs=[row(d), row(FOX_WIDTH), row(FOX_WIDTH), pl.BlockSpec((2, tm, LANES), lambda i: (0, i, 0)),
                  row(MEM_WIDTH), row(MEM_WIDTH), memb, memb, pl.BlockSpec(w_out.shape, lambda i: (0, 0))],
        out_specs=row(d),
        out_shape=jax.ShapeDtypeStruct((n, d), f32),
        compiler_params=_cparams(("arbitrary",)),
        name="mix_out_prompt" if tiles_per_batch > 1 else "mix_out_sample",
    )(x, fox, gf, ssm, mq, gm, mkt, mvt, w_out)


def _layer_weights(l, norm_g, w_in, b_forget, fox_q_norm, fox_k_norm, mem_q_norm, mem_norm, w_mem_kv,
                   mem_k_norm, w_out, w_glu, b_glu, ssm_d):
    w = w_in[l]
    nf = 4 * FOX_WIDTH
    d = w.shape[0]
    w_packed = jnp.concatenate(
        [w[:, :nf], w[:, nf + N_FOX_HEADS:], w[:, nf:nf + N_FOX_HEADS],
         jnp.zeros((d, _W_COLS - w.shape[1]), w.dtype)], axis=1).astype(bf16)
    eq, ek = _bias_selectors()
    return dict(
        norm_g=norm_g[l].reshape(1, d), w_in=w_packed,
        b_forget=jnp.pad(b_forget[l], (0, LANES - N_FOX_HEADS)).reshape(1, LANES),
        fox_q_norm=jnp.tile(fox_q_norm[l], N_FOX_HEADS).reshape(1, FOX_WIDTH),
        fox_k_norm=jnp.tile(fox_k_norm[l], N_FOX_HEADS).reshape(1, FOX_WIDTH),
        mem_q_norm=jnp.tile(mem_q_norm[l], N_MEM_HEADS).reshape(1, MEM_WIDTH),
        mem_k_norm=jnp.tile(mem_k_norm[l], N_MEM_HEADS).reshape(1, MEM_WIDTH),
        mem_norm=mem_norm[l].reshape(1, d), w_mem_kv=w_mem_kv[l].astype(bf16),
        head_mean=_head_mean_matrix(), bias_eq=eq, bias_ek=ek,
        w_out=w_out[l].astype(bf16), w_glu=w_glu[l].astype(bf16), b_glu=b_glu[l].reshape(1, SSM_WIDTH),
        ssm_d=ssm_d[l].reshape(1, SSM_WIDTH))


def _split_state(hT, batch):
    hT = hT.reshape(batch, 2, SSM_GROUPS, SSM_STATE)
    return hT[:, 0], hT[:, 1]


def kernel(x_prompt, x_sample, mem_prompt, cache_fox_k, cache_fox_v, cache_fox_logf, state_ssm_re, state_ssm_im, cache_mem_k, cache_mem_v, norm_g, w_in, b_forget, fox_q_norm, fox_k_norm, ssm_a_re, ssm_a_im, ssm_log_dt, ssm_b_re, ssm_b_im, ssm_c_re, ssm_c_im, ssm_d, w_glu, b_glu, mem_norm, w_mem_kv, mem_q_norm, mem_k_norm, w_out):
    B, T, D = x_prompt.shape
    Bs, Ts, _ = x_sample.shape
    depth = w_in.shape[0]
    past = cache_fox_k.shape[2]
    nmem = mem_prompt.shape[1]
    M = SSM_MODES

    xp = x_prompt.reshape(B * T, D)
    xs = x_sample.reshape(Bs * Ts, D)
    mem = mem_prompt.reshape(B * nmem, D)
    cache_kt = jnp.transpose(cache_fox_k, (0, 1, 3, 4, 2)).reshape(depth, Bs, FOX_WIDTH, past)
    cache_vt = jnp.transpose(cache_fox_v, (0, 1, 3, 4, 2)).reshape(depth, Bs, FOX_WIDTH, past)
    cache_lt = jnp.transpose(cache_fox_logf, (0, 1, 3, 2)).astype(f32)
    cache_mkt = jnp.transpose(cache_mem_k, (0, 1, 3, 4, 2)).reshape(depth, Bs, MEM_WIDTH, nmem).astype(bf16)
    cache_mvt = jnp.transpose(cache_mem_v, (0, 1, 3, 4, 2)).reshape(depth, Bs, MEM_WIDTH, nmem).astype(bf16)

    outs = {k: [] for k in ("pk", "pv", "pf", "pre", "pim", "pmk", "pmv", "sk", "sv", "sf", "sre", "sim")}
    for l in range(depth):
        lw = _layer_weights(l, norm_g, w_in, b_forget, fox_q_norm, fox_k_norm, mem_q_norm, mem_norm,
                            w_mem_kv, mem_k_norm, w_out, w_glu, b_glu, ssm_d)
        ops = _s5_prep(ssm_a_re[l], ssm_a_im[l], ssm_log_dt[l], ssm_b_re[l], ssm_b_im[l],
                       ssm_c_re[l], ssm_c_im[l])
        mkt, mvt = _memory_kv(mem, lw, batch=B)

        pr = _in_proj(xp, lw, batch=B, prompt=True)
        fox = _fox_prompt(pr["qa"], pr["ka"], pr["va"]).reshape(B * T, FOX_WIDTH)
        ssm, hT = _s5(pr["u"], pr["gs"], jnp.zeros((B, 1, 2 * M), f32), ops, lw["ssm_d"], lw["w_glu"],
                      lw["b_glu"], batch=B, chained=True)
        xp = _mix_out(xp, fox, pr["gf"], ssm, pr["mq"], pr["gm"], mkt.astype(bf16), mvt.astype(bf16),
                      lw["w_out"], batch=B, tm=ROW_TILE)
        hre, him = _split_state(hT, B)
        outs["pk"].append(pr["kt"])
        outs["pv"].append(pr["vt"])
        outs["pf"].append(pr["logft"])
        outs["pre"].append(hre)
        outs["pim"].append(him)
        outs["pmk"].append(mkt.reshape(B, N_MEM_HEADS, HEAD_DIM, nmem))
        outs["pmv"].append(mvt.reshape(B, N_MEM_HEADS, HEAD_DIM, nmem))

        sr = _in_proj(xs, lw, batch=Bs, prompt=False)
        fox_s = _fox_sample(sr["q"], sr["k"], sr["v"], sr["logf"], cache_kt, cache_vt, cache_lt, l, batch=Bs)
        h0 = jnp.concatenate([state_ssm_re[l].reshape(Bs, M), state_ssm_im[l].reshape(Bs, M)],
                             axis=-1).astype(f32)
        ssm_s, hT_s = _s5(sr["u"], sr["gs"], h0, ops, lw["ssm_d"], lw["w_glu"], lw["b_glu"], batch=Bs,
                          chained=False)
        xs = _mix_out(xs, fox_s, sr["gf"], ssm_s, sr["mq"], sr["gm"], cache_mkt[l], cache_mvt[l],
                      lw["w_out"], batch=Bs, tm=Ts)
        sre, sim = _split_state(hT_s, Bs)
        outs["sk"].append(sr["k"].reshape(Bs, Ts, N_FOX_HEADS, HEAD_DIM))
        outs["sv"].append(sr["v"].reshape(Bs, Ts, N_FOX_HEADS, HEAD_DIM))
        outs["sf"].append(sr["logf"][:, :N_FOX_HEADS].reshape(Bs, Ts, N_FOX_HEADS))
        outs["sre"].append(sre)
        outs["sim"].append(sim)

    st = {k: jnp.stack(v) for k, v in outs.items()}
    fox_k_prompt = jnp.transpose(st["pk"], (0, 1, 4, 2, 3))
    fox_v_prompt = jnp.transpose(st["pv"], (0, 1, 4, 2, 3))
    fox_logf_prompt = jnp.transpose(st["pf"], (0, 1, 3, 2))
    mem_k_prompt = jnp.transpose(st["pmk"], (0, 1, 4, 2, 3))
    mem_v_prompt = jnp.transpose(st["pmv"], (0, 1, 4, 2, 3))
    return (xp.reshape(B, T, D), xs.reshape(Bs, Ts, D), fox_k_prompt, fox_v_prompt, fox_logf_prompt,
            st["pre"], st["pim"], mem_k_prompt, mem_v_prompt, st["sk"], st["sv"], st["sf"], st["sre"], st["sim"])
```

```python
import functools
import math

import numpy as np
import jax
import jax.numpy as jnp
from jax import lax
from jax.experimental import pallas as pl
from jax.experimental.pallas import tpu as pltpu

f32 = jnp.float32
bf16 = jnp.bfloat16

HEAD_DIM = 64
N_FOX_HEADS = 8
FOX_WIDTH = N_FOX_HEADS * HEAD_DIM
SSM_GROUPS = 16
SSM_CH = 16
SSM_STATE = 64
SSM_WIDTH = SSM_GROUPS * SSM_CH
SSM_MODES = SSM_GROUPS * SSM_STATE
N_MEM_HEADS = 4
MEM_WIDTH = N_MEM_HEADS * HEAD_DIM
EPS = 1e-6
QK_SCALE = HEAD_DIM ** -0.5
LOG2E = math.log2(math.e)

LANES = 128
S5_CHUNK = 8
S5_SEG = 16
ROW_TILE = 512
ATTN_TQ = 512
ATTN_UNROLL = 4
VMEM_LIMIT = 56 * 1024 * 1024

_C_Q, _C_K, _C_V, _C_G = 0, 512, 1024, 1536
_C_SU, _C_SG, _C_MQ, _C_MG, _C_F = 2048, 2304, 2560, 2816, 3072
_W_COLS = 3200
_N_PIECES = 3


def _cparams(sem):
    return pltpu.CompilerParams(dimension_semantics=sem, vmem_limit_bytes=VMEM_LIMIT)


def _silu(x):
    return x * jax.nn.sigmoid(x)


def _log_sigmoid(x):
    return -(jnp.maximum(-x, 0.0) + jnp.log1p(jnp.exp(-jnp.abs(x))))


def _gelu_tanh(x):
    return 0.5 * x * (1.0 + jnp.tanh(math.sqrt(2.0 / math.pi) * (x + 0.044715 * (x * x * x))))


def _split3(x):
    hi = x.astype(bf16).astype(f32)
    r = x - hi
    mid = r.astype(bf16).astype(f32)
    lo = (r - mid).astype(bf16).astype(f32)
    return hi, mid, lo


def _pack3(x):
    hi, mid, lo = _split3(x)
    return (hi + pltpu.roll(mid, 8, 1) + pltpu.roll(lo, 16, 1)).astype(bf16)


def _unpack3(y):
    return y + pltpu.roll(y, LANES - 8, 1) + pltpu.roll(y, LANES - 16, 1)


def _lane_iota(shape):
    return lax.broadcasted_iota(jnp.int32, shape, len(shape) - 1)


def _row_iota(shape):
    return lax.broadcasted_iota(jnp.int32, shape, len(shape) - 2)


def _head_norm(z, hm_ref, g_ref):
    ms = jnp.dot((z * z).astype(bf16), hm_ref[...], preferred_element_type=f32)
    return z * lax.rsqrt(ms + EPS) * g_ref[...]


def _in_proj_kernel(*refs, prompt, tiles_per_batch):
    if prompt:
        (x_ref, ng_ref, w_ref, bf_ref, qg_ref, kg_ref, mqg_ref, hm_ref, tril_ref, eq_ref, ek_ref,
         kt_ref, vt_ref, logft_ref, gf_ref, u_ref, gs_ref, mq_ref, gm_ref, qa_ref, ka_ref, va_ref,
         carry_ref) = refs
    else:
        (x_ref, ng_ref, w_ref, bf_ref, qg_ref, kg_ref, mqg_ref, hm_ref,
         k_ref, v_ref, logf_ref, gf_ref, u_ref, gs_ref, mq_ref, gm_ref, q_ref) = refs

    x = x_ref[...]
    tm = x.shape[0]
    lane = _lane_iota((tm, LANES))
    ms = jnp.mean(x * x, axis=-1, keepdims=True)
    h = (x * lax.rsqrt(ms + EPS) * ng_ref[...]).astype(bf16)

    def seg(lo, width):
        return jnp.dot(h, w_ref[:, lo:lo + width], preferred_element_type=f32)

    def halves(ref, z):
        ref[0] = z[:, :LANES]
        ref[1] = z[:, LANES:]

    q = _head_norm(seg(_C_Q, FOX_WIDTH), hm_ref, qg_ref) * QK_SCALE
    k = _head_norm(seg(_C_K, FOX_WIDTH), hm_ref, kg_ref)
    v = seg(_C_V, FOX_WIDTH)
    gf_ref[...] = _silu(seg(_C_G, FOX_WIDTH))
    halves(u_ref, seg(_C_SU, SSM_WIDTH))
    halves(gs_ref, _silu(seg(_C_SG, SSM_WIDTH)))
    mq = _head_norm(seg(_C_MQ, MEM_WIDTH), hm_ref.at[0:MEM_WIDTH, 0:MEM_WIDTH], mqg_ref) * QK_SCALE
    mq_ref[...] = mq.astype(bf16)
    gm_ref[...] = _silu(seg(_C_MG, MEM_WIDTH))

    logf = _log_sigmoid(seg(_C_F, LANES) + bf_ref[...])
    lf = jnp.where(lane < N_FOX_HEADS, logf, 0.0)

    if not prompt:
        k_ref[...] = k
        v_ref[...] = v
        logf_ref[...] = lf
        q_ref[...] = q.astype(bf16)
        return

    kt = k.T
    vt = v.T
    for hd in range(N_FOX_HEADS):
        kt_ref[0, hd] = kt[hd * HEAD_DIM:(hd + 1) * HEAD_DIM, :]
        vt_ref[0, hd] = vt[hd * HEAD_DIM:(hd + 1) * HEAD_DIM, :]
    logft_ref[0] = lf.T[0:N_FOX_HEADS, :]

    i = pl.program_id(0)

    @pl.when(i % tiles_per_batch == 0)
    def _():
        carry_ref[...] = jnp.zeros_like(carry_ref)

    c = _unpack3(jnp.dot(tril_ref[...], _pack3(lf), preferred_element_type=f32))
    c = jnp.where(lane < N_FOX_HEADS, c + carry_ref[0:1, :], 0.0)
    carry_ref[0:1, :] = c[tm - 1:tm, :]

    q = q * LOG2E
    pieces = (_pack3(c * LOG2E).astype(f32) + jnp.where(lane == 3 * N_FOX_HEADS, 1.0, 0.0)).astype(bf16)
    qaug = jnp.dot(pieces, eq_ref[...], preferred_element_type=f32)
    kaug = jnp.dot(pieces, ek_ref[...], preferred_element_type=f32)
    for hd in range(N_FOX_HEADS):
        pair = (hd // 2) * LANES
        own = (lane < HEAD_DIM) if hd % 2 == 0 else (lane >= HEAD_DIM)
        ones_lane = HEAD_DIM if hd % 2 == 0 else 0
        blk = slice(hd * LANES, (hd + 1) * LANES)
        qa_ref[0, hd] = jnp.where(own, q[:, pair:pair + LANES], qaug[:, blk]).astype(bf16)
        ka_ref[0, hd] = jnp.where(own, k[:, pair:pair + LANES], kaug[:, blk]).astype(bf16)
        va_ref[0, hd] = jnp.where(own, v[:, pair:pair + LANES],
                                  jnp.where(lane == ones_lane, 1.0, 0.0)).astype(bf16)


def _bias_selectors():
    eq = np.zeros((LANES, N_FOX_HEADS * LANES), np.float32)
    ek = np.zeros((LANES, N_FOX_HEADS * LANES), np.float32)
    one = 3 * N_FOX_HEADS
    for hd in range(N_FOX_HEADS):
        base = hd * LANES + (HEAD_DIM if hd % 2 == 0 else 0)
        for p in range(_N_PIECES):
            eq[p * N_FOX_HEADS + hd, base + p] = 1.0
            eq[one, base + _N_PIECES + p] = 1.0
            ek[one, base + p] = 1.0
            ek[p * N_FOX_HEADS + hd, base + _N_PIECES + p] = -1.0
    return jnp.asarray(eq, bf16), jnp.asarray(ek, bf16)


def _head_mean_matrix():
    idx = np.arange(FOX_WIDTH) // HEAD_DIM
    return jnp.asarray((idx[:, None] == idx[None, :]).astype(np.float32) / HEAD_DIM, bf16)


def _in_proj(x, lw, *, batch, prompt):
    n, d = x.shape
    t = n // batch
    tm = ROW_TILE if prompt else n
    assert n % tm == 0 and (t % tm == 0 or not prompt)
    tiles_per_batch = t // tm if prompt else 1
    grid = (n // tm,)
    row = lambda w: pl.BlockSpec((tm, w), lambda i: (i, 0))
    full = lambda a: pl.BlockSpec(a.shape, lambda i: (0,) * a.ndim)
    split = pl.BlockSpec((2, tm, LANES), lambda i: (0, i, 0))
    by_batch = lambda i: (i // tiles_per_batch, 0, i % tiles_per_batch, 0)

    ins = [x, lw["norm_g"], lw["w_in"], lw["b_forget"], lw["fox_q_norm"], lw["fox_k_norm"],
           lw["mem_q_norm"], lw["head_mean"]]
    in_specs = [row(d)] + [full(a) for a in ins[1:]]
    common = [((n, FOX_WIDTH), f32, row(FOX_WIDTH)), ((2, n, LANES), f32, split), ((2, n, LANES), f32, split),
              ((n, MEM_WIDTH), bf16, row(MEM_WIDTH)), ((n, MEM_WIDTH), f32, row(MEM_WIDTH))]
    scratch = []
    if prompt:
        tril = jnp.asarray(np.tril(np.ones((tm, tm), np.float32)), bf16)
        extra = [tril, lw["bias_eq"], lw["bias_ek"]]
        ins += extra
        in_specs += [full(a) for a in extra]
        tspec = pl.BlockSpec((1, N_FOX_HEADS, HEAD_DIM, tm),
                             lambda i: (i // tiles_per_batch, 0, 0, i % tiles_per_batch))
        lspec = pl.BlockSpec((1, N_FOX_HEADS, tm), lambda i: (i // tiles_per_batch, 0, i % tiles_per_batch))
        hspec = pl.BlockSpec((1, N_FOX_HEADS, tm, LANES), by_batch)
        tshape = (batch, N_FOX_HEADS, HEAD_DIM, t)
        outs = ([(tshape, f32, tspec), (tshape, f32, tspec), ((batch, N_FOX_HEADS, t), f32, lspec)] + common
                + [((batch, N_FOX_HEADS, t, LANES), bf16, hspec)] * 3)
        names = ["kt", "vt", "logft", "gf", "u", "gs", "mq", "gm", "qa", "ka", "va"]
        scratch = [pltpu.VMEM((8, LANES), f32)]
    else:
        outs = ([((n, FOX_WIDTH), f32, row(FOX_WIDTH)), ((n, FOX_WIDTH), f32, row(FOX_WIDTH)),
                 ((n, LANES), f32, row(LANES))] + common + [((n, FOX_WIDTH), bf16, row(FOX_WIDTH))])
        names = ["k", "v", "logf", "gf", "u", "gs", "mq", "gm", "q"]

    res = pl.pallas_call(
        functools.partial(_in_proj_kernel, prompt=prompt, tiles_per_batch=tiles_per_batch),
        grid=grid, in_specs=in_specs, out_specs=[o[2] for o in outs],
        out_shape=[jax.ShapeDtypeStruct(o[0], o[1]) for o in outs],
        scratch_shapes=scratch,
        compiler_params=_cparams(("arbitrary",)),
        name="in_proj_prompt" if prompt else "in_proj_sample",
    )(*ins)
    return dict(zip(names, res))


def _fox_prompt_kernel(qa_ref, ka_ref, va_ref, o_ref, s0_ref, s1_ref, m_ref, acc_ref, *, nq):
    tq = tk = ATTN_TQ
    s_refs = (s0_ref, s1_ref)
    m_ref[...] = jnp.full(m_ref.shape, -jnp.inf, f32)
    acc_ref[...] = jnp.zeros(acc_ref.shape, f32)

    def scores(qb, kb, slot):
        qoff = pl.multiple_of(qb * tq, tq)
        koff = pl.multiple_of(kb * tk, tk)
        for j in range(2):
            s_refs[slot][j] = lax.dot_general(qa_ref[0, j, pl.ds(qoff, tq), :], ka_ref[0, j, pl.ds(koff, tk), :],
                                              (((1,), (1,)), ((), ())), preferred_element_type=f32)

    def softmax_pv(qb, kb, slot, masked):
        koff = pl.multiple_of(kb * tk, tk)
        for j in range(2):
            s = s_refs[slot][j]
            if masked:
                s = jnp.where(_row_iota((tq, tk)) >= _lane_iota((tq, tk)), s, -jnp.inf)
            m = m_ref[qb, j]
            m_new = jnp.maximum(m, jnp.max(s, axis=-1, keepdims=True))
            p = jnp.exp2(s - jnp.concatenate([m_new] * (tk // LANES), axis=1))
            vv = va_ref[0, j, pl.ds(koff, tk), :]
            acc_ref[qb, j] = (jnp.exp2(m - m_new) * acc_ref[qb, j]
                              + jnp.dot(p.astype(bf16), vv, preferred_element_type=f32))
            m_ref[qb, j] = m_new

    def finalize(qb):
        h0 = acc_ref[qb, 0] / acc_ref[qb, 0, :, HEAD_DIM:HEAD_DIM + 1]
        h1 = acc_ref[qb, 1] / acc_ref[qb, 1, :, 0:1]
        o_ref[0, pl.ds(pl.multiple_of(qb * tq, tq), tq), :] = jnp.where(
            _lane_iota((tq, LANES)) < HEAD_DIM, h0, h1)

    def after(qb, kb):
        row_end = kb + 1 >= qb
        last = jnp.logical_and(row_end, qb + 1 >= nq)
        nqb = jnp.where(last, 0, jnp.where(row_end, qb + 1, qb))
        nkb = jnp.where(row_end, 0, kb + 1)
        return nqb, nkb

    def lower_step(qb, kb, slot):
        nqb, nkb = after(qb, kb)
        scores(nqb, nkb, 1 - slot)
        softmax_pv(qb, kb, slot, False)
        return nqb, nkb

    def diag_step(qb, slot):
        nxt = jnp.minimum(qb + 1, nq - 1)
        scores(nxt, nxt, 1 - slot)
        softmax_pv(qb, qb, slot, True)
        finalize(qb)

    def steps_per_trip(n):
        return max(d for d in range(2, ATTN_UNROLL + 1, 2) if n % d == 0)

    n_lower = nq * (nq - 1) // 2
    if n_lower:
        scores(1, 0, 0)
        per = steps_per_trip(n_lower)

        def lower_body(i, carry):
            for k in range(per):
                carry = lower_step(*carry, k % 2)
            return carry

        lax.fori_loop(0, n_lower // per, lower_body, (jnp.int32(1), jnp.int32(0)))
    else:
        scores(0, 0, 0)

    per_diag = steps_per_trip(nq)

    def diag_body(i, carry):
        for k in range(per_diag):
            diag_step(per_diag * i + k, k % 2)
        return carry

    lax.fori_loop(0, nq // per_diag, diag_body, 0)


def _fox_prompt(qa, ka, va):
    b, nh, t, _ = qa.shape
    nq = t // ATTN_TQ
    assert t % ATTN_TQ == 0 and nq % 2 == 0 and (nq * (nq - 1) // 2) % 2 == 0
    whole = pl.BlockSpec((1, 2, t, LANES), lambda bi, hp: (bi, hp, 0, 0))
    return pl.pallas_call(
        functools.partial(_fox_prompt_kernel, nq=nq),
        grid=(b, nh // 2),
        in_specs=[whole, whole, whole],
        out_specs=pl.BlockSpec((1, t, LANES), lambda bi, hp: (bi, 0, hp)),
        out_shape=jax.ShapeDtypeStruct((b, t, FOX_WIDTH), f32),
        scratch_shapes=[pltpu.VMEM((2, ATTN_TQ, ATTN_TQ), f32), pltpu.VMEM((2, ATTN_TQ, ATTN_TQ), f32),
                        pltpu.VMEM((nq, 2, ATTN_TQ, LANES), f32), pltpu.VMEM((nq, 2, ATTN_TQ, LANES), f32)],
        compiler_params=_cparams(("arbitrary", "arbitrary")),
        name="fox_prompt",
    )(qa, ka, va)


def _fox_sample_kernel(q_ref, kn_ref, vn_ref, ln_ref, kc_ref, vc_ref, lc_ref, triu_ref, o_ref,
                       kall_ref, vall_ref, *, past, tnew):
    nkeys = kall_ref.shape[1]
    nq = N_FOX_HEADS * tnew

    def new_cols(ref):
        z = jnp.concatenate([ref[...], jnp.zeros((LANES - tnew, ref.shape[1]), f32)], axis=0)
        return z.T

    kall_ref[:, 0:past] = kc_ref[0, 0].astype(bf16)
    vall_ref[:, 0:past] = vc_ref[0, 0].astype(bf16)
    kall_ref[:, past:nkeys] = new_cols(kn_ref).astype(bf16)
    vall_ref[:, past:nkeys] = new_cols(vn_ref).astype(bf16)

    l_all = jnp.concatenate([lc_ref[0, 0], new_cols(ln_ref)[0:N_FOX_HEADS, :]], axis=1)
    pieces = jnp.concatenate(_split3(l_all), axis=0).astype(bf16)
    c3 = jnp.dot(pieces, triu_ref[...], preferred_element_type=f32)
    c = c3[0:8] + c3[8:16] + c3[16:24]
    crow = jnp.concatenate([jnp.broadcast_to(c[hd:hd + 1, :], (tnew, nkeys)) for hd in range(N_FOX_HEADS)],
                           axis=0)
    rq = _row_iota((nq, LANES))
    cq = jnp.sum(jnp.where(_lane_iota((nq, LANES)) == rq % tnew, crow[:, past:nkeys], 0.0),
                 axis=1, keepdims=True)

    qrep = jnp.concatenate([q_ref[...]] * N_FOX_HEADS, axis=0)
    own = _row_iota((nq, FOX_WIDTH)) // tnew == _lane_iota((nq, FOX_WIDTH)) // HEAD_DIM
    qbd = jnp.where(own, qrep, jnp.zeros_like(qrep))
    s = jnp.dot(qbd, kall_ref[...], preferred_element_type=f32) + cq - crow
    s = jnp.where(_lane_iota((nq, nkeys)) <= past + _row_iota((nq, nkeys)) % tnew, s, -jnp.inf)
    p = jnp.exp(s - jnp.max(s, axis=1, keepdims=True))
    o2 = lax.dot_general(p.astype(bf16), vall_ref[...], (((1,), (1,)), ((), ())), preferred_element_type=f32)
    o2 = o2 / jnp.sum(p, axis=1, keepdims=True)
    olane = _lane_iota((tnew, FOX_WIDTH))
    out = jnp.zeros((tnew, FOX_WIDTH), f32)
    for hd in range(N_FOX_HEADS):
        out = out + jnp.where(olane // HEAD_DIM == hd, o2[hd * tnew:(hd + 1) * tnew, :], 0.0)
    o_ref[...] = out


def _fox_sample(q, k_new, v_new, logf_new, cache_kt, cache_vt, cache_lt, layer, *, batch):
    n = q.shape[0]
    tnew = n // batch
    past = cache_kt.shape[3]
    nkeys = past + LANES
    assert N_FOX_HEADS * tnew == LANES and past % LANES == 0
    triu = jnp.asarray(np.triu(np.ones((nkeys, nkeys), np.float32)), bf16)
    rows = lambda w: pl.BlockSpec((tnew, w), lambda bi: (bi, 0))
    cache = lambda h: pl.BlockSpec((1, 1, h, past), lambda bi: (layer, bi, 0, 0))
    return pl.pallas_call(
        functools.partial(_fox_sample_kernel, past=past, tnew=tnew),
        grid=(batch,),
        in_specs=[rows(FOX_WIDTH), rows(FOX_WIDTH), rows(FOX_WIDTH), rows(LANES),
                  cache(FOX_WIDTH), cache(FOX_WIDTH), cache(N_FOX_HEADS),
                  pl.BlockSpec((nkeys, nkeys), lambda bi: (0, 0))],
        out_specs=rows(FOX_WIDTH),
        out_shape=jax.ShapeDtypeStruct((n, FOX_WIDTH), f32),
        scratch_shapes=[pltpu.VMEM((FOX_WIDTH, nkeys), bf16), pltpu.VMEM((FOX_WIDTH, nkeys), bf16)],
        compiler_params=_cparams(("arbitrary",)),
        name="fox_sample",
    )(q, k_new, v_new, logf_new, cache_kt, cache_vt, cache_lt, triu)


def _s5_prep_kernel(are_r, aim_r, ldt_r, are_c, aim_c, ldt_c, bxr_ref, bxi_ref, cxr_ref, cxi_ref,
                    bst_ref, cst_ref, kt_ref, apow_ref):
    L, W, M = S5_CHUNK, SSM_WIDTH, SSM_MODES

    def cmul(xr, xi, yr, yi):
        return xr * yr - xi * yi, xr * yi + xi * yr

    def powers(are, aim, ldt, n):
        dt = jnp.exp(ldt[...])
        mag = jnp.exp(dt * are[...])
        p1 = (mag * jnp.cos(dt * aim[...]), mag * jnp.sin(dt * aim[...]))
        out = [(jnp.ones_like(mag), jnp.zeros_like(mag)), p1]
        for _ in range(n - 1):
            out.append(cmul(*out[-1], *p1))
        return out

    prow = powers(are_r, aim_r, ldt_r, L)
    pcol = powers(are_c, aim_c, ldt_c, L)
    ar, ai = are_r[...], aim_r[...]
    abr, abi = prow[1]
    den = ar * ar + ai * ai
    zr = ((abr - 1.0) * ar + abi * ai) / den
    zi = (abi * ar - (abr - 1.0) * ai) / den
    bbr, bbi = cmul(zr, zi, bxr_ref[...], bxi_ref[...])

    def b_pow(k):
        return jnp.concatenate(cmul(*prow[k], bbr, bbi), axis=1)

    def c_pow(k):
        pr, pi = pcol[k]
        cr, ci = cxr_ref[...], cxi_ref[...]
        return jnp.concatenate([cr * pr - ci * pi, -(cr * pi + ci * pr)], axis=0)

    cpow = [c_pow(k) for k in range(L + 1)]
    for tl in range(L):
        bst_ref[tl * W:(tl + 1) * W, :] = b_pow(L - 1 - tl).astype(bf16)
        cst_ref[:, tl * W:(tl + 1) * W] = cpow[tl + 1].astype(bf16)

    def hi_lo(x):
        hi = x.astype(bf16)
        return hi, (x - hi.astype(f32)).astype(bf16)

    bh, bl = hi_lo(b_pow(0))
    ch, cl = hi_lo(jnp.concatenate(cpow[:L], axis=1))
    kd = (jnp.dot(bh, ch, preferred_element_type=f32) + jnp.dot(bh, cl, preferred_element_type=f32)
          + jnp.dot(bl, ch, preferred_element_type=f32)).astype(bf16)
    for tl in range(L):
        if tl:
            kt_ref[tl * W:(tl + 1) * W, 0:tl * W] = jnp.zeros((W, tl * W), bf16)
        kt_ref[tl * W:(tl + 1) * W, tl * W:] = kd[:, :(L - tl) * W]

    step = prow[L]
    seg = step
    for _ in range(int(math.log2(S5_SEG))):
        seg = cmul(*seg, *seg)
    apow_ref[...] = jnp.concatenate(
        [jnp.concatenate(step, axis=1), jnp.concatenate(seg, axis=1), jnp.zeros((6, 2 * M), f32)], axis=0)


def _s5_prep(a_re, a_im, log_dt, b_re, b_im, c_re, c_im):
    G, N, P, L = SSM_GROUPS, SSM_STATE, SSM_CH, S5_CHUNK
    W, M = SSM_WIDTH, SSM_MODES
    eye = jnp.eye(G, dtype=f32)
    bx = lambda b: (jnp.swapaxes(b, 1, 2)[:, :, None, :] * eye[:, None, :, None]).reshape(W, M)
    cx = lambda c: (jnp.swapaxes(c, 1, 2)[:, :, None, :] * eye[:, None, :, None]).reshape(M, W)
    ldt = jnp.repeat(log_dt, N)
    ins = [a_re.reshape(1, M), a_im.reshape(1, M), ldt.reshape(1, M),
           a_re.reshape(M, 1), a_im.reshape(M, 1), ldt.reshape(M, 1),
           bx(b_re), bx(b_im), cx(c_re), cx(c_im)]
    return pl.pallas_call(
        _s5_prep_kernel,
        out_shape=[jax.ShapeDtypeStruct((L * W, 2 * M), bf16), jax.ShapeDtypeStruct((2 * M, L * W), bf16),
                   jax.ShapeDtypeStruct((L * W, L * W), bf16), jax.ShapeDtypeStruct((8, 2 * M), f32)],
        compiler_params=pltpu.CompilerParams(vmem_limit_bytes=VMEM_LIMIT),
        name="s5_prep",
    )(*ins)


def _s5_kernel(u_ref, gs_ref, h0_ref, bst_ref, cst_ref, kt_ref, apow_ref, d_ref, wg_ref, bg_ref,
               o_ref, hT_ref, ds_ref, sp_ref, carry_ref, *, chained, seglen):
    L, W, M = S5_CHUNK, SSM_WIDTH, SSM_MODES
    nr = 8 * seglen
    nlb = M // LANES

    def chunked(ref, tl):
        return jnp.concatenate([ref[0, pl.ds(tl, nr, stride=L), :], ref[1, pl.ds(tl, nr, stride=L), :]], axis=1)

    uf = [chunked(u_ref, tl) for tl in range(L)]
    ub = jnp.concatenate(uf, axis=1).astype(bf16)
    ds = jnp.dot(ub, bst_ref[...], preferred_element_type=f32)
    for lb in range(2 * nlb):
        ds_ref[lb] = ds[:, lb * LANES:(lb + 1) * LANES]
    def blocks(row):
        return ([row[:, lb * LANES:(lb + 1) * LANES] for lb in range(nlb)],
                [row[:, M + lb * LANES:M + (lb + 1) * LANES] for lb in range(nlb)])

    def bcast8(parts):
        return [jnp.broadcast_to(p, (8, LANES)) for p in parts]

    ar, ai = (bcast8(p) for p in blocks(apow_ref[0:1, :]))

    def scan(vr, vi, store):
        for i in range(seglen):
            rows = pl.ds(i, 8, stride=seglen)
            for lb in range(nlb):
                if store:
                    sp_ref[lb, rows, :] = vr[lb]
                    sp_ref[nlb + lb, rows, :] = vi[lb]
                dr, di = ds_ref[lb, rows, :], ds_ref[nlb + lb, rows, :]
                vr[lb], vi[lb] = (ar[lb] * vr[lb] - ai[lb] * vi[lb] + dr,
                                  ar[lb] * vi[lb] + ai[lb] * vr[lb] + di)
        return vr, vi

    if chained:
        ti = pl.program_id(1)

        @pl.when(ti == 0)
        def _():
            carry_ref[0:1, :] = h0_ref[0]

        zero = [jnp.zeros((8, LANES), f32) for _ in range(nlb)]
        er, ei = scan(list(zero), list(zero), store=False)
        pr, pi = blocks(apow_ref[1:2, :])
        cr, ci = blocks(carry_ref[0:1, :])
        sr, si = [[] for _ in range(nlb)], [[] for _ in range(nlb)]
        for j in range(8):
            for lb in range(nlb):
                sr[lb].append(cr[lb])
                si[lb].append(ci[lb])
                cr[lb], ci[lb] = (pr[lb] * cr[lb] - pi[lb] * ci[lb] + er[lb][j:j + 1, :],
                                  pr[lb] * ci[lb] + pi[lb] * cr[lb] + ei[lb][j:j + 1, :])
        final = jnp.concatenate(cr + ci, axis=1)
        carry_ref[0:1, :] = final
        hT_ref[0] = final
        scan([jnp.concatenate(x, axis=0) for x in sr], [jnp.concatenate(x, axis=0) for x in si], store=True)
    else:
        vr, vi = blocks(h0_ref[...])
        vr, vi = scan(vr, vi, store=True)
        hT_ref[...] = jnp.concatenate(vr + vi, axis=1)

    spb = jnp.concatenate([sp_ref[lb] for lb in range(2 * nlb)], axis=1).astype(bf16)

    def y_pair(pair):
        kk = (2 * pair + 2) * W
        cols = slice(2 * pair * W, kk)
        return (jnp.dot(ub[:, :kk], kt_ref[0:kk, cols], preferred_element_type=f32)
                + jnp.dot(spb, cst_ref[:, cols], preferred_element_type=f32))

    def finish(pair, ypair):
        for tl in (2 * pair, 2 * pair + 1):
            y = ypair[:, (tl % 2) * W:(tl % 2 + 1) * W] + d_ref[...] * uf[tl]
            g = _gelu_tanh(y)
            gate = jax.nn.sigmoid(jnp.dot(g.astype(bf16), wg_ref[...], preferred_element_type=f32) + bg_ref[...])
            out = g * gate * chunked(gs_ref, tl)
            o_ref[0, pl.ds(tl, nr, stride=L), :] = out[:, :LANES]
            o_ref[1, pl.ds(tl, nr, stride=L), :] = out[:, LANES:]

    ycur = y_pair(0)
    for pair in range(L // 2):
        ynext = y_pair(pair + 1) if pair + 1 < L // 2 else None
        finish(pair, ycur)
        ycur = ynext


def _s5(u, gs, h0, ops, d_row, w_glu, b_glu, *, batch, chained):
    L, W, M = S5_CHUNK, SSM_WIDTH, SSM_MODES
    n = u.shape[1]
    t = n // batch
    assert t % L == 0
    chunks = t // L
    bst, cst, kt, apow = ops
    if chained:
        seglen = S5_SEG
        assert chunks % (8 * seglen) == 0
        grid = (batch, chunks // (8 * seglen))
        state = pl.BlockSpec((1, 1, 2 * M), lambda bi, ti: (bi, 0, 0))
    else:
        seglen = chunks
        assert batch == 8
        grid = (1, 1)
        state = pl.BlockSpec((8, 2 * M), lambda bi, ti: (0, 0))
    tiles = grid[1]
    nr = 8 * seglen
    blk = pl.BlockSpec((2, nr * L, LANES), lambda bi, ti: (0, bi * tiles + ti, 0))
    full = lambda a: pl.BlockSpec(a.shape, lambda bi, ti: (0,) * a.ndim)
    once = lambda a: pl.BlockSpec(a.shape, lambda bi, ti: (0,) * a.ndim, pipeline_mode=pl.Buffered(1))
    return pl.pallas_call(
        functools.partial(_s5_kernel, chained=chained, seglen=seglen),
        grid=grid,
        in_specs=[blk, blk, state, once(bst), once(cst), once(kt), full(apow), full(d_row),
                  full(w_glu), full(b_glu)],
        out_specs=[blk, state],
        out_shape=[jax.ShapeDtypeStruct((2, n, LANES), f32), jax.ShapeDtypeStruct(h0.shape, f32)],
        scratch_shapes=[pltpu.VMEM((2 * M // LANES, nr, LANES), f32), pltpu.VMEM((2 * M // LANES, nr, LANES), f32),
                        pltpu.VMEM((8, 2 * M), f32)],
        compiler_params=_cparams(("arbitrary", "arbitrary")),
        name="s5_prompt" if chained else "s5_sample",
    )(u, gs, h0, bst, cst, kt, apow, d_row, w_glu, b_glu)


def _memory_kv_kernel(mem_ref, g_ref, w_ref, kg_ref, hm_ref, mkt_ref, mvt_ref):
    batch, _, nmem = mkt_ref.shape
    x = mem_ref[...]
    ms = jnp.mean(x * x, axis=-1, keepdims=True)
    h = (x * lax.rsqrt(ms + EPS) * g_ref[...]).astype(bf16)
    mk = jnp.dot(h, w_ref[:, 0:MEM_WIDTH], preferred_element_type=f32)
    mk = _head_norm(mk, hm_ref.at[0:MEM_WIDTH, 0:MEM_WIDTH], kg_ref)
    mv = jnp.dot(h, w_ref[:, MEM_WIDTH:2 * MEM_WIDTH], preferred_element_type=f32)
    for b in range(batch):
        mkt_ref[b] = mk[b * nmem:(b + 1) * nmem, :].T
        mvt_ref[b] = mv[b * nmem:(b + 1) * nmem, :].T


def _memory_kv(mem, lw, *, batch):
    n = mem.shape[0]
    return pl.pallas_call(
        _memory_kv_kernel,
        out_shape=[jax.ShapeDtypeStruct((batch, MEM_WIDTH, n // batch), f32)] * 2,
        compiler_params=pltpu.CompilerParams(vmem_limit_bytes=VMEM_LIMIT),
        name="memory_kv",
    )(mem, lw["mem_norm"], lw["w_mem_kv"], lw["mem_k_norm"], lw["head_mean"])


def _mix_out_kernel(x_ref, fox_ref, gf_ref, ssm_ref, mq_ref, gm_ref, mkt_ref, mvt_ref, w_ref, o_ref):
    tm = x_ref.shape[0]
    nmem = mkt_ref.shape[2]
    mq = mq_ref[...]
    mkt = mkt_ref[0]
    mvt = mvt_ref[0]
    khead = _row_iota((MEM_WIDTH, nmem)) // HEAD_DIM
    zero = jnp.zeros_like(mkt)
    mk_heads = jnp.concatenate([jnp.where(khead == hd, mkt, zero) for hd in range(N_MEM_HEADS)], axis=1)
    s_all = jnp.dot(mq, mk_heads, preferred_element_type=f32)
    fox = (fox_ref[...] * gf_ref[...]).astype(bf16)
    ssm = jnp.concatenate([ssm_ref[0], ssm_ref[1]], axis=1).astype(bf16)
    y = x_ref[...]
    y = y + jnp.dot(fox, w_ref[0:FOX_WIDTH, :], preferred_element_type=f32)
    y = y + jnp.dot(ssm, w_ref[FOX_WIDTH:FOX_WIDTH + SSM_WIDTH, :], preferred_element_type=f32)
    mem = jnp.zeros((tm, MEM_WIDTH), f32)
    for hd in range(N_MEM_HEADS):
        s = s_all[:, hd * nmem:(hd + 1) * nmem]
        p = jnp.exp(s - jnp.max(s, axis=-1, keepdims=True))
        p = p / jnp.sum(p, axis=-1, keepdims=True)
        vh = jnp.where(khead == hd, mvt, zero)
        mem = mem + lax.dot_general(p.astype(bf16), vh, (((1,), (1,)), ((), ())), preferred_element_type=f32)
    memg = (mem * gm_ref[...]).astype(bf16)
    o_ref[...] = y + jnp.dot(memg, w_ref[FOX_WIDTH + SSM_WIDTH:, :], preferred_element_type=f32)


def _mix_out(x, fox, gf, ssm, mq, gm, mkt, mvt, w_out, *, batch, tm):
    n, d = x.shape
    t = n // batch
    assert t % tm == 0
    tiles_per_batch = t // tm
    row = lambda w: pl.BlockSpec((tm, w), lambda i: (i, 0))
    memb = pl.BlockSpec((1,) + mkt.shape[1:], lambda i: (i // tiles_per_batch, 0, 0))
    return pl.pallas_call(
        _mix_out_kernel,
        grid=(n // tm,),
        in_specs=[row(d), row(FOX_WIDTH), row(FOX_WIDTH), pl.BlockSpec((2, tm, LANES), lambda i: (0, i, 0)),
                  row(MEM_WIDTH), row(MEM_WIDTH), memb, memb, pl.BlockSpec(w_out.shape, lambda i: (0, 0))],
        out_specs=row(d),
        out_shape=jax.ShapeDtypeStruct((n, d), f32),
        compiler_params=_cparams(("arbitrary",)),
        name="mix_out_prompt" if tiles_per_batch > 1 else "mix_out_sample",
    )(x, fox, gf, ssm, mq, gm, mkt, mvt, w_out)


def _layer_weights(l, norm_g, w_in, b_forget, fox_q_norm, fox_k_norm, mem_q_norm, mem_norm, w_mem_kv,
                   mem_k_norm, w_out, w_glu, b_glu, ssm_d):
    w = w_in[l]
    nf = 4 * FOX_WIDTH
    d = w.shape[0]
    w_packed = jnp.concatenate(
        [w[:, :nf], w[:, nf + N_FOX_HEADS:], w[:, nf:nf + N_FOX_HEADS],
         jnp.zeros((d, _W_COLS - w.shape[1]), w.dtype)], axis=1).astype(bf16)
    eq, ek = _bias_selectors()
    return dict(
        norm_g=norm_g[l].reshape(1, d), w_in=w_packed,
        b_forget=jnp.pad(b_forget[l], (0, LANES - N_FOX_HEADS)).reshape(1, LANES),
        fox_q_norm=jnp.tile(fox_q_norm[l], N_FOX_HEADS).reshape(1, FOX_WIDTH),
        fox_k_norm=jnp.tile(fox_k_norm[l], N_FOX_HEADS).reshape(1, FOX_WIDTH),
        mem_q_norm=jnp.tile(mem_q_norm[l], N_MEM_HEADS).reshape(1, MEM_WIDTH),
        mem_k_norm=jnp.tile(mem_k_norm[l], N_MEM_HEADS).reshape(1, MEM_WIDTH),
        mem_norm=mem_norm[l].reshape(1, d), w_mem_kv=w_mem_kv[l].astype(bf16),
        head_mean=_head_mean_matrix(), bias_eq=eq, bias_ek=ek,
        w_out=w_out[l].astype(bf16), w_glu=w_glu[l].astype(bf16), b_glu=b_glu[l].reshape(1, SSM_WIDTH),
        ssm_d=ssm_d[l].reshape(1, SSM_WIDTH))


def _split_state(hT, batch):
    hT = hT.reshape(batch, 2, SSM_GROUPS, SSM_STATE)
    return hT[:, 0], hT[:, 1]


def kernel(x_prompt, x_sample, mem_prompt, cache_fox_k, cache_fox_v, cache_fox_logf, state_ssm_re, state_ssm_im, cache_mem_k, cache_mem_v, norm_g, w_in, b_forget, fox_q_norm, fox_k_norm, ssm_a_re, ssm_a_im, ssm_log_dt, ssm_b_re, ssm_b_im, ssm_c_re, ssm_c_im, ssm_d, w_glu, b_glu, mem_norm, w_mem_kv, mem_q_norm, mem_k_norm, w_out):
    B, T, D = x_prompt.shape
    Bs, Ts, _ = x_sample.shape
    depth = w_in.shape[0]
    past = cache_fox_k.shape[2]
    nmem = mem_prompt.shape[1]
    M = SSM_MODES

    xp = x_prompt.reshape(B * T, D)
    xs = x_sample.reshape(Bs * Ts, D)
    mem = mem_prompt.reshape(B * nmem, D)
    cache_kt = jnp.transpose(cache_fox_k, (0, 1, 3, 4, 2)).reshape(depth, Bs, FOX_WIDTH, past)
    cache_vt = jnp.transpose(cache_fox_v, (0, 1, 3, 4, 2)).reshape(depth, Bs, FOX_WIDTH, past)
    cache_lt = jnp.transpose(cache_fox_logf, (0, 1, 3, 2)).astype(f32)
    cache_mkt = jnp.transpose(cache_mem_k, (0, 1, 3, 4, 2)).reshape(depth, Bs, MEM_WIDTH, nmem).astype(bf16)
    cache_mvt = jnp.transpose(cache_mem_v, (0, 1, 3, 4, 2)).reshape(depth, Bs, MEM_WIDTH, nmem).astype(bf16)

    outs = {k: [] for k in ("pk", "pv", "pf", "pre", "pim", "pmk", "pmv", "sk", "sv", "sf", "sre", "sim")}
    for l in range(depth):
        lw = _layer_weights(l, norm_g, w_in, b_forget, fox_q_norm, fox_k_norm, mem_q_norm, mem_norm,
                            w_mem_kv, mem_k_norm, w_out, w_glu, b_glu, ssm_d)
        ops = _s5_prep(ssm_a_re[l], ssm_a_im[l], ssm_log_dt[l], ssm_b_re[l], ssm_b_im[l],
                       ssm_c_re[l], ssm_c_im[l])
        mkt, mvt = _memory_kv(mem, lw, batch=B)

        pr = _in_proj(xp, lw, batch=B, prompt=True)
        fox = _fox_prompt(pr["qa"], pr["ka"], pr["va"]).reshape(B * T, FOX_WIDTH)
        ssm, hT = _s5(pr["u"], pr["gs"], jnp.zeros((B, 1, 2 * M), f32), ops, lw["ssm_d"], lw["w_glu"],
                      lw["b_glu"], batch=B, chained=True)
        xp = _mix_out(xp, fox, pr["gf"], ssm, pr["mq"], pr["gm"], mkt.astype(bf16), mvt.astype(bf16),
                      lw["w_out"], batch=B, tm=ROW_TILE)
        hre, him = _split_state(hT, B)
        outs["pk"].append(pr["kt"])
        outs["pv"].append(pr["vt"])
        outs["pf"].append(pr["logft"])
        outs["pre"].append(hre)
        outs["pim"].append(him)
        outs["pmk"].append(mkt.reshape(B, N_MEM_HEADS, HEAD_DIM, nmem))
        outs["pmv"].append(mvt.reshape(B, N_MEM_HEADS, HEAD_DIM, nmem))

        sr = _in_proj(xs, lw, batch=Bs, prompt=False)
        fox_s = _fox_sample(sr["q"], sr["k"], sr["v"], sr["logf"], cache_kt, cache_vt, cache_lt, l, batch=Bs)
        h0 = jnp.concatenate([state_ssm_re[l].reshape(Bs, M), state_ssm_im[l].reshape(Bs, M)],
                             axis=-1).astype(f32)
        ssm_s, hT_s = _s5(sr["u"], sr["gs"], h0, ops, lw["ssm_d"], lw["w_glu"], lw["b_glu"], batch=Bs,
                          chained=False)
        xs = _mix_out(xs, fox_s, sr["gf"], ssm_s, sr["mq"], sr["gm"], cache_mkt[l], cache_mvt[l],
                      lw["w_out"], batch=Bs, tm=Ts)
        sre, sim = _split_state(hT_s, Bs)
        outs["sk"].append(sr["k"].reshape(Bs, Ts, N_FOX_HEADS, HEAD_DIM))
        outs["sv"].append(sr["v"].reshape(Bs, Ts, N_FOX_HEADS, HEAD_DIM))
        outs["sf"].append(sr["logf"][:, :N_FOX_HEADS].reshape(Bs, Ts, N_FOX_HEADS))
        outs["sre"].append(sre)
        outs["sim"].append(sim)

    st = {k: jnp.stack(v) for k, v in outs.items()}
    fox_k_prompt = jnp.transpose(st["pk"], (0, 1, 4, 2, 3))
    fox_v_prompt = jnp.transpose(st["pv"], (0, 1, 4, 2, 3))
    fox_logf_prompt = jnp.transpose(st["pf"], (0, 1, 3, 2))
    mem_k_prompt = jnp.transpose(st["pmk"], (0, 1, 4, 2, 3))
    mem_v_prompt = jnp.transpose(st["pmv"], (0, 1, 4, 2, 3))
    return (xp.reshape(B, T, D), xs.reshape(Bs, Ts, D), fox_k_prompt, fox_v_prompt, fox_logf_prompt,
            st["pre"], st["pim"], mem_k_prompt, mem_v_prompt, st["sk"], st["sv"], st["sf"], st["sre"], st["sim"])
```

```python
import functools
import math

import numpy as np
import jax
import jax.numpy as jnp
from jax import lax
from jax.experimental import pallas as pl
from jax.experimental.pallas import tpu as pltpu

f32 = jnp.float32
bf16 = jnp.bfloat16

HEAD_DIM = 64
N_FOX_HEADS = 8
FOX_WIDTH = N_FOX_HEADS * HEAD_DIM
SSM_GROUPS = 16
SSM_CH = 16
SSM_STATE = 64
SSM_WIDTH = SSM_GROUPS * SSM_CH
SSM_MODES = SSM_GROUPS * SSM_STATE
N_MEM_HEADS = 4
MEM_WIDTH = N_MEM_HEADS * HEAD_DIM
EPS = 1e-6
QK_SCALE = HEAD_DIM ** -0.5
LOG2E = math.log2(math.e)

LANES = 128
S5_CHUNK = 8
S5_SEG = 16
ROW_TILE = 512
ATTN_TQ = 512
ATTN_UNROLL = 4
VMEM_LIMIT = 56 * 1024 * 1024

_C_Q, _C_F, _C_K, _C_V, _C_G = 0, 512, 640, 1152, 1664
_C_SU, _C_SG, _C_MQ, _C_MG = 2176, 2432, 2688, 2944
_W_COLS = 3200
_N_PIECES = 3


def _cparams(sem):
    return pltpu.CompilerParams(dimension_semantics=sem, vmem_limit_bytes=VMEM_LIMIT)


def _silu(x):
    return x * jax.nn.sigmoid(x)


def _log_sigmoid(x):
    return -(jnp.maximum(-x, 0.0) + jnp.log1p(jnp.exp(-jnp.abs(x))))


def _gelu_tanh(x):
    return 0.5 * x * (1.0 + jnp.tanh(math.sqrt(2.0 / math.pi) * (x + 0.044715 * (x * x * x))))


def _split3(x):
    hi = x.astype(bf16).astype(f32)
    r = x - hi
    mid = r.astype(bf16).astype(f32)
    lo = (r - mid).astype(bf16).astype(f32)
    return hi, mid, lo


def _pack3(x):
    hi, mid, lo = _split3(x)
    return (hi + pltpu.roll(mid, 8, 1) + pltpu.roll(lo, 16, 1)).astype(bf16)


def _unpack3(y):
    return y + pltpu.roll(y, LANES - 8, 1) + pltpu.roll(y, LANES - 16, 1)


def _lane_iota(shape):
    return lax.broadcasted_iota(jnp.int32, shape, len(shape) - 1)


def _row_iota(shape):
    return lax.broadcasted_iota(jnp.int32, shape, len(shape) - 2)


def _head_norm(z, hm_ref, g_ref):
    ms = jnp.dot((z * z).astype(bf16), hm_ref[...], preferred_element_type=f32)
    return z * lax.rsqrt(ms + EPS) * g_ref[...]


def _in_proj_kernel(*refs, prompt, tiles_per_batch):
    if prompt:
        (x_ref, ng_ref, w_ref, bf_ref, qg_ref, kg_ref, mqg_ref, hm_ref, tril_ref, eb_ref,
         kt_ref, vt_ref, logft_ref, gf_ref, u_ref, gs_ref, mq_ref, gm_ref, qa_ref, ka_ref, va_ref,
         carry_ref) = refs
    else:
        (x_ref, ng_ref, w_ref, bf_ref, qg_ref, kg_ref, mqg_ref, hm_ref,
         k_ref, v_ref, logf_ref, gf_ref, u_ref, gs_ref, mq_ref, gm_ref, q_ref) = refs

    x = x_ref[...]
    tm = x.shape[0]
    lane = _lane_iota((tm, LANES))
    ms = jnp.mean(x * x, axis=-1, keepdims=True)
    h = (x * lax.rsqrt(ms + EPS) * ng_ref[...]).astype(bf16)

    def seg(lo, width):
        return jnp.dot(h, w_ref[:, lo:lo + width], preferred_element_type=f32)

    def halves(ref, z):
        ref[0] = z[:, :LANES]
        ref[1] = z[:, LANES:]

    zqf = seg(_C_Q, FOX_WIDTH + LANES)
    logf = _log_sigmoid(zqf[:, FOX_WIDTH:] + bf_ref[...])
    lf = jnp.where(lane < N_FOX_HEADS, logf, 0.0)
    zk = seg(_C_K, FOX_WIDTH)
    if prompt:
        i = pl.program_id(0)

        @pl.when(i % tiles_per_batch == 0)
        def _():
            carry_ref[...] = jnp.zeros_like(carry_ref)

        csum = jnp.dot(tril_ref[...], _pack3(lf), preferred_element_type=f32)
    v = seg(_C_V, FOX_WIDTH)
    if prompt:
        c = jnp.where(lane < N_FOX_HEADS, _unpack3(csum) + carry_ref[0:1, :], 0.0)
        carry_ref[0:1, :] = c[tm - 1:tm, :]
        pieces = (_pack3(c * LOG2E).astype(f32) + jnp.where(lane == 3 * N_FOX_HEADS, 1.0, 0.0)).astype(bf16)
    gf_ref[...] = _silu(seg(_C_G, FOX_WIDTH))
    if prompt:
        aug = jnp.dot(pieces, eb_ref[...], preferred_element_type=f32)
    zs = seg(_C_SU, 2 * SSM_WIDTH)
    zm = seg(_C_MQ, 2 * MEM_WIDTH)
    q = _head_norm(zqf[:, :FOX_WIDTH], hm_ref, qg_ref) * QK_SCALE
    k = _head_norm(zk, hm_ref, kg_ref)
    halves(u_ref, zs[:, :SSM_WIDTH])
    halves(gs_ref, _silu(zs[:, SSM_WIDTH:]))
    mq = _head_norm(zm[:, :MEM_WIDTH], hm_ref.at[0:MEM_WIDTH, 0:MEM_WIDTH], mqg_ref) * QK_SCALE
    mq_ref[...] = mq.astype(bf16)
    gm_ref[...] = _silu(zm[:, MEM_WIDTH:])

    if not prompt:
        k_ref[...] = k
        v_ref[...] = v
        logf_ref[...] = lf
        q_ref[...] = q.astype(bf16)
        return

    kt = k.T
    vt = v.T
    for hd in range(N_FOX_HEADS):
        kt_ref[0, hd] = kt[hd * HEAD_DIM:(hd + 1) * HEAD_DIM, :]
        vt_ref[0, hd] = vt[hd * HEAD_DIM:(hd + 1) * HEAD_DIM, :]
    logft_ref[0] = lf.T[0:N_FOX_HEADS, :]

    q = q * LOG2E
    qaug, kaug = aug[:, :LANES], aug[:, LANES:]
    for hd in range(N_FOX_HEADS):
        pair = (hd // 2) * LANES
        own = (lane < HEAD_DIM) if hd % 2 == 0 else (lane >= HEAD_DIM)
        ones_lane = HEAD_DIM if hd % 2 == 0 else 0
        g0 = _bias_group(hd)
        bias = (lane >= g0) & (lane < g0 + 2 * _N_PIECES)
        qa_ref[0, hd] = jnp.where(own, q[:, pair:pair + LANES], jnp.where(bias, qaug, 0.0)).astype(bf16)
        ka_ref[0, hd] = jnp.where(own, k[:, pair:pair + LANES], jnp.where(bias, kaug, 0.0)).astype(bf16)
        va_ref[0, hd] = jnp.where(own, v[:, pair:pair + LANES],
                                  jnp.where(lane == ones_lane, 1.0, 0.0)).astype(bf16)


def _bias_group(hd):
    return (HEAD_DIM if hd % 2 == 0 else 0) + 16 * (hd // 2)


def _bias_selector():
    eb = np.zeros((LANES, 2 * LANES), np.float32)
    one = 3 * N_FOX_HEADS
    for hd in range(N_FOX_HEADS):
        g0 = _bias_group(hd)
        for p in range(_N_PIECES):
            eb[p * N_FOX_HEADS + hd, g0 + p] = 1.0
            eb[one, g0 + _N_PIECES + p] = 1.0
            eb[one, LANES + g0 + p] = 1.0
            eb[p * N_FOX_HEADS + hd, LANES + g0 + _N_PIECES + p] = -1.0
    return jnp.asarray(eb, bf16)


def _head_mean_matrix():
    idx = np.arange(FOX_WIDTH) // HEAD_DIM
    return jnp.asarray((idx[:, None] == idx[None, :]).astype(np.float32) / HEAD_DIM, bf16)


def _in_proj(x, lw, *, batch, prompt):
    n, d = x.shape
    t = n // batch
    tm = ROW_TILE if prompt else n
    assert n % tm == 0 and (t % tm == 0 or not prompt)
    tiles_per_batch = t // tm if prompt else 1
    grid = (n // tm,)
    row = lambda w: pl.BlockSpec((tm, w), lambda i: (i, 0))
    full = lambda a: pl.BlockSpec(a.shape, lambda i: (0,) * a.ndim)
    split = pl.BlockSpec((2, tm, LANES), lambda i: (0, i, 0))
    by_batch = lambda i: (i // tiles_per_batch, 0, i % tiles_per_batch, 0)

    ins = [x, lw["norm_g"], lw["w_in"], lw["b_forget"], lw["fox_q_norm"], lw["fox_k_norm"],
           lw["mem_q_norm"], lw["head_mean"]]
    in_specs = [row(d)] + [full(a) for a in ins[1:]]
    common = [((n, FOX_WIDTH), f32, row(FOX_WIDTH)), ((2, n, LANES), f32, split), ((2, n, LANES), f32, split),
              ((n, MEM_WIDTH), bf16, row(MEM_WIDTH)), ((n, MEM_WIDTH), f32, row(MEM_WIDTH))]
    scratch = []
    if prompt:
        tril = jnp.asarray(np.tril(np.ones((tm, tm), np.float32)), bf16)
        extra = [tril, lw["bias_sel"]]
        ins += extra
        in_specs += [full(a) for a in extra]
        tspec = pl.BlockSpec((1, N_FOX_HEADS, HEAD_DIM, tm),
                             lambda i: (i // tiles_per_batch, 0, 0, i % tiles_per_batch))
        lspec = pl.BlockSpec((1, N_FOX_HEADS, tm), lambda i: (i // tiles_per_batch, 0, i % tiles_per_batch))
        hspec = pl.BlockSpec((1, N_FOX_HEADS, tm, LANES), by_batch)
        tshape = (batch, N_FOX_HEADS, HEAD_DIM, t)
        outs = ([(tshape, f32, tspec), (tshape, f32, tspec), ((batch, N_FOX_HEADS, t), f32, lspec)] + common
                + [((batch, N_FOX_HEADS, t, LANES), bf16, hspec)] * 3)
        names = ["kt", "vt", "logft", "gf", "u", "gs", "mq", "gm", "qa", "ka", "va"]
        scratch = [pltpu.VMEM((8, LANES), f32)]
    else:
        outs = ([((n, FOX_WIDTH), f32, row(FOX_WIDTH)), ((n, FOX_WIDTH), f32, row(FOX_WIDTH)),
                 ((n, LANES), f32, row(LANES))] + common + [((n, FOX_WIDTH), bf16, row(FOX_WIDTH))])
        names = ["k", "v", "logf", "gf", "u", "gs", "mq", "gm", "q"]

    res = pl.pallas_call(
        functools.partial(_in_proj_kernel, prompt=prompt, tiles_per_batch=tiles_per_batch),
        grid=grid, in_specs=in_specs, out_specs=[o[2] for o in outs],
        out_shape=[jax.ShapeDtypeStruct(o[0], o[1]) for o in outs],
        scratch_shapes=scratch,
        compiler_params=_cparams(("arbitrary",)),
        name="in_proj_prompt" if prompt else "in_proj_sample",
    )(*ins)
    return dict(zip(names, res))


def _fox_prompt_kernel(qa_ref, ka_ref, va_ref, gate_ref, o_ref, s0_ref, s1_ref, m_ref, acc_ref, *, nq):
    tq = tk = ATTN_TQ
    s_refs = (s0_ref, s1_ref)
    m_ref[...] = jnp.full(m_ref.shape, -jnp.inf, f32)
    acc_ref[...] = jnp.zeros(acc_ref.shape, f32)

    def scores(qb, kb, slot):
        qoff = pl.multiple_of(qb * tq, tq)
        koff = pl.multiple_of(kb * tk, tk)
        for j in range(2):
            s_refs[slot][j] = lax.dot_general(qa_ref[0, j, pl.ds(qoff, tq), :], ka_ref[0, j, pl.ds(koff, tk), :],
                                              (((1,), (1,)), ((), ())), preferred_element_type=f32)

    def softmax_pv(qb, kb, slot, masked):
        koff = pl.multiple_of(kb * tk, tk)
        for j in range(2):
            s = s_refs[slot][j]
            if masked:
                s = jnp.where(_row_iota((tq, tk)) >= _lane_iota((tq, tk)), s, -jnp.inf)
            m = m_ref[qb, j]
            m_new = jnp.maximum(m, jnp.max(s, axis=-1, keepdims=True))
            p = jnp.exp2(s - jnp.concatenate([m_new] * (tk // LANES), axis=1))
            vv = va_ref[0, j, pl.ds(koff, tk), :]
            acc_ref[qb, j] = (jnp.exp2(m - m_new) * acc_ref[qb, j]
                              + jnp.dot(p.astype(bf16), vv, preferred_element_type=f32))
            m_ref[qb, j] = m_new

    def finalize(qb):
        h0 = acc_ref[qb, 0] / acc_ref[qb, 0, :, HEAD_DIM:HEAD_DIM + 1]
        h1 = acc_ref[qb, 1] / acc_ref[qb, 1, :, 0:1]
        rows = pl.ds(pl.multiple_of(qb * tq, tq), tq)
        out = jnp.where(_lane_iota((tq, LANES)) < HEAD_DIM, h0, h1) * gate_ref[0, rows, :]
        o_ref[0, rows, :] = out.astype(bf16)

    def after(qb, kb):
        row_end = kb + 1 >= qb
        last = jnp.logical_and(row_end, qb + 1 >= nq)
        nqb = jnp.where(last, 0, jnp.where(row_end, qb + 1, qb))
        nkb = jnp.where(row_end, 0, kb + 1)
        return nqb, nkb

    def lower_step(qb, kb, slot):
        nqb, nkb = after(qb, kb)
        scores(nqb, nkb, 1 - slot)
        softmax_pv(qb, kb, slot, False)
        return nqb, nkb

    def diag_step(qb, slot):
        nxt = jnp.minimum(qb + 1, nq - 1)
        scores(nxt, nxt, 1 - slot)
        softmax_pv(qb, qb, slot, True)
        finalize(qb)

    def steps_per_trip(n):
        return max(d for d in range(2, ATTN_UNROLL + 1, 2) if n % d == 0)

    n_lower = nq * (nq - 1) // 2
    if n_lower:
        scores(1, 0, 0)
        per = steps_per_trip(n_lower)

        def lower_body(i, carry):
            for k in range(per):
                carry = lower_step(*carry, k % 2)
            return carry

        lax.fori_loop(0, n_lower // per, lower_body, (jnp.int32(1), jnp.int32(0)))
    else:
        scores(0, 0, 0)

    per_diag = steps_per_trip(nq)

    def diag_body(i, carry):
        for k in range(per_diag):
            diag_step(per_diag * i + k, k % 2)
        return carry

    lax.fori_loop(0, nq // per_diag, diag_body, 0)


def _fox_prompt(qa, ka, va, gate):
    b, nh, t, _ = qa.shape
    nq = t // ATTN_TQ
    assert t % ATTN_TQ == 0 and nq % 2 == 0 and (nq * (nq - 1) // 2) % 2 == 0
    whole = pl.BlockSpec((1, 2, t, LANES), lambda bi, hp: (bi, hp, 0, 0))
    return pl.pallas_call(
        functools.partial(_fox_prompt_kernel, nq=nq),
        grid=(b, nh // 2),
        in_specs=[whole, whole, whole, pl.BlockSpec((1, t, LANES), lambda bi, hp: (bi, 0, hp))],
        out_specs=pl.BlockSpec((1, t, LANES), lambda bi, hp: (bi, 0, hp)),
        out_shape=jax.ShapeDtypeStruct((b, t, FOX_WIDTH), bf16),
        scratch_shapes=[pltpu.VMEM((2, ATTN_TQ, ATTN_TQ), f32), pltpu.VMEM((2, ATTN_TQ, ATTN_TQ), f32),
                        pltpu.VMEM((nq, 2, ATTN_TQ, LANES), f32), pltpu.VMEM((nq, 2, ATTN_TQ, LANES), f32)],
        compiler_params=_cparams(("arbitrary", "arbitrary")),
        name="fox_prompt",
    )(qa, ka, va, gate)


def _fox_sample_kernel(q_ref, kn_ref, vn_ref, ln_ref, gate_ref, kc_ref, vc_ref, lc_ref, triu_ref, o_ref,
                       kall_ref, vall_ref, *, past, tnew):
    nkeys = kall_ref.shape[1]
    nq = N_FOX_HEADS * tnew

    def new_cols(ref):
        z = jnp.concatenate([ref[...], jnp.zeros((LANES - tnew, ref.shape[1]), f32)], axis=0)
        return z.T

    kall_ref[:, 0:past] = kc_ref[0, 0].astype(bf16)
    vall_ref[:, 0:past] = vc_ref[0, 0].astype(bf16)
    kall_ref[:, past:nkeys] = new_cols(kn_ref).astype(bf16)
    vall_ref[:, past:nkeys] = new_cols(vn_ref).astype(bf16)

    l_all = jnp.concatenate([lc_ref[0, 0], new_cols(ln_ref)[0:N_FOX_HEADS, :]], axis=1)
    pieces = jnp.concatenate(_split3(l_all), axis=0).astype(bf16)
    c3 = jnp.dot(pieces, triu_ref[...], preferred_element_type=f32)
    c = c3[0:8] + c3[8:16] + c3[16:24]
    crow = jnp.concatenate([jnp.broadcast_to(c[hd:hd + 1, :], (tnew, nkeys)) for hd in range(N_FOX_HEADS)],
                           axis=0)
    rq = _row_iota((nq, LANES))
    cq = jnp.sum(jnp.where(_lane_iota((nq, LANES)) == rq % tnew, crow[:, past:nkeys], 0.0),
                 axis=1, keepdims=True)

    qrep = jnp.concatenate([q_ref[...]] * N_FOX_HEADS, axis=0)
    own = _row_iota((nq, FOX_WIDTH)) // tnew == _lane_iota((nq, FOX_WIDTH)) // HEAD_DIM
    qbd = jnp.where(own, qrep, jnp.zeros_like(qrep))
    s = jnp.dot(qbd, kall_ref[...], preferred_element_type=f32) + cq - crow
    s = jnp.where(_lane_iota((nq, nkeys)) <= past + _row_iota((nq, nkeys)) % tnew, s, -jnp.inf)
    p = jnp.exp(s - jnp.max(s, axis=1, keepdims=True))
    o2 = lax.dot_general(p.astype(bf16), vall_ref[...], (((1,), (1,)), ((), ())), preferred_element_type=f32)
    o2 = o2 / jnp.sum(p, axis=1, keepdims=True)
    olane = _lane_iota((tnew, FOX_WIDTH))
    out = jnp.zeros((tnew, FOX_WIDTH), f32)
    for hd in range(N_FOX_HEADS):
        out = out + jnp.where(olane // HEAD_DIM == hd, o2[hd * tnew:(hd + 1) * tnew, :], 0.0)
    o_ref[...] = (out * gate_ref[...]).astype(bf16)


def _fox_sample(q, k_new, v_new, logf_new, gate, cache_kt, cache_vt, cache_lt, layer, *, batch):
    n = q.shape[0]
    tnew = n // batch
    past = cache_kt.shape[3]
    nkeys = past + LANES
    assert N_FOX_HEADS * tnew == LANES and past % LANES == 0
    triu = jnp.asarray(np.triu(np.ones((nkeys, nkeys), np.float32)), bf16)
    rows = lambda w: pl.BlockSpec((tnew, w), lambda bi: (bi, 0))
    cache = lambda h: pl.BlockSpec((1, 1, h, past), lambda bi: (layer, bi, 0, 0))
    return pl.pallas_call(
        functools.partial(_fox_sample_kernel, past=past, tnew=tnew),
        grid=(batch,),
        in_specs=[rows(FOX_WIDTH), rows(FOX_WIDTH), rows(FOX_WIDTH), rows(LANES), rows(FOX_WIDTH),
                  cache(FOX_WIDTH), cache(FOX_WIDTH), cache(N_FOX_HEADS),
                  pl.BlockSpec((nkeys, nkeys), lambda bi: (0, 0))],
        out_specs=rows(FOX_WIDTH),
        out_shape=jax.ShapeDtypeStruct((n, FOX_WIDTH), bf16),
        scratch_shapes=[pltpu.VMEM((FOX_WIDTH, nkeys), bf16), pltpu.VMEM((FOX_WIDTH, nkeys), bf16)],
        compiler_params=_cparams(("arbitrary",)),
        name="fox_sample",
    )(q, k_new, v_new, logf_new, gate, cache_kt, cache_vt, cache_lt, triu)


def _s5_prep_kernel(are_r, aim_r, ldt_r, are_c, aim_c, ldt_c, bxr_ref, bxi_ref, cxr_ref, cxi_ref,
                    bst_ref, cst_ref, kt_ref, apow_ref):
    L, W, M = S5_CHUNK, SSM_WIDTH, SSM_MODES

    def cmul(xr, xi, yr, yi):
        return xr * yr - xi * yi, xr * yi + xi * yr

    def powers(are, aim, ldt, n):
        dt = jnp.exp(ldt[...])
        mag = jnp.exp(dt * are[...])
        p1 = (mag * jnp.cos(dt * aim[...]), mag * jnp.sin(dt * aim[...]))
        out = [(jnp.ones_like(mag), jnp.zeros_like(mag)), p1]
        for _ in range(n - 1):
            out.append(cmul(*out[-1], *p1))
        return out

    prow = powers(are_r, aim_r, ldt_r, L)
    pcol = powers(are_c, aim_c, ldt_c, L)
    ar, ai = are_r[...], aim_r[...]
    abr, abi = prow[1]
    den = ar * ar + ai * ai
    zr = ((abr - 1.0) * ar + abi * ai) / den
    zi = (abi * ar - (abr - 1.0) * ai) / den
    bbr, bbi = cmul(zr, zi, bxr_ref[...], bxi_ref[...])

    def b_pow(k):
        return jnp.concatenate(cmul(*prow[k], bbr, bbi), axis=1)

    def c_pow(k):
        pr, pi = pcol[k]
        cr, ci = cxr_ref[...], cxi_ref[...]
        return jnp.concatenate([cr * pr - ci * pi, -(cr * pi + ci * pr)], axis=0)

    cpow = [c_pow(k) for k in range(L + 1)]
    for tl in range(L):
        bst_ref[tl * W:(tl + 1) * W, :] = b_pow(L - 1 - tl).astype(bf16)
        cst_ref[:, tl * W:(tl + 1) * W] = cpow[tl + 1].astype(bf16)

    def hi_lo(x):
        hi = x.astype(bf16)
        return hi, (x - hi.astype(f32)).astype(bf16)

    bh, bl = hi_lo(b_pow(0))
    ch, cl = hi_lo(jnp.concatenate(cpow[:L], axis=1))
    kd = (jnp.dot(bh, ch, preferred_element_type=f32) + jnp.dot(bh, cl, preferred_element_type=f32)
          + jnp.dot(bl, ch, preferred_element_type=f32)).astype(bf16)
    for tl in range(L):
        if tl:
            kt_ref[tl * W:(tl + 1) * W, 0:tl * W] = jnp.zeros((W, tl * W), bf16)
        kt_ref[tl * W:(tl + 1) * W, tl * W:] = kd[:, :(L - tl) * W]

    step = prow[L]
    seg = step
    for _ in range(int(math.log2(S5_SEG))):
        seg = cmul(*seg, *seg)
    apow_ref[...] = jnp.concatenate(
        [jnp.concatenate(step, axis=1), jnp.concatenate(seg, axis=1), jnp.zeros((6, 2 * M), f32)], axis=0)


def _s5_prep(a_re, a_im, log_dt, b_re, b_im, c_re, c_im):
    G, N, P, L = SSM_GROUPS, SSM_STATE, SSM_CH, S5_CHUNK
    W, M = SSM_WIDTH, SSM_MODES
    eye = jnp.eye(G, dtype=f32)
    bx = lambda b: (jnp.swapaxes(b, 1, 2)[:, :, None, :] * eye[:, None, :, None]).reshape(W, M)
    cx = lambda c: (jnp.swapaxes(c, 1, 2)[:, :, None, :] * eye[:, None, :, None]).reshape(M, W)
    ldt = jnp.repeat(log_dt, N)
    ins = [a_re.reshape(1, M), a_im.reshape(1, M), ldt.reshape(1, M),
           a_re.reshape(M, 1), a_im.reshape(M, 1), ldt.reshape(M, 1),
           bx(b_re), bx(b_im), cx(c_re), cx(c_im)]
    return pl.pallas_call(
        _s5_prep_kernel,
        out_shape=[jax.ShapeDtypeStruct((L * W, 2 * M), bf16), jax.ShapeDtypeStruct((2 * M, L * W), bf16),
                   jax.ShapeDtypeStruct((L * W, L * W), bf16), jax.ShapeDtypeStruct((8, 2 * M), f32)],
        compiler_params=pltpu.CompilerParams(vmem_limit_bytes=VMEM_LIMIT),
        name="s5_prep",
    )(*ins)


def _s5_kernel(u_ref, gs_ref, h0_ref, bst_ref, cst_ref, kt_ref, apow_ref, d_ref, wg_ref, bg_ref,
               o_ref, hT_ref, ds_ref, sp_ref, carry_ref, *, chained, seglen):
    L, W, M = S5_CHUNK, SSM_WIDTH, SSM_MODES
    nr = 8 * seglen
    nlb = M // LANES

    def chunked(ref, tl):
        return jnp.concatenate([ref[0, pl.ds(tl, nr, stride=L), :], ref[1, pl.ds(tl, nr, stride=L), :]], axis=1)

    uf = [chunked(u_ref, tl) for tl in range(L)]
    ub = jnp.concatenate(uf, axis=1).astype(bf16)
    ds = jnp.dot(ub, bst_ref[...], preferred_element_type=f32)
    for lb in range(2 * nlb):
        ds_ref[lb] = ds[:, lb * LANES:(lb + 1) * LANES]
    def blocks(row):
        return ([row[:, lb * LANES:(lb + 1) * LANES] for lb in range(nlb)],
                [row[:, M + lb * LANES:M + (lb + 1) * LANES] for lb in range(nlb)])

    def bcast8(parts):
        return [jnp.broadcast_to(p, (8, LANES)) for p in parts]

    ar, ai = (bcast8(p) for p in blocks(apow_ref[0:1, :]))

    def scan(vr, vi, store):
        for i in range(seglen):
            rows = pl.ds(i, 8, stride=seglen)
            for lb in range(nlb):
                if store:
                    sp_ref[lb, rows, :] = vr[lb]
                    sp_ref[nlb + lb, rows, :] = vi[lb]
                dr, di = ds_ref[lb, rows, :], ds_ref[nlb + lb, rows, :]
                vr[lb], vi[lb] = (ar[lb] * vr[lb] - ai[lb] * vi[lb] + dr,
                                  ar[lb] * vi[lb] + ai[lb] * vr[lb] + di)
        return vr, vi

    if chained:
        ti = pl.program_id(1)

        @pl.when(ti == 0)
        def _():
            carry_ref[0:1, :] = h0_ref[0]

        zero = [jnp.zeros((8, LANES), f32) for _ in range(nlb)]
        er, ei = scan(list(zero), list(zero), store=False)
        pr, pi = blocks(apow_ref[1:2, :])
        cr, ci = blocks(carry_ref[0:1, :])
        sr, si = [[] for _ in range(nlb)], [[] for _ in range(nlb)]
        for j in range(8):
            for lb in range(nlb):
                sr[lb].append(cr[lb])
                si[lb].append(ci[lb])
                cr[lb], ci[lb] = (pr[lb] * cr[lb] - pi[lb] * ci[lb] + er[lb][j:j + 1, :],
                                  pr[lb] * ci[lb] + pi[lb] * cr[lb] + ei[lb][j:j + 1, :])
        final = jnp.concatenate(cr + ci, axis=1)
        carry_ref[0:1, :] = final
        hT_ref[0] = final
        scan([jnp.concatenate(x, axis=0) for x in sr], [jnp.concatenate(x, axis=0) for x in si], store=True)
    else:
        vr, vi = blocks(h0_ref[...])
        vr, vi = scan(vr, vi, store=True)
        hT_ref[...] = jnp.concatenate(vr + vi, axis=1)

    spb = jnp.concatenate([sp_ref[lb] for lb in range(2 * nlb)], axis=1).astype(bf16)

    def y_pair(pair):
        kk = (2 * pair + 2) * W
        cols = slice(2 * pair * W, kk)
        return (jnp.dot(ub[:, :kk], kt_ref[0:kk, cols], preferred_element_type=f32)
                + jnp.dot(spb, cst_ref[:, cols], preferred_element_type=f32))

    def finish(pair, ypair):
        for tl in (2 * pair, 2 * pair + 1):
            y = ypair[:, (tl % 2) * W:(tl % 2 + 1) * W] + d_ref[...] * uf[tl]
            g = _gelu_tanh(y)
            gate = jax.nn.sigmoid(jnp.dot(g.astype(bf16), wg_ref[...], preferred_element_type=f32) + bg_ref[...])
            out = g * gate * chunked(gs_ref, tl)
            o_ref[0, pl.ds(tl, nr, stride=L), :] = out[:, :LANES]
            o_ref[1, pl.ds(tl, nr, stride=L), :] = out[:, LANES:]

    ycur = y_pair(0)
    for pair in range(L // 2):
        ynext = y_pair(pair + 1) if pair + 1 < L // 2 else None
        finish(pair, ycur)
        ycur = ynext


def _s5(u, gs, h0, ops, d_row, w_glu, b_glu, *, batch, chained):
    L, W, M = S5_CHUNK, SSM_WIDTH, SSM_MODES
    n = u.shape[1]
    t = n // batch
    assert t % L == 0
    chunks = t // L
    bst, cst, kt, apow = ops
    if chained:
        seglen = S5_SEG
        assert chunks % (8 * seglen) == 0
        grid = (batch, chunks // (8 * seglen))
        state = pl.BlockSpec((1, 1, 2 * M), lambda bi, ti: (bi, 0, 0))
    else:
        seglen = chunks
        assert batch == 8
        grid = (1, 1)
        state = pl.BlockSpec((8, 2 * M), lambda bi, ti: (0, 0))
    tiles = grid[1]
    nr = 8 * seglen
    blk = pl.BlockSpec((2, nr * L, LANES), lambda bi, ti: (0, bi * tiles + ti, 0))
    full = lambda a: pl.BlockSpec(a.shape, lambda bi, ti: (0,) * a.ndim)
    once = lambda a: pl.BlockSpec(a.shape, lambda bi, ti: (0,) * a.ndim, pipeline_mode=pl.Buffered(1))
    return pl.pallas_call(
        functools.partial(_s5_kernel, chained=chained, seglen=seglen),
        grid=grid,
        in_specs=[blk, blk, state, once(bst), once(cst), once(kt), full(apow), full(d_row),
                  full(w_glu), full(b_glu)],
        out_specs=[blk, state],
        out_shape=[jax.ShapeDtypeStruct((2, n, LANES), f32), jax.ShapeDtypeStruct(h0.shape, f32)],
        scratch_shapes=[pltpu.VMEM((2 * M // LANES, nr, LANES), f32), pltpu.VMEM((2 * M // LANES, nr, LANES), f32),
                        pltpu.VMEM((8, 2 * M), f32)],
        compiler_params=_cparams(("arbitrary", "arbitrary")),
        name="s5_prompt" if chained else "s5_sample",
    )(u, gs, h0, bst, cst, kt, apow, d_row, w_glu, b_glu)


def _memory_kv_kernel(mem_ref, g_ref, w_ref, kg_ref, hm_ref, mkt_ref, mvt_ref):
    batch, _, nmem = mkt_ref.shape
    x = mem_ref[...]
    ms = jnp.mean(x * x, axis=-1, keepdims=True)
    h = (x * lax.rsqrt(ms + EPS) * g_ref[...]).astype(bf16)
    mk = jnp.dot(h, w_ref[:, 0:MEM_WIDTH], preferred_element_type=f32)
    mk = _head_norm(mk, hm_ref.at[0:MEM_WIDTH, 0:MEM_WIDTH], kg_ref)
    mv = jnp.dot(h, w_ref[:, MEM_WIDTH:2 * MEM_WIDTH], preferred_element_type=f32)
    for b in range(batch):
        mkt_ref[b] = mk[b * nmem:(b + 1) * nmem, :].T
        mvt_ref[b] = mv[b * nmem:(b + 1) * nmem, :].T


def _memory_kv(mem, lw, *, batch):
    n = mem.shape[0]
    return pl.pallas_call(
        _memory_kv_kernel,
        out_shape=[jax.ShapeDtypeStruct((batch, MEM_WIDTH, n // batch), f32)] * 2,
        compiler_params=pltpu.CompilerParams(vmem_limit_bytes=VMEM_LIMIT),
        name="memory_kv",
    )(mem, lw["mem_norm"], lw["w_mem_kv"], lw["mem_k_norm"], lw["head_mean"])


def _mix_out_kernel(x_ref, fox_ref, ssm_ref, mq_ref, gm_ref, mkt_ref, mvt_ref, w_ref, o_ref):
    tm = x_ref.shape[0]
    nmem = mkt_ref.shape[2]
    mq = mq_ref[...]
    mkt = mkt_ref[0]
    mvt = mvt_ref[0]
    khead = _row_iota((MEM_WIDTH, nmem)) // HEAD_DIM
    zero = jnp.zeros_like(mkt)
    mk_heads = jnp.concatenate([jnp.where(khead == hd, mkt, zero) for hd in range(N_MEM_HEADS)], axis=1)
    s_all = jnp.dot(mq, mk_heads, preferred_element_type=f32)
    ssm = jnp.concatenate([ssm_ref[0], ssm_ref[1]], axis=1).astype(bf16)
    y = x_ref[...]
    y = y + jnp.dot(fox_ref[...], w_ref[0:FOX_WIDTH, :], preferred_element_type=f32)
    y = y + jnp.dot(ssm, w_ref[FOX_WIDTH:FOX_WIDTH + SSM_WIDTH, :], preferred_element_type=f32)
    mem = jnp.zeros((tm, MEM_WIDTH), f32)
    for hd in range(N_MEM_HEADS):
        s = s_all[:, hd * nmem:(hd + 1) * nmem]
        p = jnp.exp(s - jnp.max(s, axis=-1, keepdims=True))
        p = p / jnp.sum(p, axis=-1, keepdims=True)
        vh = jnp.where(khead == hd, mvt, zero)
        mem = mem + lax.dot_general(p.astype(bf16), vh, (((1,), (1,)), ((), ())), preferred_element_type=f32)
    memg = (mem * gm_ref[...]).astype(bf16)
    o_ref[...] = y + jnp.dot(memg, w_ref[FOX_WIDTH + SSM_WIDTH:, :], preferred_element_type=f32)


def _mix_out(x, fox, ssm, mq, gm, mkt, mvt, w_out, *, batch, tm):
    n, d = x.shape
    t = n // batch
    assert t % tm == 0
    tiles_per_batch = t // tm
    row = lambda w: pl.BlockSpec((tm, w), lambda i: (i, 0))
    memb = pl.BlockSpec((1,) + mkt.shape[1:], lambda i: (i // tiles_per_batch, 0, 0))
    return pl.pallas_call(
        _mix_out_kernel,
        grid=(n // tm,),
        in_specs=[row(d), row(FOX_WIDTH), pl.BlockSpec((2, tm, LANES), lambda i: (0, i, 0)),
                  row(MEM_WIDTH), row(MEM_WIDTH), memb, memb, pl.BlockSpec(w_out.shape, lambda i: (0, 0))],
        out_specs=row(d),
        out_shape=jax.ShapeDtypeStruct((n, d), f32),
        compiler_params=_cparams(("arbitrary",)),
        name="mix_out_prompt" if tiles_per_batch > 1 else "mix_out_sample",
    )(x, fox, ssm, mq, gm, mkt, mvt, w_out)


def _layer_weights(l, norm_g, w_in, b_forget, fox_q_norm, fox_k_norm, mem_q_norm, mem_norm, w_mem_kv,
                   mem_k_norm, w_out, w_glu, b_glu, ssm_d):
    w = w_in[l]
    nf = 4 * FOX_WIDTH
    d = w.shape[0]
    w_packed = jnp.concatenate(
        [w[:, :FOX_WIDTH], w[:, nf:nf + N_FOX_HEADS], jnp.zeros((d, LANES - N_FOX_HEADS), w.dtype),
         w[:, FOX_WIDTH:nf], w[:, nf + N_FOX_HEADS:]], axis=1).astype(bf16)
    assert w_packed.shape[1] == _W_COLS
    return dict(
        norm_g=norm_g[l].reshape(1, d), w_in=w_packed,
        b_forget=jnp.pad(b_forget[l], (0, LANES - N_FOX_HEADS)).reshape(1, LANES),
        fox_q_norm=jnp.tile(fox_q_norm[l], N_FOX_HEADS).reshape(1, FOX_WIDTH),
        fox_k_norm=jnp.tile(fox_k_norm[l], N_FOX_HEADS).reshape(1, FOX_WIDTH),
        mem_q_norm=jnp.tile(mem_q_norm[l], N_MEM_HEADS).reshape(1, MEM_WIDTH),
        mem_k_norm=jnp.tile(mem_k_norm[l], N_MEM_HEADS).reshape(1, MEM_WIDTH),
        mem_norm=mem_norm[l].reshape(1, d), w_mem_kv=w_mem_kv[l].astype(bf16),
        head_mean=_head_mean_matrix(), bias_sel=_bias_selector(),
        w_out=w_out[l].astype(bf16), w_glu=w_glu[l].astype(bf16), b_glu=b_glu[l].reshape(1, SSM_WIDTH),
        ssm_d=ssm_d[l].reshape(1, SSM_WIDTH))


def _split_state(hT, batch):
    hT = hT.reshape(batch, 2, SSM_GROUPS, SSM_STATE)
    return hT[:, 0], hT[:, 1]


def kernel(x_prompt, x_sample, mem_prompt, cache_fox_k, cache_fox_v, cache_fox_logf, state_ssm_re, state_ssm_im, cache_mem_k, cache_mem_v, norm_g, w_in, b_forget, fox_q_norm, fox_k_norm, ssm_a_re, ssm_a_im, ssm_log_dt, ssm_b_re, ssm_b_im, ssm_c_re, ssm_c_im, ssm_d, w_glu, b_glu, mem_norm, w_mem_kv, mem_q_norm, mem_k_norm, w_out):
    B, T, D = x_prompt.shape
    Bs, Ts, _ = x_sample.shape
    depth = w_in.shape[0]
    past = cache_fox_k.shape[2]
    nmem = mem_prompt.shape[1]
    M = SSM_MODES

    xp = x_prompt.reshape(B * T, D)
    xs = x_sample.reshape(Bs * Ts, D)
    mem = mem_prompt.reshape(B * nmem, D)
    cache_kt = jnp.transpose(cache_fox_k, (0, 1, 3, 4, 2)).reshape(depth, Bs, FOX_WIDTH, past)
    cache_vt = jnp.transpose(cache_fox_v, (0, 1, 3, 4, 2)).reshape(depth, Bs, FOX_WIDTH, past)
    cache_lt = jnp.transpose(cache_fox_logf, (0, 1, 3, 2)).astype(f32)
    cache_mkt = jnp.transpose(cache_mem_k, (0, 1, 3, 4, 2)).reshape(depth, Bs, MEM_WIDTH, nmem).astype(bf16)
    cache_mvt = jnp.transpose(cache_mem_v, (0, 1, 3, 4, 2)).reshape(depth, Bs, MEM_WIDTH, nmem).astype(bf16)

    outs = {k: [] for k in ("pk", "pv", "pf", "pre", "pim", "pmk", "pmv", "sk", "sv", "sf", "sre", "sim")}
    for l in range(depth):
        lw = _layer_weights(l, norm_g, w_in, b_forget, fox_q_norm, fox_k_norm, mem_q_norm, mem_norm,
                            w_mem_kv, mem_k_norm, w_out, w_glu, b_glu, ssm_d)
        ops = _s5_prep(ssm_a_re[l], ssm_a_im[l], ssm_log_dt[l], ssm_b_re[l], ssm_b_im[l],
                       ssm_c_re[l], ssm_c_im[l])
        mkt, mvt = _memory_kv(mem, lw, batch=B)

        pr = _in_proj(xp, lw, batch=B, prompt=True)
        fox = _fox_prompt(pr["qa"], pr["ka"], pr["va"], pr["gf"].reshape(B, T, FOX_WIDTH)).reshape(B * T, FOX_WIDTH)
        ssm, hT = _s5(pr["u"], pr["gs"], jnp.zeros((B, 1, 2 * M), f32), ops, lw["ssm_d"], lw["w_glu"],
                      lw["b_glu"], batch=B, chained=True)
        xp = _mix_out(xp, fox, ssm, pr["mq"], pr["gm"], mkt.astype(bf16), mvt.astype(bf16),
                      lw["w_out"], batch=B, tm=ROW_TILE)
        hre, him = _split_state(hT, B)
        outs["pk"].append(pr["kt"])
        outs["pv"].append(pr["vt"])
        outs["pf"].append(pr["logft"])
        outs["pre"].append(hre)
        outs["pim"].append(him)
        outs["pmk"].append(mkt.reshape(B, N_MEM_HEADS, HEAD_DIM, nmem))
        outs["pmv"].append(mvt.reshape(B, N_MEM_HEADS, HEAD_DIM, nmem))

        sr = _in_proj(xs, lw, batch=Bs, prompt=False)
        fox_s = _fox_sample(sr["q"], sr["k"], sr["v"], sr["logf"], sr["gf"], cache_kt, cache_vt, cache_lt, l,
                            batch=Bs)
        h0 = jnp.concatenate([state_ssm_re[l].reshape(Bs, M), state_ssm_im[l].reshape(Bs, M)],
                             axis=-1).astype(f32)
        ssm_s, hT_s = _s5(sr["u"], sr["gs"], h0, ops, lw["ssm_d"], lw["w_glu"], lw["b_glu"], batch=Bs,
                          chained=False)
        xs = _mix_out(xs, fox_s, ssm_s, sr["mq"], sr["gm"], cache_mkt[l], cache_mvt[l],
                      lw["w_out"], batch=Bs, tm=Ts)
        sre, sim = _split_state(hT_s, Bs)
        outs["sk"].append(sr["k"].reshape(Bs, Ts, N_FOX_HEADS, HEAD_DIM))
        outs["sv"].append(sr["v"].reshape(Bs, Ts, N_FOX_HEADS, HEAD_DIM))
        outs["sf"].append(sr["logf"][:, :N_FOX_HEADS].reshape(Bs, Ts, N_FOX_HEADS))
        outs["sre"].append(sre)
        outs["sim"].append(sim)

    st = {k: jnp.stack(v) for k, v in outs.items()}
    fox_k_prompt = jnp.transpose(st["pk"], (0, 1, 4, 2, 3))
    fox_v_prompt = jnp.transpose(st["pv"], (0, 1, 4, 2, 3))
    fox_logf_prompt = jnp.transpose(st["pf"], (0, 1, 3, 2))
    mem_k_prompt = jnp.transpose(st["pmk"], (0, 1, 4, 2, 3))
    mem_v_prompt = jnp.transpose(st["pmv"], (0, 1, 4, 2, 3))
    return (xp.reshape(B, T, D), xs.reshape(Bs, Ts, D), fox_k_prompt, fox_v_prompt, fox_logf_prompt,
            st["pre"], st["pim"], mem_k_prompt, mem_v_prompt, st["sk"], st["sv"], st["sf"], st["sre"], st["sim"])
```

```python
import functools
import math

import numpy as np
import jax
import jax.numpy as jnp
from jax import lax
from jax.experimental import pallas as pl
from jax.experimental.pallas import tpu as pltpu

f32 = jnp.float32
bf16 = jnp.bfloat16

HEAD_DIM = 64
N_FOX_HEADS = 8
FOX_WIDTH = N_FOX_HEADS * HEAD_DIM
SSM_GROUPS = 16
SSM_CH = 16
SSM_STATE = 64
SSM_WIDTH = SSM_GROUPS * SSM_CH
SSM_MODES = SSM_GROUPS * SSM_STATE
N_MEM_HEADS = 4
MEM_WIDTH = N_MEM_HEADS * HEAD_DIM
EPS = 1e-6
QK_SCALE = HEAD_DIM ** -0.5
LOG2E = math.log2(math.e)

LANES = 128
S5_CHUNK = 8
S5_SEG = 16
ROW_TILE = 512
ATTN_TQ = 512
ATTN_UNROLL = 4
VMEM_LIMIT = 56 * 1024 * 1024

_C_Q, _C_F, _C_K, _C_V, _C_G = 0, 512, 640, 1152, 1664
_C_SU, _C_SG, _C_MQ, _C_MG = 2176, 2432, 2688, 2944
_W_COLS = 3200
_N_PIECES = 3


def _cparams(sem):
    return pltpu.CompilerParams(dimension_semantics=sem, vmem_limit_bytes=VMEM_LIMIT)


def _silu(x):
    return x * jax.nn.sigmoid(x)


def _log_sigmoid(x):
    return -(jnp.maximum(-x, 0.0) + jnp.log1p(jnp.exp(-jnp.abs(x))))


def _gelu_tanh(x):
    return 0.5 * x * (1.0 + jnp.tanh(math.sqrt(2.0 / math.pi) * (x + 0.044715 * (x * x * x))))


def _split3(x):
    hi = x.astype(bf16).astype(f32)
    r = x - hi
    mid = r.astype(bf16).astype(f32)
    lo = (r - mid).astype(bf16).astype(f32)
    return hi, mid, lo


def _pack3(x):
    hi, mid, lo = _split3(x)
    return (hi + pltpu.roll(mid, 8, 1) + pltpu.roll(lo, 16, 1)).astype(bf16)


def _unpack3(y):
    return y + pltpu.roll(y, LANES - 8, 1) + pltpu.roll(y, LANES - 16, 1)


def _lane_iota(shape):
    return lax.broadcasted_iota(jnp.int32, shape, len(shape) - 1)


def _row_iota(shape):
    return lax.broadcasted_iota(jnp.int32, shape, len(shape) - 2)


def _head_norm(z, hm_ref, g_ref):
    ms = jnp.dot((z * z).astype(bf16), hm_ref[...], preferred_element_type=f32)
    return z * lax.rsqrt(ms + EPS) * g_ref[...]


def _in_proj_kernel(*refs, prompt, tiles_per_batch, aliased):
    if prompt:
        if aliased:
            refs = refs[:10] + refs[12:]
        (x_ref, ng_ref, w_ref, bf_ref, qg_ref, kg_ref, mqg_ref, hm_ref, tril_ref, eb_ref,
         kt_ref, vt_ref, logft_ref, gf_ref, u_ref, gs_ref, mq_ref, gm_ref, qa_ref, ka_ref, va_ref,
         carry_ref) = refs
    else:
        (x_ref, ng_ref, w_ref, bf_ref, qg_ref, kg_ref, mqg_ref, hm_ref,
         k_ref, v_ref, logf_ref, gf_ref, u_ref, gs_ref, mq_ref, gm_ref, q_ref) = refs

    x = x_ref[...]
    tm = x.shape[0]
    lane = _lane_iota((tm, LANES))
    ms = jnp.mean(x * x, axis=-1, keepdims=True)
    h = (x * lax.rsqrt(ms + EPS) * ng_ref[...]).astype(bf16)

    def seg(lo, width):
        return jnp.dot(h, w_ref[:, lo:lo + width], preferred_element_type=f32)

    def halves(ref, z):
        ref[0] = z[:, :LANES]
        ref[1] = z[:, LANES:]

    zqf = seg(_C_Q, FOX_WIDTH + LANES)
    logf = _log_sigmoid(zqf[:, FOX_WIDTH:] + bf_ref[...])
    lf = jnp.where(lane < N_FOX_HEADS, logf, 0.0)
    zk = seg(_C_K, FOX_WIDTH)
    if prompt:
        i = pl.program_id(0)

        @pl.when(i % tiles_per_batch == 0)
        def _():
            carry_ref[...] = jnp.zeros_like(carry_ref)

        csum = jnp.dot(tril_ref[...], _pack3(lf), preferred_element_type=f32)
    v = seg(_C_V, FOX_WIDTH)
    if prompt:
        c = jnp.where(lane < N_FOX_HEADS, _unpack3(csum) + carry_ref[0:1, :], 0.0)
        carry_ref[0:1, :] = c[tm - 1:tm, :]
        pieces = (_pack3(c * LOG2E).astype(f32) + jnp.where(lane == 3 * N_FOX_HEADS, 1.0, 0.0)).astype(bf16)
    gf_ref[...] = _silu(seg(_C_G, FOX_WIDTH))
    if prompt:
        aug = jnp.dot(pieces, eb_ref[...], preferred_element_type=f32)
    zs = seg(_C_SU, 2 * SSM_WIDTH)
    zm = seg(_C_MQ, 2 * MEM_WIDTH)
    q = _head_norm(zqf[:, :FOX_WIDTH], hm_ref, qg_ref) * QK_SCALE
    k = _head_norm(zk, hm_ref, kg_ref)
    halves(u_ref, zs[:, :SSM_WIDTH])
    halves(gs_ref, _silu(zs[:, SSM_WIDTH:]))
    mq = _head_norm(zm[:, :MEM_WIDTH], hm_ref.at[0:MEM_WIDTH, 0:MEM_WIDTH], mqg_ref) * QK_SCALE
    mq_ref[...] = mq.astype(bf16)
    gm_ref[...] = _silu(zm[:, MEM_WIDTH:])

    if not prompt:
        k_ref[...] = k
        v_ref[...] = v
        logf_ref[...] = lf
        q_ref[...] = q.astype(bf16)
        return

    kt = k.T
    vt = v.T
    for hd in range(N_FOX_HEADS):
        kt_ref[0, 0, hd] = kt[hd * HEAD_DIM:(hd + 1) * HEAD_DIM, :]
        vt_ref[0, 0, hd] = vt[hd * HEAD_DIM:(hd + 1) * HEAD_DIM, :]
    for later in range(1, kt_ref.shape[0]):
        kt_ref[later] = jnp.zeros(kt_ref.shape[1:], f32)
        vt_ref[later] = jnp.zeros(vt_ref.shape[1:], f32)
    logft_ref[0] = lf.T[0:N_FOX_HEADS, :]

    q = q * LOG2E
    qaug, kaug = aug[:, :LANES], aug[:, LANES:]
    for hd in range(N_FOX_HEADS):
        pair = (hd // 2) * LANES
        own = (lane < HEAD_DIM) if hd % 2 == 0 else (lane >= HEAD_DIM)
        ones_lane = HEAD_DIM if hd % 2 == 0 else 0
        g0 = _bias_group(hd)
        bias = (lane >= g0) & (lane < g0 + 2 * _N_PIECES)
        qa_ref[0, hd] = jnp.where(own, q[:, pair:pair + LANES], jnp.where(bias, qaug, 0.0)).astype(bf16)
        ka_ref[0, hd] = jnp.where(own, k[:, pair:pair + LANES], kaug).astype(bf16)
        va_ref[0, hd] = jnp.where(own, v[:, pair:pair + LANES],
                                  jnp.where(lane == ones_lane, 1.0, 0.0)).astype(bf16)


def _bias_group(hd):
    return (HEAD_DIM if hd % 2 == 0 else 0) + 16 * (hd // 2)


def _bias_selector():
    eb = np.zeros((LANES, 2 * LANES), np.float32)
    one = 3 * N_FOX_HEADS
    for hd in range(N_FOX_HEADS):
        g0 = _bias_group(hd)
        for p in range(_N_PIECES):
            eb[p * N_FOX_HEADS + hd, g0 + p] = 1.0
            eb[one, g0 + _N_PIECES + p] = 1.0
            eb[one, LANES + g0 + p] = 1.0
            eb[p * N_FOX_HEADS + hd, LANES + g0 + _N_PIECES + p] = -1.0
    return jnp.asarray(eb, bf16)


def _head_mean_matrix():
    idx = np.arange(FOX_WIDTH) // HEAD_DIM
    return jnp.asarray((idx[:, None] == idx[None, :]).astype(np.float32) / HEAD_DIM, bf16)


def _in_proj(x, lw, *, batch, prompt, layer=0, depth=1, kv_leaves=None):
    n, d = x.shape
    t = n // batch
    tm = ROW_TILE if prompt else n
    assert n % tm == 0 and (t % tm == 0 or not prompt)
    tiles_per_batch = t // tm if prompt else 1
    grid = (n // tm,)
    row = lambda w: pl.BlockSpec((tm, w), lambda i: (i, 0))
    full = lambda a: pl.BlockSpec(a.shape, lambda i: (0,) * a.ndim)
    split = pl.BlockSpec((2, tm, LANES), lambda i: (0, i, 0))
    by_batch = lambda i: (i // tiles_per_batch, 0, i % tiles_per_batch, 0)

    ins = [x, lw["norm_g"], lw["w_in"], lw["b_forget"], lw["fox_q_norm"], lw["fox_k_norm"],
           lw["mem_q_norm"], lw["head_mean"]]
    in_specs = [row(d)] + [full(a) for a in ins[1:]]
    common = [((n, FOX_WIDTH), f32, row(FOX_WIDTH)), ((2, n, LANES), f32, split), ((2, n, LANES), f32, split),
              ((n, MEM_WIDTH), bf16, row(MEM_WIDTH)), ((n, MEM_WIDTH), f32, row(MEM_WIDTH))]
    scratch = []
    if prompt:
        tril = jnp.asarray(np.tril(np.ones((tm, tm), np.float32)), bf16)
        extra = [tril, lw["bias_sel"]]
        ins += extra
        in_specs += [full(a) for a in extra]
        aliases = {}
        if kv_leaves is None:
            tspec = pl.BlockSpec((depth, 1, N_FOX_HEADS, HEAD_DIM, tm),
                                 lambda i: (0, i // tiles_per_batch, 0, 0, i % tiles_per_batch))
        else:
            aliases = {len(ins): 0, len(ins) + 1: 1}
            ins += list(kv_leaves)
            in_specs += [pl.BlockSpec(memory_space=pl.ANY)] * 2
            tspec = pl.BlockSpec((1, 1, N_FOX_HEADS, HEAD_DIM, tm),
                                 lambda i: (layer, i // tiles_per_batch, 0, 0, i % tiles_per_batch))
        lspec = pl.BlockSpec((1, N_FOX_HEADS, tm), lambda i: (i // tiles_per_batch, 0, i % tiles_per_batch))
        hspec = pl.BlockSpec((1, N_FOX_HEADS, tm, LANES), by_batch)
        tshape = (depth, batch, N_FOX_HEADS, HEAD_DIM, t)
        outs = ([(tshape, f32, tspec), (tshape, f32, tspec), ((batch, N_FOX_HEADS, t), f32, lspec)] + common
                + [((batch, N_FOX_HEADS, t, LANES), bf16, hspec)] * 3)
        names = ["kt", "vt", "logft", "gf", "u", "gs", "mq", "gm", "qa", "ka", "va"]
        scratch = [pltpu.VMEM((8, LANES), f32)]
    else:
        outs = ([((n, FOX_WIDTH), f32, row(FOX_WIDTH)), ((n, FOX_WIDTH), f32, row(FOX_WIDTH)),
                 ((n, LANES), f32, row(LANES))] + common + [((n, FOX_WIDTH), bf16, row(FOX_WIDTH))])
        names = ["k", "v", "logf", "gf", "u", "gs", "mq", "gm", "q"]
        aliases = {}

    res = pl.pallas_call(
        functools.partial(_in_proj_kernel, prompt=prompt, tiles_per_batch=tiles_per_batch, aliased=bool(aliases)),
        grid=grid, in_specs=in_specs, out_specs=[o[2] for o in outs],
        out_shape=[jax.ShapeDtypeStruct(o[0], o[1]) for o in outs],
        scratch_shapes=scratch, input_output_aliases=aliases,
        compiler_params=_cparams(("arbitrary",)),
        name="in_proj_prompt" if prompt else "in_proj_sample",
    )(*ins)
    return dict(zip(names, res))


def _fox_prompt_kernel(qa_ref, ka_ref, va_ref, gate_ref, o_ref, s0_ref, s1_ref, m_ref, acc_ref, *, nq):
    tq = tk = ATTN_TQ
    s_refs = (s0_ref, s1_ref)
    m_ref[...] = jnp.full(m_ref.shape, -jnp.inf, f32)
    acc_ref[...] = jnp.zeros(acc_ref.shape, f32)

    def scores(qb, kb, slot):
        qoff = pl.multiple_of(qb * tq, tq)
        koff = pl.multiple_of(kb * tk, tk)
        for j in range(2):
            s_refs[slot][j] = lax.dot_general(qa_ref[0, j, pl.ds(qoff, tq), :], ka_ref[0, j, pl.ds(koff, tk), :],
                                              (((1,), (1,)), ((), ())), preferred_element_type=f32)

    def softmax_pv(qb, kb, slot, masked):
        koff = pl.multiple_of(kb * tk, tk)
        for j in range(2):
            s = s_refs[slot][j]
            if masked:
                s = jnp.where(_row_iota((tq, tk)) >= _lane_iota((tq, tk)), s, -jnp.inf)
            m = m_ref[qb, j]
            m_new = jnp.maximum(m, jnp.max(s, axis=-1, keepdims=True))
            p = jnp.exp2(s - jnp.concatenate([m_new] * (tk // LANES), axis=1))
            vv = va_ref[0, j, pl.ds(koff, tk), :]
            acc_ref[qb, j] = (jnp.exp2(m - m_new) * acc_ref[qb, j]
                              + jnp.dot(p.astype(bf16), vv, preferred_element_type=f32))
            m_ref[qb, j] = m_new

    def finalize(qb):
        h0 = acc_ref[qb, 0] / acc_ref[qb, 0, :, HEAD_DIM:HEAD_DIM + 1]
        h1 = acc_ref[qb, 1] / acc_ref[qb, 1, :, 0:1]
        rows = pl.ds(pl.multiple_of(qb * tq, tq), tq)
        out = jnp.where(_lane_iota((tq, LANES)) < HEAD_DIM, h0, h1) * gate_ref[0, rows, :]
        o_ref[0, rows, :] = out.astype(bf16)

    def after(qb, kb):
        row_end = kb + 1 >= qb
        last = jnp.logical_and(row_end, qb + 1 >= nq)
        nqb = jnp.where(last, 0, jnp.where(row_end, qb + 1, qb))
        nkb = jnp.where(row_end, 0, kb + 1)
        return nqb, nkb

    def lower_step(qb, kb, slot):
        nqb, nkb = after(qb, kb)
        scores(nqb, nkb, 1 - slot)
        softmax_pv(qb, kb, slot, False)
        return nqb, nkb

    def diag_step(qb, slot):
        nxt = jnp.minimum(qb + 1, nq - 1)
        scores(nxt, nxt, 1 - slot)
        softmax_pv(qb, qb, slot, True)
        finalize(qb)

    def steps_per_trip(n):
        return max(d for d in range(2, ATTN_UNROLL + 1, 2) if n % d == 0)

    n_lower = nq * (nq - 1) // 2
    if n_lower:
        scores(1, 0, 0)
        per = steps_per_trip(n_lower)

        def lower_body(i, carry):
            for k in range(per):
                carry = lower_step(*carry, k % 2)
            return carry

        lax.fori_loop(0, n_lower // per, lower_body, (jnp.int32(1), jnp.int32(0)))
    else:
        scores(0, 0, 0)

    per_diag = steps_per_trip(nq)

    def diag_body(i, carry):
        for k in range(per_diag):
            diag_step(per_diag * i + k, k % 2)
        return carry

    lax.fori_loop(0, nq // per_diag, diag_body, 0)


def _fox_prompt(qa, ka, va, gate):
    b, nh, t, _ = qa.shape
    nq = t // ATTN_TQ
    assert t % ATTN_TQ == 0 and nq % 2 == 0 and (nq * (nq - 1) // 2) % 2 == 0
    whole = pl.BlockSpec((1, 2, t, LANES), lambda bi, hp: (bi, hp, 0, 0))
    return pl.pallas_call(
        functools.partial(_fox_prompt_kernel, nq=nq),
        grid=(b, nh // 2),
        in_specs=[whole, whole, whole, pl.BlockSpec((1, t, LANES), lambda bi, hp: (bi, 0, hp))],
        out_specs=pl.BlockSpec((1, t, LANES), lambda bi, hp: (bi, 0, hp)),
        out_shape=jax.ShapeDtypeStruct((b, t, FOX_WIDTH), bf16),
        scratch_shapes=[pltpu.VMEM((2, ATTN_TQ, ATTN_TQ), f32), pltpu.VMEM((2, ATTN_TQ, ATTN_TQ), f32),
                        pltpu.VMEM((nq, 2, ATTN_TQ, LANES), f32), pltpu.VMEM((nq, 2, ATTN_TQ, LANES), f32)],
        compiler_params=_cparams(("arbitrary", "arbitrary")),
        name="fox_prompt",
    )(qa, ka, va, gate)


def _fox_sample_kernel(q_ref, kn_ref, vn_ref, ln_ref, gate_ref, kc_ref, vc_ref, lc_ref, triu_ref, o_ref,
                       kall_ref, vall_ref, *, past, tnew):
    nkeys = kall_ref.shape[1]
    nq = N_FOX_HEADS * tnew

    def new_cols(ref):
        z = jnp.concatenate([ref[...], jnp.zeros((LANES - tnew, ref.shape[1]), f32)], axis=0)
        return z.T

    kall_ref[:, 0:past] = kc_ref[0, 0].astype(bf16)
    vall_ref[:, 0:past] = vc_ref[0, 0].astype(bf16)
    kall_ref[:, past:nkeys] = new_cols(kn_ref).astype(bf16)
    vall_ref[:, past:nkeys] = new_cols(vn_ref).astype(bf16)

    l_all = jnp.concatenate([lc_ref[0, 0], new_cols(ln_ref)[0:N_FOX_HEADS, :]], axis=1)
    pieces = jnp.concatenate(_split3(l_all), axis=0).astype(bf16)
    c3 = jnp.dot(pieces, triu_ref[...], preferred_element_type=f32)
    c = c3[0:8] + c3[8:16] + c3[16:24]
    crow = jnp.concatenate([jnp.broadcast_to(c[hd:hd + 1, :], (tnew, nkeys)) for hd in range(N_FOX_HEADS)],
                           axis=0)
    rq = _row_iota((nq, LANES))
    cq = jnp.sum(jnp.where(_lane_iota((nq, LANES)) == rq % tnew, crow[:, past:nkeys], 0.0),
                 axis=1, keepdims=True)

    qrep = jnp.concatenate([q_ref[...]] * N_FOX_HEADS, axis=0)
    own = _row_iota((nq, FOX_WIDTH)) // tnew == _lane_iota((nq, FOX_WIDTH)) // HEAD_DIM
    qbd = jnp.where(own, qrep, jnp.zeros_like(qrep))
    s = jnp.dot(qbd, kall_ref[...], preferred_element_type=f32) + cq - crow
    s = jnp.where(_lane_iota((nq, nkeys)) <= past + _row_iota((nq, nkeys)) % tnew, s, -jnp.inf)
    p = jnp.exp(s - jnp.max(s, axis=1, keepdims=True))
    o2 = lax.dot_general(p.astype(bf16), vall_ref[...], (((1,), (1,)), ((), ())), preferred_element_type=f32)
    o2 = o2 / jnp.sum(p, axis=1, keepdims=True)
    olane = _lane_iota((tnew, FOX_WIDTH))
    out = jnp.zeros((tnew, FOX_WIDTH), f32)
    for hd in range(N_FOX_HEADS):
        out = out + jnp.where(olane // HEAD_DIM == hd, o2[hd * tnew:(hd + 1) * tnew, :], 0.0)
    o_ref[...] = (out * gate_ref[...]).astype(bf16)


def _fox_sample(q, k_new, v_new, logf_new, gate, cache_kt, cache_vt, cache_lt, layer, *, batch):
    n = q.shape[0]
    tnew = n // batch
    past = cache_kt.shape[3]
    nkeys = past + LANES
    assert N_FOX_HEADS * tnew == LANES and past % LANES == 0
    triu = jnp.asarray(np.triu(np.ones((nkeys, nkeys), np.float32)), bf16)
    rows = lambda w: pl.BlockSpec((tnew, w), lambda bi: (bi, 0))
    cache = lambda h: pl.BlockSpec((1, 1, h, past), lambda bi: (layer, bi, 0, 0))
    return pl.pallas_call(
        functools.partial(_fox_sample_kernel, past=past, tnew=tnew),
        grid=(batch,),
        in_specs=[rows(FOX_WIDTH), rows(FOX_WIDTH), rows(FOX_WIDTH), rows(LANES), rows(FOX_WIDTH),
                  cache(FOX_WIDTH), cache(FOX_WIDTH), cache(N_FOX_HEADS),
                  pl.BlockSpec((nkeys, nkeys), lambda bi: (0, 0))],
        out_specs=rows(FOX_WIDTH),
        out_shape=jax.ShapeDtypeStruct((n, FOX_WIDTH), bf16),
        scratch_shapes=[pltpu.VMEM((FOX_WIDTH, nkeys), bf16), pltpu.VMEM((FOX_WIDTH, nkeys), bf16)],
        compiler_params=_cparams(("arbitrary",)),
        name="fox_sample",
    )(q, k_new, v_new, logf_new, gate, cache_kt, cache_vt, cache_lt, triu)


def _s5_prep_kernel(are_r, aim_r, ldt_r, are_c, aim_c, ldt_c, bxr_ref, bxi_ref, cxr_ref, cxi_ref,
                    bst_ref, cst_ref, kt_ref, apow_ref):
    L, W, M = S5_CHUNK, SSM_WIDTH, SSM_MODES

    def cmul(xr, xi, yr, yi):
        return xr * yr - xi * yi, xr * yi + xi * yr

    def powers(are, aim, ldt, n):
        dt = jnp.exp(ldt[...])
        mag = jnp.exp(dt * are[...])
        p1 = (mag * jnp.cos(dt * aim[...]), mag * jnp.sin(dt * aim[...]))
        out = [(jnp.ones_like(mag), jnp.zeros_like(mag)), p1]
        for _ in range(n - 1):
            out.append(cmul(*out[-1], *p1))
        return out

    prow = powers(are_r, aim_r, ldt_r, L)
    pcol = powers(are_c, aim_c, ldt_c, L)
    ar, ai = are_r[...], aim_r[...]
    abr, abi = prow[1]
    den = ar * ar + ai * ai
    zr = ((abr - 1.0) * ar + abi * ai) / den
    zi = (abi * ar - (abr - 1.0) * ai) / den
    bbr, bbi = cmul(zr, zi, bxr_ref[...], bxi_ref[...])

    def b_pow(k):
        return jnp.concatenate(cmul(*prow[k], bbr, bbi), axis=1)

    def c_pow(k):
        pr, pi = pcol[k]
        cr, ci = cxr_ref[...], cxi_ref[...]
        return jnp.concatenate([cr * pr - ci * pi, -(cr * pi + ci * pr)], axis=0)

    cpow = [c_pow(k) for k in range(L + 1)]
    for tl in range(L):
        bst_ref[tl * W:(tl + 1) * W, :] = b_pow(L - 1 - tl).astype(bf16)
        cst_ref[:, tl * W:(tl + 1) * W] = cpow[tl + 1].astype(bf16)

    def hi_lo(x):
        hi = x.astype(bf16)
        return hi, (x - hi.astype(f32)).astype(bf16)

    bh, bl = hi_lo(b_pow(0))
    ch, cl = hi_lo(jnp.concatenate(cpow[:L], axis=1))
    kd = (jnp.dot(bh, ch, preferred_element_type=f32) + jnp.dot(bh, cl, preferred_element_type=f32)
          + jnp.dot(bl, ch, preferred_element_type=f32)).astype(bf16)
    for tl in range(L):
        if tl:
            kt_ref[tl * W:(tl + 1) * W, 0:tl * W] = jnp.zeros((W, tl * W), bf16)
        kt_ref[tl * W:(tl + 1) * W, tl * W:] = kd[:, :(L - tl) * W]

    step = prow[L]
    seg = step
    for _ in range(int(math.log2(S5_SEG))):
        seg = cmul(*seg, *seg)
    apow_ref[...] = jnp.concatenate(
        [jnp.concatenate(step, axis=1), jnp.concatenate(seg, axis=1), jnp.zeros((6, 2 * M), f32)], axis=0)


def _s5_prep(a_re, a_im, log_dt, b_re, b_im, c_re, c_im):
    G, N, P, L = SSM_GROUPS, SSM_STATE, SSM_CH, S5_CHUNK
    W, M = SSM_WIDTH, SSM_MODES
    eye = jnp.eye(G, dtype=f32)
    bx = lambda b: (jnp.swapaxes(b, 1, 2)[:, :, None, :] * eye[:, None, :, None]).reshape(W, M)
    cx = lambda c: (jnp.swapaxes(c, 1, 2)[:, :, None, :] * eye[:, None, :, None]).reshape(M, W)
    ldt = jnp.repeat(log_dt, N)
    ins = [a_re.reshape(1, M), a_im.reshape(1, M), ldt.reshape(1, M),
           a_re.reshape(M, 1), a_im.reshape(M, 1), ldt.reshape(M, 1),
           bx(b_re), bx(b_im), cx(c_re), cx(c_im)]
    return pl.pallas_call(
        _s5_prep_kernel,
        out_shape=[jax.ShapeDtypeStruct((L * W, 2 * M), bf16), jax.ShapeDtypeStruct((2 * M, L * W), bf16),
                   jax.ShapeDtypeStruct((L * W, L * W), bf16), jax.ShapeDtypeStruct((8, 2 * M), f32)],
        compiler_params=pltpu.CompilerParams(vmem_limit_bytes=VMEM_LIMIT),
        name="s5_prep",
    )(*ins)


def _s5_kernel(u_ref, gs_ref, h0_ref, bst_ref, cst_ref, kt_ref, apow_ref, d_ref, wg_ref, bg_ref,
               o_ref, hT_ref, ds_ref, sp_ref, carry_ref, *, chained, seglen):
    L, W, M = S5_CHUNK, SSM_WIDTH, SSM_MODES
    nr = 8 * seglen
    nlb = M // LANES

    def chunked(ref, tl):
        return jnp.concatenate([ref[0, pl.ds(tl, nr, stride=L), :], ref[1, pl.ds(tl, nr, stride=L), :]], axis=1)

    uf = [chunked(u_ref, tl) for tl in range(L)]
    ub = jnp.concatenate(uf, axis=1).astype(bf16)
    ds = jnp.dot(ub, bst_ref[...], preferred_element_type=f32)
    for lb in range(2 * nlb):
        ds_ref[lb] = ds[:, lb * LANES:(lb + 1) * LANES]
    def blocks(row):
        return ([row[:, lb * LANES:(lb + 1) * LANES] for lb in range(nlb)],
                [row[:, M + lb * LANES:M + (lb + 1) * LANES] for lb in range(nlb)])

    def bcast8(parts):
        return [jnp.broadcast_to(p, (8, LANES)) for p in parts]

    ar, ai = (bcast8(p) for p in blocks(apow_ref[0:1, :]))

    def scan(vr, vi, store):
        for i in range(seglen):
            rows = pl.ds(i, 8, stride=seglen)
            for lb in range(nlb):
                if store:
                    sp_ref[lb, rows, :] = vr[lb]
                    sp_ref[nlb + lb, rows, :] = vi[lb]
                dr, di = ds_ref[lb, rows, :], ds_ref[nlb + lb, rows, :]
                vr[lb], vi[lb] = (ar[lb] * vr[lb] - ai[lb] * vi[lb] + dr,
                                  ar[lb] * vi[lb] + ai[lb] * vr[lb] + di)
        return vr, vi

    if chained:
        ti = pl.program_id(1)

        @pl.when(ti == 0)
        def _():
            carry_ref[0:1, :] = h0_ref[0]

        zero = [jnp.zeros((8, LANES), f32) for _ in range(nlb)]
        er, ei = scan(list(zero), list(zero), store=False)
        pr, pi = blocks(apow_ref[1:2, :])
        cr, ci = blocks(carry_ref[0:1, :])
        sr, si = [[] for _ in range(nlb)], [[] for _ in range(nlb)]
        for j in range(8):
            for lb in range(nlb):
                sr[lb].append(cr[lb])
                si[lb].append(ci[lb])
                cr[lb], ci[lb] = (pr[lb] * cr[lb] - pi[lb] * ci[lb] + er[lb][j:j + 1, :],
                                  pr[lb] * ci[lb] + pi[lb] * cr[lb] + ei[lb][j:j + 1, :])
        final = jnp.concatenate(cr + ci, axis=1)
        carry_ref[0:1, :] = final
        hT_ref[0] = final
        scan([jnp.concatenate(x, axis=0) for x in sr], [jnp.concatenate(x, axis=0) for x in si], store=True)
    else:
        vr, vi = blocks(h0_ref[...])
        vr, vi = scan(vr, vi, store=True)
        hT_ref[...] = jnp.concatenate(vr + vi, axis=1)

    spb = jnp.concatenate([sp_ref[lb] for lb in range(2 * nlb)], axis=1).astype(bf16)

    def y_pair(pair):
        kk = (2 * pair + 2) * W
        cols = slice(2 * pair * W, kk)
        return (jnp.dot(ub[:, :kk], kt_ref[0:kk, cols], preferred_element_type=f32)
                + jnp.dot(spb, cst_ref[:, cols], preferred_element_type=f32))

    def finish(pair, ypair):
        for tl in (2 * pair, 2 * pair + 1):
            y = ypair[:, (tl % 2) * W:(tl % 2 + 1) * W] + d_ref[...] * uf[tl]
            g = _gelu_tanh(y)
            gate = jax.nn.sigmoid(jnp.dot(g.astype(bf16), wg_ref[...], preferred_element_type=f32) + bg_ref[...])
            out = g * gate * chunked(gs_ref, tl)
            o_ref[0, pl.ds(tl, nr, stride=L), :] = out[:, :LANES]
            o_ref[1, pl.ds(tl, nr, stride=L), :] = out[:, LANES:]

    ycur = y_pair(0)
    for pair in range(L // 2):
        ynext = y_pair(pair + 1) if pair + 1 < L // 2 else None
        finish(pair, ycur)
        ycur = ynext


def _s5(u, gs, h0, ops, d_row, w_glu, b_glu, *, batch, chained):
    L, W, M = S5_CHUNK, SSM_WIDTH, SSM_MODES
    n = u.shape[1]
    t = n // batch
    assert t % L == 0
    chunks = t // L
    bst, cst, kt, apow = ops
    if chained:
        seglen = S5_SEG
        assert chunks % (8 * seglen) == 0
        grid = (batch, chunks // (8 * seglen))
        state = pl.BlockSpec((1, 1, 2 * M), lambda bi, ti: (bi, 0, 0))
    else:
        seglen = chunks
        assert batch == 8
        grid = (1, 1)
        state = pl.BlockSpec((8, 2 * M), lambda bi, ti: (0, 0))
    tiles = grid[1]
    nr = 8 * seglen
    blk = pl.BlockSpec((2, nr * L, LANES), lambda bi, ti: (0, bi * tiles + ti, 0))
    full = lambda a: pl.BlockSpec(a.shape, lambda bi, ti: (0,) * a.ndim)
    once = lambda a: pl.BlockSpec(a.shape, lambda bi, ti: (0,) * a.ndim, pipeline_mode=pl.Buffered(1))
    return pl.pallas_call(
        functools.partial(_s5_kernel, chained=chained, seglen=seglen),
        grid=grid,
        in_specs=[blk, blk, state, once(bst), once(cst), once(kt), full(apow), full(d_row),
                  full(w_glu), full(b_glu)],
        out_specs=[blk, state],
        out_shape=[jax.ShapeDtypeStruct((2, n, LANES), f32), jax.ShapeDtypeStruct(h0.shape, f32)],
        scratch_shapes=[pltpu.VMEM((2 * M // LANES, nr, LANES), f32), pltpu.VMEM((2 * M // LANES, nr, LANES), f32),
                        pltpu.VMEM((8, 2 * M), f32)],
        compiler_params=_cparams(("arbitrary", "arbitrary")),
        name="s5_prompt" if chained else "s5_sample",
    )(u, gs, h0, bst, cst, kt, apow, d_row, w_glu, b_glu)


def _memory_kv_kernel(mem_ref, g_ref, w_ref, kg_ref, hm_ref, mkt_ref, mvt_ref):
    batch, _, nmem = mkt_ref.shape
    x = mem_ref[...]
    ms = jnp.mean(x * x, axis=-1, keepdims=True)
    h = (x * lax.rsqrt(ms + EPS) * g_ref[...]).astype(bf16)
    mk = jnp.dot(h, w_ref[:, 0:MEM_WIDTH], preferred_element_type=f32)
    mk = _head_norm(mk, hm_ref.at[0:MEM_WIDTH, 0:MEM_WIDTH], kg_ref)
    mv = jnp.dot(h, w_ref[:, MEM_WIDTH:2 * MEM_WIDTH], preferred_element_type=f32)
    for b in range(batch):
        mkt_ref[b] = mk[b * nmem:(b + 1) * nmem, :].T
        mvt_ref[b] = mv[b * nmem:(b + 1) * nmem, :].T


def _memory_kv(mem, lw, *, batch):
    n = mem.shape[0]
    return pl.pallas_call(
        _memory_kv_kernel,
        out_shape=[jax.ShapeDtypeStruct((batch, MEM_WIDTH, n // batch), f32)] * 2,
        compiler_params=pltpu.CompilerParams(vmem_limit_bytes=VMEM_LIMIT),
        name="memory_kv",
    )(mem, lw["mem_norm"], lw["w_mem_kv"], lw["mem_k_norm"], lw["head_mean"])


def _mix_out_kernel(x_ref, fox_ref, ssm_ref, mq_ref, gm_ref, mkt_ref, mvt_ref, w_ref, o_ref):
    tm = x_ref.shape[0]
    nmem = mkt_ref.shape[2]
    mq = mq_ref[...]
    mkt = mkt_ref[0]
    mvt = mvt_ref[0]
    khead = _row_iota((MEM_WIDTH, nmem)) // HEAD_DIM
    zero = jnp.zeros_like(mkt)
    mk_heads = jnp.concatenate([jnp.where(khead == hd, mkt, zero) for hd in range(N_MEM_HEADS)], axis=1)
    s_all = jnp.dot(mq, mk_heads, preferred_element_type=f32)
    ssm = jnp.concatenate([ssm_ref[0], ssm_ref[1]], axis=1).astype(bf16)
    y = x_ref[...]
    y = y + jnp.dot(fox_ref[...], w_ref[0:FOX_WIDTH, :], preferred_element_type=f32)
    y = y + jnp.dot(ssm, w_ref[FOX_WIDTH:FOX_WIDTH + SSM_WIDTH, :], preferred_element_type=f32)
    mem = jnp.zeros((tm, MEM_WIDTH), f32)
    for hd in range(N_MEM_HEADS):
        s = s_all[:, hd * nmem:(hd + 1) * nmem]
        p = jnp.exp(s - jnp.max(s, axis=-1, keepdims=True))
        p = p / jnp.sum(p, axis=-1, keepdims=True)
        vh = jnp.where(khead == hd, mvt, zero)
        mem = mem + lax.dot_general(p.astype(bf16), vh, (((1,), (1,)), ((), ())), preferred_element_type=f32)
    memg = (mem * gm_ref[...]).astype(bf16)
    o_ref[...] = y + jnp.dot(memg, w_ref[FOX_WIDTH + SSM_WIDTH:, :], preferred_element_type=f32)


def _mix_out(x, fox, ssm, mq, gm, mkt, mvt, w_out, *, batch, tm):
    n, d = x.shape
    t = n // batch
    assert t % tm == 0
    tiles_per_batch = t // tm
    row = lambda w: pl.BlockSpec((tm, w), lambda i: (i, 0))
    memb = pl.BlockSpec((1,) + mkt.shape[1:], lambda i: (i // tiles_per_batch, 0, 0))
    return pl.pallas_call(
        _mix_out_kernel,
        grid=(n // tm,),
        in_specs=[row(d), row(FOX_WIDTH), pl.BlockSpec((2, tm, LANES), lambda i: (0, i, 0)),
                  row(MEM_WIDTH), row(MEM_WIDTH), memb, memb, pl.BlockSpec(w_out.shape, lambda i: (0, 0))],
        out_specs=row(d),
        out_shape=jax.ShapeDtypeStruct((n, d), f32),
        compiler_params=_cparams(("arbitrary",)),
        name="mix_out_prompt" if tiles_per_batch > 1 else "mix_out_sample",
    )(x, fox, ssm, mq, gm, mkt, mvt, w_out)


def _layer_weights(l, norm_g, w_in, b_forget, fox_q_norm, fox_k_norm, mem_q_norm, mem_norm, w_mem_kv,
                   mem_k_norm, w_out, w_glu, b_glu, ssm_d):
    w = w_in[l]
    nf = 4 * FOX_WIDTH
    d = w.shape[0]
    w_packed = jnp.concatenate(
        [w[:, :FOX_WIDTH], w[:, nf:nf + N_FOX_HEADS], jnp.zeros((d, LANES - N_FOX_HEADS), w.dtype),
         w[:, FOX_WIDTH:nf], w[:, nf + N_FOX_HEADS:]], axis=1).astype(bf16)
    assert w_packed.shape[1] == _W_COLS
    return dict(
        norm_g=norm_g[l].reshape(1, d), w_in=w_packed,
        b_forget=jnp.pad(b_forget[l], (0, LANES - N_FOX_HEADS)).reshape(1, LANES),
        fox_q_norm=jnp.tile(fox_q_norm[l], N_FOX_HEADS).reshape(1, FOX_WIDTH),
        fox_k_norm=jnp.tile(fox_k_norm[l], N_FOX_HEADS).reshape(1, FOX_WIDTH),
        mem_q_norm=jnp.tile(mem_q_norm[l], N_MEM_HEADS).reshape(1, MEM_WIDTH),
        mem_k_norm=jnp.tile(mem_k_norm[l], N_MEM_HEADS).reshape(1, MEM_WIDTH),
        mem_norm=mem_norm[l].reshape(1, d), w_mem_kv=w_mem_kv[l].astype(bf16),
        head_mean=_head_mean_matrix(), bias_sel=_bias_selector(),
        w_out=w_out[l].astype(bf16), w_glu=w_glu[l].astype(bf16), b_glu=b_glu[l].reshape(1, SSM_WIDTH),
        ssm_d=ssm_d[l].reshape(1, SSM_WIDTH))


def _split_state(hT, batch):
    hT = hT.reshape(batch, 2, SSM_GROUPS, SSM_STATE)
    return hT[:, 0], hT[:, 1]


def kernel(x_prompt, x_sample, mem_prompt, cache_fox_k, cache_fox_v, cache_fox_logf, state_ssm_re, state_ssm_im, cache_mem_k, cache_mem_v, norm_g, w_in, b_forget, fox_q_norm, fox_k_norm, ssm_a_re, ssm_a_im, ssm_log_dt, ssm_b_re, ssm_b_im, ssm_c_re, ssm_c_im, ssm_d, w_glu, b_glu, mem_norm, w_mem_kv, mem_q_norm, mem_k_norm, w_out):
    B, T, D = x_prompt.shape
    Bs, Ts, _ = x_sample.shape
    depth = w_in.shape[0]
    past = cache_fox_k.shape[2]
    nmem = mem_prompt.shape[1]
    M = SSM_MODES

    xp = x_prompt.reshape(B * T, D)
    xs = x_sample.reshape(Bs * Ts, D)
    mem = mem_prompt.reshape(B * nmem, D)
    cache_kt = jnp.transpose(cache_fox_k, (0, 1, 3, 4, 2)).reshape(depth, Bs, FOX_WIDTH, past)
    cache_vt = jnp.transpose(cache_fox_v, (0, 1, 3, 4, 2)).reshape(depth, Bs, FOX_WIDTH, past)
    cache_lt = jnp.transpose(cache_fox_logf, (0, 1, 3, 2)).astype(f32)
    cache_mkt = jnp.transpose(cache_mem_k, (0, 1, 3, 4, 2)).reshape(depth, Bs, MEM_WIDTH, nmem).astype(bf16)
    cache_mvt = jnp.transpose(cache_mem_v, (0, 1, 3, 4, 2)).reshape(depth, Bs, MEM_WIDTH, nmem).astype(bf16)

    outs = {k: [] for k in ("pf", "pre", "pim", "pmk", "pmv", "sk", "sv", "sf", "sre", "sim")}
    kv_leaves = None
    for l in range(depth):
        lw = _layer_weights(l, norm_g, w_in, b_forget, fox_q_norm, fox_k_norm, mem_q_norm, mem_norm,
                            w_mem_kv, mem_k_norm, w_out, w_glu, b_glu, ssm_d)
        ops = _s5_prep(ssm_a_re[l], ssm_a_im[l], ssm_log_dt[l], ssm_b_re[l], ssm_b_im[l],
                       ssm_c_re[l], ssm_c_im[l])
        mkt, mvt = _memory_kv(mem, lw, batch=B)

        pr = _in_proj(xp, lw, batch=B, prompt=True, layer=l, depth=depth, kv_leaves=kv_leaves)
        kv_leaves = (pr["kt"], pr["vt"])
        fox = _fox_prompt(pr["qa"], pr["ka"], pr["va"], pr["gf"].reshape(B, T, FOX_WIDTH)).reshape(B * T, FOX_WIDTH)
        ssm, hT = _s5(pr["u"], pr["gs"], jnp.zeros((B, 1, 2 * M), f32), ops, lw["ssm_d"], lw["w_glu"],
                      lw["b_glu"], batch=B, chained=True)
        xp = _mix_out(xp, fox, ssm, pr["mq"], pr["gm"], mkt.astype(bf16), mvt.astype(bf16),
                      lw["w_out"], batch=B, tm=ROW_TILE)
        hre, him = _split_state(hT, B)
        outs["pf"].append(pr["logft"])
        outs["pre"].append(hre)
        outs["pim"].append(him)
        outs["pmk"].append(mkt.reshape(B, N_MEM_HEADS, HEAD_DIM, nmem))
        outs["pmv"].append(mvt.reshape(B, N_MEM_HEADS, HEAD_DIM, nmem))

        sr = _in_proj(xs, lw, batch=Bs, prompt=False)
        fox_s = _fox_sample(sr["q"], sr["k"], sr["v"], sr["logf"], sr["gf"], cache_kt, cache_vt, cache_lt, l,
                            batch=Bs)
        h0 = jnp.concatenate([state_ssm_re[l].reshape(Bs, M), state_ssm_im[l].reshape(Bs, M)],
                             axis=-1).astype(f32)
        ssm_s, hT_s = _s5(sr["u"], sr["gs"], h0, ops, lw["ssm_d"], lw["w_glu"], lw["b_glu"], batch=Bs,
                          chained=False)
        xs = _mix_out(xs, fox_s, ssm_s, sr["mq"], sr["gm"], cache_mkt[l], cache_mvt[l],
                      lw["w_out"], batch=Bs, tm=Ts)
        sre, sim = _split_state(hT_s, Bs)
        outs["sk"].append(sr["k"].reshape(Bs, Ts, N_FOX_HEADS, HEAD_DIM))
        outs["sv"].append(sr["v"].reshape(Bs, Ts, N_FOX_HEADS, HEAD_DIM))
        outs["sf"].append(sr["logf"][:, :N_FOX_HEADS].reshape(Bs, Ts, N_FOX_HEADS))
        outs["sre"].append(sre)
        outs["sim"].append(sim)

    st = {k: jnp.stack(v) for k, v in outs.items()}
    fox_k_prompt = jnp.transpose(kv_leaves[0], (0, 1, 4, 2, 3))
    fox_v_prompt = jnp.transpose(kv_leaves[1], (0, 1, 4, 2, 3))
    fox_logf_prompt = jnp.transpose(st["pf"], (0, 1, 3, 2))
    mem_k_prompt = jnp.transpose(st["pmk"], (0, 1, 4, 2, 3))
    mem_v_prompt = jnp.transpose(st["pmv"], (0, 1, 4, 2, 3))
    return (xp.reshape(B, T, D), xs.reshape(Bs, Ts, D), fox_k_prompt, fox_v_prompt, fox_logf_prompt,
            st["pre"], st["pim"], mem_k_prompt, mem_v_prompt, st["sk"], st["sv"], st["sf"], st["sre"], st["sim"])
```

```python
import functools
import math

import numpy as np
import jax
import jax.numpy as jnp
from jax import lax
from jax.experimental import pallas as pl
from jax.experimental.pallas import tpu as pltpu

f32 = jnp.float32
bf16 = jnp.bfloat16

HEAD_DIM = 64
N_FOX_HEADS = 8
FOX_WIDTH = N_FOX_HEADS * HEAD_DIM
SSM_GROUPS = 16
SSM_CH = 16
SSM_STATE = 64
SSM_WIDTH = SSM_GROUPS * SSM_CH
SSM_MODES = SSM_GROUPS * SSM_STATE
N_MEM_HEADS = 4
MEM_WIDTH = N_MEM_HEADS * HEAD_DIM
EPS = 1e-6
QK_SCALE = HEAD_DIM ** -0.5
LOG2E = math.log2(math.e)

LANES = 128
S5_CHUNK = 8
S5_SEG = 16
ROW_TILE = 512
ATTN_TQ = 512
ATTN_UNROLL = 14
VMEM_LIMIT = 56 * 1024 * 1024

_C_Q, _C_F, _C_K, _C_V, _C_G = 0, 512, 640, 1152, 1664
_C_SU, _C_SG, _C_MQ, _C_MG = 2176, 2432, 2688, 2944
_W_COLS = 3200
_N_PIECES = 3


def _cparams(sem):
    return pltpu.CompilerParams(dimension_semantics=sem, vmem_limit_bytes=VMEM_LIMIT)


def _silu(x):
    return x * jax.nn.sigmoid(x)


def _log_sigmoid(x):
    return -(jnp.maximum(-x, 0.0) + jnp.log1p(jnp.exp(-jnp.abs(x))))


def _gelu_tanh(x):
    return 0.5 * x * (1.0 + jnp.tanh(math.sqrt(2.0 / math.pi) * (x + 0.044715 * (x * x * x))))


def _split3(x):
    hi = x.astype(bf16).astype(f32)
    r = x - hi
    mid = r.astype(bf16).astype(f32)
    lo = (r - mid).astype(bf16).astype(f32)
    return hi, mid, lo


def _pack3(x):
    hi, mid, lo = _split3(x)
    return (hi + pltpu.roll(mid, 8, 1) + pltpu.roll(lo, 16, 1)).astype(bf16)


def _unpack3(y):
    return y + pltpu.roll(y, LANES - 8, 1) + pltpu.roll(y, LANES - 16, 1)


def _lane_iota(shape):
    return lax.broadcasted_iota(jnp.int32, shape, len(shape) - 1)


def _row_iota(shape):
    return lax.broadcasted_iota(jnp.int32, shape, len(shape) - 2)


def _head_norm(z, hm_ref, g_ref):
    ms = jnp.dot((z * z).astype(bf16), hm_ref[...], preferred_element_type=f32)
    return z * lax.rsqrt(ms + EPS) * g_ref[...]


def _in_proj_kernel(*refs, prompt, tiles_per_batch, aliased):
    if prompt:
        if aliased:
            refs = refs[:10] + refs[12:]
        (x_ref, ng_ref, w_ref, bf_ref, qg_ref, kg_ref, mqg_ref, hm_ref, tril_ref, eb_ref,
         kt_ref, vt_ref, logft_ref, gf_ref, u_ref, gs_ref, mq_ref, gm_ref, qa_ref, ka_ref, va_ref,
         carry_ref) = refs
    else:
        (x_ref, ng_ref, w_ref, bf_ref, qg_ref, kg_ref, mqg_ref, hm_ref,
         k_ref, v_ref, logf_ref, gf_ref, u_ref, gs_ref, mq_ref, gm_ref, q_ref) = refs

    x = x_ref[...]
    tm = x.shape[0]
    lane = _lane_iota((tm, LANES))
    ms = jnp.mean(x * x, axis=-1, keepdims=True)
    h = (x * lax.rsqrt(ms + EPS) * ng_ref[...]).astype(bf16)

    def seg(lo, width):
        return jnp.dot(h, w_ref[:, lo:lo + width], preferred_element_type=f32)

    def halves(ref, z):
        ref[0] = z[:, :LANES]
        ref[1] = z[:, LANES:]

    zqf = seg(_C_Q, FOX_WIDTH + LANES)
    logf = _log_sigmoid(zqf[:, FOX_WIDTH:] + bf_ref[...])
    lf = jnp.where(lane < N_FOX_HEADS, logf, 0.0)
    zk = seg(_C_K, FOX_WIDTH)
    if prompt:
        i = pl.program_id(0)

        @pl.when(i % tiles_per_batch == 0)
        def _():
            carry_ref[...] = jnp.zeros_like(carry_ref)

        csum = jnp.dot(tril_ref[...], _pack3(lf), preferred_element_type=f32)
    v = seg(_C_V, FOX_WIDTH)
    if prompt:
        c = jnp.where(lane < N_FOX_HEADS, _unpack3(csum) + carry_ref[0:1, :], 0.0)
        carry_ref[0:1, :] = c[tm - 1:tm, :]
        pieces = (_pack3(c * LOG2E).astype(f32) + jnp.where(lane == 3 * N_FOX_HEADS, 1.0, 0.0)).astype(bf16)
    gf_ref[...] = _silu(seg(_C_G, FOX_WIDTH))
    if prompt:
        aug = jnp.dot(pieces, eb_ref[...], preferred_element_type=f32)
    zs = seg(_C_SU, 2 * SSM_WIDTH)
    zm = seg(_C_MQ, 2 * MEM_WIDTH)
    q = _head_norm(zqf[:, :FOX_WIDTH], hm_ref, qg_ref) * QK_SCALE
    k = _head_norm(zk, hm_ref, kg_ref)
    halves(u_ref, zs[:, :SSM_WIDTH])
    halves(gs_ref, _silu(zs[:, SSM_WIDTH:]))
    mq = _head_norm(zm[:, :MEM_WIDTH], hm_ref.at[0:MEM_WIDTH, 0:MEM_WIDTH], mqg_ref) * QK_SCALE
    mq_ref[...] = mq.astype(bf16)
    gm_ref[...] = _silu(zm[:, MEM_WIDTH:])

    if not prompt:
        k_ref[...] = k
        v_ref[...] = v
        logf_ref[...] = lf
        q_ref[...] = q.astype(bf16)
        return

    kt = k.T
    vt = v.T
    for hd in range(N_FOX_HEADS):
        kt_ref[0, 0, hd] = kt[hd * HEAD_DIM:(hd + 1) * HEAD_DIM, :]
        vt_ref[0, 0, hd] = vt[hd * HEAD_DIM:(hd + 1) * HEAD_DIM, :]
    for later in range(1, kt_ref.shape[0]):
        kt_ref[later] = jnp.zeros(kt_ref.shape[1:], f32)
        vt_ref[later] = jnp.zeros(vt_ref.shape[1:], f32)
    logft_ref[0] = lf.T[0:N_FOX_HEADS, :]

    q = q * LOG2E
    qaug, kaug = aug[:, :LANES], aug[:, LANES:]
    for hd in range(N_FOX_HEADS):
        pair = (hd // 2) * LANES
        own = (lane < HEAD_DIM) if hd % 2 == 0 else (lane >= HEAD_DIM)
        ones_lane = HEAD_DIM if hd % 2 == 0 else 0
        g0 = _bias_group(hd)
        bias = (lane >= g0) & (lane < g0 + 2 * _N_PIECES)
        qa_ref[0, hd] = jnp.where(own, q[:, pair:pair + LANES], jnp.where(bias, qaug, 0.0)).astype(bf16)
        ka_ref[0, hd] = jnp.where(own, k[:, pair:pair + LANES], kaug).astype(bf16)
        va_ref[0, hd] = jnp.where(own, v[:, pair:pair + LANES],
                                  jnp.where(lane == ones_lane, 1.0, 0.0)).astype(bf16)


def _bias_group(hd):
    return (HEAD_DIM if hd % 2 == 0 else 0) + 16 * (hd // 2)


def _bias_selector():
    eb = np.zeros((LANES, 2 * LANES), np.float32)
    one = 3 * N_FOX_HEADS
    for hd in range(N_FOX_HEADS):
        g0 = _bias_group(hd)
        for p in range(_N_PIECES):
            eb[p * N_FOX_HEADS + hd, g0 + p] = 1.0
            eb[one, g0 + _N_PIECES + p] = 1.0
            eb[one, LANES + g0 + p] = 1.0
            eb[p * N_FOX_HEADS + hd, LANES + g0 + _N_PIECES + p] = -1.0
    return jnp.asarray(eb, bf16)


def _head_mean_matrix():
    idx = np.arange(FOX_WIDTH) // HEAD_DIM
    return jnp.asarray((idx[:, None] == idx[None, :]).astype(np.float32) / HEAD_DIM, bf16)


def _in_proj(x, lw, *, batch, prompt, layer=0, depth=1, kv_leaves=None):
    n, d = x.shape
    t = n // batch
    tm = ROW_TILE if prompt else n
    assert n % tm == 0 and (t % tm == 0 or not prompt)
    tiles_per_batch = t // tm if prompt else 1
    grid = (n // tm,)
    row = lambda w: pl.BlockSpec((tm, w), lambda i: (i, 0))
    full = lambda a: pl.BlockSpec(a.shape, lambda i: (0,) * a.ndim)
    split = pl.BlockSpec((2, tm, LANES), lambda i: (0, i, 0))
    by_batch = lambda i: (i // tiles_per_batch, 0, i % tiles_per_batch, 0)

    ins = [x, lw["norm_g"], lw["w_in"], lw["b_forget"], lw["fox_q_norm"], lw["fox_k_norm"],
           lw["mem_q_norm"], lw["head_mean"]]
    in_specs = [row(d)] + [full(a) for a in ins[1:]]
    common = [((n, FOX_WIDTH), f32, row(FOX_WIDTH)), ((2, n, LANES), f32, split), ((2, n, LANES), f32, split),
              ((n, MEM_WIDTH), bf16, row(MEM_WIDTH)), ((n, MEM_WIDTH), f32, row(MEM_WIDTH))]
    scratch = []
    if prompt:
        tril = jnp.asarray(np.tril(np.ones((tm, tm), np.float32)), bf16)
        extra = [tril, lw["bias_sel"]]
        ins += extra
        in_specs += [full(a) for a in extra]
        aliases = {}
        if kv_leaves is None:
            tspec = pl.BlockSpec((depth, 1, N_FOX_HEADS, HEAD_DIM, tm),
                                 lambda i: (0, i // tiles_per_batch, 0, 0, i % tiles_per_batch))
        else:
            aliases = {len(ins): 0, len(ins) + 1: 1}
            ins += list(kv_leaves)
            in_specs += [pl.BlockSpec(memory_space=pl.ANY)] * 2
            tspec = pl.BlockSpec((1, 1, N_FOX_HEADS, HEAD_DIM, tm),
                                 lambda i: (layer, i // tiles_per_batch, 0, 0, i % tiles_per_batch))
        lspec = pl.BlockSpec((1, N_FOX_HEADS, tm), lambda i: (i // tiles_per_batch, 0, i % tiles_per_batch))
        hspec = pl.BlockSpec((1, N_FOX_HEADS, tm, LANES), by_batch)
        tshape = (depth, batch, N_FOX_HEADS, HEAD_DIM, t)
        outs = ([(tshape, f32, tspec), (tshape, f32, tspec), ((batch, N_FOX_HEADS, t), f32, lspec)] + common
                + [((batch, N_FOX_HEADS, t, LANES), bf16, hspec)] * 3)
        names = ["kt", "vt", "logft", "gf", "u", "gs", "mq", "gm", "qa", "ka", "va"]
        scratch = [pltpu.VMEM((8, LANES), f32)]
    else:
        outs = ([((n, FOX_WIDTH), f32, row(FOX_WIDTH)), ((n, FOX_WIDTH), f32, row(FOX_WIDTH)),
                 ((n, LANES), f32, row(LANES))] + common + [((n, FOX_WIDTH), bf16, row(FOX_WIDTH))])
        names = ["k", "v", "logf", "gf", "u", "gs", "mq", "gm", "q"]
        aliases = {}

    res = pl.pallas_call(
        functools.partial(_in_proj_kernel, prompt=prompt, tiles_per_batch=tiles_per_batch, aliased=bool(aliases)),
        grid=grid, in_specs=in_specs, out_specs=[o[2] for o in outs],
        out_shape=[jax.ShapeDtypeStruct(o[0], o[1]) for o in outs],
        scratch_shapes=scratch, input_output_aliases=aliases,
        compiler_params=_cparams(("arbitrary",)),
        name="in_proj_prompt" if prompt else "in_proj_sample",
    )(*ins)
    return dict(zip(names, res))


def _fox_prompt_kernel(qa_ref, ka_ref, va_ref, gate_ref, o_ref, s0_ref, s1_ref, m_ref, acc_ref, *, nq):
    tq = tk = ATTN_TQ
    s_refs = (s0_ref, s1_ref)
    m_ref[...] = jnp.full(m_ref.shape, -jnp.inf, f32)
    acc_ref[...] = jnp.zeros(acc_ref.shape, f32)

    def scores(qb, kb, slot):
        qoff = pl.multiple_of(qb * tq, tq)
        koff = pl.multiple_of(kb * tk, tk)
        for j in range(2):
            s_refs[slot][j] = lax.dot_general(qa_ref[0, j, pl.ds(qoff, tq), :], ka_ref[0, j, pl.ds(koff, tk), :],
                                              (((1,), (1,)), ((), ())), preferred_element_type=f32)

    def softmax_pv(qb, kb, slot, masked):
        koff = pl.multiple_of(kb * tk, tk)
        for j in range(2):
            s = s_refs[slot][j]
            if masked:
                s = jnp.where(_row_iota((tq, tk)) >= _lane_iota((tq, tk)), s, -jnp.inf)
            m = m_ref[qb, j]
            m_new = jnp.maximum(m, jnp.max(s, axis=-1, keepdims=True))
            p = jnp.exp2(s - jnp.concatenate([m_new] * (tk // LANES), axis=1))
            vv = va_ref[0, j, pl.ds(koff, tk), :]
            acc_ref[qb, j] = (jnp.exp2(m - m_new) * acc_ref[qb, j]
                              + jnp.dot(p.astype(bf16), vv, preferred_element_type=f32))
            m_ref[qb, j] = m_new

    def finalize(qb):
        h0 = acc_ref[qb, 0] / acc_ref[qb, 0, :, HEAD_DIM:HEAD_DIM + 1]
        h1 = acc_ref[qb, 1] / acc_ref[qb, 1, :, 0:1]
        rows = pl.ds(pl.multiple_of(qb * tq, tq), tq)
        out = jnp.where(_lane_iota((tq, LANES)) < HEAD_DIM, h0, h1) * gate_ref[0, rows, :]
        o_ref[0, rows, :] = out.astype(bf16)

    def after(qb, kb):
        row_end = kb + 1 >= qb
        last = jnp.logical_and(row_end, qb + 1 >= nq)
        nqb = jnp.where(last, 0, jnp.where(row_end, qb + 1, qb))
        nkb = jnp.where(row_end, 0, kb + 1)
        return nqb, nkb

    def lower_step(qb, kb, slot):
        nqb, nkb = after(qb, kb)
        scores(nqb, nkb, 1 - slot)
        softmax_pv(qb, kb, slot, False)
        return nqb, nkb

    def diag_step(qb, slot):
        nxt = jnp.minimum(qb + 1, nq - 1)
        scores(nxt, nxt, 1 - slot)
        softmax_pv(qb, qb, slot, True)
        finalize(qb)

    def steps_per_trip(n):
        return max(d for d in range(2, ATTN_UNROLL + 1, 2) if n % d == 0)

    n_lower = nq * (nq - 1) // 2
    if n_lower:
        scores(1, 0, 0)
        per = steps_per_trip(n_lower)

        def lower_body(i, carry):
            for k in range(per):
                carry = lower_step(*carry, k % 2)
            return carry

        lax.fori_loop(0, n_lower // per, lower_body, (jnp.int32(1), jnp.int32(0)))
    else:
        scores(0, 0, 0)

    per_diag = steps_per_trip(nq)

    def diag_body(i, carry):
        for k in range(per_diag):
            diag_step(per_diag * i + k, k % 2)
        return carry

    lax.fori_loop(0, nq // per_diag, diag_body, 0)


def _fox_prompt(qa, ka, va, gate):
    b, nh, t, _ = qa.shape
    nq = t // ATTN_TQ
    assert t % ATTN_TQ == 0 and nq % 2 == 0 and (nq * (nq - 1) // 2) % 2 == 0
    whole = pl.BlockSpec((1, 2, t, LANES), lambda bi, hp: (bi, hp, 0, 0))
    return pl.pallas_call(
        functools.partial(_fox_prompt_kernel, nq=nq),
        grid=(b, nh // 2),
        in_specs=[whole, whole, whole, pl.BlockSpec((1, t, LANES), lambda bi, hp: (bi, 0, hp))],
        out_specs=pl.BlockSpec((1, t, LANES), lambda bi, hp: (bi, 0, hp)),
        out_shape=jax.ShapeDtypeStruct((b, t, FOX_WIDTH), bf16),
        scratch_shapes=[pltpu.VMEM((2, ATTN_TQ, ATTN_TQ), f32), pltpu.VMEM((2, ATTN_TQ, ATTN_TQ), f32),
                        pltpu.VMEM((nq, 2, ATTN_TQ, LANES), f32), pltpu.VMEM((nq, 2, ATTN_TQ, LANES), f32)],
        compiler_params=_cparams(("arbitrary", "arbitrary")),
        name="fox_prompt",
    )(qa, ka, va, gate)


def _fox_sample_kernel(q_ref, kn_ref, vn_ref, ln_ref, gate_ref, kc_ref, vc_ref, lc_ref, triu_ref, o_ref,
                       kall_ref, vall_ref, *, past, tnew):
    nkeys = kall_ref.shape[1]
    nq = N_FOX_HEADS * tnew

    def new_cols(ref):
        z = jnp.concatenate([ref[...], jnp.zeros((LANES - tnew, ref.shape[1]), f32)], axis=0)
        return z.T

    kall_ref[:, 0:past] = kc_ref[0, 0].astype(bf16)
    vall_ref[:, 0:past] = vc_ref[0, 0].astype(bf16)
    kall_ref[:, past:nkeys] = new_cols(kn_ref).astype(bf16)
    vall_ref[:, past:nkeys] = new_cols(vn_ref).astype(bf16)

    l_all = jnp.concatenate([lc_ref[0, 0], new_cols(ln_ref)[0:N_FOX_HEADS, :]], axis=1)
    pieces = jnp.concatenate(_split3(l_all), axis=0).astype(bf16)
    c3 = jnp.dot(pieces, triu_ref[...], preferred_element_type=f32)
    c = c3[0:8] + c3[8:16] + c3[16:24]
    crow = jnp.concatenate([jnp.broadcast_to(c[hd:hd + 1, :], (tnew, nkeys)) for hd in range(N_FOX_HEADS)],
                           axis=0)
    rq = _row_iota((nq, LANES))
    cq = jnp.sum(jnp.where(_lane_iota((nq, LANES)) == rq % tnew, crow[:, past:nkeys], 0.0),
                 axis=1, keepdims=True)

    qrep = jnp.concatenate([q_ref[...]] * N_FOX_HEADS, axis=0)
    own = _row_iota((nq, FOX_WIDTH)) // tnew == _lane_iota((nq, FOX_WIDTH)) // HEAD_DIM
    qbd = jnp.where(own, qrep, jnp.zeros_like(qrep))
    s = jnp.dot(qbd, kall_ref[...], preferred_element_type=f32) + cq - crow
    s = jnp.where(_lane_iota((nq, nkeys)) <= past + _row_iota((nq, nkeys)) % tnew, s, -jnp.inf)
    p = jnp.exp(s - jnp.max(s, axis=1, keepdims=True))
    o2 = lax.dot_general(p.astype(bf16), vall_ref[...], (((1,), (1,)), ((), ())), preferred_element_type=f32)
    o2 = o2 / jnp.sum(p, axis=1, keepdims=True)
    olane = _lane_iota((tnew, FOX_WIDTH))
    out = jnp.zeros((tnew, FOX_WIDTH), f32)
    for hd in range(N_FOX_HEADS):
        out = out + jnp.where(olane // HEAD_DIM == hd, o2[hd * tnew:(hd + 1) * tnew, :], 0.0)
    o_ref[...] = (out * gate_ref[...]).astype(bf16)


def _fox_sample(q, k_new, v_new, logf_new, gate, cache_kt, cache_vt, cache_lt, layer, *, batch):
    n = q.shape[0]
    tnew = n // batch
    past = cache_kt.shape[3]
    nkeys = past + LANES
    assert N_FOX_HEADS * tnew == LANES and past % LANES == 0
    triu = jnp.asarray(np.triu(np.ones((nkeys, nkeys), np.float32)), bf16)
    rows = lambda w: pl.BlockSpec((tnew, w), lambda bi: (bi, 0))
    cache = lambda h: pl.BlockSpec((1, 1, h, past), lambda bi: (layer, bi, 0, 0))
    return pl.pallas_call(
        functools.partial(_fox_sample_kernel, past=past, tnew=tnew),
        grid=(batch,),
        in_specs=[rows(FOX_WIDTH), rows(FOX_WIDTH), rows(FOX_WIDTH), rows(LANES), rows(FOX_WIDTH),
                  cache(FOX_WIDTH), cache(FOX_WIDTH), cache(N_FOX_HEADS),
                  pl.BlockSpec((nkeys, nkeys), lambda bi: (0, 0))],
        out_specs=rows(FOX_WIDTH),
        out_shape=jax.ShapeDtypeStruct((n, FOX_WIDTH), bf16),
        scratch_shapes=[pltpu.VMEM((FOX_WIDTH, nkeys), bf16), pltpu.VMEM((FOX_WIDTH, nkeys), bf16)],
        compiler_params=_cparams(("arbitrary",)),
        name="fox_sample",
    )(q, k_new, v_new, logf_new, gate, cache_kt, cache_vt, cache_lt, triu)


def _s5_prep_kernel(are_r, aim_r, ldt_r, are_c, aim_c, ldt_c, bxr_ref, bxi_ref, cxr_ref, cxi_ref,
                    bst_ref, cst_ref, kt_ref, apow_ref):
    L, W, M = S5_CHUNK, SSM_WIDTH, SSM_MODES

    def cmul(xr, xi, yr, yi):
        return xr * yr - xi * yi, xr * yi + xi * yr

    def powers(are, aim, ldt, n):
        dt = jnp.exp(ldt[...])
        mag = jnp.exp(dt * are[...])
        p1 = (mag * jnp.cos(dt * aim[...]), mag * jnp.sin(dt * aim[...]))
        out = [(jnp.ones_like(mag), jnp.zeros_like(mag)), p1]
        for _ in range(n - 1):
            out.append(cmul(*out[-1], *p1))
        return out

    prow = powers(are_r, aim_r, ldt_r, L)
    pcol = powers(are_c, aim_c, ldt_c, L)
    ar, ai = are_r[...], aim_r[...]
    abr, abi = prow[1]
    den = ar * ar + ai * ai
    zr = ((abr - 1.0) * ar + abi * ai) / den
    zi = (abi * ar - (abr - 1.0) * ai) / den
    bbr, bbi = cmul(zr, zi, bxr_ref[...], bxi_ref[...])

    def b_pow(k):
        return jnp.concatenate(cmul(*prow[k], bbr, bbi), axis=1)

    def c_pow(k):
        pr, pi = pcol[k]
        cr, ci = cxr_ref[...], cxi_ref[...]
        return jnp.concatenate([cr * pr - ci * pi, -(cr * pi + ci * pr)], axis=0)

    cpow = [c_pow(k) for k in range(L + 1)]
    for tl in range(L):
        bst_ref[tl * W:(tl + 1) * W, :] = b_pow(L - 1 - tl).astype(bf16)
        cst_ref[:, tl * W:(tl + 1) * W] = cpow[tl + 1].astype(bf16)

    def hi_lo(x):
        hi = x.astype(bf16)
        return hi, (x - hi.astype(f32)).astype(bf16)

    bh, bl = hi_lo(b_pow(0))
    ch, cl = hi_lo(jnp.concatenate(cpow[:L], axis=1))
    kd = (jnp.dot(bh, ch, preferred_element_type=f32) + jnp.dot(bh, cl, preferred_element_type=f32)
          + jnp.dot(bl, ch, preferred_element_type=f32)).astype(bf16)
    for tl in range(L):
        if tl:
            kt_ref[tl * W:(tl + 1) * W, 0:tl * W] = jnp.zeros((W, tl * W), bf16)
        kt_ref[tl * W:(tl + 1) * W, tl * W:] = kd[:, :(L - tl) * W]

    step = prow[L]
    seg = step
    for _ in range(int(math.log2(S5_SEG))):
        seg = cmul(*seg, *seg)
    apow_ref[...] = jnp.concatenate(
        [jnp.concatenate(step, axis=1), jnp.concatenate(seg, axis=1), jnp.zeros((6, 2 * M), f32)], axis=0)


def _s5_prep(a_re, a_im, log_dt, b_re, b_im, c_re, c_im):
    G, N, P, L = SSM_GROUPS, SSM_STATE, SSM_CH, S5_CHUNK
    W, M = SSM_WIDTH, SSM_MODES
    eye = jnp.eye(G, dtype=f32)
    bx = lambda b: (jnp.swapaxes(b, 1, 2)[:, :, None, :] * eye[:, None, :, None]).reshape(W, M)
    cx = lambda c: (jnp.swapaxes(c, 1, 2)[:, :, None, :] * eye[:, None, :, None]).reshape(M, W)
    ldt = jnp.repeat(log_dt, N)
    ins = [a_re.reshape(1, M), a_im.reshape(1, M), ldt.reshape(1, M),
           a_re.reshape(M, 1), a_im.reshape(M, 1), ldt.reshape(M, 1),
           bx(b_re), bx(b_im), cx(c_re), cx(c_im)]
    return pl.pallas_call(
        _s5_prep_kernel,
        out_shape=[jax.ShapeDtypeStruct((L * W, 2 * M), bf16), jax.ShapeDtypeStruct((2 * M, L * W), bf16),
                   jax.ShapeDtypeStruct((L * W, L * W), bf16), jax.ShapeDtypeStruct((8, 2 * M), f32)],
        compiler_params=pltpu.CompilerParams(vmem_limit_bytes=VMEM_LIMIT),
        name="s5_prep",
    )(*ins)


def _s5_kernel(u_ref, gs_ref, h0_ref, bst_ref, cst_ref, kt_ref, apow_ref, d_ref, wg_ref, bg_ref,
               o_ref, hT_ref, ds_ref, sp_ref, carry_ref, *, chained, seglen):
    L, W, M = S5_CHUNK, SSM_WIDTH, SSM_MODES
    nr = 8 * seglen
    nlb = M // LANES

    def chunked(ref, tl):
        return jnp.concatenate([ref[0, pl.ds(tl, nr, stride=L), :], ref[1, pl.ds(tl, nr, stride=L), :]], axis=1)

    uf = [chunked(u_ref, tl) for tl in range(L)]
    ub = jnp.concatenate(uf, axis=1).astype(bf16)
    ds = jnp.dot(ub, bst_ref[...], preferred_element_type=f32)
    for lb in range(2 * nlb):
        ds_ref[lb] = ds[:, lb * LANES:(lb + 1) * LANES]
    def blocks(row):
        return ([row[:, lb * LANES:(lb + 1) * LANES] for lb in range(nlb)],
                [row[:, M + lb * LANES:M + (lb + 1) * LANES] for lb in range(nlb)])

    def bcast8(parts):
        return [jnp.broadcast_to(p, (8, LANES)) for p in parts]

    ar, ai = (bcast8(p) for p in blocks(apow_ref[0:1, :]))

    def scan(vr, vi, store):
        for i in range(seglen):
            rows = pl.ds(i, 8, stride=seglen)
            for lb in range(nlb):
                if store:
                    sp_ref[lb, rows, :] = vr[lb]
                    sp_ref[nlb + lb, rows, :] = vi[lb]
                dr, di = ds_ref[lb, rows, :], ds_ref[nlb + lb, rows, :]
                vr[lb], vi[lb] = (ar[lb] * vr[lb] - ai[lb] * vi[lb] + dr,
                                  ar[lb] * vi[lb] + ai[lb] * vr[lb] + di)
        return vr, vi

    if chained:
        ti = pl.program_id(1)

        @pl.when(ti == 0)
        def _():
            carry_ref[0:1, :] = h0_ref[0]

        zero = [jnp.zeros((8, LANES), f32) for _ in range(nlb)]
        er, ei = scan(list(zero), list(zero), store=False)
        pr, pi = blocks(apow_ref[1:2, :])
        cr, ci = blocks(carry_ref[0:1, :])
        sr, si = [[] for _ in range(nlb)], [[] for _ in range(nlb)]
        for j in range(8):
            for lb in range(nlb):
                sr[lb].append(cr[lb])
                si[lb].append(ci[lb])
                cr[lb], ci[lb] = (pr[lb] * cr[lb] - pi[lb] * ci[lb] + er[lb][j:j + 1, :],
                                  pr[lb] * ci[lb] + pi[lb] * cr[lb] + ei[lb][j:j + 1, :])
        final = jnp.concatenate(cr + ci, axis=1)
        carry_ref[0:1, :] = final
        hT_ref[0] = final
        scan([jnp.concatenate(x, axis=0) for x in sr], [jnp.concatenate(x, axis=0) for x in si], store=True)
    else:
        vr, vi = blocks(h0_ref[...])
        vr, vi = scan(vr, vi, store=True)
        hT_ref[...] = jnp.concatenate(vr + vi, axis=1)

    spb = jnp.concatenate([sp_ref[lb] for lb in range(2 * nlb)], axis=1).astype(bf16)

    def y_pair(pair):
        kk = (2 * pair + 2) * W
        cols = slice(2 * pair * W, kk)
        return (jnp.dot(ub[:, :kk], kt_ref[0:kk, cols], preferred_element_type=f32)
                + jnp.dot(spb, cst_ref[:, cols], preferred_element_type=f32))

    def finish(pair, ypair):
        for tl in (2 * pair, 2 * pair + 1):
            y = ypair[:, (tl % 2) * W:(tl % 2 + 1) * W] + d_ref[...] * uf[tl]
            g = _gelu_tanh(y)
            gate = jax.nn.sigmoid(jnp.dot(g.astype(bf16), wg_ref[...], preferred_element_type=f32) + bg_ref[...])
            out = g * gate * chunked(gs_ref, tl)
            o_ref[0, pl.ds(tl, nr, stride=L), :] = out[:, :LANES]
            o_ref[1, pl.ds(tl, nr, stride=L), :] = out[:, LANES:]

    ycur = y_pair(0)
    for pair in range(L // 2):
        ynext = y_pair(pair + 1) if pair + 1 < L // 2 else None
        finish(pair, ycur)
        ycur = ynext


def _s5(u, gs, h0, ops, d_row, w_glu, b_glu, *, batch, chained):
    L, W, M = S5_CHUNK, SSM_WIDTH, SSM_MODES
    n = u.shape[1]
    t = n // batch
    assert t % L == 0
    chunks = t // L
    bst, cst, kt, apow = ops
    if chained:
        seglen = S5_SEG
        assert chunks % (8 * seglen) == 0
        grid = (batch, chunks // (8 * seglen))
        state = pl.BlockSpec((1, 1, 2 * M), lambda bi, ti: (bi, 0, 0))
    else:
        seglen = chunks
        assert batch == 8
        grid = (1, 1)
        state = pl.BlockSpec((8, 2 * M), lambda bi, ti: (0, 0))
    tiles = grid[1]
    nr = 8 * seglen
    blk = pl.BlockSpec((2, nr * L, LANES), lambda bi, ti: (0, bi * tiles + ti, 0))
    full = lambda a: pl.BlockSpec(a.shape, lambda bi, ti: (0,) * a.ndim)
    once = lambda a: pl.BlockSpec(a.shape, lambda bi, ti: (0,) * a.ndim, pipeline_mode=pl.Buffered(1))
    return pl.pallas_call(
        functools.partial(_s5_kernel, chained=chained, seglen=seglen),
        grid=grid,
        in_specs=[blk, blk, state, once(bst), once(cst), once(kt), full(apow), full(d_row),
                  full(w_glu), full(b_glu)],
        out_specs=[blk, state],
        out_shape=[jax.ShapeDtypeStruct((2, n, LANES), f32), jax.ShapeDtypeStruct(h0.shape, f32)],
        scratch_shapes=[pltpu.VMEM((2 * M // LANES, nr, LANES), f32), pltpu.VMEM((2 * M // LANES, nr, LANES), f32),
                        pltpu.VMEM((8, 2 * M), f32)],
        compiler_params=_cparams(("arbitrary", "arbitrary")),
        name="s5_prompt" if chained else "s5_sample",
    )(u, gs, h0, bst, cst, kt, apow, d_row, w_glu, b_glu)


def _memory_kv_kernel(mem_ref, g_ref, w_ref, kg_ref, hm_ref, mkt_ref, mvt_ref):
    batch, _, nmem = mkt_ref.shape
    x = mem_ref[...]
    ms = jnp.mean(x * x, axis=-1, keepdims=True)
    h = (x * lax.rsqrt(ms + EPS) * g_ref[...]).astype(bf16)
    mk = jnp.dot(h, w_ref[:, 0:MEM_WIDTH], preferred_element_type=f32)
    mk = _head_norm(mk, hm_ref.at[0:MEM_WIDTH, 0:MEM_WIDTH], kg_ref)
    mv = jnp.dot(h, w_ref[:, MEM_WIDTH:2 * MEM_WIDTH], preferred_element_type=f32)
    for b in range(batch):
        mkt_ref[b] = mk[b * nmem:(b + 1) * nmem, :].T
        mvt_ref[b] = mv[b * nmem:(b + 1) * nmem, :].T


def _memory_kv(mem, lw, *, batch):
    n = mem.shape[0]
    return pl.pallas_call(
        _memory_kv_kernel,
        out_shape=[jax.ShapeDtypeStruct((batch, MEM_WIDTH, n // batch), f32)] * 2,
        compiler_params=pltpu.CompilerParams(vmem_limit_bytes=VMEM_LIMIT),
        name="memory_kv",
    )(mem, lw["mem_norm"], lw["w_mem_kv"], lw["mem_k_norm"], lw["head_mean"])


def _mix_out_kernel(x_ref, fox_ref, ssm_ref, mq_ref, gm_ref, mkt_ref, mvt_ref, w_ref, o_ref):
    tm = x_ref.shape[0]
    nmem = mkt_ref.shape[2]
    mq = mq_ref[...]
    mkt = mkt_ref[0]
    mvt = mvt_ref[0]
    khead = _row_iota((MEM_WIDTH, nmem)) // HEAD_DIM
    zero = jnp.zeros_like(mkt)
    mk_heads = jnp.concatenate([jnp.where(khead == hd, mkt, zero) for hd in range(N_MEM_HEADS)], axis=1)
    s_all = jnp.dot(mq, mk_heads, preferred_element_type=f32)
    ssm = jnp.concatenate([ssm_ref[0], ssm_ref[1]], axis=1).astype(bf16)
    y = x_ref[...]
    y = y + jnp.dot(fox_ref[...], w_ref[0:FOX_WIDTH, :], preferred_element_type=f32)
    y = y + jnp.dot(ssm, w_ref[FOX_WIDTH:FOX_WIDTH + SSM_WIDTH, :], preferred_element_type=f32)
    mem = jnp.zeros((tm, MEM_WIDTH), f32)
    for hd in range(N_MEM_HEADS):
        s = s_all[:, hd * nmem:(hd + 1) * nmem]
        p = jnp.exp(s - jnp.max(s, axis=-1, keepdims=True))
        p = p / jnp.sum(p, axis=-1, keepdims=True)
        vh = jnp.where(khead == hd, mvt, zero)
        mem = mem + lax.dot_general(p.astype(bf16), vh, (((1,), (1,)), ((), ())), preferred_element_type=f32)
    memg = (mem * gm_ref[...]).astype(bf16)
    o_ref[...] = y + jnp.dot(memg, w_ref[FOX_WIDTH + SSM_WIDTH:, :], preferred_element_type=f32)


def _mix_out(x, fox, ssm, mq, gm, mkt, mvt, w_out, *, batch, tm):
    n, d = x.shape
    t = n // batch
    assert t % tm == 0
    tiles_per_batch = t // tm
    row = lambda w: pl.BlockSpec((tm, w), lambda i: (i, 0))
    memb = pl.BlockSpec((1,) + mkt.shape[1:], lambda i: (i // tiles_per_batch, 0, 0))
    return pl.pallas_call(
        _mix_out_kernel,
        grid=(n // tm,),
        in_specs=[row(d), row(FOX_WIDTH), pl.BlockSpec((2, tm, LANES), lambda i: (0, i, 0)),
                  row(MEM_WIDTH), row(MEM_WIDTH), memb, memb, pl.BlockSpec(w_out.shape, lambda i: (0, 0))],
        out_specs=row(d),
        out_shape=jax.ShapeDtypeStruct((n, d), f32),
        compiler_params=_cparams(("arbitrary",)),
        name="mix_out_prompt" if tiles_per_batch > 1 else "mix_out_sample",
    )(x, fox, ssm, mq, gm, mkt, mvt, w_out)


def _layer_weights(l, norm_g, w_in, b_forget, fox_q_norm, fox_k_norm, mem_q_norm, mem_norm, w_mem_kv,
                   mem_k_norm, w_out, w_glu, b_glu, ssm_d):
    w = w_in[l]
    nf = 4 * FOX_WIDTH
    d = w.shape[0]
    w_packed = jnp.concatenate(
        [w[:, :FOX_WIDTH], w[:, nf:nf + N_FOX_HEADS], jnp.zeros((d, LANES - N_FOX_HEADS), w.dtype),
         w[:, FOX_WIDTH:nf], w[:, nf + N_FOX_HEADS:]], axis=1).astype(bf16)
    assert w_packed.shape[1] == _W_COLS
    return dict(
        norm_g=norm_g[l].reshape(1, d), w_in=w_packed,
        b_forget=jnp.pad(b_forget[l], (0, LANES - N_FOX_HEADS)).reshape(1, LANES),
        fox_q_norm=jnp.tile(fox_q_norm[l], N_FOX_HEADS).reshape(1, FOX_WIDTH),
        fox_k_norm=jnp.tile(fox_k_norm[l], N_FOX_HEADS).reshape(1, FOX_WIDTH),
        mem_q_norm=jnp.tile(mem_q_norm[l], N_MEM_HEADS).reshape(1, MEM_WIDTH),
        mem_k_norm=jnp.tile(mem_k_norm[l], N_MEM_HEADS).reshape(1, MEM_WIDTH),
        mem_norm=mem_norm[l].reshape(1, d), w_mem_kv=w_mem_kv[l].astype(bf16),
        head_mean=_head_mean_matrix(), bias_sel=_bias_selector(),
        w_out=w_out[l].astype(bf16), w_glu=w_glu[l].astype(bf16), b_glu=b_glu[l].reshape(1, SSM_WIDTH),
        ssm_d=ssm_d[l].reshape(1, SSM_WIDTH))


def _split_state(hT, batch):
    hT = hT.reshape(batch, 2, SSM_GROUPS, SSM_STATE)
    return hT[:, 0], hT[:, 1]


def kernel(x_prompt, x_sample, mem_prompt, cache_fox_k, cache_fox_v, cache_fox_logf, state_ssm_re, state_ssm_im, cache_mem_k, cache_mem_v, norm_g, w_in, b_forget, fox_q_norm, fox_k_norm, ssm_a_re, ssm_a_im, ssm_log_dt, ssm_b_re, ssm_b_im, ssm_c_re, ssm_c_im, ssm_d, w_glu, b_glu, mem_norm, w_mem_kv, mem_q_norm, mem_k_norm, w_out):
    B, T, D = x_prompt.shape
    Bs, Ts, _ = x_sample.shape
    depth = w_in.shape[0]
    past = cache_fox_k.shape[2]
    nmem = mem_prompt.shape[1]
    M = SSM_MODES

    xp = x_prompt.reshape(B * T, D)
    xs = x_sample.reshape(Bs * Ts, D)
    mem = mem_prompt.reshape(B * nmem, D)
    cache_kt = jnp.transpose(cache_fox_k, (0, 1, 3, 4, 2)).reshape(depth, Bs, FOX_WIDTH, past)
    cache_vt = jnp.transpose(cache_fox_v, (0, 1, 3, 4, 2)).reshape(depth, Bs, FOX_WIDTH, past)
    cache_lt = jnp.transpose(cache_fox_logf, (0, 1, 3, 2)).astype(f32)
    cache_mkt = jnp.transpose(cache_mem_k, (0, 1, 3, 4, 2)).reshape(depth, Bs, MEM_WIDTH, nmem).astype(bf16)
    cache_mvt = jnp.transpose(cache_mem_v, (0, 1, 3, 4, 2)).reshape(depth, Bs, MEM_WIDTH, nmem).astype(bf16)

    outs = {k: [] for k in ("pf", "pre", "pim", "pmk", "pmv", "sk", "sv", "sf", "sre", "sim")}
    kv_leaves = None
    for l in range(depth):
        lw = _layer_weights(l, norm_g, w_in, b_forget, fox_q_norm, fox_k_norm, mem_q_norm, mem_norm,
                            w_mem_kv, mem_k_norm, w_out, w_glu, b_glu, ssm_d)
        ops = _s5_prep(ssm_a_re[l], ssm_a_im[l], ssm_log_dt[l], ssm_b_re[l], ssm_b_im[l],
                       ssm_c_re[l], ssm_c_im[l])
        mkt, mvt = _memory_kv(mem, lw, batch=B)

        pr = _in_proj(xp, lw, batch=B, prompt=True, layer=l, depth=depth, kv_leaves=kv_leaves)
        kv_leaves = (pr["kt"], pr["vt"])
        fox = _fox_prompt(pr["qa"], pr["ka"], pr["va"], pr["gf"].reshape(B, T, FOX_WIDTH)).reshape(B * T, FOX_WIDTH)
        ssm, hT = _s5(pr["u"], pr["gs"], jnp.zeros((B, 1, 2 * M), f32), ops, lw["ssm_d"], lw["w_glu"],
                      lw["b_glu"], batch=B, chained=True)
        xp = _mix_out(xp, fox, ssm, pr["mq"], pr["gm"], mkt.astype(bf16), mvt.astype(bf16),
                      lw["w_out"], batch=B, tm=ROW_TILE)
        hre, him = _split_state(hT, B)
        outs["pf"].append(pr["logft"])
        outs["pre"].append(hre)
        outs["pim"].append(him)
        outs["pmk"].append(mkt.reshape(B, N_MEM_HEADS, HEAD_DIM, nmem))
        outs["pmv"].append(mvt.reshape(B, N_MEM_HEADS, HEAD_DIM, nmem))

        sr = _in_proj(xs, lw, batch=Bs, prompt=False)
        fox_s = _fox_sample(sr["q"], sr["k"], sr["v"], sr["logf"], sr["gf"], cache_kt, cache_vt, cache_lt, l,
                            batch=Bs)
        h0 = jnp.concatenate([state_ssm_re[l].reshape(Bs, M), state_ssm_im[l].reshape(Bs, M)],
                             axis=-1).astype(f32)
        ssm_s, hT_s = _s5(sr["u"], sr["gs"], h0, ops, lw["ssm_d"], lw["w_glu"], lw["b_glu"], batch=Bs,
                          chained=False)
        xs = _mix_out(xs, fox_s, ssm_s, sr["mq"], sr["gm"], cache_mkt[l], cache_mvt[l],
                      lw["w_out"], batch=Bs, tm=Ts)
        sre, sim = _split_state(hT_s, Bs)
        outs["sk"].append(sr["k"].reshape(Bs, Ts, N_FOX_HEADS, HEAD_DIM))
        outs["sv"].append(sr["v"].reshape(Bs, Ts, N_FOX_HEADS, HEAD_DIM))
        outs["sf"].append(sr["logf"][:, :N_FOX_HEADS].reshape(Bs, Ts, N_FOX_HEADS))
        outs["sre"].append(sre)
        outs["sim"].append(sim)

    st = {k: jnp.stack(v) for k, v in outs.items()}
    fox_k_prompt = jnp.transpose(kv_leaves[0], (0, 1, 4, 2, 3))
    fox_v_prompt = jnp.transpose(kv_leaves[1], (0, 1, 4, 2, 3))
    fox_logf_prompt = jnp.transpose(st["pf"], (0, 1, 3, 2))
    mem_k_prompt = jnp.transpose(st["pmk"], (0, 1, 4, 2, 3))
    mem_v_prompt = jnp.transpose(st["pmv"], (0, 1, 4, 2, 3))
    return (xp.reshape(B, T, D), xs.reshape(Bs, Ts, D), fox_k_prompt, fox_v_prompt, fox_logf_prompt,
            st["pre"], st["pim"], mem_k_prompt, mem_v_prompt, st["sk"], st["sv"], st["sf"], st["sre"], st["sim"])
```

```python
import functools
import math

import numpy as np
import jax
import jax.numpy as jnp
from jax import lax
from jax.experimental import pallas as pl
from jax.experimental.pallas import tpu as pltpu

f32 = jnp.float32
bf16 = jnp.bfloat16

HEAD_DIM = 64
N_FOX_HEADS = 8
FOX_WIDTH = N_FOX_HEADS * HEAD_DIM
SSM_GROUPS = 16
SSM_CH = 16
SSM_STATE = 64
SSM_WIDTH = SSM_GROUPS * SSM_CH
SSM_MODES = SSM_GROUPS * SSM_STATE
N_MEM_HEADS = 4
MEM_WIDTH = N_MEM_HEADS * HEAD_DIM
EPS = 1e-6
QK_SCALE = HEAD_DIM ** -0.5
LOG2E = math.log2(math.e)

LANES = 128
S5_CHUNK = 8
S5_SEG = 16
ROW_TILE = 512
ATTN_TQ = 512
ATTN_UNROLL = 14
VMEM_LIMIT = 56 * 1024 * 1024

_C_Q, _C_F, _C_K, _C_V, _C_G = 0, 512, 640, 1152, 1664
_C_SU, _C_SG, _C_MQ, _C_MG = 2176, 2432, 2688, 2944
_W_COLS = 3200
_N_PIECES = 3


def _cparams(sem):
    return pltpu.CompilerParams(dimension_semantics=sem, vmem_limit_bytes=VMEM_LIMIT)


def _wspec(a, layer):
    if a.ndim == 3:
        return pl.BlockSpec((None,) + a.shape[1:], lambda *_: (layer, 0, 0))
    return pl.BlockSpec(a.shape, lambda *_: (0,) * a.ndim)


def _silu(x):
    return x * jax.nn.sigmoid(x)


def _log_sigmoid(x):
    return -(jnp.maximum(-x, 0.0) + jnp.log1p(jnp.exp(-jnp.abs(x))))


def _gelu_tanh(x):
    return 0.5 * x * (1.0 + jnp.tanh(math.sqrt(2.0 / math.pi) * (x + 0.044715 * (x * x * x))))


def _split3(x):
    hi = x.astype(bf16).astype(f32)
    r = x - hi
    mid = r.astype(bf16).astype(f32)
    lo = (r - mid).astype(bf16).astype(f32)
    return hi, mid, lo


def _pack3(x):
    hi, mid, lo = _split3(x)
    return (hi + pltpu.roll(mid, 8, 1) + pltpu.roll(lo, 16, 1)).astype(bf16)


def _unpack3(y):
    return y + pltpu.roll(y, LANES - 8, 1) + pltpu.roll(y, LANES - 16, 1)


def _lane_iota(shape):
    return lax.broadcasted_iota(jnp.int32, shape, len(shape) - 1)


def _row_iota(shape):
    return lax.broadcasted_iota(jnp.int32, shape, len(shape) - 2)


def _head_norm(z, hm_ref, g_ref):
    ms = jnp.dot((z * z).astype(bf16), hm_ref[...], preferred_element_type=f32)
    return z * lax.rsqrt(ms + EPS) * g_ref[...]


def _in_proj_kernel(*refs, prompt, tiles_per_batch, aliased):
    if prompt:
        if aliased:
            refs = refs[:10] + refs[12:]
        (x_ref, ng_ref, w_ref, bf_ref, qg_ref, kg_ref, mqg_ref, hm_ref, tril_ref, eb_ref,
         kt_ref, vt_ref, logft_ref, gf_ref, u_ref, gs_ref, mq_ref, gm_ref, qa_ref, ka_ref, va_ref,
         carry_ref) = refs
    else:
        (x_ref, ng_ref, w_ref, bf_ref, qg_ref, kg_ref, mqg_ref, hm_ref,
         k_ref, v_ref, logf_ref, gf_ref, u_ref, gs_ref, mq_ref, gm_ref, q_ref) = refs

    x = x_ref[...]
    tm = x.shape[0]
    lane = _lane_iota((tm, LANES))
    ms = jnp.mean(x * x, axis=-1, keepdims=True)
    h = (x * lax.rsqrt(ms + EPS) * ng_ref[...]).astype(bf16)

    def seg(lo, width):
        return jnp.dot(h, w_ref[:, lo:lo + width], preferred_element_type=f32)

    def halves(ref, z):
        ref[0] = z[:, :LANES]
        ref[1] = z[:, LANES:]

    zqf = seg(_C_Q, FOX_WIDTH + LANES)
    logf = _log_sigmoid(zqf[:, FOX_WIDTH:] + bf_ref[...])
    lf = jnp.where(lane < N_FOX_HEADS, logf, 0.0)
    zk = seg(_C_K, FOX_WIDTH)
    if prompt:
        i = pl.program_id(0)

        @pl.when(i % tiles_per_batch == 0)
        def _():
            carry_ref[...] = jnp.zeros_like(carry_ref)

        csum = jnp.dot(tril_ref[...], _pack3(lf), preferred_element_type=f32)
    v = seg(_C_V, FOX_WIDTH)
    if prompt:
        c = jnp.where(lane < N_FOX_HEADS, _unpack3(csum) + carry_ref[0:1, :], 0.0)
        carry_ref[0:1, :] = c[tm - 1:tm, :]
        pieces = (_pack3(c * LOG2E).astype(f32) + jnp.where(lane == 3 * N_FOX_HEADS, 1.0, 0.0)).astype(bf16)
    if prompt:
        aug = jnp.dot(pieces, eb_ref[...], preferred_element_type=f32)
    q = _head_norm(zqf[:, :FOX_WIDTH], hm_ref, qg_ref) * QK_SCALE
    k = _head_norm(zk, hm_ref, kg_ref)

    def gate_proj():
        gf_ref[...] = _silu(seg(_C_G, FOX_WIDTH))

    def ssm_proj():
        zs = seg(_C_SU, 2 * SSM_WIDTH)
        halves(u_ref, zs[:, :SSM_WIDTH])
        halves(gs_ref, _silu(zs[:, SSM_WIDTH:]))

    def mem_proj():
        zm = seg(_C_MQ, 2 * MEM_WIDTH)
        mq = _head_norm(zm[:, :MEM_WIDTH], hm_ref.at[0:MEM_WIDTH, 0:MEM_WIDTH], mqg_ref) * QK_SCALE
        mq_ref[...] = mq.astype(bf16)
        gm_ref[...] = _silu(zm[:, MEM_WIDTH:])

    late = [gate_proj, ssm_proj, mem_proj]
    if not prompt:
        for proj in late:
            proj()
        k_ref[...] = k
        v_ref[...] = v
        logf_ref[...] = lf
        q_ref[...] = q.astype(bf16)
        return

    kt = k.T
    vt = v.T
    for hd in range(N_FOX_HEADS):
        kt_ref[0, 0, hd] = kt[hd * HEAD_DIM:(hd + 1) * HEAD_DIM, :]
        vt_ref[0, 0, hd] = vt[hd * HEAD_DIM:(hd + 1) * HEAD_DIM, :]
    for later in range(1, kt_ref.shape[0]):
        kt_ref[later] = jnp.zeros(kt_ref.shape[1:], f32)
        vt_ref[later] = jnp.zeros(vt_ref.shape[1:], f32)
    logft_ref[0] = lf.T[0:N_FOX_HEADS, :]

    q = q * LOG2E
    qaug, kaug = aug[:, :LANES], aug[:, LANES:]
    for hd in range(N_FOX_HEADS):
        if hd % 3 == 0 and late:
            late.pop(0)()
        pair = (hd // 2) * LANES
        own = (lane < HEAD_DIM) if hd % 2 == 0 else (lane >= HEAD_DIM)
        ones_lane = HEAD_DIM if hd % 2 == 0 else 0
        g0 = _bias_group(hd)
        bias = (lane >= g0) & (lane < g0 + 2 * _N_PIECES)
        qa_ref[0, hd] = jnp.where(own, q[:, pair:pair + LANES], jnp.where(bias, qaug, 0.0)).astype(bf16)
        ka_ref[0, hd] = jnp.where(own, k[:, pair:pair + LANES], kaug).astype(bf16)
        va_ref[0, hd] = jnp.where(own, v[:, pair:pair + LANES],
                                  jnp.where(lane == ones_lane, 1.0, 0.0)).astype(bf16)


def _bias_group(hd):
    return (HEAD_DIM if hd % 2 == 0 else 0) + 16 * (hd // 2)


def _bias_selector():
    eb = np.zeros((LANES, 2 * LANES), np.float32)
    one = 3 * N_FOX_HEADS
    for hd in range(N_FOX_HEADS):
        g0 = _bias_group(hd)
        for p in range(_N_PIECES):
            eb[p * N_FOX_HEADS + hd, g0 + p] = 1.0
            eb[one, g0 + _N_PIECES + p] = 1.0
            eb[one, LANES + g0 + p] = 1.0
            eb[p * N_FOX_HEADS + hd, LANES + g0 + _N_PIECES + p] = -1.0
    return jnp.asarray(eb, bf16)


def _head_mean_matrix():
    idx = np.arange(FOX_WIDTH) // HEAD_DIM
    return jnp.asarray((idx[:, None] == idx[None, :]).astype(np.float32) / HEAD_DIM, bf16)


def _in_proj(x, lw, *, batch, prompt, layer=0, depth=1, kv_leaves=None):
    n, d = x.shape
    t = n // batch
    tm = ROW_TILE if prompt else n
    assert n % tm == 0 and (t % tm == 0 or not prompt)
    tiles_per_batch = t // tm if prompt else 1
    grid = (n // tm,)
    row = lambda w: pl.BlockSpec((tm, w), lambda i: (i, 0))
    full = lambda a: _wspec(a, layer)
    split = pl.BlockSpec((2, tm, LANES), lambda i: (0, i, 0))
    by_batch = lambda i: (i // tiles_per_batch, 0, i % tiles_per_batch, 0)

    ins = [x, lw["norm_g"], lw["w_in"], lw["b_forget"], lw["fox_q_norm"], lw["fox_k_norm"],
           lw["mem_q_norm"], lw["head_mean"]]
    in_specs = [row(d)] + [full(a) for a in ins[1:]]
    common = [((n, FOX_WIDTH), f32, row(FOX_WIDTH)), ((2, n, LANES), f32, split), ((2, n, LANES), f32, split),
              ((n, MEM_WIDTH), bf16, row(MEM_WIDTH)), ((n, MEM_WIDTH), f32, row(MEM_WIDTH))]
    scratch = []
    if prompt:
        tril = jnp.asarray(np.tril(np.ones((tm, tm), np.float32)), bf16)
        extra = [tril, lw["bias_sel"]]
        ins += extra
        in_specs += [full(a) for a in extra]
        aliases = {}
        if kv_leaves is None:
            tspec = pl.BlockSpec((depth, 1, N_FOX_HEADS, HEAD_DIM, tm),
                                 lambda i: (0, i // tiles_per_batch, 0, 0, i % tiles_per_batch))
        else:
            aliases = {len(ins): 0, len(ins) + 1: 1}
            ins += list(kv_leaves)
            in_specs += [pl.BlockSpec(memory_space=pl.ANY)] * 2
            tspec = pl.BlockSpec((1, 1, N_FOX_HEADS, HEAD_DIM, tm),
                                 lambda i: (layer, i // tiles_per_batch, 0, 0, i % tiles_per_batch))
        lspec = pl.BlockSpec((1, N_FOX_HEADS, tm), lambda i: (i // tiles_per_batch, 0, i % tiles_per_batch))
        hspec = pl.BlockSpec((1, N_FOX_HEADS, tm, LANES), by_batch)
        tshape = (depth, batch, N_FOX_HEADS, HEAD_DIM, t)
        outs = ([(tshape, f32, tspec), (tshape, f32, tspec), ((batch, N_FOX_HEADS, t), f32, lspec)] + common
                + [((batch, N_FOX_HEADS, t, LANES), bf16, hspec)] * 3)
        names = ["kt", "vt", "logft", "gf", "u", "gs", "mq", "gm", "qa", "ka", "va"]
        scratch = [pltpu.VMEM((8, LANES), f32)]
    else:
        outs = ([((n, FOX_WIDTH), f32, row(FOX_WIDTH)), ((n, FOX_WIDTH), f32, row(FOX_WIDTH)),
                 ((n, LANES), f32, row(LANES))] + common + [((n, FOX_WIDTH), bf16, row(FOX_WIDTH))])
        names = ["k", "v", "logf", "gf", "u", "gs", "mq", "gm", "q"]
        aliases = {}

    res = pl.pallas_call(
        functools.partial(_in_proj_kernel, prompt=prompt, tiles_per_batch=tiles_per_batch, aliased=bool(aliases)),
        grid=grid, in_specs=in_specs, out_specs=[o[2] for o in outs],
        out_shape=[jax.ShapeDtypeStruct(o[0], o[1]) for o in outs],
        scratch_shapes=scratch, input_output_aliases=aliases,
        compiler_params=_cparams(("arbitrary",)),
        name="in_proj_prompt" if prompt else "in_proj_sample",
    )(*ins)
    return dict(zip(names, res))


def _fox_prompt_kernel(qa_ref, ka_ref, va_ref, gate_ref, o_ref, s0_ref, s1_ref, m_ref, acc_ref, *, nq):
    tq = tk = ATTN_TQ
    s_refs = (s0_ref, s1_ref)
    m_ref[...] = jnp.full(m_ref.shape, -jnp.inf, f32)
    acc_ref[...] = jnp.zeros(acc_ref.shape, f32)

    def scores(qb, kb, slot):
        qoff = pl.multiple_of(qb * tq, tq)
        koff = pl.multiple_of(kb * tk, tk)
        for j in range(2):
            s_refs[slot][j] = lax.dot_general(qa_ref[0, j, pl.ds(qoff, tq), :], ka_ref[0, j, pl.ds(koff, tk), :],
                                              (((1,), (1,)), ((), ())), preferred_element_type=f32)

    def softmax_pv(qb, kb, slot, masked):
        koff = pl.multiple_of(kb * tk, tk)
        for j in range(2):
            s = s_refs[slot][j]
            if masked:
                s = jnp.where(_row_iota((tq, tk)) >= _lane_iota((tq, tk)), s, -jnp.inf)
            m = m_ref[qb, j]
            m_new = jnp.maximum(m, jnp.max(s, axis=-1, keepdims=True))
            p = jnp.exp2(s - jnp.concatenate([m_new] * (tk // LANES), axis=1))
            vv = va_ref[0, j, pl.ds(koff, tk), :]
            acc_ref[qb, j] = (jnp.exp2(m - m_new) * acc_ref[qb, j]
                              + jnp.dot(p.astype(bf16), vv, preferred_element_type=f32))
            m_ref[qb, j] = m_new

    def finalize(qb):
        h0 = acc_ref[qb, 0] / acc_ref[qb, 0, :, HEAD_DIM:HEAD_DIM + 1]
        h1 = acc_ref[qb, 1] / acc_ref[qb, 1, :, 0:1]
        rows = pl.ds(pl.multiple_of(qb * tq, tq), tq)
        out = jnp.where(_lane_iota((tq, LANES)) < HEAD_DIM, h0, h1) * gate_ref[0, rows, :]
        o_ref[0, rows, :] = out.astype(bf16)

    def after(qb, kb):
        row_end = kb + 1 >= qb
        last = jnp.logical_and(row_end, qb + 1 >= nq)
        nqb = jnp.where(last, 0, jnp.where(row_end, qb + 1, qb))
        nkb = jnp.where(row_end, 0, kb + 1)
        return nqb, nkb

    def lower_step(qb, kb, slot):
        nqb, nkb = after(qb, kb)
        scores(nqb, nkb, 1 - slot)
        softmax_pv(qb, kb, slot, False)
        return nqb, nkb

    def diag_step(qb, slot):
        nxt = jnp.minimum(qb + 1, nq - 1)
        scores(nxt, nxt, 1 - slot)
        softmax_pv(qb, qb, slot, True)
        finalize(qb)

    def steps_per_trip(n):
        return max(d for d in range(2, ATTN_UNROLL + 1, 2) if n % d == 0)

    n_lower = nq * (nq - 1) // 2
    if n_lower:
        scores(1, 0, 0)
        per = steps_per_trip(n_lower)

        def lower_body(i, carry):
            for k in range(per):
                carry = lower_step(*carry, k % 2)
            return carry

        lax.fori_loop(0, n_lower // per, lower_body, (jnp.int32(1), jnp.int32(0)))
    else:
        scores(0, 0, 0)

    per_diag = steps_per_trip(nq)

    def diag_body(i, carry):
        for k in range(per_diag):
            diag_step(per_diag * i + k, k % 2)
        return carry

    lax.fori_loop(0, nq // per_diag, diag_body, 0)


def _fox_prompt(qa, ka, va, gate):
    b, nh, t, _ = qa.shape
    nq = t // ATTN_TQ
    assert t % ATTN_TQ == 0 and nq % 2 == 0 and (nq * (nq - 1) // 2) % 2 == 0
    whole = pl.BlockSpec((1, 2, t, LANES), lambda bi, hp: (bi, hp, 0, 0))
    return pl.pallas_call(
        functools.partial(_fox_prompt_kernel, nq=nq),
        grid=(b, nh // 2),
        in_specs=[whole, whole, whole, pl.BlockSpec((1, t, LANES), lambda bi, hp: (bi, 0, hp))],
        out_specs=pl.BlockSpec((1, t, LANES), lambda bi, hp: (bi, 0, hp)),
        out_shape=jax.ShapeDtypeStruct((b, t, FOX_WIDTH), bf16),
        scratch_shapes=[pltpu.VMEM((2, ATTN_TQ, ATTN_TQ), f32), pltpu.VMEM((2, ATTN_TQ, ATTN_TQ), f32),
                        pltpu.VMEM((nq, 2, ATTN_TQ, LANES), f32), pltpu.VMEM((nq, 2, ATTN_TQ, LANES), f32)],
        compiler_params=_cparams(("arbitrary", "arbitrary")),
        name="fox_prompt",
    )(qa, ka, va, gate)


def _fox_sample_kernel(q_ref, kn_ref, vn_ref, ln_ref, gate_ref, kc_ref, vc_ref, lc_ref, triu_ref, o_ref,
                       kall_ref, vall_ref, *, past, tnew):
    nkeys = kall_ref.shape[1]
    nq = N_FOX_HEADS * tnew

    def new_cols(ref):
        z = jnp.concatenate([ref[...], jnp.zeros((LANES - tnew, ref.shape[1]), f32)], axis=0)
        return z.T

    kall_ref[:, 0:past] = kc_ref[0, 0].astype(bf16)
    vall_ref[:, 0:past] = vc_ref[0, 0].astype(bf16)
    kall_ref[:, past:nkeys] = new_cols(kn_ref).astype(bf16)
    vall_ref[:, past:nkeys] = new_cols(vn_ref).astype(bf16)

    l_all = jnp.concatenate([lc_ref[0, 0], new_cols(ln_ref)[0:N_FOX_HEADS, :]], axis=1)
    pieces = jnp.concatenate(_split3(l_all), axis=0).astype(bf16)
    c3 = jnp.dot(pieces, triu_ref[...], preferred_element_type=f32)
    c = c3[0:8] + c3[8:16] + c3[16:24]
    crow = jnp.concatenate([jnp.broadcast_to(c[hd:hd + 1, :], (tnew, nkeys)) for hd in range(N_FOX_HEADS)],
                           axis=0)
    rq = _row_iota((nq, LANES))
    cq = jnp.sum(jnp.where(_lane_iota((nq, LANES)) == rq % tnew, crow[:, past:nkeys], 0.0),
                 axis=1, keepdims=True)

    qrep = jnp.concatenate([q_ref[...]] * N_FOX_HEADS, axis=0)
    own = _row_iota((nq, FOX_WIDTH)) // tnew == _lane_iota((nq, FOX_WIDTH)) // HEAD_DIM
    qbd = jnp.where(own, qrep, jnp.zeros_like(qrep))
    s = jnp.dot(qbd, kall_ref[...], preferred_element_type=f32) + cq - crow
    s = jnp.where(_lane_iota((nq, nkeys)) <= past + _row_iota((nq, nkeys)) % tnew, s, -jnp.inf)
    p = jnp.exp(s - jnp.max(s, axis=1, keepdims=True))
    o2 = lax.dot_general(p.astype(bf16), vall_ref[...], (((1,), (1,)), ((), ())), preferred_element_type=f32)
    o2 = o2 / jnp.sum(p, axis=1, keepdims=True)
    olane = _lane_iota((tnew, FOX_WIDTH))
    out = jnp.zeros((tnew, FOX_WIDTH), f32)
    for hd in range(N_FOX_HEADS):
        out = out + jnp.where(olane // HEAD_DIM == hd, o2[hd * tnew:(hd + 1) * tnew, :], 0.0)
    o_ref[...] = (out * gate_ref[...]).astype(bf16)


def _fox_sample(q, k_new, v_new, logf_new, gate, cache_kt, cache_vt, cache_lt, layer, *, batch):
    n = q.shape[0]
    tnew = n // batch
    past = cache_kt.shape[3]
    nkeys = past + LANES
    assert N_FOX_HEADS * tnew == LANES and past % LANES == 0
    triu = jnp.asarray(np.triu(np.ones((nkeys, nkeys), np.float32)), bf16)
    rows = lambda w: pl.BlockSpec((tnew, w), lambda bi: (bi, 0))
    cache = lambda h: pl.BlockSpec((1, 1, h, past), lambda bi: (layer, bi, 0, 0))
    return pl.pallas_call(
        functools.partial(_fox_sample_kernel, past=past, tnew=tnew),
        grid=(batch,),
        in_specs=[rows(FOX_WIDTH), rows(FOX_WIDTH), rows(FOX_WIDTH), rows(LANES), rows(FOX_WIDTH),
                  cache(FOX_WIDTH), cache(FOX_WIDTH), cache(N_FOX_HEADS),
                  pl.BlockSpec((nkeys, nkeys), lambda bi: (0, 0))],
        out_specs=rows(FOX_WIDTH),
        out_shape=jax.ShapeDtypeStruct((n, FOX_WIDTH), bf16),
        scratch_shapes=[pltpu.VMEM((FOX_WIDTH, nkeys), bf16), pltpu.VMEM((FOX_WIDTH, nkeys), bf16)],
        compiler_params=_cparams(("arbitrary",)),
        name="fox_sample",
    )(q, k_new, v_new, logf_new, gate, cache_kt, cache_vt, cache_lt, triu)


def _s5_prep_kernel(are_r, aim_r, ldt_r, are_c, aim_c, ldt_c, bxr_ref, bxi_ref, cxr_ref, cxi_ref,
                    bst_ref, cst_ref, kt_ref, apow_ref):
    L, W, M = S5_CHUNK, SSM_WIDTH, SSM_MODES

    def cmul(xr, xi, yr, yi):
        return xr * yr - xi * yi, xr * yi + xi * yr

    def powers(are, aim, ldt, n):
        dt = jnp.exp(ldt[...])
        mag = jnp.exp(dt * are[...])
        p1 = (mag * jnp.cos(dt * aim[...]), mag * jnp.sin(dt * aim[...]))
        out = [(jnp.ones_like(mag), jnp.zeros_like(mag)), p1]
        for _ in range(n - 1):
            out.append(cmul(*out[-1], *p1))
        return out

    prow = powers(are_r, aim_r, ldt_r, L)
    pcol = powers(are_c, aim_c, ldt_c, L)
    ar, ai = are_r[...], aim_r[...]
    abr, abi = prow[1]
    den = ar * ar + ai * ai
    zr = ((abr - 1.0) * ar + abi * ai) / den
    zi = (abi * ar - (abr - 1.0) * ai) / den
    bbr, bbi = cmul(zr, zi, bxr_ref[...], bxi_ref[...])

    def b_pow(k):
        return jnp.concatenate(cmul(*prow[k], bbr, bbi), axis=1)

    def c_pow(k):
        pr, pi = pcol[k]
        cr, ci = cxr_ref[...], cxi_ref[...]
        return jnp.concatenate([cr * pr - ci * pi, -(cr * pi + ci * pr)], axis=0)

    cpow = [c_pow(k) for k in range(L + 1)]
    for tl in range(L):
        bst_ref[tl * W:(tl + 1) * W, :] = b_pow(L - 1 - tl).astype(bf16)
        cst_ref[:, tl * W:(tl + 1) * W] = cpow[tl + 1].astype(bf16)

    def hi_lo(x):
        hi = x.astype(bf16)
        return hi, (x - hi.astype(f32)).astype(bf16)

    bh, bl = hi_lo(b_pow(0))
    ch, cl = hi_lo(jnp.concatenate(cpow[:L], axis=1))
    kd = (jnp.dot(bh, ch, preferred_element_type=f32) + jnp.dot(bh, cl, preferred_element_type=f32)
          + jnp.dot(bl, ch, preferred_element_type=f32)).astype(bf16)
    for tl in range(L):
        if tl:
            kt_ref[tl * W:(tl + 1) * W, 0:tl * W] = jnp.zeros((W, tl * W), bf16)
        kt_ref[tl * W:(tl + 1) * W, tl * W:] = kd[:, :(L - tl) * W]

    step = prow[L]
    seg = step
    for _ in range(int(math.log2(S5_SEG))):
        seg = cmul(*seg, *seg)
    apow_ref[...] = jnp.concatenate(
        [jnp.concatenate(step, axis=1), jnp.concatenate(seg, axis=1), jnp.zeros((6, 2 * M), f32)], axis=0)


def _s5_prep_inputs(a_re, a_im, log_dt, b_re, b_im, c_re, c_im):
    G, N = SSM_GROUPS, SSM_STATE
    W, M = SSM_WIDTH, SSM_MODES
    depth = a_re.shape[0]
    eye = jnp.eye(G, dtype=f32)
    bx = lambda b: (jnp.swapaxes(b, 2, 3)[:, :, :, None, :] * eye[None, :, None, :, None]).reshape(depth, W, M)
    cx = lambda c: (jnp.swapaxes(c, 2, 3)[:, :, :, None, :] * eye[None, :, None, :, None]).reshape(depth, M, W)
    ldt = jnp.repeat(log_dt, N, axis=1)
    return [a_re.reshape(depth, 1, M), a_im.reshape(depth, 1, M), ldt.reshape(depth, 1, M),
            a_re.reshape(depth, M, 1), a_im.reshape(depth, M, 1), ldt.reshape(depth, M, 1),
            bx(b_re), bx(b_im), cx(c_re), cx(c_im)]


def _s5_prep(ins, layer):
    L, W, M = S5_CHUNK, SSM_WIDTH, SSM_MODES
    shapes = [((L * W, 2 * M), bf16), ((2 * M, L * W), bf16), ((L * W, L * W), bf16), ((8, 2 * M), f32)]
    return pl.pallas_call(
        _s5_prep_kernel,
        grid=(1,),
        in_specs=[_wspec(a, layer) for a in ins],
        out_specs=[pl.BlockSpec(sh, lambda i: (0, 0)) for sh, _ in shapes],
        out_shape=[jax.ShapeDtypeStruct(sh, dt) for sh, dt in shapes],
        compiler_params=_cparams(("arbitrary",)),
        name="s5_prep",
    )(*ins)


def _s5_kernel(u_ref, gs_ref, h0_ref, bst_ref, cst_ref, kt_ref, apow_ref, d_ref, wg_ref, bg_ref,
               o_ref, hT_ref, ds_ref, sp_ref, carry_ref, *, chained, seglen):
    L, W, M = S5_CHUNK, SSM_WIDTH, SSM_MODES
    nr = 8 * seglen
    nlb = M // LANES

    def chunked(ref, tl):
        return jnp.concatenate([ref[0, pl.ds(tl, nr, stride=L), :], ref[1, pl.ds(tl, nr, stride=L), :]], axis=1)

    uf = [chunked(u_ref, tl) for tl in range(L)]
    ub = jnp.concatenate(uf, axis=1).astype(bf16)
    ds = jnp.dot(ub, bst_ref[...], preferred_element_type=f32)
    for lb in range(2 * nlb):
        ds_ref[lb] = ds[:, lb * LANES:(lb + 1) * LANES]
    def blocks(row):
        return ([row[:, lb * LANES:(lb + 1) * LANES] for lb in range(nlb)],
                [row[:, M + lb * LANES:M + (lb + 1) * LANES] for lb in range(nlb)])

    def bcast8(parts):
        return [jnp.broadcast_to(p, (8, LANES)) for p in parts]

    ar, ai = (bcast8(p) for p in blocks(apow_ref[0:1, :]))

    def scan(vr, vi, store):
        for i in range(seglen):
            rows = pl.ds(i, 8, stride=seglen)
            for lb in range(nlb):
                if store:
                    sp_ref[lb, rows, :] = vr[lb]
                    sp_ref[nlb + lb, rows, :] = vi[lb]
                dr, di = ds_ref[lb, rows, :], ds_ref[nlb + lb, rows, :]
                vr[lb], vi[lb] = (ar[lb] * vr[lb] - ai[lb] * vi[lb] + dr,
                                  ar[lb] * vi[lb] + ai[lb] * vr[lb] + di)
        return vr, vi

    if chained:
        ti = pl.program_id(1)

        @pl.when(ti == 0)
        def _():
            carry_ref[0:1, :] = h0_ref[0]

        zero = [jnp.zeros((8, LANES), f32) for _ in range(nlb)]
        er, ei = scan(list(zero), list(zero), store=False)
        pr, pi = blocks(apow_ref[1:2, :])
        cr, ci = blocks(carry_ref[0:1, :])
        sr, si = [[] for _ in range(nlb)], [[] for _ in range(nlb)]
        for j in range(8):
            for lb in range(nlb):
                sr[lb].append(cr[lb])
                si[lb].append(ci[lb])
                cr[lb], ci[lb] = (pr[lb] * cr[lb] - pi[lb] * ci[lb] + er[lb][j:j + 1, :],
                                  pr[lb] * ci[lb] + pi[lb] * cr[lb] + ei[lb][j:j + 1, :])
        final = jnp.concatenate(cr + ci, axis=1)
        carry_ref[0:1, :] = final
        hT_ref[0] = final
        scan([jnp.concatenate(x, axis=0) for x in sr], [jnp.concatenate(x, axis=0) for x in si], store=True)
    else:
        vr, vi = blocks(h0_ref[...])
        vr, vi = scan(vr, vi, store=True)
        hT_ref[...] = jnp.concatenate(vr + vi, axis=1)

    spb = jnp.concatenate([sp_ref[lb] for lb in range(2 * nlb)], axis=1).astype(bf16)

    def y_pair(pair):
        kk = (2 * pair + 2) * W
        cols = slice(2 * pair * W, kk)
        return (jnp.dot(ub[:, :kk], kt_ref[0:kk, cols], preferred_element_type=f32)
                + jnp.dot(spb, cst_ref[:, cols], preferred_element_type=f32))

    def finish(pair, ypair):
        for tl in (2 * pair, 2 * pair + 1):
            y = ypair[:, (tl % 2) * W:(tl % 2 + 1) * W] + d_ref[...] * uf[tl]
            g = _gelu_tanh(y)
            gate = jax.nn.sigmoid(jnp.dot(g.astype(bf16), wg_ref[...], preferred_element_type=f32) + bg_ref[...])
            out = g * gate * chunked(gs_ref, tl)
            o_ref[0, pl.ds(tl, nr, stride=L), :] = out[:, :LANES]
            o_ref[1, pl.ds(tl, nr, stride=L), :] = out[:, LANES:]

    ycur = y_pair(0)
    for pair in range(L // 2):
        ynext = y_pair(pair + 1) if pair + 1 < L // 2 else None
        finish(pair, ycur)
        ycur = ynext


def _s5(u, gs, h0, ops, d_row, w_glu, b_glu, *, batch, chained, layer):
    L, W, M = S5_CHUNK, SSM_WIDTH, SSM_MODES
    n = u.shape[1]
    t = n // batch
    assert t % L == 0
    chunks = t // L
    bst, cst, kt, apow = ops
    if chained:
        seglen = S5_SEG
        assert chunks % (8 * seglen) == 0
        grid = (batch, chunks // (8 * seglen))
        state = pl.BlockSpec((1, 1, 2 * M), lambda bi, ti: (bi, 0, 0))
    else:
        seglen = chunks
        assert batch == 8
        grid = (1, 1)
        state = pl.BlockSpec((8, 2 * M), lambda bi, ti: (0, 0))
    tiles = grid[1]
    nr = 8 * seglen
    blk = pl.BlockSpec((2, nr * L, LANES), lambda bi, ti: (0, bi * tiles + ti, 0))
    full = lambda a: _wspec(a, layer)
    once = lambda a: pl.BlockSpec(a.shape, lambda bi, ti: (0,) * a.ndim, pipeline_mode=pl.Buffered(1))
    return pl.pallas_call(
        functools.partial(_s5_kernel, chained=chained, seglen=seglen),
        grid=grid,
        in_specs=[blk, blk, state, once(bst), once(cst), once(kt), full(apow), full(d_row),
                  full(w_glu), full(b_glu)],
        out_specs=[blk, state],
        out_shape=[jax.ShapeDtypeStruct((2, n, LANES), f32), jax.ShapeDtypeStruct(h0.shape, f32)],
        scratch_shapes=[pltpu.VMEM((2 * M // LANES, nr, LANES), f32), pltpu.VMEM((2 * M // LANES, nr, LANES), f32),
                        pltpu.VMEM((8, 2 * M), f32)],
        compiler_params=_cparams(("arbitrary", "arbitrary")),
        name="s5_prompt" if chained else "s5_sample",
    )(u, gs, h0, bst, cst, kt, apow, d_row, w_glu, b_glu)


def _memory_kv_kernel(mem_ref, g_ref, w_ref, kg_ref, hm_ref, mkt_ref, mvt_ref):
    batch, _, nmem = mkt_ref.shape
    x = mem_ref[...]
    ms = jnp.mean(x * x, axis=-1, keepdims=True)
    h = (x * lax.rsqrt(ms + EPS) * g_ref[...]).astype(bf16)
    mk = jnp.dot(h, w_ref[:, 0:MEM_WIDTH], preferred_element_type=f32)
    mk = _head_norm(mk, hm_ref.at[0:MEM_WIDTH, 0:MEM_WIDTH], kg_ref)
    mv = jnp.dot(h, w_ref[:, MEM_WIDTH:2 * MEM_WIDTH], preferred_element_type=f32)
    for b in range(batch):
        mkt_ref[b] = mk[b * nmem:(b + 1) * nmem, :].T
        mvt_ref[b] = mv[b * nmem:(b + 1) * nmem, :].T


def _memory_kv(mem, lw, *, batch, layer):
    n = mem.shape[0]
    ins = [mem, lw["mem_norm"], lw["w_mem_kv"], lw["mem_k_norm"], lw["head_mean"]]
    out = pl.BlockSpec((batch, MEM_WIDTH, n // batch), lambda i: (0, 0, 0))
    return pl.pallas_call(
        _memory_kv_kernel,
        grid=(1,),
        in_specs=[_wspec(a, layer) for a in ins],
        out_specs=[out, out],
        out_shape=[jax.ShapeDtypeStruct((batch, MEM_WIDTH, n // batch), f32)] * 2,
        compiler_params=_cparams(("arbitrary",)),
        name="memory_kv",
    )(*ins)


def _mix_out_kernel(x_ref, fox_ref, ssm_ref, mq_ref, gm_ref, mkt_ref, mvt_ref, w_ref, o_ref):
    tm = x_ref.shape[0]
    nmem = mkt_ref.shape[2]
    mq = mq_ref[...]
    mkt = mkt_ref[0]
    mvt = mvt_ref[0]
    khead = _row_iota((MEM_WIDTH, nmem)) // HEAD_DIM
    zero = jnp.zeros_like(mkt)
    mk_heads = jnp.concatenate([jnp.where(khead == hd, mkt, zero) for hd in range(N_MEM_HEADS)], axis=1)
    s_all = jnp.dot(mq, mk_heads, preferred_element_type=f32)
    ssm = jnp.concatenate([ssm_ref[0], ssm_ref[1]], axis=1).astype(bf16)
    y = x_ref[...]
    y = y + jnp.dot(fox_ref[...], w_ref[0:FOX_WIDTH, :], preferred_element_type=f32)
    y = y + jnp.dot(ssm, w_ref[FOX_WIDTH:FOX_WIDTH + SSM_WIDTH, :], preferred_element_type=f32)
    mem = jnp.zeros((tm, MEM_WIDTH), f32)
    for hd in range(N_MEM_HEADS):
        s = s_all[:, hd * nmem:(hd + 1) * nmem]
        p = jnp.exp(s - jnp.max(s, axis=-1, keepdims=True))
        p = p / jnp.sum(p, axis=-1, keepdims=True)
        vh = jnp.where(khead == hd, mvt, zero)
        mem = mem + lax.dot_general(p.astype(bf16), vh, (((1,), (1,)), ((), ())), preferred_element_type=f32)
    memg = (mem * gm_ref[...]).astype(bf16)
    o_ref[...] = y + jnp.dot(memg, w_ref[FOX_WIDTH + SSM_WIDTH:, :], preferred_element_type=f32)


def _mix_out(x, fox, ssm, mq, gm, mkt, mvt, w_out, *, batch, tm, layer):
    n, d = x.shape
    t = n // batch
    assert t % tm == 0
    tiles_per_batch = t // tm
    row = lambda w: pl.BlockSpec((tm, w), lambda i: (i, 0))
    if mkt.ndim == 4:
        memb = pl.BlockSpec((None, 1) + mkt.shape[2:], lambda i: (layer, i // tiles_per_batch, 0, 0))
    else:
        memb = pl.BlockSpec((1,) + mkt.shape[1:], lambda i: (i // tiles_per_batch, 0, 0))
    return pl.pallas_call(
        _mix_out_kernel,
        grid=(n // tm,),
        in_specs=[row(d), row(FOX_WIDTH), pl.BlockSpec((2, tm, LANES), lambda i: (0, i, 0)),
                  row(MEM_WIDTH), row(MEM_WIDTH), memb, memb, _wspec(w_out, layer)],
        out_specs=row(d),
        out_shape=jax.ShapeDtypeStruct((n, d), f32),
        compiler_params=_cparams(("arbitrary",)),
        name="mix_out_prompt" if tiles_per_batch > 1 else "mix_out_sample",
    )(x, fox, ssm, mq, gm, mkt, mvt, w_out)


def _stacked_weights(norm_g, w_in, b_forget, fox_q_norm, fox_k_norm, mem_q_norm, mem_norm, w_mem_kv,
                     mem_k_norm, w_out, w_glu, b_glu, ssm_d):
    depth, d, _ = w_in.shape
    nf = 4 * FOX_WIDTH
    w_packed = jnp.concatenate(
        [w_in[:, :, :FOX_WIDTH], w_in[:, :, nf:nf + N_FOX_HEADS],
         jnp.zeros((depth, d, LANES - N_FOX_HEADS), w_in.dtype),
         w_in[:, :, FOX_WIDTH:nf], w_in[:, :, nf + N_FOX_HEADS:]], axis=2).astype(bf16)
    assert w_packed.shape[2] == _W_COLS
    row = lambda a: a.reshape(depth, 1, -1)
    per_head = lambda g, heads: jnp.tile(g, (1, heads)).reshape(depth, 1, heads * HEAD_DIM)
    return dict(
        norm_g=row(norm_g), w_in=w_packed,
        b_forget=row(jnp.pad(b_forget, ((0, 0), (0, LANES - N_FOX_HEADS)))),
        fox_q_norm=per_head(fox_q_norm, N_FOX_HEADS), fox_k_norm=per_head(fox_k_norm, N_FOX_HEADS),
        mem_q_norm=per_head(mem_q_norm, N_MEM_HEADS), mem_k_norm=per_head(mem_k_norm, N_MEM_HEADS),
        mem_norm=row(mem_norm), w_mem_kv=w_mem_kv.astype(bf16),
        head_mean=_head_mean_matrix(), bias_sel=_bias_selector(),
        w_out=w_out.astype(bf16), w_glu=w_glu.astype(bf16), b_glu=row(b_glu), ssm_d=row(ssm_d))


def _split_state(hT, batch):
    hT = hT.reshape(batch, 2, SSM_GROUPS, SSM_STATE)
    return hT[:, 0], hT[:, 1]


def kernel(x_prompt, x_sample, mem_prompt, cache_fox_k, cache_fox_v, cache_fox_logf, state_ssm_re, state_ssm_im, cache_mem_k, cache_mem_v, norm_g, w_in, b_forget, fox_q_norm, fox_k_norm, ssm_a_re, ssm_a_im, ssm_log_dt, ssm_b_re, ssm_b_im, ssm_c_re, ssm_c_im, ssm_d, w_glu, b_glu, mem_norm, w_mem_kv, mem_q_norm, mem_k_norm, w_out):
    B, T, D = x_prompt.shape
    Bs, Ts, _ = x_sample.shape
    depth = w_in.shape[0]
    past = cache_fox_k.shape[2]
    nmem = mem_prompt.shape[1]
    M = SSM_MODES

    xp = x_prompt.reshape(B * T, D)
    xs = x_sample.reshape(Bs * Ts, D)
    mem = mem_prompt.reshape(B * nmem, D)
    cache_kt = jnp.transpose(cache_fox_k, (0, 1, 3, 4, 2)).reshape(depth, Bs, FOX_WIDTH, past)
    cache_vt = jnp.transpose(cache_fox_v, (0, 1, 3, 4, 2)).reshape(depth, Bs, FOX_WIDTH, past)
    cache_lt = jnp.transpose(cache_fox_logf, (0, 1, 3, 2)).astype(f32)
    cache_mkt = jnp.transpose(cache_mem_k, (0, 1, 3, 4, 2)).reshape(depth, Bs, MEM_WIDTH, nmem).astype(bf16)
    cache_mvt = jnp.transpose(cache_mem_v, (0, 1, 3, 4, 2)).reshape(depth, Bs, MEM_WIDTH, nmem).astype(bf16)

    outs = {k: [] for k in ("pf", "pre", "pim", "pmk", "pmv", "sk", "sv", "sf", "sre", "sim")}
    kv_leaves = None
    lw = _stacked_weights(norm_g, w_in, b_forget, fox_q_norm, fox_k_norm, mem_q_norm, mem_norm,
                          w_mem_kv, mem_k_norm, w_out, w_glu, b_glu, ssm_d)
    s5_params = _s5_prep_inputs(ssm_a_re, ssm_a_im, ssm_log_dt, ssm_b_re, ssm_b_im, ssm_c_re, ssm_c_im)
    for l in range(depth):
        ops = _s5_prep(s5_params, l)
        mkt, mvt = _memory_kv(mem, lw, batch=B, layer=l)

        pr = _in_proj(xp, lw, batch=B, prompt=True, layer=l, depth=depth, kv_leaves=kv_leaves)
        kv_leaves = (pr["kt"], pr["vt"])
        fox = _fox_prompt(pr["qa"], pr["ka"], pr["va"], pr["gf"].reshape(B, T, FOX_WIDTH)).reshape(B * T, FOX_WIDTH)
        ssm, hT = _s5(pr["u"], pr["gs"], jnp.zeros((B, 1, 2 * M), f32), ops, lw["ssm_d"], lw["w_glu"],
                      lw["b_glu"], batch=B, chained=True, layer=l)
        xp = _mix_out(xp, fox, ssm, pr["mq"], pr["gm"], mkt.astype(bf16), mvt.astype(bf16),
                      lw["w_out"], batch=B, tm=ROW_TILE, layer=l)
        hre, him = _split_state(hT, B)
        outs["pf"].append(pr["logft"])
        outs["pre"].append(hre)
        outs["pim"].append(him)
        outs["pmk"].append(mkt.reshape(B, N_MEM_HEADS, HEAD_DIM, nmem))
        outs["pmv"].append(mvt.reshape(B, N_MEM_HEADS, HEAD_DIM, nmem))

        sr = _in_proj(xs, lw, batch=Bs, prompt=False, layer=l)
        fox_s = _fox_sample(sr["q"], sr["k"], sr["v"], sr["logf"], sr["gf"], cache_kt, cache_vt, cache_lt, l,
                            batch=Bs)
        h0 = jnp.concatenate([state_ssm_re[l].reshape(Bs, M), state_ssm_im[l].reshape(Bs, M)],
                             axis=-1).astype(f32)
        ssm_s, hT_s = _s5(sr["u"], sr["gs"], h0, ops, lw["ssm_d"], lw["w_glu"], lw["b_glu"], batch=Bs,
                          chained=False, layer=l)
        xs = _mix_out(xs, fox_s, ssm_s, sr["mq"], sr["gm"], cache_mkt, cache_mvt,
                      lw["w_out"], batch=Bs, tm=Ts, layer=l)
        sre, sim = _split_state(hT_s, Bs)
        outs["sk"].append(sr["k"].reshape(Bs, Ts, N_FOX_HEADS, HEAD_DIM))
        outs["sv"].append(sr["v"].reshape(Bs, Ts, N_FOX_HEADS, HEAD_DIM))
        outs["sf"].append(sr["logf"][:, :N_FOX_HEADS].reshape(Bs, Ts, N_FOX_HEADS))
        outs["sre"].append(sre)
        outs["sim"].append(sim)

    st = {k: jnp.stack(v) for k, v in outs.items()}
    fox_k_prompt = jnp.transpose(kv_leaves[0], (0, 1, 4, 2, 3))
    fox_v_prompt = jnp.transpose(kv_leaves[1], (0, 1, 4, 2, 3))
    fox_logf_prompt = jnp.transpose(st["pf"], (0, 1, 3, 2))
    mem_k_prompt = jnp.transpose(st["pmk"], (0, 1, 4, 2, 3))
    mem_v_prompt = jnp.transpose(st["pmv"], (0, 1, 4, 2, 3))
    return (xp.reshape(B, T, D), xs.reshape(Bs, Ts, D), fox_k_prompt, fox_v_prompt, fox_logf_prompt,
            st["pre"], st["pim"], mem_k_prompt, mem_v_prompt, st["sk"], st["sv"], st["sf"], st["sre"], st["sim"])
```

```python
import functools
import math

import numpy as np
import jax
import jax.numpy as jnp
from jax import lax
from jax.experimental import pallas as pl
from jax.experimental.pallas import tpu as pltpu

f32 = jnp.float32
bf16 = jnp.bfloat16

HEAD_DIM = 64
N_FOX_HEADS = 8
FOX_WIDTH = N_FOX_HEADS * HEAD_DIM
SSM_GROUPS = 16
SSM_CH = 16
SSM_STATE = 64
SSM_WIDTH = SSM_GROUPS * SSM_CH
SSM_MODES = SSM_GROUPS * SSM_STATE
N_MEM_HEADS = 4
MEM_WIDTH = N_MEM_HEADS * HEAD_DIM
EPS = 1e-6
QK_SCALE = HEAD_DIM ** -0.5
LOG2E = math.log2(math.e)

LANES = 128
S5_CHUNK = 8
S5_SEG = 16
ROW_TILE = 512
MIX_TILE = 1024
ATTN_TQ = 512
ATTN_UNROLL = 14
VMEM_LIMIT = 56 * 1024 * 1024

_C_Q, _C_F, _C_K, _C_V, _C_G = 0, 512, 640, 1152, 1664
_C_SU, _C_SG, _C_MQ, _C_MG = 2176, 2432, 2688, 2944
_W_COLS = 3200
_N_PIECES = 3


def _cparams(sem):
    return pltpu.CompilerParams(dimension_semantics=sem, vmem_limit_bytes=VMEM_LIMIT)


def _wspec(a, layer):
    if a.ndim == 3:
        return pl.BlockSpec((None,) + a.shape[1:], lambda *_: (layer, 0, 0))
    return pl.BlockSpec(a.shape, lambda *_: (0,) * a.ndim)


def _silu(x):
    return x * jax.nn.sigmoid(x)


def _log_sigmoid(x):
    return -(jnp.maximum(-x, 0.0) + jnp.log1p(jnp.exp(-jnp.abs(x))))


def _gelu_tanh(x):
    return 0.5 * x * (1.0 + jnp.tanh(math.sqrt(2.0 / math.pi) * (x + 0.044715 * (x * x * x))))


def _split3(x):
    hi = x.astype(bf16).astype(f32)
    r = x - hi
    mid = r.astype(bf16).astype(f32)
    lo = (r - mid).astype(bf16).astype(f32)
    return hi, mid, lo


def _pack3(x):
    hi, mid, lo = _split3(x)
    return (hi + pltpu.roll(mid, 8, 1) + pltpu.roll(lo, 16, 1)).astype(bf16)


def _unpack3(y):
    return y + pltpu.roll(y, LANES - 8, 1) + pltpu.roll(y, LANES - 16, 1)


def _lane_iota(shape):
    return lax.broadcasted_iota(jnp.int32, shape, len(shape) - 1)


def _row_iota(shape):
    return lax.broadcasted_iota(jnp.int32, shape, len(shape) - 2)


def _head_norm(z, hm_ref, g_ref):
    ms = jnp.dot((z * z).astype(bf16), hm_ref[...], preferred_element_type=f32)
    return z * lax.rsqrt(ms + EPS) * g_ref[...]


def _in_proj_kernel(*refs, prompt, tiles_per_batch, aliased):
    if prompt:
        if aliased:
            refs = refs[:10] + refs[12:]
        (x_ref, ng_ref, w_ref, bf_ref, qg_ref, kg_ref, mqg_ref, hm_ref, tril_ref, eb_ref,
         kt_ref, vt_ref, logft_ref, gf_ref, u_ref, gs_ref, mq_ref, gm_ref, qa_ref, ka_ref, va_ref,
         carry_ref) = refs
    else:
        (x_ref, ng_ref, w_ref, bf_ref, qg_ref, kg_ref, mqg_ref, hm_ref,
         k_ref, v_ref, logf_ref, gf_ref, u_ref, gs_ref, mq_ref, gm_ref, q_ref) = refs

    x = x_ref[...]
    tm = x.shape[0]
    lane = _lane_iota((tm, LANES))
    ms = jnp.mean(x * x, axis=-1, keepdims=True)
    h = (x * lax.rsqrt(ms + EPS) * ng_ref[...]).astype(bf16)

    def seg(lo, width):
        return jnp.dot(h, w_ref[:, lo:lo + width], preferred_element_type=f32)

    def halves(ref, z):
        ref[0] = z[:, :LANES]
        ref[1] = z[:, LANES:]

    zqf = seg(_C_Q, FOX_WIDTH + LANES)
    logf = _log_sigmoid(zqf[:, FOX_WIDTH:] + bf_ref[...])
    lf = jnp.where(lane < N_FOX_HEADS, logf, 0.0)
    zk = seg(_C_K, FOX_WIDTH)
    if prompt:
        i = pl.program_id(0)

        @pl.when(i % tiles_per_batch == 0)
        def _():
            carry_ref[...] = jnp.zeros_like(carry_ref)

        csum = jnp.dot(tril_ref[...], _pack3(lf), preferred_element_type=f32)
    v = seg(_C_V, FOX_WIDTH)
    if prompt:
        c = jnp.where(lane < N_FOX_HEADS, _unpack3(csum) + carry_ref[0:1, :], 0.0)
        carry_ref[0:1, :] = c[tm - 1:tm, :]
        pieces = (_pack3(c * LOG2E).astype(f32) + jnp.where(lane == 3 * N_FOX_HEADS, 1.0, 0.0)).astype(bf16)
    if prompt:
        aug = jnp.dot(pieces, eb_ref[...], preferred_element_type=f32)
    q = _head_norm(zqf[:, :FOX_WIDTH], hm_ref, qg_ref) * QK_SCALE
    k = _head_norm(zk, hm_ref, kg_ref)

    def gate_proj():
        gf_ref[...] = _silu(seg(_C_G, FOX_WIDTH))

    def ssm_proj():
        zs = seg(_C_SU, 2 * SSM_WIDTH)
        halves(u_ref, zs[:, :SSM_WIDTH])
        halves(gs_ref, _silu(zs[:, SSM_WIDTH:]))

    def mem_proj():
        zm = seg(_C_MQ, 2 * MEM_WIDTH)
        mq = _head_norm(zm[:, :MEM_WIDTH], hm_ref.at[0:MEM_WIDTH, 0:MEM_WIDTH], mqg_ref) * QK_SCALE
        mq_ref[...] = mq.astype(bf16)
        gm_ref[...] = _silu(zm[:, MEM_WIDTH:])

    late = [gate_proj, ssm_proj, mem_proj]
    if not prompt:
        for proj in late:
            proj()
        k_ref[...] = k
        v_ref[...] = v
        logf_ref[...] = lf
        q_ref[...] = q.astype(bf16)
        return

    kt = k.T
    vt = v.T
    for hd in range(N_FOX_HEADS):
        kt_ref[0, 0, hd] = kt[hd * HEAD_DIM:(hd + 1) * HEAD_DIM, :]
        vt_ref[0, 0, hd] = vt[hd * HEAD_DIM:(hd + 1) * HEAD_DIM, :]
    for later in range(1, kt_ref.shape[0]):
        kt_ref[later] = jnp.zeros(kt_ref.shape[1:], f32)
        vt_ref[later] = jnp.zeros(vt_ref.shape[1:], f32)
    logft_ref[0] = lf.T[0:N_FOX_HEADS, :]

    q = q * LOG2E
    qaug, kaug = aug[:, :LANES], aug[:, LANES:]
    for hd in range(N_FOX_HEADS):
        if hd % 3 == 0 and late:
            late.pop(0)()
        pair = (hd // 2) * LANES
        own = (lane < HEAD_DIM) if hd % 2 == 0 else (lane >= HEAD_DIM)
        ones_lane = HEAD_DIM if hd % 2 == 0 else 0
        g0 = _bias_group(hd)
        bias = (lane >= g0) & (lane < g0 + 2 * _N_PIECES)
        qa_ref[0, hd] = jnp.where(own, q[:, pair:pair + LANES], jnp.where(bias, qaug, 0.0)).astype(bf16)
        ka_ref[0, hd] = jnp.where(own, k[:, pair:pair + LANES], kaug).astype(bf16)
        va_ref[0, hd] = jnp.where(own, v[:, pair:pair + LANES],
                                  jnp.where(lane == ones_lane, 1.0, 0.0)).astype(bf16)


def _bias_group(hd):
    return (HEAD_DIM if hd % 2 == 0 else 0) + 16 * (hd // 2)


def _bias_selector():
    eb = np.zeros((LANES, 2 * LANES), np.float32)
    one = 3 * N_FOX_HEADS
    for hd in range(N_FOX_HEADS):
        g0 = _bias_group(hd)
        for p in range(_N_PIECES):
            eb[p * N_FOX_HEADS + hd, g0 + p] = 1.0
            eb[one, g0 + _N_PIECES + p] = 1.0
            eb[one, LANES + g0 + p] = 1.0
            eb[p * N_FOX_HEADS + hd, LANES + g0 + _N_PIECES + p] = -1.0
    return jnp.asarray(eb, bf16)


def _head_mean_matrix():
    idx = np.arange(FOX_WIDTH) // HEAD_DIM
    return jnp.asarray((idx[:, None] == idx[None, :]).astype(np.float32) / HEAD_DIM, bf16)


def _in_proj(x, lw, *, batch, prompt, layer=0, depth=1, kv_leaves=None):
    n, d = x.shape
    t = n // batch
    tm = ROW_TILE if prompt else n
    assert n % tm == 0 and (t % tm == 0 or not prompt)
    tiles_per_batch = t // tm if prompt else 1
    grid = (n // tm,)
    row = lambda w: pl.BlockSpec((tm, w), lambda i: (i, 0))
    full = lambda a: _wspec(a, layer)
    split = pl.BlockSpec((2, tm, LANES), lambda i: (0, i, 0))
    by_batch = lambda i: (i // tiles_per_batch, 0, i % tiles_per_batch, 0)

    ins = [x, lw["norm_g"], lw["w_in"], lw["b_forget"], lw["fox_q_norm"], lw["fox_k_norm"],
           lw["mem_q_norm"], lw["head_mean"]]
    in_specs = [row(d)] + [full(a) for a in ins[1:]]
    common = [((n, FOX_WIDTH), f32, row(FOX_WIDTH)), ((2, n, LANES), f32, split), ((2, n, LANES), f32, split),
              ((n, MEM_WIDTH), bf16, row(MEM_WIDTH)), ((n, MEM_WIDTH), f32, row(MEM_WIDTH))]
    scratch = []
    if prompt:
        tril = jnp.asarray(np.tril(np.ones((tm, tm), np.float32)), bf16)
        extra = [tril, lw["bias_sel"]]
        ins += extra
        in_specs += [full(a) for a in extra]
        aliases = {}
        if kv_leaves is None:
            tspec = pl.BlockSpec((depth, 1, N_FOX_HEADS, HEAD_DIM, tm),
                                 lambda i: (0, i // tiles_per_batch, 0, 0, i % tiles_per_batch))
        else:
            aliases = {len(ins): 0, len(ins) + 1: 1}
            ins += list(kv_leaves)
            in_specs += [pl.BlockSpec(memory_space=pl.ANY)] * 2
            tspec = pl.BlockSpec((1, 1, N_FOX_HEADS, HEAD_DIM, tm),
                                 lambda i: (layer, i // tiles_per_batch, 0, 0, i % tiles_per_batch))
        lspec = pl.BlockSpec((1, N_FOX_HEADS, tm), lambda i: (i // tiles_per_batch, 0, i % tiles_per_batch))
        hspec = pl.BlockSpec((1, N_FOX_HEADS, tm, LANES), by_batch)
        tshape = (depth, batch, N_FOX_HEADS, HEAD_DIM, t)
        outs = ([(tshape, f32, tspec), (tshape, f32, tspec), ((batch, N_FOX_HEADS, t), f32, lspec)] + common
                + [((batch, N_FOX_HEADS, t, LANES), bf16, hspec)] * 3)
        names = ["kt", "vt", "logft", "gf", "u", "gs", "mq", "gm", "qa", "ka", "va"]
        scratch = [pltpu.VMEM((8, LANES), f32)]
    else:
        outs = ([((n, FOX_WIDTH), f32, row(FOX_WIDTH)), ((n, FOX_WIDTH), f32, row(FOX_WIDTH)),
                 ((n, LANES), f32, row(LANES))] + common + [((n, FOX_WIDTH), bf16, row(FOX_WIDTH))])
        names = ["k", "v", "logf", "gf", "u", "gs", "mq", "gm", "q"]
        aliases = {}

    res = pl.pallas_call(
        functools.partial(_in_proj_kernel, prompt=prompt, tiles_per_batch=tiles_per_batch, aliased=bool(aliases)),
        grid=grid, in_specs=in_specs, out_specs=[o[2] for o in outs],
        out_shape=[jax.ShapeDtypeStruct(o[0], o[1]) for o in outs],
        scratch_shapes=scratch, input_output_aliases=aliases,
        compiler_params=_cparams(("arbitrary",)),
        name="in_proj_prompt" if prompt else "in_proj_sample",
    )(*ins)
    return dict(zip(names, res))


def _fox_prompt_kernel(qa_ref, ka_ref, va_ref, gate_ref, o_ref, s0_ref, s1_ref, m_ref, acc_ref, *, nq):
    tq = tk = ATTN_TQ
    s_refs = (s0_ref, s1_ref)
    m_ref[...] = jnp.full(m_ref.shape, -jnp.inf, f32)
    acc_ref[...] = jnp.zeros(acc_ref.shape, f32)

    def scores(qb, kb, slot):
        qoff = pl.multiple_of(qb * tq, tq)
        koff = pl.multiple_of(kb * tk, tk)
        for j in range(2):
            s_refs[slot][j] = lax.dot_general(qa_ref[0, j, pl.ds(qoff, tq), :], ka_ref[0, j, pl.ds(koff, tk), :],
                                              (((1,), (1,)), ((), ())), preferred_element_type=f32)

    def softmax_pv(qb, kb, slot, masked):
        koff = pl.multiple_of(kb * tk, tk)
        for j in range(2):
            s = s_refs[slot][j]
            if masked:
                s = jnp.where(_row_iota((tq, tk)) >= _lane_iota((tq, tk)), s, -jnp.inf)
            m = m_ref[qb, j]
            m_new = jnp.maximum(m, jnp.max(s, axis=-1, keepdims=True))
            p = jnp.exp2(s - jnp.concatenate([m_new] * (tk // LANES), axis=1))
            vv = va_ref[0, j, pl.ds(koff, tk), :]
            acc_ref[qb, j] = (jnp.exp2(m - m_new) * acc_ref[qb, j]
                              + jnp.dot(p.astype(bf16), vv, preferred_element_type=f32))
            m_ref[qb, j] = m_new

    def finalize(qb):
        h0 = acc_ref[qb, 0] / acc_ref[qb, 0, :, HEAD_DIM:HEAD_DIM + 1]
        h1 = acc_ref[qb, 1] / acc_ref[qb, 1, :, 0:1]
        rows = pl.ds(pl.multiple_of(qb * tq, tq), tq)
        out = jnp.where(_lane_iota((tq, LANES)) < HEAD_DIM, h0, h1) * gate_ref[0, rows, :]
        o_ref[0, rows, :] = out.astype(bf16)

    def after(qb, kb):
        row_end = kb + 1 >= qb
        last = jnp.logical_and(row_end, qb + 1 >= nq)
        nqb = jnp.where(last, 0, jnp.where(row_end, qb + 1, qb))
        nkb = jnp.where(row_end, 0, kb + 1)
        return nqb, nkb

    def lower_step(qb, kb, slot):
        nqb, nkb = after(qb, kb)
        scores(nqb, nkb, 1 - slot)
        softmax_pv(qb, kb, slot, False)
        return nqb, nkb

    def diag_step(qb, slot):
        nxt = jnp.minimum(qb + 1, nq - 1)
        scores(nxt, nxt, 1 - slot)
        softmax_pv(qb, qb, slot, True)
        finalize(qb)

    def steps_per_trip(n):
        return max(d for d in range(2, ATTN_UNROLL + 1, 2) if n % d == 0)

    n_lower = nq * (nq - 1) // 2
    if n_lower:
        scores(1, 0, 0)
        per = steps_per_trip(n_lower)

        def lower_body(i, carry):
            for k in range(per):
                carry = lower_step(*carry, k % 2)
            return carry

        lax.fori_loop(0, n_lower // per, lower_body, (jnp.int32(1), jnp.int32(0)))
    else:
        scores(0, 0, 0)

    per_diag = steps_per_trip(nq)

    def diag_body(i, carry):
        for k in range(per_diag):
            diag_step(per_diag * i + k, k % 2)
        return carry

    lax.fori_loop(0, nq // per_diag, diag_body, 0)


def _fox_prompt(qa, ka, va, gate):
    b, nh, t, _ = qa.shape
    nq = t // ATTN_TQ
    assert t % ATTN_TQ == 0 and nq % 2 == 0 and (nq * (nq - 1) // 2) % 2 == 0
    whole = pl.BlockSpec((1, 2, t, LANES), lambda bi, hp: (bi, hp, 0, 0))
    return pl.pallas_call(
        functools.partial(_fox_prompt_kernel, nq=nq),
        grid=(b, nh // 2),
        in_specs=[whole, whole, whole, pl.BlockSpec((1, t, LANES), lambda bi, hp: (bi, 0, hp))],
        out_specs=pl.BlockSpec((1, t, LANES), lambda bi, hp: (bi, 0, hp)),
        out_shape=jax.ShapeDtypeStruct((b, t, FOX_WIDTH), bf16),
        scratch_shapes=[pltpu.VMEM((2, ATTN_TQ, ATTN_TQ), f32), pltpu.VMEM((2, ATTN_TQ, ATTN_TQ), f32),
                        pltpu.VMEM((nq, 2, ATTN_TQ, LANES), f32), pltpu.VMEM((nq, 2, ATTN_TQ, LANES), f32)],
        compiler_params=_cparams(("arbitrary", "arbitrary")),
        name="fox_prompt",
    )(qa, ka, va, gate)


def _fox_sample_kernel(q_ref, kn_ref, vn_ref, ln_ref, gate_ref, kc_ref, vc_ref, lc_ref, triu_ref, o_ref,
                       kall_ref, vall_ref, *, past, tnew):
    nkeys = kall_ref.shape[1]
    nq = N_FOX_HEADS * tnew

    def new_cols(ref):
        z = jnp.concatenate([ref[...], jnp.zeros((LANES - tnew, ref.shape[1]), f32)], axis=0)
        return z.T

    kall_ref[:, 0:past] = kc_ref[0, 0].astype(bf16)
    vall_ref[:, 0:past] = vc_ref[0, 0].astype(bf16)
    kall_ref[:, past:nkeys] = new_cols(kn_ref).astype(bf16)
    vall_ref[:, past:nkeys] = new_cols(vn_ref).astype(bf16)

    l_all = jnp.concatenate([lc_ref[0, 0], new_cols(ln_ref)[0:N_FOX_HEADS, :]], axis=1)
    pieces = jnp.concatenate(_split3(l_all), axis=0).astype(bf16)
    c3 = jnp.dot(pieces, triu_ref[...], preferred_element_type=f32)
    c = c3[0:8] + c3[8:16] + c3[16:24]
    crow = jnp.concatenate([jnp.broadcast_to(c[hd:hd + 1, :], (tnew, nkeys)) for hd in range(N_FOX_HEADS)],
                           axis=0)
    rq = _row_iota((nq, LANES))
    cq = jnp.sum(jnp.where(_lane_iota((nq, LANES)) == rq % tnew, crow[:, past:nkeys], 0.0),
                 axis=1, keepdims=True)

    qrep = jnp.concatenate([q_ref[...]] * N_FOX_HEADS, axis=0)
    own = _row_iota((nq, FOX_WIDTH)) // tnew == _lane_iota((nq, FOX_WIDTH)) // HEAD_DIM
    qbd = jnp.where(own, qrep, jnp.zeros_like(qrep))
    s = jnp.dot(qbd, kall_ref[...], preferred_element_type=f32) + cq - crow
    s = jnp.where(_lane_iota((nq, nkeys)) <= past + _row_iota((nq, nkeys)) % tnew, s, -jnp.inf)
    p = jnp.exp(s - jnp.max(s, axis=1, keepdims=True))
    o2 = lax.dot_general(p.astype(bf16), vall_ref[...], (((1,), (1,)), ((), ())), preferred_element_type=f32)
    o2 = o2 / jnp.sum(p, axis=1, keepdims=True)
    olane = _lane_iota((tnew, FOX_WIDTH))
    out = jnp.zeros((tnew, FOX_WIDTH), f32)
    for hd in range(N_FOX_HEADS):
        out = out + jnp.where(olane // HEAD_DIM == hd, o2[hd * tnew:(hd + 1) * tnew, :], 0.0)
    o_ref[...] = (out * gate_ref[...]).astype(bf16)


def _fox_sample(q, k_new, v_new, logf_new, gate, cache_kt, cache_vt, cache_lt, layer, *, batch):
    n = q.shape[0]
    tnew = n // batch
    past = cache_kt.shape[3]
    nkeys = past + LANES
    assert N_FOX_HEADS * tnew == LANES and past % LANES == 0
    triu = jnp.asarray(np.triu(np.ones((nkeys, nkeys), np.float32)), bf16)
    rows = lambda w: pl.BlockSpec((tnew, w), lambda bi: (bi, 0))
    cache = lambda h: pl.BlockSpec((1, 1, h, past), lambda bi: (layer, bi, 0, 0))
    return pl.pallas_call(
        functools.partial(_fox_sample_kernel, past=past, tnew=tnew),
        grid=(batch,),
        in_specs=[rows(FOX_WIDTH), rows(FOX_WIDTH), rows(FOX_WIDTH), rows(LANES), rows(FOX_WIDTH),
                  cache(FOX_WIDTH), cache(FOX_WIDTH), cache(N_FOX_HEADS),
                  pl.BlockSpec((nkeys, nkeys), lambda bi: (0, 0))],
        out_specs=rows(FOX_WIDTH),
        out_shape=jax.ShapeDtypeStruct((n, FOX_WIDTH), bf16),
        scratch_shapes=[pltpu.VMEM((FOX_WIDTH, nkeys), bf16), pltpu.VMEM((FOX_WIDTH, nkeys), bf16)],
        compiler_params=_cparams(("arbitrary",)),
        name="fox_sample",
    )(q, k_new, v_new, logf_new, gate, cache_kt, cache_vt, cache_lt, triu)


def _s5_prep_kernel(are_r, aim_r, ldt_r, are_c, aim_c, ldt_c, bxr_ref, bxi_ref, cxr_ref, cxi_ref,
                    bst_ref, cst_ref, kt_ref, apow_ref):
    L, W, M = S5_CHUNK, SSM_WIDTH, SSM_MODES

    def cmul(xr, xi, yr, yi):
        return xr * yr - xi * yi, xr * yi + xi * yr

    def powers(are, aim, ldt, n):
        dt = jnp.exp(ldt[...])
        mag = jnp.exp(dt * are[...])
        p1 = (mag * jnp.cos(dt * aim[...]), mag * jnp.sin(dt * aim[...]))
        out = [(jnp.ones_like(mag), jnp.zeros_like(mag)), p1]
        for _ in range(n - 1):
            out.append(cmul(*out[-1], *p1))
        return out

    prow = powers(are_r, aim_r, ldt_r, L)
    pcol = powers(are_c, aim_c, ldt_c, L)
    ar, ai = are_r[...], aim_r[...]
    abr, abi = prow[1]
    den = ar * ar + ai * ai
    zr = ((abr - 1.0) * ar + abi * ai) / den
    zi = (abi * ar - (abr - 1.0) * ai) / den
    bbr, bbi = cmul(zr, zi, bxr_ref[...], bxi_ref[...])

    def b_pow(k):
        return jnp.concatenate(cmul(*prow[k], bbr, bbi), axis=1)

    def c_pow(k):
        pr, pi = pcol[k]
        cr, ci = cxr_ref[...], cxi_ref[...]
        return jnp.concatenate([cr * pr - ci * pi, -(cr * pi + ci * pr)], axis=0)

    cpow = [c_pow(k) for k in range(L + 1)]
    for tl in range(L):
        bst_ref[tl * W:(tl + 1) * W, :] = b_pow(L - 1 - tl).astype(bf16)
        cst_ref[:, tl * W:(tl + 1) * W] = cpow[tl + 1].astype(bf16)

    def hi_lo(x):
        hi = x.astype(bf16)
        return hi, (x - hi.astype(f32)).astype(bf16)

    bh, bl = hi_lo(b_pow(0))
    ch, cl = hi_lo(jnp.concatenate(cpow[:L], axis=1))
    kd = (jnp.dot(bh, ch, preferred_element_type=f32) + jnp.dot(bh, cl, preferred_element_type=f32)
          + jnp.dot(bl, ch, preferred_element_type=f32)).astype(bf16)
    for tl in range(L):
        if tl:
            kt_ref[tl * W:(tl + 1) * W, 0:tl * W] = jnp.zeros((W, tl * W), bf16)
        kt_ref[tl * W:(tl + 1) * W, tl * W:] = kd[:, :(L - tl) * W]

    step = prow[L]
    seg = step
    for _ in range(int(math.log2(S5_SEG))):
        seg = cmul(*seg, *seg)
    apow_ref[...] = jnp.concatenate(
        [jnp.concatenate(step, axis=1), jnp.concatenate(seg, axis=1), jnp.zeros((6, 2 * M), f32)], axis=0)


def _s5_prep_inputs(a_re, a_im, log_dt, b_re, b_im, c_re, c_im):
    G, N = SSM_GROUPS, SSM_STATE
    W, M = SSM_WIDTH, SSM_MODES
    depth = a_re.shape[0]
    eye = jnp.eye(G, dtype=f32)
    bx = lambda b: (jnp.swapaxes(b, 2, 3)[:, :, :, None, :] * eye[None, :, None, :, None]).reshape(depth, W, M)
    cx = lambda c: (jnp.swapaxes(c, 2, 3)[:, :, :, None, :] * eye[None, :, None, :, None]).reshape(depth, M, W)
    ldt = jnp.repeat(log_dt, N, axis=1)
    return [a_re.reshape(depth, 1, M), a_im.reshape(depth, 1, M), ldt.reshape(depth, 1, M),
            a_re.reshape(depth, M, 1), a_im.reshape(depth, M, 1), ldt.reshape(depth, M, 1),
            bx(b_re), bx(b_im), cx(c_re), cx(c_im)]


def _s5_prep(ins, layer):
    L, W, M = S5_CHUNK, SSM_WIDTH, SSM_MODES
    shapes = [((L * W, 2 * M), bf16), ((2 * M, L * W), bf16), ((L * W, L * W), bf16), ((8, 2 * M), f32)]
    return pl.pallas_call(
        _s5_prep_kernel,
        grid=(1,),
        in_specs=[_wspec(a, layer) for a in ins],
        out_specs=[pl.BlockSpec(sh, lambda i: (0, 0)) for sh, _ in shapes],
        out_shape=[jax.ShapeDtypeStruct(sh, dt) for sh, dt in shapes],
        compiler_params=_cparams(("arbitrary",)),
        name="s5_prep",
    )(*ins)


def _s5_kernel(u_ref, gs_ref, h0_ref, bst_ref, cst_ref, kt_ref, apow_ref, d_ref, wg_ref, bg_ref,
               o_ref, hT_ref, ds_ref, sp_ref, carry_ref, *, chained, seglen):
    L, W, M = S5_CHUNK, SSM_WIDTH, SSM_MODES
    nr = 8 * seglen
    nlb = M // LANES

    def chunked(ref, tl):
        return jnp.concatenate([ref[0, pl.ds(tl, nr, stride=L), :], ref[1, pl.ds(tl, nr, stride=L), :]], axis=1)

    uf = [chunked(u_ref, tl) for tl in range(L)]
    ub = jnp.concatenate(uf, axis=1).astype(bf16)
    ds = jnp.dot(ub, bst_ref[...], preferred_element_type=f32)
    for lb in range(2 * nlb):
        ds_ref[lb] = ds[:, lb * LANES:(lb + 1) * LANES]
    def blocks(row):
        return ([row[:, lb * LANES:(lb + 1) * LANES] for lb in range(nlb)],
                [row[:, M + lb * LANES:M + (lb + 1) * LANES] for lb in range(nlb)])

    def bcast8(parts):
        return [jnp.broadcast_to(p, (8, LANES)) for p in parts]

    ar, ai = (bcast8(p) for p in blocks(apow_ref[0:1, :]))

    def scan(vr, vi, store):
        for i in range(seglen):
            rows = pl.ds(i, 8, stride=seglen)
            for lb in range(nlb):
                if store:
                    sp_ref[lb, rows, :] = vr[lb]
                    sp_ref[nlb + lb, rows, :] = vi[lb]
                dr, di = ds_ref[lb, rows, :], ds_ref[nlb + lb, rows, :]
                vr[lb], vi[lb] = (ar[lb] * vr[lb] - ai[lb] * vi[lb] + dr,
                                  ar[lb] * vi[lb] + ai[lb] * vr[lb] + di)
        return vr, vi

    if chained:
        ti = pl.program_id(1)

        @pl.when(ti == 0)
        def _():
            carry_ref[0:1, :] = h0_ref[0]

        zero = [jnp.zeros((8, LANES), f32) for _ in range(nlb)]
        er, ei = scan(list(zero), list(zero), store=False)
        pr, pi = blocks(apow_ref[1:2, :])
        cr, ci = blocks(carry_ref[0:1, :])
        sr, si = [[] for _ in range(nlb)], [[] for _ in range(nlb)]
        for j in range(8):
            for lb in range(nlb):
                sr[lb].append(cr[lb])
                si[lb].append(ci[lb])
                cr[lb], ci[lb] = (pr[lb] * cr[lb] - pi[lb] * ci[lb] + er[lb][j:j + 1, :],
                                  pr[lb] * ci[lb] + pi[lb] * cr[lb] + ei[lb][j:j + 1, :])
        final = jnp.concatenate(cr + ci, axis=1)
        carry_ref[0:1, :] = final
        hT_ref[0] = final
        scan([jnp.concatenate(x, axis=0) for x in sr], [jnp.concatenate(x, axis=0) for x in si], store=True)
    else:
        vr, vi = blocks(h0_ref[...])
        vr, vi = scan(vr, vi, store=True)
        hT_ref[...] = jnp.concatenate(vr + vi, axis=1)

    spb = jnp.concatenate([sp_ref[lb] for lb in range(2 * nlb)], axis=1).astype(bf16)

    def y_pair(pair):
        kk = (2 * pair + 2) * W
        cols = slice(2 * pair * W, kk)
        return (jnp.dot(ub[:, :kk], kt_ref[0:kk, cols], preferred_element_type=f32)
                + jnp.dot(spb, cst_ref[:, cols], preferred_element_type=f32))

    def finish(pair, ypair):
        for tl in (2 * pair, 2 * pair + 1):
            y = ypair[:, (tl % 2) * W:(tl % 2 + 1) * W] + d_ref[...] * uf[tl]
            g = _gelu_tanh(y)
            gate = jax.nn.sigmoid(jnp.dot(g.astype(bf16), wg_ref[...], preferred_element_type=f32) + bg_ref[...])
            out = g * gate * chunked(gs_ref, tl)
            o_ref[0, pl.ds(tl, nr, stride=L), :] = out[:, :LANES]
            o_ref[1, pl.ds(tl, nr, stride=L), :] = out[:, LANES:]

    ycur = y_pair(0)
    for pair in range(L // 2):
        ynext = y_pair(pair + 1) if pair + 1 < L // 2 else None
        finish(pair, ycur)
        ycur = ynext


def _s5(u, gs, h0, ops, d_row, w_glu, b_glu, *, batch, chained, layer):
    L, W, M = S5_CHUNK, SSM_WIDTH, SSM_MODES
    n = u.shape[1]
    t = n // batch
    assert t % L == 0
    chunks = t // L
    bst, cst, kt, apow = ops
    if chained:
        seglen = S5_SEG
        assert chunks % (8 * seglen) == 0
        grid = (batch, chunks // (8 * seglen))
        state = pl.BlockSpec((1, 1, 2 * M), lambda bi, ti: (bi, 0, 0))
    else:
        seglen = chunks
        assert batch == 8
        grid = (1, 1)
        state = pl.BlockSpec((8, 2 * M), lambda bi, ti: (0, 0))
    tiles = grid[1]
    nr = 8 * seglen
    blk = pl.BlockSpec((2, nr * L, LANES), lambda bi, ti: (0, bi * tiles + ti, 0))
    full = lambda a: _wspec(a, layer)
    once = lambda a: pl.BlockSpec(a.shape, lambda bi, ti: (0,) * a.ndim, pipeline_mode=pl.Buffered(1))
    return pl.pallas_call(
        functools.partial(_s5_kernel, chained=chained, seglen=seglen),
        grid=grid,
        in_specs=[blk, blk, state, once(bst), once(cst), once(kt), full(apow), full(d_row),
                  full(w_glu), full(b_glu)],
        out_specs=[blk, state],
        out_shape=[jax.ShapeDtypeStruct((2, n, LANES), f32), jax.ShapeDtypeStruct(h0.shape, f32)],
        scratch_shapes=[pltpu.VMEM((2 * M // LANES, nr, LANES), f32), pltpu.VMEM((2 * M // LANES, nr, LANES), f32),
                        pltpu.VMEM((8, 2 * M), f32)],
        compiler_params=_cparams(("arbitrary", "arbitrary")),
        name="s5_prompt" if chained else "s5_sample",
    )(u, gs, h0, bst, cst, kt, apow, d_row, w_glu, b_glu)


def _memory_kv_kernel(mem_ref, g_ref, w_ref, kg_ref, hm_ref, mkt_ref, mvt_ref):
    batch, _, nmem = mkt_ref.shape
    x = mem_ref[...]
    ms = jnp.mean(x * x, axis=-1, keepdims=True)
    h = (x * lax.rsqrt(ms + EPS) * g_ref[...]).astype(bf16)
    mk = jnp.dot(h, w_ref[:, 0:MEM_WIDTH], preferred_element_type=f32)
    mk = _head_norm(mk, hm_ref.at[0:MEM_WIDTH, 0:MEM_WIDTH], kg_ref)
    mv = jnp.dot(h, w_ref[:, MEM_WIDTH:2 * MEM_WIDTH], preferred_element_type=f32)
    for b in range(batch):
        mkt_ref[b] = mk[b * nmem:(b + 1) * nmem, :].T
        mvt_ref[b] = mv[b * nmem:(b + 1) * nmem, :].T


def _memory_kv(mem, lw, *, batch, layer):
    n = mem.shape[0]
    ins = [mem, lw["mem_norm"], lw["w_mem_kv"], lw["mem_k_norm"], lw["head_mean"]]
    out = pl.BlockSpec((batch, MEM_WIDTH, n // batch), lambda i: (0, 0, 0))
    return pl.pallas_call(
        _memory_kv_kernel,
        grid=(1,),
        in_specs=[_wspec(a, layer) for a in ins],
        out_specs=[out, out],
        out_shape=[jax.ShapeDtypeStruct((batch, MEM_WIDTH, n // batch), f32)] * 2,
        compiler_params=_cparams(("arbitrary",)),
        name="memory_kv",
    )(*ins)


def _mix_out_kernel(x_ref, fox_ref, ssm_ref, mq_ref, gm_ref, mkt_ref, mvt_ref, w_ref, o_ref):
    tm = x_ref.shape[0]
    nmem = mkt_ref.shape[2]
    mq = mq_ref[...]
    mkt = mkt_ref[0]
    mvt = mvt_ref[0]
    khead = _row_iota((MEM_WIDTH, nmem)) // HEAD_DIM
    zero = jnp.zeros_like(mkt)
    mk_heads = jnp.concatenate([jnp.where(khead == hd, mkt, zero) for hd in range(N_MEM_HEADS)], axis=1)
    s_all = jnp.dot(mq, mk_heads, preferred_element_type=f32)
    ssm = jnp.concatenate([ssm_ref[0], ssm_ref[1]], axis=1).astype(bf16)
    y = x_ref[...]
    y = y + jnp.dot(fox_ref[...], w_ref[0:FOX_WIDTH, :], preferred_element_type=f32)
    y = y + jnp.dot(ssm, w_ref[FOX_WIDTH:FOX_WIDTH + SSM_WIDTH, :], preferred_element_type=f32)
    mem = jnp.zeros((tm, MEM_WIDTH), f32)
    for hd in range(N_MEM_HEADS):
        s = s_all[:, hd * nmem:(hd + 1) * nmem]
        p = jnp.exp(s - jnp.max(s, axis=-1, keepdims=True))
        p = p / jnp.sum(p, axis=-1, keepdims=True)
        vh = jnp.where(khead == hd, mvt, zero)
        mem = mem + lax.dot_general(p.astype(bf16), vh, (((1,), (1,)), ((), ())), preferred_element_type=f32)
    memg = (mem * gm_ref[...]).astype(bf16)
    o_ref[...] = y + jnp.dot(memg, w_ref[FOX_WIDTH + SSM_WIDTH:, :], preferred_element_type=f32)


def _mix_out(x, fox, ssm, mq, gm, mkt, mvt, w_out, *, batch, tm, layer):
    n, d = x.shape
    t = n // batch
    assert t % tm == 0
    tiles_per_batch = t // tm
    row = lambda w: pl.BlockSpec((tm, w), lambda i: (i, 0))
    if mkt.ndim == 4:
        memb = pl.BlockSpec((None, 1) + mkt.shape[2:], lambda i: (layer, i // tiles_per_batch, 0, 0))
    else:
        memb = pl.BlockSpec((1,) + mkt.shape[1:], lambda i: (i // tiles_per_batch, 0, 0))
    return pl.pallas_call(
        _mix_out_kernel,
        grid=(n // tm,),
        in_specs=[row(d), row(FOX_WIDTH), pl.BlockSpec((2, tm, LANES), lambda i: (0, i, 0)),
                  row(MEM_WIDTH), row(MEM_WIDTH), memb, memb, _wspec(w_out, layer)],
        out_specs=row(d),
        out_shape=jax.ShapeDtypeStruct((n, d), f32),
        compiler_params=_cparams(("arbitrary",)),
        name="mix_out_prompt" if tiles_per_batch > 1 else "mix_out_sample",
    )(x, fox, ssm, mq, gm, mkt, mvt, w_out)


def _stacked_weights(norm_g, w_in, b_forget, fox_q_norm, fox_k_norm, mem_q_norm, mem_norm, w_mem_kv,
                     mem_k_norm, w_out, w_glu, b_glu, ssm_d):
    depth, d, _ = w_in.shape
    nf = 4 * FOX_WIDTH
    w_packed = jnp.concatenate(
        [w_in[:, :, :FOX_WIDTH], w_in[:, :, nf:nf + N_FOX_HEADS],
         jnp.zeros((depth, d, LANES - N_FOX_HEADS), w_in.dtype),
         w_in[:, :, FOX_WIDTH:nf], w_in[:, :, nf + N_FOX_HEADS:]], axis=2).astype(bf16)
    assert w_packed.shape[2] == _W_COLS
    row = lambda a: a.reshape(depth, 1, -1)
    per_head = lambda g, heads: jnp.tile(g, (1, heads)).reshape(depth, 1, heads * HEAD_DIM)
    return dict(
        norm_g=row(norm_g), w_in=w_packed,
        b_forget=row(jnp.pad(b_forget, ((0, 0), (0, LANES - N_FOX_HEADS)))),
        fox_q_norm=per_head(fox_q_norm, N_FOX_HEADS), fox_k_norm=per_head(fox_k_norm, N_FOX_HEADS),
        mem_q_norm=per_head(mem_q_norm, N_MEM_HEADS), mem_k_norm=per_head(mem_k_norm, N_MEM_HEADS),
        mem_norm=row(mem_norm), w_mem_kv=w_mem_kv.astype(bf16),
        head_mean=_head_mean_matrix(), bias_sel=_bias_selector(),
        w_out=w_out.astype(bf16), w_glu=w_glu.astype(bf16), b_glu=row(b_glu), ssm_d=row(ssm_d))


def _split_state(hT, batch):
    hT = hT.reshape(batch, 2, SSM_GROUPS, SSM_STATE)
    return hT[:, 0], hT[:, 1]


def kernel(x_prompt, x_sample, mem_prompt, cache_fox_k, cache_fox_v, cache_fox_logf, state_ssm_re, state_ssm_im, cache_mem_k, cache_mem_v, norm_g, w_in, b_forget, fox_q_norm, fox_k_norm, ssm_a_re, ssm_a_im, ssm_log_dt, ssm_b_re, ssm_b_im, ssm_c_re, ssm_c_im, ssm_d, w_glu, b_glu, mem_norm, w_mem_kv, mem_q_norm, mem_k_norm, w_out):
    B, T, D = x_prompt.shape
    Bs, Ts, _ = x_sample.shape
    depth = w_in.shape[0]
    past = cache_fox_k.shape[2]
    nmem = mem_prompt.shape[1]
    M = SSM_MODES

    xp = x_prompt.reshape(B * T, D)
    xs = x_sample.reshape(Bs * Ts, D)
    mem = mem_prompt.reshape(B * nmem, D)
    cache_kt = jnp.transpose(cache_fox_k, (0, 1, 3, 4, 2)).reshape(depth, Bs, FOX_WIDTH, past)
    cache_vt = jnp.transpose(cache_fox_v, (0, 1, 3, 4, 2)).reshape(depth, Bs, FOX_WIDTH, past)
    cache_lt = jnp.transpose(cache_fox_logf, (0, 1, 3, 2)).astype(f32)
    cache_mkt = jnp.transpose(cache_mem_k, (0, 1, 3, 4, 2)).reshape(depth, Bs, MEM_WIDTH, nmem).astype(bf16)
    cache_mvt = jnp.transpose(cache_mem_v, (0, 1, 3, 4, 2)).reshape(depth, Bs, MEM_WIDTH, nmem).astype(bf16)

    outs = {k: [] for k in ("pf", "pre", "pim", "pmk", "pmv", "sk", "sv", "sf", "sre", "sim")}
    kv_leaves = None
    lw = _stacked_weights(norm_g, w_in, b_forget, fox_q_norm, fox_k_norm, mem_q_norm, mem_norm,
                          w_mem_kv, mem_k_norm, w_out, w_glu, b_glu, ssm_d)
    s5_params = _s5_prep_inputs(ssm_a_re, ssm_a_im, ssm_log_dt, ssm_b_re, ssm_b_im, ssm_c_re, ssm_c_im)
    for l in range(depth):
        ops = _s5_prep(s5_params, l)
        mkt, mvt = _memory_kv(mem, lw, batch=B, layer=l)

        pr = _in_proj(xp, lw, batch=B, prompt=True, layer=l, depth=depth, kv_leaves=kv_leaves)
        kv_leaves = (pr["kt"], pr["vt"])
        fox = _fox_prompt(pr["qa"], pr["ka"], pr["va"], pr["gf"].reshape(B, T, FOX_WIDTH)).reshape(B * T, FOX_WIDTH)
        ssm, hT = _s5(pr["u"], pr["gs"], jnp.zeros((B, 1, 2 * M), f32), ops, lw["ssm_d"], lw["w_glu"],
                      lw["b_glu"], batch=B, chained=True, layer=l)
        xp = _mix_out(xp, fox, ssm, pr["mq"], pr["gm"], mkt.astype(bf16), mvt.astype(bf16),
                      lw["w_out"], batch=B, tm=MIX_TILE, layer=l)
        hre, him = _split_state(hT, B)
        outs["pf"].append(pr["logft"])
        outs["pre"].append(hre)
        outs["pim"].append(him)
        outs["pmk"].append(mkt.reshape(B, N_MEM_HEADS, HEAD_DIM, nmem))
        outs["pmv"].append(mvt.reshape(B, N_MEM_HEADS, HEAD_DIM, nmem))

        sr = _in_proj(xs, lw, batch=Bs, prompt=False, layer=l)
        fox_s = _fox_sample(sr["q"], sr["k"], sr["v"], sr["logf"], sr["gf"], cache_kt, cache_vt, cache_lt, l,
                            batch=Bs)
        h0 = jnp.concatenate([state_ssm_re[l].reshape(Bs, M), state_ssm_im[l].reshape(Bs, M)],
                             axis=-1).astype(f32)
        ssm_s, hT_s = _s5(sr["u"], sr["gs"], h0, ops, lw["ssm_d"], lw["w_glu"], lw["b_glu"], batch=Bs,
                          chained=False, layer=l)
        xs = _mix_out(xs, fox_s, ssm_s, sr["mq"], sr["gm"], cache_mkt, cache_mvt,
                      lw["w_out"], batch=Bs, tm=Ts, layer=l)
        sre, sim = _split_state(hT_s, Bs)
        outs["sk"].append(sr["k"].reshape(Bs, Ts, N_FOX_HEADS, HEAD_DIM))
        outs["sv"].append(sr["v"].reshape(Bs, Ts, N_FOX_HEADS, HEAD_DIM))
        outs["sf"].append(sr["logf"][:, :N_FOX_HEADS].reshape(Bs, Ts, N_FOX_HEADS))
        outs["sre"].append(sre)
        outs["sim"].append(sim)

    st = {k: jnp.stack(v) for k, v in outs.items()}
    fox_k_prompt = jnp.transpose(kv_leaves[0], (0, 1, 4, 2, 3))
    fox_v_prompt = jnp.transpose(kv_leaves[1], (0, 1, 4, 2, 3))
    fox_logf_prompt = jnp.transpose(st["pf"], (0, 1, 3, 2))
    mem_k_prompt = jnp.transpose(st["pmk"], (0, 1, 4, 2, 3))
    mem_v_prompt = jnp.transpose(st["pmv"], (0, 1, 4, 2, 3))
    return (xp.reshape(B, T, D), xs.reshape(Bs, Ts, D), fox_k_prompt, fox_v_prompt, fox_logf_prompt,
            st["pre"], st["pim"], mem_k_prompt, mem_v_prompt, st["sk"], st["sv"], st["sf"], st["sre"], st["sim"])
```

```python
import functools
import math

import numpy as np
import jax
import jax.numpy as jnp
from jax import lax
from jax.experimental import pallas as pl
from jax.experimental.pallas import tpu as pltpu

f32 = jnp.float32
bf16 = jnp.bfloat16

HEAD_DIM = 64
N_FOX_HEADS = 8
FOX_WIDTH = N_FOX_HEADS * HEAD_DIM
SSM_GROUPS = 16
SSM_CH = 16
SSM_STATE = 64
SSM_WIDTH = SSM_GROUPS * SSM_CH
SSM_MODES = SSM_GROUPS * SSM_STATE
N_MEM_HEADS = 4
MEM_WIDTH = N_MEM_HEADS * HEAD_DIM
EPS = 1e-6
QK_SCALE = HEAD_DIM ** -0.5
LOG2E = math.log2(math.e)

LANES = 128
S5_CHUNK = 8
S5_SEG = 16
ROW_TILE = 512
MIX_TILE = 1024
ATTN_TQ = 512
ATTN_UNROLL = 14
VMEM_LIMIT = 56 * 1024 * 1024

_C_Q, _C_F, _C_K, _C_V, _C_G = 0, 512, 640, 1152, 1664
_C_SU, _C_SG, _C_MQ, _C_MG = 2176, 2432, 2688, 2944
_W_COLS = 3200
_N_PIECES = 3


def _cparams(sem):
    return pltpu.CompilerParams(dimension_semantics=sem, vmem_limit_bytes=VMEM_LIMIT)


def _wspec(a, layer):
    if a.ndim == 3:
        return pl.BlockSpec((None,) + a.shape[1:], lambda *_: (layer, 0, 0))
    return pl.BlockSpec(a.shape, lambda *_: (0,) * a.ndim)


def _silu(x):
    return x * jax.nn.sigmoid(x)


def _log_sigmoid(x):
    return -(jnp.maximum(-x, 0.0) + jnp.log1p(jnp.exp(-jnp.abs(x))))


def _gelu_tanh(x):
    return 0.5 * x * (1.0 + jnp.tanh(math.sqrt(2.0 / math.pi) * (x + 0.044715 * (x * x * x))))


def _split3(x):
    hi = x.astype(bf16).astype(f32)
    r = x - hi
    mid = r.astype(bf16).astype(f32)
    lo = (r - mid).astype(bf16).astype(f32)
    return hi, mid, lo


def _pack3(x):
    hi, mid, lo = _split3(x)
    return (hi + pltpu.roll(mid, 8, 1) + pltpu.roll(lo, 16, 1)).astype(bf16)


def _unpack3(y):
    return y + pltpu.roll(y, LANES - 8, 1) + pltpu.roll(y, LANES - 16, 1)


def _lane_iota(shape):
    return lax.broadcasted_iota(jnp.int32, shape, len(shape) - 1)


def _row_iota(shape):
    return lax.broadcasted_iota(jnp.int32, shape, len(shape) - 2)


def _head_norm(z, hm_ref, g_ref):
    ms = jnp.dot((z * z).astype(bf16), hm_ref[...], preferred_element_type=f32)
    return z * lax.rsqrt(ms + EPS) * g_ref[...]


def _in_proj_kernel(*refs, prompt, tiles_per_batch, aliased):
    if prompt:
        if aliased:
            refs = refs[:10] + refs[12:]
        (x_ref, ng_ref, w_ref, bf_ref, qg_ref, kg_ref, mqg_ref, hm_ref, tril_ref, eb_ref,
         kt_ref, vt_ref, logft_ref, gf_ref, u_ref, gs_ref, mq_ref, gm_ref, qa_ref, ka_ref, va_ref,
         carry_ref) = refs
    else:
        (x_ref, ng_ref, w_ref, bf_ref, qg_ref, kg_ref, mqg_ref, hm_ref,
         k_ref, v_ref, logf_ref, gf_ref, u_ref, gs_ref, mq_ref, gm_ref, q_ref) = refs

    x = x_ref[...]
    tm = x.shape[0]
    lane = _lane_iota((tm, LANES))
    ms = jnp.mean(x * x, axis=-1, keepdims=True)
    h = (x * lax.rsqrt(ms + EPS) * ng_ref[...]).astype(bf16)

    def seg(lo, width):
        return jnp.dot(h, w_ref[:, lo:lo + width], preferred_element_type=f32)

    def halves(ref, z):
        ref[0] = z[:, :LANES]
        ref[1] = z[:, LANES:]

    zqf = seg(_C_Q, FOX_WIDTH + LANES)
    logf = _log_sigmoid(zqf[:, FOX_WIDTH:] + bf_ref[...])
    lf = jnp.where(lane < N_FOX_HEADS, logf, 0.0)
    zk = seg(_C_K, FOX_WIDTH)
    if prompt:
        i = pl.program_id(0)

        @pl.when(i % tiles_per_batch == 0)
        def _():
            carry_ref[...] = jnp.zeros_like(carry_ref)

        csum = jnp.dot(tril_ref[...], _pack3(lf), preferred_element_type=f32)
    v = seg(_C_V, FOX_WIDTH)
    if prompt:
        c = jnp.where(lane < N_FOX_HEADS, _unpack3(csum) + carry_ref[0:1, :], 0.0)
        carry_ref[0:1, :] = c[tm - 1:tm, :]
        pieces = (_pack3(c * LOG2E).astype(f32) + jnp.where(lane == 3 * N_FOX_HEADS, 1.0, 0.0)).astype(bf16)
    if prompt:
        aug = jnp.dot(pieces, eb_ref[...], preferred_element_type=f32)
    q = _head_norm(zqf[:, :FOX_WIDTH], hm_ref, qg_ref) * QK_SCALE
    k = _head_norm(zk, hm_ref, kg_ref)

    def gate_proj():
        gf_ref[...] = _silu(seg(_C_G, FOX_WIDTH))

    def ssm_proj():
        zs = seg(_C_SU, 2 * SSM_WIDTH)
        halves(u_ref, zs[:, :SSM_WIDTH])
        halves(gs_ref, _silu(zs[:, SSM_WIDTH:]))

    def mem_proj():
        zm = seg(_C_MQ, 2 * MEM_WIDTH)
        mq = _head_norm(zm[:, :MEM_WIDTH], hm_ref.at[0:MEM_WIDTH, 0:MEM_WIDTH], mqg_ref) * QK_SCALE
        mq_ref[...] = mq.astype(bf16)
        gm_ref[...] = _silu(zm[:, MEM_WIDTH:])

    late = [gate_proj, ssm_proj, mem_proj]
    if not prompt:
        for proj in late:
            proj()
        k_ref[...] = k
        v_ref[...] = v
        logf_ref[...] = lf
        q_ref[...] = q.astype(bf16)
        return

    kt = k.T
    vt = v.T
    for hd in range(N_FOX_HEADS):
        kt_ref[0, 0, hd] = kt[hd * HEAD_DIM:(hd + 1) * HEAD_DIM, :]
        vt_ref[0, 0, hd] = vt[hd * HEAD_DIM:(hd + 1) * HEAD_DIM, :]
    for later in range(1, kt_ref.shape[0]):
        kt_ref[later] = jnp.zeros(kt_ref.shape[1:], f32)
        vt_ref[later] = jnp.zeros(vt_ref.shape[1:], f32)
    logft_ref[0] = lf.T[0:N_FOX_HEADS, :]

    q = q * LOG2E
    qaug, kaug = aug[:, :LANES], aug[:, LANES:]
    for hd in range(N_FOX_HEADS):
        if hd % 3 == 0 and late:
            late.pop(0)()
        pair = (hd // 2) * LANES
        own = (lane < HEAD_DIM) if hd % 2 == 0 else (lane >= HEAD_DIM)
        ones_lane = HEAD_DIM if hd % 2 == 0 else 0
        g0 = _bias_group(hd)
        bias = (lane >= g0) & (lane < g0 + 2 * _N_PIECES)
        qa_ref[0, hd] = jnp.where(own, q[:, pair:pair + LANES], jnp.where(bias, qaug, 0.0)).astype(bf16)
        ka_ref[0, hd] = jnp.where(own, k[:, pair:pair + LANES], kaug).astype(bf16)
        va_ref[0, hd] = jnp.where(own, v[:, pair:pair + LANES],
                                  jnp.where(lane == ones_lane, 1.0, 0.0)).astype(bf16)


def _bias_group(hd):
    return (HEAD_DIM if hd % 2 == 0 else 0) + 16 * (hd // 2)


def _bias_selector():
    eb = np.zeros((LANES, 2 * LANES), np.float32)
    one = 3 * N_FOX_HEADS
    for hd in range(N_FOX_HEADS):
        g0 = _bias_group(hd)
        for p in range(_N_PIECES):
            eb[p * N_FOX_HEADS + hd, g0 + p] = 1.0
            eb[one, g0 + _N_PIECES + p] = 1.0
            eb[one, LANES + g0 + p] = 1.0
            eb[p * N_FOX_HEADS + hd, LANES + g0 + _N_PIECES + p] = -1.0
    return jnp.asarray(eb, bf16)


def _head_mean_matrix():
    idx = np.arange(FOX_WIDTH) // HEAD_DIM
    return jnp.asarray((idx[:, None] == idx[None, :]).astype(np.float32) / HEAD_DIM, bf16)


def _in_proj(x, lw, *, batch, prompt, layer=0, depth=1, kv_leaves=None):
    n, d = x.shape
    t = n // batch
    tm = ROW_TILE if prompt else n
    assert n % tm == 0 and (t % tm == 0 or not prompt)
    tiles_per_batch = t // tm if prompt else 1
    grid = (n // tm,)
    row = lambda w: pl.BlockSpec((tm, w), lambda i: (i, 0))
    full = lambda a: _wspec(a, layer)
    split = pl.BlockSpec((2, tm, LANES), lambda i: (0, i, 0))
    by_batch = lambda i: (i // tiles_per_batch, 0, i % tiles_per_batch, 0)

    ins = [x, lw["norm_g"], lw["w_in"], lw["b_forget"], lw["fox_q_norm"], lw["fox_k_norm"],
           lw["mem_q_norm"], lw["head_mean"]]
    in_specs = [row(d)] + [full(a) for a in ins[1:]]
    common = [((n, FOX_WIDTH), f32, row(FOX_WIDTH)), ((2, n, LANES), f32, split), ((2, n, LANES), f32, split),
              ((n, MEM_WIDTH), bf16, row(MEM_WIDTH)), ((n, MEM_WIDTH), f32, row(MEM_WIDTH))]
    scratch = []
    if prompt:
        tril = jnp.asarray(np.tril(np.ones((tm, tm), np.float32)), bf16)
        extra = [tril, lw["bias_sel"]]
        ins += extra
        in_specs += [full(a) for a in extra]
        aliases = {}
        if kv_leaves is None:
            tspec = pl.BlockSpec((depth, 1, N_FOX_HEADS, HEAD_DIM, tm),
                                 lambda i: (0, i // tiles_per_batch, 0, 0, i % tiles_per_batch))
        else:
            aliases = {len(ins): 0, len(ins) + 1: 1}
            ins += list(kv_leaves)
            in_specs += [pl.BlockSpec(memory_space=pl.ANY)] * 2
            tspec = pl.BlockSpec((1, 1, N_FOX_HEADS, HEAD_DIM, tm),
                                 lambda i: (layer, i // tiles_per_batch, 0, 0, i % tiles_per_batch))
        lspec = pl.BlockSpec((1, N_FOX_HEADS, tm), lambda i: (i // tiles_per_batch, 0, i % tiles_per_batch))
        hspec = pl.BlockSpec((1, N_FOX_HEADS, tm, LANES), by_batch)
        tshape = (depth, batch, N_FOX_HEADS, HEAD_DIM, t)
        outs = ([(tshape, f32, tspec), (tshape, f32, tspec), ((batch, N_FOX_HEADS, t), f32, lspec)] + common
                + [((batch, N_FOX_HEADS, t, LANES), bf16, hspec)] * 3)
        names = ["kt", "vt", "logft", "gf", "u", "gs", "mq", "gm", "qa", "ka", "va"]
        scratch = [pltpu.VMEM((8, LANES), f32)]
    else:
        outs = ([((n, FOX_WIDTH), f32, row(FOX_WIDTH)), ((n, FOX_WIDTH), f32, row(FOX_WIDTH)),
                 ((n, LANES), f32, row(LANES))] + common + [((n, FOX_WIDTH), bf16, row(FOX_WIDTH))])
        names = ["k", "v", "logf", "gf", "u", "gs", "mq", "gm", "q"]
        aliases = {}

    res = pl.pallas_call(
        functools.partial(_in_proj_kernel, prompt=prompt, tiles_per_batch=tiles_per_batch, aliased=bool(aliases)),
        grid=grid, in_specs=in_specs, out_specs=[o[2] for o in outs],
        out_shape=[jax.ShapeDtypeStruct(o[0], o[1]) for o in outs],
        scratch_shapes=scratch, input_output_aliases=aliases,
        compiler_params=_cparams(("arbitrary",)),
        name="in_proj_prompt" if prompt else "in_proj_sample",
    )(*ins)
    return dict(zip(names, res))


def _fox_prompt_kernel(qa_ref, ka_ref, va_ref, gate_ref, o_ref, s0_ref, s1_ref, m_ref, acc_ref, *, nq):
    tq = tk = ATTN_TQ
    s_refs = (s0_ref, s1_ref)
    m_ref[...] = jnp.full(m_ref.shape, -jnp.inf, f32)
    acc_ref[...] = jnp.zeros(acc_ref.shape, f32)

    def scores(qb, kb, slot):
        qoff = pl.multiple_of(qb * tq, tq)
        koff = pl.multiple_of(kb * tk, tk)
        for j in range(2):
            s_refs[slot][j] = lax.dot_general(qa_ref[0, j, pl.ds(qoff, tq), :], ka_ref[0, j, pl.ds(koff, tk), :],
                                              (((1,), (1,)), ((), ())), preferred_element_type=f32)

    def softmax_pv(qb, kb, slot, masked):
        koff = pl.multiple_of(kb * tk, tk)
        for j in range(2):
            s = s_refs[slot][j]
            if masked:
                s = jnp.where(_row_iota((tq, tk)) >= _lane_iota((tq, tk)), s, -jnp.inf)
            m = m_ref[qb, j]
            m_new = jnp.maximum(m, jnp.max(s, axis=-1, keepdims=True))
            p = jnp.exp2(s - jnp.concatenate([m_new] * (tk // LANES), axis=1))
            vv = va_ref[0, j, pl.ds(koff, tk), :]
            acc_ref[qb, j] = (jnp.exp2(m - m_new) * acc_ref[qb, j]
                              + jnp.dot(p.astype(bf16), vv, preferred_element_type=f32))
            m_ref[qb, j] = m_new

    def finalize(qb):
        h0 = acc_ref[qb, 0] / acc_ref[qb, 0, :, HEAD_DIM:HEAD_DIM + 1]
        h1 = acc_ref[qb, 1] / acc_ref[qb, 1, :, 0:1]
        rows = pl.ds(pl.multiple_of(qb * tq, tq), tq)
        out = jnp.where(_lane_iota((tq, LANES)) < HEAD_DIM, h0, h1) * gate_ref[0, rows, :]
        o_ref[0, rows, :] = out.astype(bf16)

    def after(qb, kb):
        row_end = kb + 1 >= qb
        last = jnp.logical_and(row_end, qb + 1 >= nq)
        nqb = jnp.where(last, 0, jnp.where(row_end, qb + 1, qb))
        nkb = jnp.where(row_end, 0, kb + 1)
        return nqb, nkb

    def lower_step(qb, kb, slot):
        nqb, nkb = after(qb, kb)
        scores(nqb, nkb, 1 - slot)
        softmax_pv(qb, kb, slot, False)
        return nqb, nkb

    def diag_step(qb, slot):
        nxt = jnp.minimum(qb + 1, nq - 1)
        scores(nxt, nxt, 1 - slot)
        softmax_pv(qb, qb, slot, True)
        finalize(qb)

    def steps_per_trip(n):
        return max(d for d in range(2, ATTN_UNROLL + 1, 2) if n % d == 0)

    n_lower = nq * (nq - 1) // 2
    if n_lower:
        scores(1, 0, 0)
        per = steps_per_trip(n_lower)

        def lower_body(i, carry):
            for k in range(per):
                carry = lower_step(*carry, k % 2)
            return carry

        lax.fori_loop(0, n_lower // per, lower_body, (jnp.int32(1), jnp.int32(0)))
    else:
        scores(0, 0, 0)

    per_diag = steps_per_trip(nq)

    def diag_body(i, carry):
        for k in range(per_diag):
            diag_step(per_diag * i + k, k % 2)
        return carry

    lax.fori_loop(0, nq // per_diag, diag_body, 0)


def _fox_prompt(qa, ka, va, gate):
    b, nh, t, _ = qa.shape
    nq = t // ATTN_TQ
    assert t % ATTN_TQ == 0 and nq % 2 == 0 and (nq * (nq - 1) // 2) % 2 == 0
    whole = pl.BlockSpec((1, 2, t, LANES), lambda bi, hp: (bi, hp, 0, 0))
    return pl.pallas_call(
        functools.partial(_fox_prompt_kernel, nq=nq),
        grid=(b, nh // 2),
        in_specs=[whole, whole, whole, pl.BlockSpec((1, t, LANES), lambda bi, hp: (bi, 0, hp))],
        out_specs=pl.BlockSpec((1, t, LANES), lambda bi, hp: (bi, 0, hp)),
        out_shape=jax.ShapeDtypeStruct((b, t, FOX_WIDTH), bf16),
        scratch_shapes=[pltpu.VMEM((2, ATTN_TQ, ATTN_TQ), f32), pltpu.VMEM((2, ATTN_TQ, ATTN_TQ), f32),
                        pltpu.VMEM((nq, 2, ATTN_TQ, LANES), f32), pltpu.VMEM((nq, 2, ATTN_TQ, LANES), f32)],
        compiler_params=_cparams(("arbitrary", "arbitrary")),
        name="fox_prompt",
    )(qa, ka, va, gate)


def _fox_sample_kernel(q_ref, kn_ref, vn_ref, ln_ref, gate_ref, kc_ref, vc_ref, lc_ref, triu_ref, o_ref,
                       kall_ref, vall_ref, *, past, tnew):
    nkeys = kall_ref.shape[1]
    nq = N_FOX_HEADS * tnew

    def new_cols(ref):
        z = jnp.concatenate([ref[...], jnp.zeros((LANES - tnew, ref.shape[1]), f32)], axis=0)
        return z.T

    kall_ref[:, 0:past] = kc_ref[0, 0].astype(bf16)
    vall_ref[:, 0:past] = vc_ref[0, 0].astype(bf16)
    kall_ref[:, past:nkeys] = new_cols(kn_ref).astype(bf16)
    vall_ref[:, past:nkeys] = new_cols(vn_ref).astype(bf16)

    l_all = jnp.concatenate([lc_ref[0, 0], new_cols(ln_ref)[0:N_FOX_HEADS, :]], axis=1)
    pieces = jnp.concatenate(_split3(l_all), axis=0).astype(bf16)
    c3 = jnp.dot(pieces, triu_ref[...], preferred_element_type=f32)
    c = c3[0:8] + c3[8:16] + c3[16:24]
    crow = jnp.concatenate([jnp.broadcast_to(c[hd:hd + 1, :], (tnew, nkeys)) for hd in range(N_FOX_HEADS)],
                           axis=0)
    rq = _row_iota((nq, LANES))
    cq = jnp.sum(jnp.where(_lane_iota((nq, LANES)) == rq % tnew, crow[:, past:nkeys], 0.0),
                 axis=1, keepdims=True)

    qrep = jnp.concatenate([q_ref[...]] * N_FOX_HEADS, axis=0)
    own = _row_iota((nq, FOX_WIDTH)) // tnew == _lane_iota((nq, FOX_WIDTH)) // HEAD_DIM
    qbd = jnp.where(own, qrep, jnp.zeros_like(qrep))
    s = jnp.dot(qbd, kall_ref[...], preferred_element_type=f32) + cq - crow
    s = jnp.where(_lane_iota((nq, nkeys)) <= past + _row_iota((nq, nkeys)) % tnew, s, -jnp.inf)
    p = jnp.exp(s - jnp.max(s, axis=1, keepdims=True))
    o2 = lax.dot_general(p.astype(bf16), vall_ref[...], (((1,), (1,)), ((), ())), preferred_element_type=f32)
    o2 = o2 / jnp.sum(p, axis=1, keepdims=True)
    olane = _lane_iota((tnew, FOX_WIDTH))
    out = jnp.zeros((tnew, FOX_WIDTH), f32)
    for hd in range(N_FOX_HEADS):
        out = out + jnp.where(olane // HEAD_DIM == hd, o2[hd * tnew:(hd + 1) * tnew, :], 0.0)
    o_ref[...] = (out * gate_ref[...]).astype(bf16)


def _fox_sample(q, k_new, v_new, logf_new, gate, cache_kt, cache_vt, cache_lt, layer, *, batch):
    n = q.shape[0]
    tnew = n // batch
    past = cache_kt.shape[3]
    nkeys = past + LANES
    assert N_FOX_HEADS * tnew == LANES and past % LANES == 0
    triu = jnp.asarray(np.triu(np.ones((nkeys, nkeys), np.float32)), bf16)
    rows = lambda w: pl.BlockSpec((tnew, w), lambda bi: (bi, 0))
    cache = lambda h: pl.BlockSpec((1, 1, h, past), lambda bi: (layer, bi, 0, 0))
    return pl.pallas_call(
        functools.partial(_fox_sample_kernel, past=past, tnew=tnew),
        grid=(batch,),
        in_specs=[rows(FOX_WIDTH), rows(FOX_WIDTH), rows(FOX_WIDTH), rows(LANES), rows(FOX_WIDTH),
                  cache(FOX_WIDTH), cache(FOX_WIDTH), cache(N_FOX_HEADS),
                  pl.BlockSpec((nkeys, nkeys), lambda bi: (0, 0))],
        out_specs=rows(FOX_WIDTH),
        out_shape=jax.ShapeDtypeStruct((n, FOX_WIDTH), bf16),
        scratch_shapes=[pltpu.VMEM((FOX_WIDTH, nkeys), bf16), pltpu.VMEM((FOX_WIDTH, nkeys), bf16)],
        compiler_params=_cparams(("arbitrary",)),
        name="fox_sample",
    )(q, k_new, v_new, logf_new, gate, cache_kt, cache_vt, cache_lt, triu)


def _s5_prep_kernel(are_r, aim_r, ldt_r, are_c, aim_c, ldt_c, bxr_ref, bxi_ref, cxr_ref, cxi_ref,
                    bst_ref, cst_ref, kt_ref, apow_ref):
    L, W, M = S5_CHUNK, SSM_WIDTH, SSM_MODES

    def cmul(xr, xi, yr, yi):
        return xr * yr - xi * yi, xr * yi + xi * yr

    def powers(are, aim, ldt, n):
        dt = jnp.exp(ldt[...])
        mag = jnp.exp(dt * are[...])
        p1 = (mag * jnp.cos(dt * aim[...]), mag * jnp.sin(dt * aim[...]))
        out = [(jnp.ones_like(mag), jnp.zeros_like(mag)), p1]
        for _ in range(n - 1):
            out.append(cmul(*out[-1], *p1))
        return out

    prow = powers(are_r, aim_r, ldt_r, L)
    pcol = powers(are_c, aim_c, ldt_c, L)
    ar, ai = are_r[...], aim_r[...]
    abr, abi = prow[1]
    den = ar * ar + ai * ai
    zr = ((abr - 1.0) * ar + abi * ai) / den
    zi = (abi * ar - (abr - 1.0) * ai) / den
    bbr, bbi = cmul(zr, zi, bxr_ref[...], bxi_ref[...])

    def b_pow(k):
        return jnp.concatenate(cmul(*prow[k], bbr, bbi), axis=1)

    def c_pow(k):
        pr, pi = pcol[k]
        cr, ci = cxr_ref[...], cxi_ref[...]
        return jnp.concatenate([cr * pr - ci * pi, -(cr * pi + ci * pr)], axis=0)

    cpow = [c_pow(k) for k in range(L + 1)]
    for tl in range(L):
        bst_ref[tl * W:(tl + 1) * W, :] = b_pow(L - 1 - tl).astype(bf16)
        cst_ref[:, tl * W:(tl + 1) * W] = cpow[tl + 1].astype(bf16)

    def hi_lo(x):
        hi = x.astype(bf16)
        return hi, (x - hi.astype(f32)).astype(bf16)

    bh, bl = hi_lo(b_pow(0))
    ch, cl = hi_lo(jnp.concatenate(cpow[:L], axis=1))
    kd = (jnp.dot(bh, ch, preferred_element_type=f32) + jnp.dot(bh, cl, preferred_element_type=f32)
          + jnp.dot(bl, ch, preferred_element_type=f32)).astype(bf16)
    for tl in range(L):
        if tl:
            kt_ref[tl * W:(tl + 1) * W, 0:tl * W] = jnp.zeros((W, tl * W), bf16)
        kt_ref[tl * W:(tl + 1) * W, tl * W:] = kd[:, :(L - tl) * W]

    step = prow[L]
    seg = step
    for _ in range(int(math.log2(S5_SEG))):
        seg = cmul(*seg, *seg)
    apow_ref[...] = jnp.concatenate(
        [jnp.concatenate(step, axis=1), jnp.concatenate(seg, axis=1), jnp.zeros((6, 2 * M), f32)], axis=0)


def _s5_prep_inputs(a_re, a_im, log_dt, b_re, b_im, c_re, c_im):
    G, N = SSM_GROUPS, SSM_STATE
    W, M = SSM_WIDTH, SSM_MODES
    depth = a_re.shape[0]
    eye = jnp.eye(G, dtype=f32)
    bx = lambda b: (jnp.swapaxes(b, 2, 3)[:, :, :, None, :] * eye[None, :, None, :, None]).reshape(depth, W, M)
    cx = lambda c: (jnp.swapaxes(c, 2, 3)[:, :, :, None, :] * eye[None, :, None, :, None]).reshape(depth, M, W)
    ldt = jnp.repeat(log_dt, N, axis=1)
    return [a_re.reshape(depth, 1, M), a_im.reshape(depth, 1, M), ldt.reshape(depth, 1, M),
            a_re.reshape(depth, M, 1), a_im.reshape(depth, M, 1), ldt.reshape(depth, M, 1),
            bx(b_re), bx(b_im), cx(c_re), cx(c_im)]


def _s5_prep(ins, layer):
    L, W, M = S5_CHUNK, SSM_WIDTH, SSM_MODES
    shapes = [((L * W, 2 * M), bf16), ((2 * M, L * W), bf16), ((L * W, L * W), bf16), ((8, 2 * M), f32)]
    return pl.pallas_call(
        _s5_prep_kernel,
        grid=(1,),
        in_specs=[_wspec(a, layer) for a in ins],
        out_specs=[pl.BlockSpec(sh, lambda i: (0, 0)) for sh, _ in shapes],
        out_shape=[jax.ShapeDtypeStruct(sh, dt) for sh, dt in shapes],
        compiler_params=_cparams(("arbitrary",)),
        name="s5_prep",
    )(*ins)


def _s5_kernel(u_ref, gs_ref, h0_ref, bst_ref, cst_ref, kt_ref, apow_ref, d_ref, wg_ref, bg_ref,
               o_ref, hT_ref, ds_ref, sp_ref, carry_ref, *, chained, seglen):
    L, W, M = S5_CHUNK, SSM_WIDTH, SSM_MODES
    nr = 8 * seglen
    nlb = M // LANES

    def chunked(ref, tl):
        return jnp.concatenate([ref[0, pl.ds(tl, nr, stride=L), :], ref[1, pl.ds(tl, nr, stride=L), :]], axis=1)

    uf = [chunked(u_ref, tl) for tl in range(L)]
    ub = jnp.concatenate(uf, axis=1).astype(bf16)
    ds = jnp.dot(ub, bst_ref[...], preferred_element_type=f32)
    for lb in range(2 * nlb):
        ds_ref[lb] = ds[:, lb * LANES:(lb + 1) * LANES]
    def blocks(row):
        return ([row[:, lb * LANES:(lb + 1) * LANES] for lb in range(nlb)],
                [row[:, M + lb * LANES:M + (lb + 1) * LANES] for lb in range(nlb)])

    def bcast8(parts):
        return [jnp.broadcast_to(p, (8, LANES)) for p in parts]

    ar, ai = (bcast8(p) for p in blocks(apow_ref[0:1, :]))

    def scan(vr, vi, store):
        for i in range(seglen):
            rows = pl.ds(i, 8, stride=seglen)
            for lb in range(nlb):
                if store:
                    sp_ref[lb, rows, :] = vr[lb]
                    sp_ref[nlb + lb, rows, :] = vi[lb]
                dr, di = ds_ref[lb, rows, :], ds_ref[nlb + lb, rows, :]
                vr[lb], vi[lb] = (ar[lb] * vr[lb] - ai[lb] * vi[lb] + dr,
                                  ar[lb] * vi[lb] + ai[lb] * vr[lb] + di)
        return vr, vi

    if chained:
        ti = pl.program_id(1)

        @pl.when(ti == 0)
        def _():
            carry_ref[0:1, :] = h0_ref[0]

        zero = [jnp.zeros((8, LANES), f32) for _ in range(nlb)]
        er, ei = scan(list(zero), list(zero), store=False)
        pr, pi = blocks(apow_ref[1:2, :])
        cr, ci = blocks(carry_ref[0:1, :])
        sr, si = [[] for _ in range(nlb)], [[] for _ in range(nlb)]
        for j in range(8):
            for lb in range(nlb):
                sr[lb].append(cr[lb])
                si[lb].append(ci[lb])
                cr[lb], ci[lb] = (pr[lb] * cr[lb] - pi[lb] * ci[lb] + er[lb][j:j + 1, :],
                                  pr[lb] * ci[lb] + pi[lb] * cr[lb] + ei[lb][j:j + 1, :])
        final = jnp.concatenate(cr + ci, axis=1)
        carry_ref[0:1, :] = final
        hT_ref[0] = final
        scan([jnp.concatenate(x, axis=0) for x in sr], [jnp.concatenate(x, axis=0) for x in si], store=True)
    else:
        vr, vi = blocks(h0_ref[...])
        vr, vi = scan(vr, vi, store=True)
        hT_ref[...] = jnp.concatenate(vr + vi, axis=1)

    spb = jnp.concatenate([sp_ref[lb] for lb in range(2 * nlb)], axis=1).astype(bf16)

    def y_pair(pair):
        kk = (2 * pair + 2) * W
        cols = slice(2 * pair * W, kk)
        return (jnp.dot(ub[:, :kk], kt_ref[0:kk, cols], preferred_element_type=f32)
                + jnp.dot(spb, cst_ref[:, cols], preferred_element_type=f32))

    def finish(pair, ypair):
        for tl in (2 * pair, 2 * pair + 1):
            y = ypair[:, (tl % 2) * W:(tl % 2 + 1) * W] + d_ref[...] * uf[tl]
            g = _gelu_tanh(y)
            gate = jax.nn.sigmoid(jnp.dot(g.astype(bf16), wg_ref[...], preferred_element_type=f32) + bg_ref[...])
            out = g * gate * chunked(gs_ref, tl)
            o_ref[0, pl.ds(tl, nr, stride=L), :] = out[:, :LANES]
            o_ref[1, pl.ds(tl, nr, stride=L), :] = out[:, LANES:]

    ycur = y_pair(0)
    for pair in range(L // 2):
        ynext = y_pair(pair + 1) if pair + 1 < L // 2 else None
        finish(pair, ycur)
        ycur = ynext


def _s5(u, gs, h0, ops, d_row, w_glu, b_glu, *, batch, chained, layer):
    L, W, M = S5_CHUNK, SSM_WIDTH, SSM_MODES
    n = u.shape[1]
    t = n // batch
    assert t % L == 0
    chunks = t // L
    bst, cst, kt, apow = ops
    if chained:
        seglen = S5_SEG
        assert chunks % (8 * seglen) == 0
        grid = (batch, chunks // (8 * seglen))
        state = pl.BlockSpec((1, 1, 2 * M), lambda bi, ti: (bi, 0, 0))
    else:
        seglen = chunks
        assert batch == 8
        grid = (1, 1)
        state = pl.BlockSpec((8, 2 * M), lambda bi, ti: (0, 0))
    tiles = grid[1]
    nr = 8 * seglen
    blk = pl.BlockSpec((2, nr * L, LANES), lambda bi, ti: (0, bi * tiles + ti, 0))
    full = lambda a: _wspec(a, layer)
    once = lambda a: pl.BlockSpec(a.shape, lambda bi, ti: (0,) * a.ndim, pipeline_mode=pl.Buffered(1))
    return pl.pallas_call(
        functools.partial(_s5_kernel, chained=chained, seglen=seglen),
        grid=grid,
        in_specs=[blk, blk, state, once(bst), once(cst), once(kt), full(apow), full(d_row),
                  full(w_glu), full(b_glu)],
        out_specs=[blk, state],
        out_shape=[jax.ShapeDtypeStruct((2, n, LANES), f32), jax.ShapeDtypeStruct(h0.shape, f32)],
        scratch_shapes=[pltpu.VMEM((2 * M // LANES, nr, LANES), f32), pltpu.VMEM((2 * M // LANES, nr, LANES), f32),
                        pltpu.VMEM((8, 2 * M), f32)],
        compiler_params=_cparams(("arbitrary", "arbitrary")),
        name="s5_prompt" if chained else "s5_sample",
    )(u, gs, h0, bst, cst, kt, apow, d_row, w_glu, b_glu)


def _memory_kv_kernel(mem_ref, g_ref, w_ref, kg_ref, hm_ref, mkt_ref, mvt_ref):
    batch, _, nmem = mkt_ref.shape
    x = mem_ref[...]
    ms = jnp.mean(x * x, axis=-1, keepdims=True)
    h = (x * lax.rsqrt(ms + EPS) * g_ref[...]).astype(bf16)
    mk = jnp.dot(h, w_ref[:, 0:MEM_WIDTH], preferred_element_type=f32)
    mk = _head_norm(mk, hm_ref.at[0:MEM_WIDTH, 0:MEM_WIDTH], kg_ref)
    mv = jnp.dot(h, w_ref[:, MEM_WIDTH:2 * MEM_WIDTH], preferred_element_type=f32)
    for b in range(batch):
        mkt_ref[b] = mk[b * nmem:(b + 1) * nmem, :].T
        mvt_ref[b] = mv[b * nmem:(b + 1) * nmem, :].T


def _memory_kv(mem, lw, *, batch, layer):
    n = mem.shape[0]
    ins = [mem, lw["mem_norm"], lw["w_mem_kv"], lw["mem_k_norm"], lw["head_mean"]]
    out = pl.BlockSpec((batch, MEM_WIDTH, n // batch), lambda i: (0, 0, 0))
    return pl.pallas_call(
        _memory_kv_kernel,
        grid=(1,),
        in_specs=[_wspec(a, layer) for a in ins],
        out_specs=[out, out],
        out_shape=[jax.ShapeDtypeStruct((batch, MEM_WIDTH, n // batch), f32)] * 2,
        compiler_params=_cparams(("arbitrary",)),
        name="memory_kv",
    )(*ins)


def _mix_out_kernel(x_ref, fox_ref, ssm_ref, mq_ref, gm_ref, mkt_ref, mvt_ref, w_ref, o_ref):
    tm = x_ref.shape[0]
    nmem = mkt_ref.shape[2]
    mq = mq_ref[...]
    mkt = mkt_ref[0].astype(bf16)
    mvt = mvt_ref[0].astype(bf16)
    khead = _row_iota((MEM_WIDTH, nmem)) // HEAD_DIM
    zero = jnp.zeros_like(mkt)
    mk_heads = jnp.concatenate([jnp.where(khead == hd, mkt, zero) for hd in range(N_MEM_HEADS)], axis=1)
    s_all = jnp.dot(mq, mk_heads, preferred_element_type=f32)
    ssm = jnp.concatenate([ssm_ref[0], ssm_ref[1]], axis=1).astype(bf16)
    y = x_ref[...]
    y = y + jnp.dot(fox_ref[...], w_ref[0:FOX_WIDTH, :], preferred_element_type=f32)
    y = y + jnp.dot(ssm, w_ref[FOX_WIDTH:FOX_WIDTH + SSM_WIDTH, :], preferred_element_type=f32)
    mem = jnp.zeros((tm, MEM_WIDTH), f32)
    for hd in range(N_MEM_HEADS):
        s = s_all[:, hd * nmem:(hd + 1) * nmem]
        p = jnp.exp(s - jnp.max(s, axis=-1, keepdims=True))
        p = p / jnp.sum(p, axis=-1, keepdims=True)
        vh = jnp.where(khead == hd, mvt, zero)
        mem = mem + lax.dot_general(p.astype(bf16), vh, (((1,), (1,)), ((), ())), preferred_element_type=f32)
    memg = (mem * gm_ref[...]).astype(bf16)
    o_ref[...] = y + jnp.dot(memg, w_ref[FOX_WIDTH + SSM_WIDTH:, :], preferred_element_type=f32)


def _mix_out(x, fox, ssm, mq, gm, mkt, mvt, w_out, *, batch, tm, layer):
    n, d = x.shape
    t = n // batch
    assert t % tm == 0
    tiles_per_batch = t // tm
    row = lambda w: pl.BlockSpec((tm, w), lambda i: (i, 0))
    if mkt.ndim == 4:
        memb = pl.BlockSpec((None, 1) + mkt.shape[2:], lambda i: (layer, i // tiles_per_batch, 0, 0))
    else:
        memb = pl.BlockSpec((1,) + mkt.shape[1:], lambda i: (i // tiles_per_batch, 0, 0))
    return pl.pallas_call(
        _mix_out_kernel,
        grid=(n // tm,),
        in_specs=[row(d), row(FOX_WIDTH), pl.BlockSpec((2, tm, LANES), lambda i: (0, i, 0)),
                  row(MEM_WIDTH), row(MEM_WIDTH), memb, memb, _wspec(w_out, layer)],
        out_specs=row(d),
        out_shape=jax.ShapeDtypeStruct((n, d), f32),
        compiler_params=_cparams(("arbitrary",)),
        name="mix_out_prompt" if tiles_per_batch > 1 else "mix_out_sample",
    )(x, fox, ssm, mq, gm, mkt, mvt, w_out)


def _stacked_weights(norm_g, w_in, b_forget, fox_q_norm, fox_k_norm, mem_q_norm, mem_norm, w_mem_kv,
                     mem_k_norm, w_out, w_glu, b_glu, ssm_d):
    depth, d, _ = w_in.shape
    nf = 4 * FOX_WIDTH
    wb = w_in.astype(bf16)
    w_packed = jnp.concatenate(
        [wb[:, :, :FOX_WIDTH], wb[:, :, nf:nf + N_FOX_HEADS], jnp.zeros((depth, d, LANES - N_FOX_HEADS), bf16),
         wb[:, :, FOX_WIDTH:nf], wb[:, :, nf + N_FOX_HEADS:]], axis=2)
    assert w_packed.shape[2] == _W_COLS
    row = lambda a: a.reshape(depth, 1, -1)
    per_head = lambda g, heads: jnp.tile(g, (1, heads)).reshape(depth, 1, heads * HEAD_DIM)
    return dict(
        norm_g=row(norm_g), w_in=w_packed,
        b_forget=row(jnp.pad(b_forget, ((0, 0), (0, LANES - N_FOX_HEADS)))),
        fox_q_norm=per_head(fox_q_norm, N_FOX_HEADS), fox_k_norm=per_head(fox_k_norm, N_FOX_HEADS),
        mem_q_norm=per_head(mem_q_norm, N_MEM_HEADS), mem_k_norm=per_head(mem_k_norm, N_MEM_HEADS),
        mem_norm=row(mem_norm), w_mem_kv=w_mem_kv.astype(bf16),
        head_mean=_head_mean_matrix(), bias_sel=_bias_selector(),
        w_out=w_out.astype(bf16), w_glu=w_glu.astype(bf16), b_glu=row(b_glu), ssm_d=row(ssm_d))


def _split_state(hT, batch):
    hT = hT.reshape(batch, 2, SSM_GROUPS, SSM_STATE)
    return hT[:, 0], hT[:, 1]


def kernel(x_prompt, x_sample, mem_prompt, cache_fox_k, cache_fox_v, cache_fox_logf, state_ssm_re, state_ssm_im, cache_mem_k, cache_mem_v, norm_g, w_in, b_forget, fox_q_norm, fox_k_norm, ssm_a_re, ssm_a_im, ssm_log_dt, ssm_b_re, ssm_b_im, ssm_c_re, ssm_c_im, ssm_d, w_glu, b_glu, mem_norm, w_mem_kv, mem_q_norm, mem_k_norm, w_out):
    B, T, D = x_prompt.shape
    Bs, Ts, _ = x_sample.shape
    depth = w_in.shape[0]
    past = cache_fox_k.shape[2]
    nmem = mem_prompt.shape[1]
    M = SSM_MODES

    xp = x_prompt.reshape(B * T, D)
    xs = x_sample.reshape(Bs * Ts, D)
    mem = mem_prompt.reshape(B * nmem, D)
    cache_kt = jnp.transpose(cache_fox_k, (0, 1, 3, 4, 2)).reshape(depth, Bs, FOX_WIDTH, past)
    cache_vt = jnp.transpose(cache_fox_v, (0, 1, 3, 4, 2)).reshape(depth, Bs, FOX_WIDTH, past)
    cache_lt = jnp.transpose(cache_fox_logf, (0, 1, 3, 2)).astype(f32)
    cache_mkt = jnp.transpose(cache_mem_k, (0, 1, 3, 4, 2)).reshape(depth, Bs, MEM_WIDTH, nmem)
    cache_mvt = jnp.transpose(cache_mem_v, (0, 1, 3, 4, 2)).reshape(depth, Bs, MEM_WIDTH, nmem)

    outs = {k: [] for k in ("pf", "pre", "pim", "pmk", "pmv", "sk", "sv", "sf", "sre", "sim")}
    kv_leaves = None
    lw = _stacked_weights(norm_g, w_in, b_forget, fox_q_norm, fox_k_norm, mem_q_norm, mem_norm,
                          w_mem_kv, mem_k_norm, w_out, w_glu, b_glu, ssm_d)
    s5_params = _s5_prep_inputs(ssm_a_re, ssm_a_im, ssm_log_dt, ssm_b_re, ssm_b_im, ssm_c_re, ssm_c_im)
    for l in range(depth):
        ops = _s5_prep(s5_params, l)
        mkt, mvt = _memory_kv(mem, lw, batch=B, layer=l)

        pr = _in_proj(xp, lw, batch=B, prompt=True, layer=l, depth=depth, kv_leaves=kv_leaves)
        kv_leaves = (pr["kt"], pr["vt"])
        fox = _fox_prompt(pr["qa"], pr["ka"], pr["va"], pr["gf"].reshape(B, T, FOX_WIDTH)).reshape(B * T, FOX_WIDTH)
        ssm, hT = _s5(pr["u"], pr["gs"], jnp.zeros((B, 1, 2 * M), f32), ops, lw["ssm_d"], lw["w_glu"],
                      lw["b_glu"], batch=B, chained=True, layer=l)
        xp = _mix_out(xp, fox, ssm, pr["mq"], pr["gm"], mkt, mvt,
                      lw["w_out"], batch=B, tm=MIX_TILE, layer=l)
        hre, him = _split_state(hT, B)
        outs["pf"].append(pr["logft"])
        outs["pre"].append(hre)
        outs["pim"].append(him)
        outs["pmk"].append(mkt.reshape(B, N_MEM_HEADS, HEAD_DIM, nmem))
        outs["pmv"].append(mvt.reshape(B, N_MEM_HEADS, HEAD_DIM, nmem))

        sr = _in_proj(xs, lw, batch=Bs, prompt=False, layer=l)
        fox_s = _fox_sample(sr["q"], sr["k"], sr["v"], sr["logf"], sr["gf"], cache_kt, cache_vt, cache_lt, l,
                            batch=Bs)
        h0 = jnp.concatenate([state_ssm_re[l].reshape(Bs, M), state_ssm_im[l].reshape(Bs, M)],
                             axis=-1).astype(f32)
        ssm_s, hT_s = _s5(sr["u"], sr["gs"], h0, ops, lw["ssm_d"], lw["w_glu"], lw["b_glu"], batch=Bs,
                          chained=False, layer=l)
        xs = _mix_out(xs, fox_s, ssm_s, sr["mq"], sr["gm"], cache_mkt, cache_mvt,
                      lw["w_out"], batch=Bs, tm=Ts, layer=l)
        sre, sim = _split_state(hT_s, Bs)
        outs["sk"].append(sr["k"].reshape(Bs, Ts, N_FOX_HEADS, HEAD_DIM))
        outs["sv"].append(sr["v"].reshape(Bs, Ts, N_FOX_HEADS, HEAD_DIM))
        outs["sf"].append(sr["logf"][:, :N_FOX_HEADS].reshape(Bs, Ts, N_FOX_HEADS))
        outs["sre"].append(sre)
        outs["sim"].append(sim)

    st = {k: jnp.stack(v) for k, v in outs.items()}
    fox_k_prompt = jnp.transpose(kv_leaves[0], (0, 1, 4, 2, 3))
    fox_v_prompt = jnp.transpose(kv_leaves[1], (0, 1, 4, 2, 3))
    fox_logf_prompt = jnp.transpose(st["pf"], (0, 1, 3, 2))
    mem_k_prompt = jnp.transpose(st["pmk"], (0, 1, 4, 2, 3))
    mem_v_prompt = jnp.transpose(st["pmv"], (0, 1, 4, 2, 3))
    return (xp.reshape(B, T, D), xs.reshape(Bs, Ts, D), fox_k_prompt, fox_v_prompt, fox_logf_prompt,
            st["pre"], st["pim"], mem_k_prompt, mem_v_prompt, st["sk"], st["sv"], st["sf"], st["sre"], st["sim"])
```

```python
import functools
import math

import numpy as np
import jax
import jax.numpy as jnp
from jax import lax
from jax.experimental import pallas as pl
from jax.experimental.pallas import tpu as pltpu

f32 = jnp.float32
bf16 = jnp.bfloat16

HEAD_DIM = 64
N_FOX_HEADS = 8
FOX_WIDTH = N_FOX_HEADS * HEAD_DIM
SSM_GROUPS = 16
SSM_CH = 16
SSM_STATE = 64
SSM_WIDTH = SSM_GROUPS * SSM_CH
SSM_MODES = SSM_GROUPS * SSM_STATE
N_MEM_HEADS = 4
MEM_WIDTH = N_MEM_HEADS * HEAD_DIM
EPS = 1e-6
QK_SCALE = HEAD_DIM ** -0.5
LOG2E = math.log2(math.e)

LANES = 128
S5_CHUNK = 8
S5_SEG = 16
ROW_TILE = 512
MIX_TILE = 1024
ATTN_TQ = 512
ATTN_UNROLL = 14
VMEM_LIMIT = 56 * 1024 * 1024

_C_Q, _C_F, _C_K, _C_V, _C_G = 0, 512, 640, 1152, 1664
_C_SU, _C_SG, _C_MQ, _C_MG = 2176, 2432, 2688, 2944
_W_COLS = 3200
_N_PIECES = 3


def _cparams(sem):
    return pltpu.CompilerParams(dimension_semantics=sem, vmem_limit_bytes=VMEM_LIMIT)


def _wspec(a, layer):
    if a.ndim == 3:
        return pl.BlockSpec((None,) + a.shape[1:], lambda *_: (layer, 0, 0))
    return pl.BlockSpec(a.shape, lambda *_: (0,) * a.ndim)


def _silu(x):
    return x * jax.nn.sigmoid(x)


def _log_sigmoid(x):
    return -(jnp.maximum(-x, 0.0) + jnp.log1p(jnp.exp(-jnp.abs(x))))


def _gelu_tanh(x):
    return 0.5 * x * (1.0 + jnp.tanh(math.sqrt(2.0 / math.pi) * (x + 0.044715 * (x * x * x))))


def _split3(x):
    hi = x.astype(bf16).astype(f32)
    r = x - hi
    mid = r.astype(bf16).astype(f32)
    lo = (r - mid).astype(bf16).astype(f32)
    return hi, mid, lo


def _pack3(x):
    hi, mid, lo = _split3(x)
    return (hi + pltpu.roll(mid, 8, 1) + pltpu.roll(lo, 16, 1)).astype(bf16)


def _unpack3(y):
    return y + pltpu.roll(y, LANES - 8, 1) + pltpu.roll(y, LANES - 16, 1)


def _lane_iota(shape):
    return lax.broadcasted_iota(jnp.int32, shape, len(shape) - 1)


def _row_iota(shape):
    return lax.broadcasted_iota(jnp.int32, shape, len(shape) - 2)


def _head_norm(z, hm_ref, g_ref):
    ms = jnp.dot((z * z).astype(bf16), hm_ref[...], preferred_element_type=f32)
    return z * lax.rsqrt(ms + EPS) * g_ref[...]


def _in_proj_kernel(*refs, prompt, tiles_per_batch, aliased):
    if prompt:
        if aliased:
            refs = refs[:10] + refs[12:]
        (x_ref, ng_ref, w_ref, bf_ref, qg_ref, kg_ref, mqg_ref, hm_ref, tril_ref, eb_ref,
         kt_ref, vt_ref, logft_ref, gf_ref, u_ref, gs_ref, mq_ref, gm_ref, qa_ref, ka_ref, va_ref,
         carry_ref) = refs
    else:
        (x_ref, ng_ref, w_ref, bf_ref, qg_ref, kg_ref, mqg_ref, hm_ref,
         k_ref, v_ref, logf_ref, gf_ref, u_ref, gs_ref, mq_ref, gm_ref, q_ref) = refs

    x = x_ref[...]
    tm = x.shape[0]
    lane = _lane_iota((tm, LANES))
    ms = jnp.mean(x * x, axis=-1, keepdims=True)
    h = (x * lax.rsqrt(ms + EPS) * ng_ref[...]).astype(bf16)

    def seg(lo, width):
        return jnp.dot(h, w_ref[:, lo:lo + width], preferred_element_type=f32)

    def halves(ref, z):
        ref[0] = z[:, :LANES]
        ref[1] = z[:, LANES:]

    zqf = seg(_C_Q, FOX_WIDTH + LANES)
    logf = _log_sigmoid(zqf[:, FOX_WIDTH:] + bf_ref[...])
    lf = jnp.where(lane < N_FOX_HEADS, logf, 0.0)
    zk = seg(_C_K, FOX_WIDTH)
    if prompt:
        i = pl.program_id(0)

        @pl.when(i % tiles_per_batch == 0)
        def _():
            carry_ref[...] = jnp.zeros_like(carry_ref)

        csum = jnp.dot(tril_ref[...], _pack3(lf), preferred_element_type=f32)
    v = seg(_C_V, FOX_WIDTH)
    if prompt:
        c = jnp.where(lane < N_FOX_HEADS, _unpack3(csum) + carry_ref[0:1, :], 0.0)
        carry_ref[0:1, :] = c[tm - 1:tm, :]
        pieces = (_pack3(c * LOG2E).astype(f32) + jnp.where(lane == 3 * N_FOX_HEADS, 1.0, 0.0)).astype(bf16)
    if prompt:
        aug = jnp.dot(pieces, eb_ref[...], preferred_element_type=f32)
    q = _head_norm(zqf[:, :FOX_WIDTH], hm_ref, qg_ref) * QK_SCALE
    k = _head_norm(zk, hm_ref, kg_ref)

    def gate_proj():
        gf_ref[...] = _silu(seg(_C_G, FOX_WIDTH))

    def ssm_proj():
        zs = seg(_C_SU, 2 * SSM_WIDTH)
        halves(u_ref, zs[:, :SSM_WIDTH])
        halves(gs_ref, _silu(zs[:, SSM_WIDTH:]))

    def mem_proj():
        zm = seg(_C_MQ, 2 * MEM_WIDTH)
        mq = _head_norm(zm[:, :MEM_WIDTH], hm_ref.at[0:MEM_WIDTH, 0:MEM_WIDTH], mqg_ref) * QK_SCALE
        mq_ref[...] = mq.astype(bf16)
        gm_ref[...] = _silu(zm[:, MEM_WIDTH:])

    late = [gate_proj, ssm_proj, mem_proj]
    if not prompt:
        for proj in late:
            proj()
        k_ref[...] = k
        v_ref[...] = v
        logf_ref[...] = lf
        q_ref[...] = q.astype(bf16)
        return

    kt = k.T
    vt = v.T
    for hd in range(N_FOX_HEADS):
        kt_ref[0, 0, hd] = kt[hd * HEAD_DIM:(hd + 1) * HEAD_DIM, :]
        vt_ref[0, 0, hd] = vt[hd * HEAD_DIM:(hd + 1) * HEAD_DIM, :]
    for later in range(1, kt_ref.shape[0]):
        kt_ref[later] = jnp.zeros(kt_ref.shape[1:], f32)
        vt_ref[later] = jnp.zeros(vt_ref.shape[1:], f32)
    logft_ref[0] = lf.T[0:N_FOX_HEADS, :]

    q = q * LOG2E
    qaug, kaug = aug[:, :LANES], aug[:, LANES:]
    for hd in range(N_FOX_HEADS):
        if hd % 3 == 0 and late:
            late.pop(0)()
        pair = (hd // 2) * LANES
        own = (lane < HEAD_DIM) if hd % 2 == 0 else (lane >= HEAD_DIM)
        ones_lane = HEAD_DIM if hd % 2 == 0 else 0
        g0 = _bias_group(hd)
        bias = (lane >= g0) & (lane < g0 + 2 * _N_PIECES)
        qa_ref[0, hd] = jnp.where(own, q[:, pair:pair + LANES], jnp.where(bias, qaug, 0.0)).astype(bf16)
        ka_ref[0, hd] = jnp.where(own, k[:, pair:pair + LANES], kaug).astype(bf16)
        va_ref[0, hd] = jnp.where(own, v[:, pair:pair + LANES],
                                  jnp.where(lane == ones_lane, 1.0, 0.0)).astype(bf16)


def _bias_group(hd):
    return (HEAD_DIM if hd % 2 == 0 else 0) + 16 * (hd // 2)


def _bias_selector():
    eb = np.zeros((LANES, 2 * LANES), np.float32)
    one = 3 * N_FOX_HEADS
    for hd in range(N_FOX_HEADS):
        g0 = _bias_group(hd)
        for p in range(_N_PIECES):
            eb[p * N_FOX_HEADS + hd, g0 + p] = 1.0
            eb[one, g0 + _N_PIECES + p] = 1.0
            eb[one, LANES + g0 + p] = 1.0
            eb[p * N_FOX_HEADS + hd, LANES + g0 + _N_PIECES + p] = -1.0
    return jnp.asarray(eb, bf16)


def _head_mean_matrix():
    idx = np.arange(FOX_WIDTH) // HEAD_DIM
    return jnp.asarray((idx[:, None] == idx[None, :]).astype(np.float32) / HEAD_DIM, bf16)


def _in_proj(x, lw, *, batch, prompt, layer=0, depth=1, kv_leaves=None):
    n, d = x.shape
    t = n // batch
    tm = ROW_TILE if prompt else n
    assert n % tm == 0 and (t % tm == 0 or not prompt)
    tiles_per_batch = t // tm if prompt else 1
    grid = (n // tm,)
    row = lambda w: pl.BlockSpec((tm, w), lambda i: (i, 0))
    full = lambda a: _wspec(a, layer)
    split = pl.BlockSpec((2, tm, LANES), lambda i: (0, i, 0))
    by_batch = lambda i: (i // tiles_per_batch, 0, i % tiles_per_batch, 0)

    ins = [x, lw["norm_g"], lw["w_in"], lw["b_forget"], lw["fox_q_norm"], lw["fox_k_norm"],
           lw["mem_q_norm"], lw["head_mean"]]
    in_specs = [row(d)] + [full(a) for a in ins[1:]]
    common = [((n, FOX_WIDTH), f32, row(FOX_WIDTH)), ((2, n, LANES), f32, split), ((2, n, LANES), f32, split),
              ((n, MEM_WIDTH), bf16, row(MEM_WIDTH)), ((n, MEM_WIDTH), f32, row(MEM_WIDTH))]
    scratch = []
    if prompt:
        tril = jnp.asarray(np.tril(np.ones((tm, tm), np.float32)), bf16)
        extra = [tril, lw["bias_sel"]]
        ins += extra
        in_specs += [full(a) for a in extra]
        aliases = {}
        if kv_leaves is None:
            tspec = pl.BlockSpec((depth, 1, N_FOX_HEADS, HEAD_DIM, tm),
                                 lambda i: (0, i // tiles_per_batch, 0, 0, i % tiles_per_batch))
        else:
            aliases = {len(ins): 0, len(ins) + 1: 1}
            ins += list(kv_leaves)
            in_specs += [pl.BlockSpec(memory_space=pl.ANY)] * 2
            tspec = pl.BlockSpec((1, 1, N_FOX_HEADS, HEAD_DIM, tm),
                                 lambda i: (layer, i // tiles_per_batch, 0, 0, i % tiles_per_batch))
        lspec = pl.BlockSpec((1, N_FOX_HEADS, tm), lambda i: (i // tiles_per_batch, 0, i % tiles_per_batch))
        hspec = pl.BlockSpec((1, N_FOX_HEADS, tm, LANES), by_batch)
        tshape = (depth, batch, N_FOX_HEADS, HEAD_DIM, t)
        outs = ([(tshape, f32, tspec), (tshape, f32, tspec), ((batch, N_FOX_HEADS, t), f32, lspec)] + common
                + [((batch, N_FOX_HEADS, t, LANES), bf16, hspec)] * 3)
        names = ["kt", "vt", "logft", "gf", "u", "gs", "mq", "gm", "qa", "ka", "va"]
        scratch = [pltpu.VMEM((8, LANES), f32)]
    else:
        outs = ([((n, FOX_WIDTH), f32, row(FOX_WIDTH)), ((n, FOX_WIDTH), f32, row(FOX_WIDTH)),
                 ((n, LANES), f32, row(LANES))] + common + [((n, FOX_WIDTH), bf16, row(FOX_WIDTH))])
        names = ["k", "v", "logf", "gf", "u", "gs", "mq", "gm", "q"]
        aliases = {}

    res = pl.pallas_call(
        functools.partial(_in_proj_kernel, prompt=prompt, tiles_per_batch=tiles_per_batch, aliased=bool(aliases)),
        grid=grid, in_specs=in_specs, out_specs=[o[2] for o in outs],
        out_shape=[jax.ShapeDtypeStruct(o[0], o[1]) for o in outs],
        scratch_shapes=scratch, input_output_aliases=aliases,
        compiler_params=_cparams(("arbitrary",)),
        name="in_proj_prompt" if prompt else "in_proj_sample",
    )(*ins)
    return dict(zip(names, res))


def _fox_prompt_kernel(qa_ref, ka_ref, va_ref, gate_ref, o_ref, s0_ref, s1_ref, m_ref, acc_ref, *, nq):
    tq = tk = ATTN_TQ
    s_refs = (s0_ref, s1_ref)
    m_ref[...] = jnp.full(m_ref.shape, -jnp.inf, f32)
    acc_ref[...] = jnp.zeros(acc_ref.shape, f32)

    def scores(qb, kb, slot):
        qoff = pl.multiple_of(qb * tq, tq)
        koff = pl.multiple_of(kb * tk, tk)
        for j in range(2):
            s_refs[slot][j] = lax.dot_general(qa_ref[0, j, pl.ds(qoff, tq), :], ka_ref[0, j, pl.ds(koff, tk), :],
                                              (((1,), (1,)), ((), ())), preferred_element_type=f32)

    def softmax_pv(qb, kb, slot, masked):
        koff = pl.multiple_of(kb * tk, tk)
        for j in range(2):
            s = s_refs[slot][j]
            if masked:
                s = jnp.where(_row_iota((tq, tk)) >= _lane_iota((tq, tk)), s, -jnp.inf)
            m = m_ref[qb, j]
            m_new = jnp.maximum(m, jnp.max(s, axis=-1, keepdims=True))
            p = jnp.exp2(s - jnp.concatenate([m_new] * (tk // LANES), axis=1))
            vv = va_ref[0, j, pl.ds(koff, tk), :]
            acc_ref[qb, j] = (jnp.exp2(m - m_new) * acc_ref[qb, j]
                              + jnp.dot(p.astype(bf16), vv, preferred_element_type=f32))
            m_ref[qb, j] = m_new

    def finalize(qb):
        h0 = acc_ref[qb, 0] / acc_ref[qb, 0, :, HEAD_DIM:HEAD_DIM + 1]
        h1 = acc_ref[qb, 1] / acc_ref[qb, 1, :, 0:1]
        rows = pl.ds(pl.multiple_of(qb * tq, tq), tq)
        out = jnp.where(_lane_iota((tq, LANES)) < HEAD_DIM, h0, h1) * gate_ref[0, rows, :]
        o_ref[0, rows, :] = out.astype(bf16)

    def after(qb, kb):
        row_end = kb + 1 >= qb
        last = jnp.logical_and(row_end, qb + 1 >= nq)
        nqb = jnp.where(last, 0, jnp.where(row_end, qb + 1, qb))
        nkb = jnp.where(row_end, 0, kb + 1)
        return nqb, nkb

    def lower_step(qb, kb, slot):
        nqb, nkb = after(qb, kb)
        scores(nqb, nkb, 1 - slot)
        softmax_pv(qb, kb, slot, False)
        return nqb, nkb

    def diag_step(qb, slot):
        nxt = jnp.minimum(qb + 1, nq - 1)
        scores(nxt, nxt, 1 - slot)
        softmax_pv(qb, qb, slot, True)
        finalize(qb)

    def steps_per_trip(n):
        return max(d for d in range(2, ATTN_UNROLL + 1, 2) if n % d == 0)

    n_lower = nq * (nq - 1) // 2
    if n_lower:
        scores(1, 0, 0)
        per = steps_per_trip(n_lower)

        def lower_body(i, carry):
            for k in range(per):
                carry = lower_step(*carry, k % 2)
            return carry

        lax.fori_loop(0, n_lower // per, lower_body, (jnp.int32(1), jnp.int32(0)))
    else:
        scores(0, 0, 0)

    per_diag = steps_per_trip(nq)

    def diag_body(i, carry):
        for k in range(per_diag):
            diag_step(per_diag * i + k, k % 2)
        return carry

    lax.fori_loop(0, nq // per_diag, diag_body, 0)


def _fox_prompt(qa, ka, va, gate):
    b, nh, t, _ = qa.shape
    nq = t // ATTN_TQ
    assert t % ATTN_TQ == 0 and nq % 2 == 0 and (nq * (nq - 1) // 2) % 2 == 0
    whole = pl.BlockSpec((1, 2, t, LANES), lambda bi, hp: (bi, hp, 0, 0))
    return pl.pallas_call(
        functools.partial(_fox_prompt_kernel, nq=nq),
        grid=(b, nh // 2),
        in_specs=[whole, whole, whole, pl.BlockSpec((1, t, LANES), lambda bi, hp: (bi, 0, hp))],
        out_specs=pl.BlockSpec((1, t, LANES), lambda bi, hp: (bi, 0, hp)),
        out_shape=jax.ShapeDtypeStruct((b, t, FOX_WIDTH), bf16),
        scratch_shapes=[pltpu.VMEM((2, ATTN_TQ, ATTN_TQ), f32), pltpu.VMEM((2, ATTN_TQ, ATTN_TQ), f32),
                        pltpu.VMEM((nq, 2, ATTN_TQ, LANES), f32), pltpu.VMEM((nq, 2, ATTN_TQ, LANES), f32)],
        compiler_params=_cparams(("arbitrary", "arbitrary")),
        name="fox_prompt",
    )(qa, ka, va, gate)


def _fox_sample_kernel(q_ref, kn_ref, vn_ref, ln_ref, gate_ref, kc_ref, vc_ref, lc_ref, triu_ref, o_ref,
                       kall_ref, vall_ref, *, past, tnew):
    nkeys = kall_ref.shape[1]
    nq = N_FOX_HEADS * tnew

    def new_cols(ref):
        z = jnp.concatenate([ref[...], jnp.zeros((LANES - tnew, ref.shape[1]), f32)], axis=0)
        return z.T

    kall_ref[:, 0:past] = kc_ref[0, 0].astype(bf16)
    vall_ref[:, 0:past] = vc_ref[0, 0].astype(bf16)
    kall_ref[:, past:nkeys] = new_cols(kn_ref).astype(bf16)
    vall_ref[:, past:nkeys] = new_cols(vn_ref).astype(bf16)

    l_all = jnp.concatenate([lc_ref[0, 0], new_cols(ln_ref)[0:N_FOX_HEADS, :]], axis=1)
    pieces = jnp.concatenate(_split3(l_all), axis=0).astype(bf16)
    c3 = jnp.dot(pieces, triu_ref[...], preferred_element_type=f32)
    c = c3[0:8] + c3[8:16] + c3[16:24]
    crow = jnp.concatenate([jnp.broadcast_to(c[hd:hd + 1, :], (tnew, nkeys)) for hd in range(N_FOX_HEADS)],
                           axis=0)
    rq = _row_iota((nq, LANES))
    cq = jnp.sum(jnp.where(_lane_iota((nq, LANES)) == rq % tnew, crow[:, past:nkeys], 0.0),
                 axis=1, keepdims=True)

    qrep = jnp.concatenate([q_ref[...]] * N_FOX_HEADS, axis=0)
    own = _row_iota((nq, FOX_WIDTH)) // tnew == _lane_iota((nq, FOX_WIDTH)) // HEAD_DIM
    qbd = jnp.where(own, qrep, jnp.zeros_like(qrep))
    s = jnp.dot(qbd, kall_ref[...], preferred_element_type=f32) + cq - crow
    s = jnp.where(_lane_iota((nq, nkeys)) <= past + _row_iota((nq, nkeys)) % tnew, s, -jnp.inf)
    p = jnp.exp(s - jnp.max(s, axis=1, keepdims=True))
    o2 = lax.dot_general(p.astype(bf16), vall_ref[...], (((1,), (1,)), ((), ())), preferred_element_type=f32)
    o2 = o2 / jnp.sum(p, axis=1, keepdims=True)
    olane = _lane_iota((tnew, FOX_WIDTH))
    out = jnp.zeros((tnew, FOX_WIDTH), f32)
    for hd in range(N_FOX_HEADS):
        out = out + jnp.where(olane // HEAD_DIM == hd, o2[hd * tnew:(hd + 1) * tnew, :], 0.0)
    o_ref[...] = (out * gate_ref[...]).astype(bf16)


def _fox_sample(q, k_new, v_new, logf_new, gate, cache_kt, cache_vt, cache_lt, layer, *, batch):
    n = q.shape[0]
    tnew = n // batch
    past = cache_kt.shape[3]
    nkeys = past + LANES
    assert N_FOX_HEADS * tnew == LANES and past % LANES == 0
    triu = jnp.asarray(np.triu(np.ones((nkeys, nkeys), np.float32)), bf16)
    rows = lambda w: pl.BlockSpec((tnew, w), lambda bi: (bi, 0))
    cache = lambda h: pl.BlockSpec((1, 1, h, past), lambda bi: (layer, bi, 0, 0))
    return pl.pallas_call(
        functools.partial(_fox_sample_kernel, past=past, tnew=tnew),
        grid=(batch,),
        in_specs=[rows(FOX_WIDTH), rows(FOX_WIDTH), rows(FOX_WIDTH), rows(LANES), rows(FOX_WIDTH),
                  cache(FOX_WIDTH), cache(FOX_WIDTH), cache(N_FOX_HEADS),
                  pl.BlockSpec((nkeys, nkeys), lambda bi: (0, 0))],
        out_specs=rows(FOX_WIDTH),
        out_shape=jax.ShapeDtypeStruct((n, FOX_WIDTH), bf16),
        scratch_shapes=[pltpu.VMEM((FOX_WIDTH, nkeys), bf16), pltpu.VMEM((FOX_WIDTH, nkeys), bf16)],
        compiler_params=_cparams(("arbitrary",)),
        name="fox_sample",
    )(q, k_new, v_new, logf_new, gate, cache_kt, cache_vt, cache_lt, triu)


def _s5_prep_kernel(are_r, aim_r, ldt_r, are_c, aim_c, ldt_c, bxr_ref, bxi_ref, cxr_ref, cxi_ref,
                    bst_ref, cst_ref, kt_ref, apow_ref):
    L, W, M = S5_CHUNK, SSM_WIDTH, SSM_MODES

    def cmul(xr, xi, yr, yi):
        return xr * yr - xi * yi, xr * yi + xi * yr

    def powers(are, aim, ldt, n):
        dt = jnp.exp(ldt[...])
        mag = jnp.exp(dt * are[...])
        p1 = (mag * jnp.cos(dt * aim[...]), mag * jnp.sin(dt * aim[...]))
        out = [(jnp.ones_like(mag), jnp.zeros_like(mag)), p1]
        for _ in range(n - 1):
            out.append(cmul(*out[-1], *p1))
        return out

    prow = powers(are_r, aim_r, ldt_r, L)
    pcol = powers(are_c, aim_c, ldt_c, L)
    ar, ai = are_r[...], aim_r[...]
    abr, abi = prow[1]
    den = ar * ar + ai * ai
    zr = ((abr - 1.0) * ar + abi * ai) / den
    zi = (abi * ar - (abr - 1.0) * ai) / den
    bbr, bbi = cmul(zr, zi, bxr_ref[...], bxi_ref[...])

    def b_pow(k):
        return jnp.concatenate(cmul(*prow[k], bbr, bbi), axis=1)

    def c_pow(k):
        pr, pi = pcol[k]
        cr, ci = cxr_ref[...], cxi_ref[...]
        return jnp.concatenate([cr * pr - ci * pi, -(cr * pi + ci * pr)], axis=0)

    cpow = [c_pow(k) for k in range(L + 1)]
    for tl in range(L):
        bst_ref[tl * W:(tl + 1) * W, :] = b_pow(L - 1 - tl).astype(bf16)
        cst_ref[:, tl * W:(tl + 1) * W] = cpow[tl + 1].astype(bf16)

    def hi_lo(x):
        hi = x.astype(bf16)
        return hi, (x - hi.astype(f32)).astype(bf16)

    bh, bl = hi_lo(b_pow(0))
    ch, cl = hi_lo(jnp.concatenate(cpow[:L], axis=1))
    kd = (jnp.dot(bh, ch, preferred_element_type=f32) + jnp.dot(bh, cl, preferred_element_type=f32)
          + jnp.dot(bl, ch, preferred_element_type=f32)).astype(bf16)
    for tl in range(L):
        if tl:
            kt_ref[tl * W:(tl + 1) * W, 0:tl * W] = jnp.zeros((W, tl * W), bf16)
        kt_ref[tl * W:(tl + 1) * W, tl * W:] = kd[:, :(L - tl) * W]

    step = prow[L]
    seg = step
    for _ in range(int(math.log2(S5_SEG))):
        seg = cmul(*seg, *seg)
    apow_ref[...] = jnp.concatenate(
        [jnp.concatenate(step, axis=1), jnp.concatenate(seg, axis=1), jnp.zeros((6, 2 * M), f32)], axis=0)


def _s5_prep_inputs(a_re, a_im, log_dt, b_re, b_im, c_re, c_im):
    G, N = SSM_GROUPS, SSM_STATE
    W, M = SSM_WIDTH, SSM_MODES
    depth = a_re.shape[0]
    eye = jnp.eye(G, dtype=f32)
    bx = lambda b: (jnp.swapaxes(b, 2, 3)[:, :, :, None, :] * eye[None, :, None, :, None]).reshape(depth, W, M)
    cx = lambda c: (jnp.swapaxes(c, 2, 3)[:, :, :, None, :] * eye[None, :, None, :, None]).reshape(depth, M, W)
    ldt = jnp.repeat(log_dt, N, axis=1)
    return [a_re.reshape(depth, 1, M), a_im.reshape(depth, 1, M), ldt.reshape(depth, 1, M),
            a_re.reshape(depth, M, 1), a_im.reshape(depth, M, 1), ldt.reshape(depth, M, 1),
            bx(b_re), bx(b_im), cx(c_re), cx(c_im)]


def _s5_prep(ins, layer):
    L, W, M = S5_CHUNK, SSM_WIDTH, SSM_MODES
    shapes = [((L * W, 2 * M), bf16), ((2 * M, L * W), bf16), ((L * W, L * W), bf16), ((8, 2 * M), f32)]
    return pl.pallas_call(
        _s5_prep_kernel,
        grid=(1,),
        in_specs=[_wspec(a, layer) for a in ins],
        out_specs=[pl.BlockSpec(sh, lambda i: (0, 0)) for sh, _ in shapes],
        out_shape=[jax.ShapeDtypeStruct(sh, dt) for sh, dt in shapes],
        compiler_params=_cparams(("arbitrary",)),
        name="s5_prep",
    )(*ins)


def _s5_kernel(u_ref, gs_ref, h0_ref, bst_ref, cst_ref, kt_ref, apow_ref, d_ref, wg_ref, bg_ref,
               o_ref, hT_ref, ds_ref, sp_ref, carry_ref, *, chained, seglen):
    L, W, M = S5_CHUNK, SSM_WIDTH, SSM_MODES
    nr = 8 * seglen
    nlb = M // LANES

    def chunked(ref, tl):
        return jnp.concatenate([ref[0, pl.ds(tl, nr, stride=L), :], ref[1, pl.ds(tl, nr, stride=L), :]], axis=1)

    uf = [chunked(u_ref, tl) for tl in range(L)]
    ub = jnp.concatenate(uf, axis=1).astype(bf16)
    ds = jnp.dot(ub, bst_ref[...], preferred_element_type=f32)
    for lb in range(2 * nlb):
        ds_ref[lb] = ds[:, lb * LANES:(lb + 1) * LANES]
    def blocks(row):
        return ([row[:, lb * LANES:(lb + 1) * LANES] for lb in range(nlb)],
                [row[:, M + lb * LANES:M + (lb + 1) * LANES] for lb in range(nlb)])

    def bcast8(parts):
        return [jnp.broadcast_to(p, (8, LANES)) for p in parts]

    ar, ai = (bcast8(p) for p in blocks(apow_ref[0:1, :]))

    def scan(vr, vi, store):
        for i in range(seglen):
            rows = pl.ds(i, 8, stride=seglen)
            for lb in range(nlb):
                if store:
                    sp_ref[lb, rows, :] = vr[lb]
                    sp_ref[nlb + lb, rows, :] = vi[lb]
                dr, di = ds_ref[lb, rows, :], ds_ref[nlb + lb, rows, :]
                vr[lb], vi[lb] = (ar[lb] * vr[lb] - ai[lb] * vi[lb] + dr,
                                  ar[lb] * vi[lb] + ai[lb] * vr[lb] + di)
        return vr, vi

    if chained:
        ti = pl.program_id(1)

        @pl.when(ti == 0)
        def _():
            carry_ref[0:1, :] = h0_ref[0]

        zero = [jnp.zeros((8, LANES), f32) for _ in range(nlb)]
        er, ei = scan(list(zero), list(zero), store=False)
        pr, pi = blocks(apow_ref[1:2, :])
        cr, ci = blocks(carry_ref[0:1, :])
        sr, si = [[] for _ in range(nlb)], [[] for _ in range(nlb)]
        for j in range(8):
            for lb in range(nlb):
                sr[lb].append(cr[lb])
                si[lb].append(ci[lb])
                cr[lb], ci[lb] = (pr[lb] * cr[lb] - pi[lb] * ci[lb] + er[lb][j:j + 1, :],
                                  pr[lb] * ci[lb] + pi[lb] * cr[lb] + ei[lb][j:j + 1, :])
        final = jnp.concatenate(cr + ci, axis=1)
        carry_ref[0:1, :] = final
        hT_ref[0] = final
        scan([jnp.concatenate(x, axis=0) for x in sr], [jnp.concatenate(x, axis=0) for x in si], store=True)
    else:
        vr, vi = blocks(h0_ref[...])
        vr, vi = scan(vr, vi, store=True)
        hT_ref[...] = jnp.concatenate(vr + vi, axis=1)

    spb = jnp.concatenate([sp_ref[lb] for lb in range(2 * nlb)], axis=1).astype(bf16)

    def y_pair(pair):
        kk = (2 * pair + 2) * W
        cols = slice(2 * pair * W, kk)
        return (jnp.dot(ub[:, :kk], kt_ref[0:kk, cols], preferred_element_type=f32)
                + jnp.dot(spb, cst_ref[:, cols], preferred_element_type=f32))

    def finish(pair, ypair):
        for tl in (2 * pair, 2 * pair + 1):
            y = ypair[:, (tl % 2) * W:(tl % 2 + 1) * W] + d_ref[...] * uf[tl]
            g = _gelu_tanh(y)
            gate = jax.nn.sigmoid(jnp.dot(g.astype(bf16), wg_ref[...], preferred_element_type=f32) + bg_ref[...])
            out = g * gate * chunked(gs_ref, tl)
            o_ref[0, pl.ds(tl, nr, stride=L), :] = out[:, :LANES]
            o_ref[1, pl.ds(tl, nr, stride=L), :] = out[:, LANES:]

    ycur = y_pair(0)
    for pair in range(L // 2):
        ynext = y_pair(pair + 1) if pair + 1 < L // 2 else None
        finish(pair, ycur)
        ycur = ynext


def _s5(u, gs, h0, ops, d_row, w_glu, b_glu, *, batch, chained, layer):
    L, W, M = S5_CHUNK, SSM_WIDTH, SSM_MODES
    n = u.shape[1]
    t = n // batch
    assert t % L == 0
    chunks = t // L
    bst, cst, kt, apow = ops
    if chained:
        seglen = S5_SEG
        assert chunks % (8 * seglen) == 0
        grid = (batch, chunks // (8 * seglen))
        state = pl.BlockSpec((1, 1, 2 * M), lambda bi, ti: (bi, 0, 0))
    else:
        seglen = chunks
        assert batch == 8
        grid = (1, 1)
        state = pl.BlockSpec((8, 2 * M), lambda bi, ti: (0, 0))
    tiles = grid[1]
    nr = 8 * seglen
    blk = pl.BlockSpec((2, nr * L, LANES), lambda bi, ti: (0, bi * tiles + ti, 0))
    full = lambda a: _wspec(a, layer)
    once = lambda a: pl.BlockSpec(a.shape, lambda bi, ti: (0,) * a.ndim, pipeline_mode=pl.Buffered(1))
    return pl.pallas_call(
        functools.partial(_s5_kernel, chained=chained, seglen=seglen),
        grid=grid,
        in_specs=[blk, blk, state, once(bst), once(cst), once(kt), full(apow), full(d_row),
                  full(w_glu), full(b_glu)],
        out_specs=[blk, state],
        out_shape=[jax.ShapeDtypeStruct((2, n, LANES), f32), jax.ShapeDtypeStruct(h0.shape, f32)],
        scratch_shapes=[pltpu.VMEM((2 * M // LANES, nr, LANES), f32), pltpu.VMEM((2 * M // LANES, nr, LANES), f32),
                        pltpu.VMEM((8, 2 * M), f32)],
        compiler_params=_cparams(("arbitrary", "arbitrary")),
        name="s5_prompt" if chained else "s5_sample",
    )(u, gs, h0, bst, cst, kt, apow, d_row, w_glu, b_glu)


def _memory_kv_kernel(mem_ref, g_ref, w_ref, kg_ref, hm_ref, mkt_ref, mvt_ref):
    batch, _, nmem = mkt_ref.shape
    x = mem_ref[...]
    ms = jnp.mean(x * x, axis=-1, keepdims=True)
    h = (x * lax.rsqrt(ms + EPS) * g_ref[...]).astype(bf16)
    mk = jnp.dot(h, w_ref[:, 0:MEM_WIDTH], preferred_element_type=f32)
    mk = _head_norm(mk, hm_ref.at[0:MEM_WIDTH, 0:MEM_WIDTH], kg_ref)
    mv = jnp.dot(h, w_ref[:, MEM_WIDTH:2 * MEM_WIDTH], preferred_element_type=f32)
    for b in range(batch):
        mkt_ref[b] = mk[b * nmem:(b + 1) * nmem, :].T
        mvt_ref[b] = mv[b * nmem:(b + 1) * nmem, :].T


def _memory_kv(mem, lw, *, batch, layer):
    n = mem.shape[0]
    ins = [mem, lw["mem_norm"], lw["w_mem_kv"], lw["mem_k_norm"], lw["head_mean"]]
    out = pl.BlockSpec((batch, MEM_WIDTH, n // batch), lambda i: (0, 0, 0))
    return pl.pallas_call(
        _memory_kv_kernel,
        grid=(1,),
        in_specs=[_wspec(a, layer) for a in ins],
        out_specs=[out, out],
        out_shape=[jax.ShapeDtypeStruct((batch, MEM_WIDTH, n // batch), f32)] * 2,
        compiler_params=_cparams(("arbitrary",)),
        name="memory_kv",
    )(*ins)


def _mix_out_kernel(x_ref, fox_ref, ssm_ref, mq_ref, gm_ref, mkt_ref, mvt_ref, w_ref, o_ref):
    tm = x_ref.shape[0]
    nmem = mkt_ref.shape[2]
    mq = mq_ref[...]
    mkt = mkt_ref[0].astype(bf16)
    mvt = mvt_ref[0].astype(bf16)
    khead = _row_iota((MEM_WIDTH, nmem)) // HEAD_DIM
    zero = jnp.zeros_like(mkt)
    mk_heads = jnp.concatenate([jnp.where(khead == hd, mkt, zero) for hd in range(N_MEM_HEADS)], axis=1)
    s_all = jnp.dot(mq, mk_heads, preferred_element_type=f32)
    ssm = jnp.concatenate([ssm_ref[0], ssm_ref[1]], axis=1).astype(bf16)
    y = x_ref[...]
    y = y + jnp.dot(fox_ref[...], w_ref[0:FOX_WIDTH, :], preferred_element_type=f32)
    y = y + jnp.dot(ssm, w_ref[FOX_WIDTH:FOX_WIDTH + SSM_WIDTH, :], preferred_element_type=f32)
    mem = jnp.zeros((tm, MEM_WIDTH), f32)
    for hd in range(N_MEM_HEADS):
        s = s_all[:, hd * nmem:(hd + 1) * nmem]
        p = jnp.exp(s - jnp.max(s, axis=-1, keepdims=True))
        p = p / jnp.sum(p, axis=-1, keepdims=True)
        vh = jnp.where(khead == hd, mvt, zero)
        mem = mem + lax.dot_general(p.astype(bf16), vh, (((1,), (1,)), ((), ())), preferred_element_type=f32)
    memg = (mem * gm_ref[...]).astype(bf16)
    o_ref[...] = y + jnp.dot(memg, w_ref[FOX_WIDTH + SSM_WIDTH:, :], preferred_element_type=f32)


def _mix_out(x, fox, ssm, mq, gm, mkt, mvt, w_out, *, batch, tm, layer):
    n, d = x.shape
    t = n // batch
    assert t % tm == 0
    tiles_per_batch = t // tm
    row = lambda w: pl.BlockSpec((tm, w), lambda i: (i, 0))
    if mkt.ndim == 4:
        memb = pl.BlockSpec((None, 1) + mkt.shape[2:], lambda i: (layer, i // tiles_per_batch, 0, 0))
    else:
        memb = pl.BlockSpec((1,) + mkt.shape[1:], lambda i: (i // tiles_per_batch, 0, 0))
    return pl.pallas_call(
        _mix_out_kernel,
        grid=(n // tm,),
        in_specs=[row(d), row(FOX_WIDTH), pl.BlockSpec((2, tm, LANES), lambda i: (0, i, 0)),
                  row(MEM_WIDTH), row(MEM_WIDTH), memb, memb, _wspec(w_out, layer)],
        out_specs=row(d),
        out_shape=jax.ShapeDtypeStruct((n, d), f32),
        compiler_params=_cparams(("arbitrary",)),
        name="mix_out_prompt" if tiles_per_batch > 1 else "mix_out_sample",
    )(x, fox, ssm, mq, gm, mkt, mvt, w_out)


def _repack_w_in_kernel(w_ref, o_ref):
    nf = 4 * FOX_WIDTH
    rows = w_ref.shape[0]
    o_ref[:, _C_Q:_C_F] = w_ref[:, 0:FOX_WIDTH].astype(bf16)
    o_ref[:, _C_F:_C_F + N_FOX_HEADS] = w_ref[:, nf:nf + N_FOX_HEADS].astype(bf16)
    o_ref[:, _C_F + N_FOX_HEADS:_C_K] = jnp.zeros((rows, LANES - N_FOX_HEADS), bf16)
    o_ref[:, _C_K:_C_SU] = w_ref[:, FOX_WIDTH:nf].astype(bf16)
    o_ref[:, _C_SU:_W_COLS] = w_ref[:, nf + N_FOX_HEADS:].astype(bf16)


def _repack_w_in(w_in):
    depth, d, cols = w_in.shape
    assert cols == 4 * FOX_WIDTH + N_FOX_HEADS + 2 * SSM_WIDTH + 2 * MEM_WIDTH
    rows = 256
    return pl.pallas_call(
        _repack_w_in_kernel,
        grid=(depth, d // rows),
        in_specs=[pl.BlockSpec((None, rows, cols), lambda l, i: (l, i, 0))],
        out_specs=pl.BlockSpec((None, rows, _W_COLS), lambda l, i: (l, i, 0)),
        out_shape=jax.ShapeDtypeStruct((depth, d, _W_COLS), bf16),
        compiler_params=_cparams(("arbitrary", "arbitrary")),
        name="repack_w_in",
    )(w_in)


def _stacked_weights(norm_g, w_in, b_forget, fox_q_norm, fox_k_norm, mem_q_norm, mem_norm, w_mem_kv,
                     mem_k_norm, w_out, w_glu, b_glu, ssm_d):
    depth, d, _ = w_in.shape
    w_packed = _repack_w_in(w_in)
    row = lambda a: a.reshape(depth, 1, -1)
    per_head = lambda g, heads: jnp.tile(g, (1, heads)).reshape(depth, 1, heads * HEAD_DIM)
    return dict(
        norm_g=row(norm_g), w_in=w_packed,
        b_forget=row(jnp.pad(b_forget, ((0, 0), (0, LANES - N_FOX_HEADS)))),
        fox_q_norm=per_head(fox_q_norm, N_FOX_HEADS), fox_k_norm=per_head(fox_k_norm, N_FOX_HEADS),
        mem_q_norm=per_head(mem_q_norm, N_MEM_HEADS), mem_k_norm=per_head(mem_k_norm, N_MEM_HEADS),
        mem_norm=row(mem_norm), w_mem_kv=w_mem_kv.astype(bf16),
        head_mean=_head_mean_matrix(), bias_sel=_bias_selector(),
        w_out=w_out.astype(bf16), w_glu=w_glu.astype(bf16), b_glu=row(b_glu), ssm_d=row(ssm_d))


def _split_state(hT, batch):
    hT = hT.reshape(batch, 2, SSM_GROUPS, SSM_STATE)
    return hT[:, 0], hT[:, 1]


def kernel(x_prompt, x_sample, mem_prompt, cache_fox_k, cache_fox_v, cache_fox_logf, state_ssm_re, state_ssm_im, cache_mem_k, cache_mem_v, norm_g, w_in, b_forget, fox_q_norm, fox_k_norm, ssm_a_re, ssm_a_im, ssm_log_dt, ssm_b_re, ssm_b_im, ssm_c_re, ssm_c_im, ssm_d, w_glu, b_glu, mem_norm, w_mem_kv, mem_q_norm, mem_k_norm, w_out):
    B, T, D = x_prompt.shape
    Bs, Ts, _ = x_sample.shape
    depth = w_in.shape[0]
    past = cache_fox_k.shape[2]
    nmem = mem_prompt.shape[1]
    M = SSM_MODES

    xp = x_prompt.reshape(B * T, D)
    xs = x_sample.reshape(Bs * Ts, D)
    mem = mem_prompt.reshape(B * nmem, D)
    cache_kt = jnp.transpose(cache_fox_k, (0, 1, 3, 4, 2)).reshape(depth, Bs, FOX_WIDTH, past)
    cache_vt = jnp.transpose(cache_fox_v, (0, 1, 3, 4, 2)).reshape(depth, Bs, FOX_WIDTH, past)
    cache_lt = jnp.transpose(cache_fox_logf, (0, 1, 3, 2)).astype(f32)
    cache_mkt = jnp.transpose(cache_mem_k, (0, 1, 3, 4, 2)).reshape(depth, Bs, MEM_WIDTH, nmem)
    cache_mvt = jnp.transpose(cache_mem_v, (0, 1, 3, 4, 2)).reshape(depth, Bs, MEM_WIDTH, nmem)

    outs = {k: [] for k in ("pf", "pre", "pim", "pmk", "pmv", "sk", "sv", "sf", "sre", "sim")}
    kv_leaves = None
    lw = _stacked_weights(norm_g, w_in, b_forget, fox_q_norm, fox_k_norm, mem_q_norm, mem_norm,
                          w_mem_kv, mem_k_norm, w_out, w_glu, b_glu, ssm_d)
    s5_params = _s5_prep_inputs(ssm_a_re, ssm_a_im, ssm_log_dt, ssm_b_re, ssm_b_im, ssm_c_re, ssm_c_im)
    for l in range(depth):
        ops = _s5_prep(s5_params, l)
        mkt, mvt = _memory_kv(mem, lw, batch=B, layer=l)

        pr = _in_proj(xp, lw, batch=B, prompt=True, layer=l, depth=depth, kv_leaves=kv_leaves)
        kv_leaves = (pr["kt"], pr["vt"])
        fox = _fox_prompt(pr["qa"], pr["ka"], pr["va"], pr["gf"].reshape(B, T, FOX_WIDTH)).reshape(B * T, FOX_WIDTH)
        ssm, hT = _s5(pr["u"], pr["gs"], jnp.zeros((B, 1, 2 * M), f32), ops, lw["ssm_d"], lw["w_glu"],
                      lw["b_glu"], batch=B, chained=True, layer=l)
        xp = _mix_out(xp, fox, ssm, pr["mq"], pr["gm"], mkt, mvt,
                      lw["w_out"], batch=B, tm=MIX_TILE, layer=l)
        hre, him = _split_state(hT, B)
        outs["pf"].append(pr["logft"])
        outs["pre"].append(hre)
        outs["pim"].append(him)
        outs["pmk"].append(mkt.reshape(B, N_MEM_HEADS, HEAD_DIM, nmem))
        outs["pmv"].append(mvt.reshape(B, N_MEM_HEADS, HEAD_DIM, nmem))

        sr = _in_proj(xs, lw, batch=Bs, prompt=False, layer=l)
        fox_s = _fox_sample(sr["q"], sr["k"], sr["v"], sr["logf"], sr["gf"], cache_kt, cache_vt, cache_lt, l,
                            batch=Bs)
        h0 = jnp.concatenate([state_ssm_re[l].reshape(Bs, M), state_ssm_im[l].reshape(Bs, M)],
                             axis=-1).astype(f32)
        ssm_s, hT_s = _s5(sr["u"], sr["gs"], h0, ops, lw["ssm_d"], lw["w_glu"], lw["b_glu"], batch=Bs,
                          chained=False, layer=l)
        xs = _mix_out(xs, fox_s, ssm_s, sr["mq"], sr["gm"], cache_mkt, cache_mvt,
                      lw["w_out"], batch=Bs, tm=Ts, layer=l)
        sre, sim = _split_state(hT_s, Bs)
        outs["sk"].append(sr["k"].reshape(Bs, Ts, N_FOX_HEADS, HEAD_DIM))
        outs["sv"].append(sr["v"].reshape(Bs, Ts, N_FOX_HEADS, HEAD_DIM))
        outs["sf"].append(sr["logf"][:, :N_FOX_HEADS].reshape(Bs, Ts, N_FOX_HEADS))
        outs["sre"].append(sre)
        outs["sim"].append(sim)

    st = {k: jnp.stack(v) for k, v in outs.items()}
    fox_k_prompt = jnp.transpose(kv_leaves[0], (0, 1, 4, 2, 3))
    fox_v_prompt = jnp.transpose(kv_leaves[1], (0, 1, 4, 2, 3))
    fox_logf_prompt = jnp.transpose(st["pf"], (0, 1, 3, 2))
    mem_k_prompt = jnp.transpose(st["pmk"], (0, 1, 4, 2, 3))
    mem_v_prompt = jnp.transpose(st["pmv"], (0, 1, 4, 2, 3))
    return (xp.reshape(B, T, D), xs.reshape(Bs, Ts, D), fox_k_prompt, fox_v_prompt, fox_logf_prompt,
            st["pre"], st["pim"], mem_k_prompt, mem_v_prompt, st["sk"], st["sv"], st["sf"], st["sre"], st["sim"])
```

```python
import functools
import math

import numpy as np
import jax
import jax.numpy as jnp
from jax import lax
from jax.experimental import pallas as pl
from jax.experimental.pallas import tpu as pltpu

f32 = jnp.float32
bf16 = jnp.bfloat16

HEAD_DIM = 64
N_FOX_HEADS = 8
FOX_WIDTH = N_FOX_HEADS * HEAD_DIM
SSM_GROUPS = 16
SSM_CH = 16
SSM_STATE = 64
SSM_WIDTH = SSM_GROUPS * SSM_CH
SSM_MODES = SSM_GROUPS * SSM_STATE
N_MEM_HEADS = 4
MEM_WIDTH = N_MEM_HEADS * HEAD_DIM
EPS = 1e-6
QK_SCALE = HEAD_DIM ** -0.5
LOG2E = math.log2(math.e)

LANES = 128
S5_CHUNK = 8
S5_SEG = 16
ROW_TILE = 512
MIX_TILE = 1024
ATTN_TQ = 512
ATTN_UNROLL = 14
VMEM_LIMIT = 56 * 1024 * 1024

_C_Q, _C_F, _C_K, _C_V, _C_G = 0, 512, 640, 1152, 1664
_C_SU, _C_SG, _C_MQ, _C_MG = 2176, 2432, 2688, 2944
_W_COLS = 3200
_N_PIECES = 3


def _cparams(sem):
    return pltpu.CompilerParams(dimension_semantics=sem, vmem_limit_bytes=VMEM_LIMIT)


def _wspec(a, layer):
    if a.ndim == 3:
        return pl.BlockSpec((None,) + a.shape[1:], lambda *_: (layer, 0, 0))
    return pl.BlockSpec(a.shape, lambda *_: (0,) * a.ndim)


def _silu(x):
    return x * jax.nn.sigmoid(x)


def _log_sigmoid(x):
    return -(jnp.maximum(-x, 0.0) + jnp.log1p(jnp.exp(-jnp.abs(x))))


def _gelu_tanh(x):
    return 0.5 * x * (1.0 + jnp.tanh(math.sqrt(2.0 / math.pi) * (x + 0.044715 * (x * x * x))))


def _split3(x):
    hi = x.astype(bf16).astype(f32)
    r = x - hi
    mid = r.astype(bf16).astype(f32)
    lo = (r - mid).astype(bf16).astype(f32)
    return hi, mid, lo


def _pack3(x):
    hi, mid, lo = _split3(x)
    return (hi + pltpu.roll(mid, 8, 1) + pltpu.roll(lo, 16, 1)).astype(bf16)


def _unpack3(y):
    return y + pltpu.roll(y, LANES - 8, 1) + pltpu.roll(y, LANES - 16, 1)


def _lane_iota(shape):
    return lax.broadcasted_iota(jnp.int32, shape, len(shape) - 1)


def _row_iota(shape):
    return lax.broadcasted_iota(jnp.int32, shape, len(shape) - 2)


def _head_norm(z, hm_ref, g_ref):
    ms = jnp.dot((z * z).astype(bf16), hm_ref[...], preferred_element_type=f32)
    return z * lax.rsqrt(ms + EPS) * g_ref[...]


def _in_proj_kernel(*refs, prompt, tiles_per_batch, aliased):
    if prompt:
        if aliased:
            refs = refs[:10] + refs[12:]
        (x_ref, ng_ref, w_ref, bf_ref, qg_ref, kg_ref, mqg_ref, hm_ref, tril_ref, eb_ref,
         kt_ref, vt_ref, logft_ref, gf_ref, u_ref, gs_ref, mq_ref, gm_ref, qa_ref, ka_ref, va_ref,
         carry_ref) = refs
    else:
        (x_ref, ng_ref, w_ref, bf_ref, qg_ref, kg_ref, mqg_ref, hm_ref,
         k_ref, v_ref, logf_ref, gf_ref, u_ref, gs_ref, mq_ref, gm_ref, q_ref) = refs

    x = x_ref[...]
    tm = x.shape[0]
    lane = _lane_iota((tm, LANES))
    ms = jnp.mean(x * x, axis=-1, keepdims=True)
    h = (x * lax.rsqrt(ms + EPS) * ng_ref[...]).astype(bf16)

    def seg(lo, width):
        return jnp.dot(h, w_ref[:, lo:lo + width], preferred_element_type=f32)

    def halves(ref, z):
        ref[0] = z[:, :LANES]
        ref[1] = z[:, LANES:]

    zqf = seg(_C_Q, FOX_WIDTH + LANES)
    logf = _log_sigmoid(zqf[:, FOX_WIDTH:] + bf_ref[...])
    lf = jnp.where(lane < N_FOX_HEADS, logf, 0.0)
    zk = seg(_C_K, FOX_WIDTH)
    if prompt:
        i = pl.program_id(0)

        @pl.when(i % tiles_per_batch == 0)
        def _():
            carry_ref[...] = jnp.zeros_like(carry_ref)

        csum = jnp.dot(tril_ref[...], _pack3(lf), preferred_element_type=f32)
    v = seg(_C_V, FOX_WIDTH)
    if prompt:
        c = jnp.where(lane < N_FOX_HEADS, _unpack3(csum) + carry_ref[0:1, :], 0.0)
        carry_ref[0:1, :] = c[tm - 1:tm, :]
        pieces = (_pack3(c * LOG2E).astype(f32) + jnp.where(lane == 3 * N_FOX_HEADS, 1.0, 0.0)).astype(bf16)
    if prompt:
        aug = jnp.dot(pieces, eb_ref[...], preferred_element_type=f32)
    q = _head_norm(zqf[:, :FOX_WIDTH], hm_ref, qg_ref) * QK_SCALE
    k = _head_norm(zk, hm_ref, kg_ref)

    def gate_proj():
        gf_ref[...] = _silu(seg(_C_G, FOX_WIDTH))

    def ssm_proj():
        zs = seg(_C_SU, 2 * SSM_WIDTH)
        halves(u_ref, zs[:, :SSM_WIDTH])
        halves(gs_ref, _silu(zs[:, SSM_WIDTH:]))

    def mem_proj():
        zm = seg(_C_MQ, 2 * MEM_WIDTH)
        mq = _head_norm(zm[:, :MEM_WIDTH], hm_ref.at[0:MEM_WIDTH, 0:MEM_WIDTH], mqg_ref) * QK_SCALE
        mq_ref[...] = mq.astype(bf16)
        gm_ref[...] = _silu(zm[:, MEM_WIDTH:])

    late = [gate_proj, ssm_proj, mem_proj]
    if not prompt:
        for proj in late:
            proj()
        k_ref[...] = k
        v_ref[...] = v
        logf_ref[...] = lf
        q_ref[...] = q.astype(bf16)
        return

    kt = k.T
    vt = v.T
    for hd in range(N_FOX_HEADS):
        kt_ref[0, 0, hd] = kt[hd * HEAD_DIM:(hd + 1) * HEAD_DIM, :]
        vt_ref[0, 0, hd] = vt[hd * HEAD_DIM:(hd + 1) * HEAD_DIM, :]
    for later in range(1, kt_ref.shape[0]):
        kt_ref[later] = jnp.zeros(kt_ref.shape[1:], f32)
        vt_ref[later] = jnp.zeros(vt_ref.shape[1:], f32)
    logft_ref[0] = lf.T[0:N_FOX_HEADS, :]

    q = q * LOG2E
    qaug, kaug = aug[:, :LANES], aug[:, LANES:]
    for hd in range(N_FOX_HEADS):
        if hd % 3 == 0 and late:
            late.pop(0)()
        pair = (hd // 2) * LANES
        own = (lane < HEAD_DIM) if hd % 2 == 0 else (lane >= HEAD_DIM)
        ones_lane = HEAD_DIM if hd % 2 == 0 else 0
        g0 = _bias_group(hd)
        bias = (lane >= g0) & (lane < g0 + 2 * _N_PIECES)
        qa_ref[0, hd] = jnp.where(own, q[:, pair:pair + LANES], jnp.where(bias, qaug, 0.0)).astype(bf16)
        ka_ref[0, hd] = jnp.where(own, k[:, pair:pair + LANES], kaug).astype(bf16)
        va_ref[0, hd] = jnp.where(own, v[:, pair:pair + LANES],
                                  jnp.where(lane == ones_lane, 1.0, 0.0)).astype(bf16)


def _bias_group(hd):
    return (HEAD_DIM if hd % 2 == 0 else 0) + 16 * (hd // 2)


def _bias_selector():
    eb = np.zeros((LANES, 2 * LANES), np.float32)
    one = 3 * N_FOX_HEADS
    for hd in range(N_FOX_HEADS):
        g0 = _bias_group(hd)
        for p in range(_N_PIECES):
            eb[p * N_FOX_HEADS + hd, g0 + p] = 1.0
            eb[one, g0 + _N_PIECES + p] = 1.0
            eb[one, LANES + g0 + p] = 1.0
            eb[p * N_FOX_HEADS + hd, LANES + g0 + _N_PIECES + p] = -1.0
    return jnp.asarray(eb, bf16)


def _head_mean_matrix():
    idx = np.arange(FOX_WIDTH) // HEAD_DIM
    return jnp.asarray((idx[:, None] == idx[None, :]).astype(np.float32) / HEAD_DIM, bf16)


def _in_proj(x, lw, *, batch, prompt, layer=0, depth=1, kv_leaves=None):
    n, d = x.shape
    t = n // batch
    tm = ROW_TILE if prompt else n
    assert n % tm == 0 and (t % tm == 0 or not prompt)
    tiles_per_batch = t // tm if prompt else 1
    grid = (n // tm,)
    row = lambda w: pl.BlockSpec((tm, w), lambda i: (i, 0))
    full = lambda a: _wspec(a, layer)
    split = pl.BlockSpec((2, tm, LANES), lambda i: (0, i, 0))
    by_batch = lambda i: (i // tiles_per_batch, 0, i % tiles_per_batch, 0)

    ins = [x, lw["norm_g"], lw["w_in"], lw["b_forget"], lw["fox_q_norm"], lw["fox_k_norm"],
           lw["mem_q_norm"], lw["head_mean"]]
    in_specs = [row(d)] + [full(a) for a in ins[1:]]
    common = [((n, FOX_WIDTH), f32, row(FOX_WIDTH)), ((2, n, LANES), f32, split), ((2, n, LANES), f32, split),
              ((n, MEM_WIDTH), bf16, row(MEM_WIDTH)), ((n, MEM_WIDTH), f32, row(MEM_WIDTH))]
    scratch = []
    if prompt:
        tril = jnp.asarray(np.tril(np.ones((tm, tm), np.float32)), bf16)
        extra = [tril, lw["bias_sel"]]
        ins += extra
        in_specs += [full(a) for a in extra]
        aliases = {}
        if kv_leaves is None:
            tspec = pl.BlockSpec((depth, 1, N_FOX_HEADS, HEAD_DIM, tm),
                                 lambda i: (0, i // tiles_per_batch, 0, 0, i % tiles_per_batch))
        else:
            aliases = {len(ins): 0, len(ins) + 1: 1}
            ins += list(kv_leaves)
            in_specs += [pl.BlockSpec(memory_space=pl.ANY)] * 2
            tspec = pl.BlockSpec((1, 1, N_FOX_HEADS, HEAD_DIM, tm),
                                 lambda i: (layer, i // tiles_per_batch, 0, 0, i % tiles_per_batch))
        lspec = pl.BlockSpec((1, N_FOX_HEADS, tm), lambda i: (i // tiles_per_batch, 0, i % tiles_per_batch))
        hspec = pl.BlockSpec((1, N_FOX_HEADS, tm, LANES), by_batch)
        tshape = (depth, batch, N_FOX_HEADS, HEAD_DIM, t)
        outs = ([(tshape, f32, tspec), (tshape, f32, tspec), ((batch, N_FOX_HEADS, t), f32, lspec)] + common
                + [((batch, N_FOX_HEADS, t, LANES), bf16, hspec)] * 3)
        names = ["kt", "vt", "logft", "gf", "u", "gs", "mq", "gm", "qa", "ka", "va"]
        scratch = [pltpu.VMEM((8, LANES), f32)]
    else:
        outs = ([((n, FOX_WIDTH), f32, row(FOX_WIDTH)), ((n, FOX_WIDTH), f32, row(FOX_WIDTH)),
                 ((n, LANES), f32, row(LANES))] + common + [((n, FOX_WIDTH), bf16, row(FOX_WIDTH))])
        names = ["k", "v", "logf", "gf", "u", "gs", "mq", "gm", "q"]
        aliases = {}

    res = pl.pallas_call(
        functools.partial(_in_proj_kernel, prompt=prompt, tiles_per_batch=tiles_per_batch, aliased=bool(aliases)),
        grid=grid, in_specs=in_specs, out_specs=[o[2] for o in outs],
        out_shape=[jax.ShapeDtypeStruct(o[0], o[1]) for o in outs],
        scratch_shapes=scratch, input_output_aliases=aliases,
        compiler_params=_cparams(("arbitrary",)),
        name="in_proj_prompt" if prompt else "in_proj_sample",
    )(*ins)
    return dict(zip(names, res))


def _fox_prompt_kernel(qa_ref, ka_ref, va_ref, gate_ref, o_ref, s0_ref, s1_ref, m_ref, acc_ref, *, nq):
    tq = tk = ATTN_TQ
    s_refs = (s0_ref, s1_ref)
    m_ref[...] = jnp.full(m_ref.shape, -jnp.inf, f32)
    acc_ref[...] = jnp.zeros(acc_ref.shape, f32)

    def scores(qb, kb, slot):
        qoff = pl.multiple_of(qb * tq, tq)
        koff = pl.multiple_of(kb * tk, tk)
        for j in range(2):
            s_refs[slot][j] = lax.dot_general(qa_ref[0, j, pl.ds(qoff, tq), :], ka_ref[0, j, pl.ds(koff, tk), :],
                                              (((1,), (1,)), ((), ())), preferred_element_type=f32)

    def softmax_pv(qb, kb, slot, masked):
        koff = pl.multiple_of(kb * tk, tk)
        for j in range(2):
            s = s_refs[slot][j]
            if masked:
                s = jnp.where(_row_iota((tq, tk)) >= _lane_iota((tq, tk)), s, -jnp.inf)
            m = m_ref[qb, j]
            m_new = jnp.maximum(m, jnp.max(s, axis=-1, keepdims=True))
            p = jnp.exp2(s - jnp.concatenate([m_new] * (tk // LANES), axis=1))
            vv = va_ref[0, j, pl.ds(koff, tk), :]
            acc_ref[qb, j] = (jnp.exp2(m - m_new) * acc_ref[qb, j]
                              + jnp.dot(p.astype(bf16), vv, preferred_element_type=f32))
            m_ref[qb, j] = m_new

    def finalize(qb):
        h0 = acc_ref[qb, 0] / acc_ref[qb, 0, :, HEAD_DIM:HEAD_DIM + 1]
        h1 = acc_ref[qb, 1] / acc_ref[qb, 1, :, 0:1]
        rows = pl.ds(pl.multiple_of(qb * tq, tq), tq)
        out = jnp.where(_lane_iota((tq, LANES)) < HEAD_DIM, h0, h1) * gate_ref[0, rows, :]
        o_ref[0, rows, :] = out.astype(bf16)

    def after(qb, kb):
        row_end = kb + 1 >= qb
        last = jnp.logical_and(row_end, qb + 1 >= nq)
        nqb = jnp.where(last, 0, jnp.where(row_end, qb + 1, qb))
        nkb = jnp.where(row_end, 0, kb + 1)
        return nqb, nkb

    def lower_step(qb, kb, slot):
        nqb, nkb = after(qb, kb)
        scores(nqb, nkb, 1 - slot)
        softmax_pv(qb, kb, slot, False)
        return nqb, nkb

    def diag_step(qb, slot):
        nxt = jnp.minimum(qb + 1, nq - 1)
        scores(nxt, nxt, 1 - slot)
        softmax_pv(qb, qb, slot, True)
        finalize(qb)

    def steps_per_trip(n):
        return max(d for d in range(2, ATTN_UNROLL + 1, 2) if n % d == 0)

    n_lower = nq * (nq - 1) // 2
    if n_lower:
        scores(1, 0, 0)
        per = steps_per_trip(n_lower)

        def lower_body(i, carry):
            for k in range(per):
                carry = lower_step(*carry, k % 2)
            return carry

        lax.fori_loop(0, n_lower // per, lower_body, (jnp.int32(1), jnp.int32(0)))
    else:
        scores(0, 0, 0)

    per_diag = steps_per_trip(nq)

    def diag_body(i, carry):
        for k in range(per_diag):
            diag_step(per_diag * i + k, k % 2)
        return carry

    lax.fori_loop(0, nq // per_diag, diag_body, 0)


def _fox_prompt(qa, ka, va, gate):
    b, nh, t, _ = qa.shape
    nq = t // ATTN_TQ
    assert t % ATTN_TQ == 0 and nq % 2 == 0 and (nq * (nq - 1) // 2) % 2 == 0
    whole = pl.BlockSpec((1, 2, t, LANES), lambda bi, hp: (bi, hp, 0, 0))
    return pl.pallas_call(
        functools.partial(_fox_prompt_kernel, nq=nq),
        grid=(b, nh // 2),
        in_specs=[whole, whole, whole, pl.BlockSpec((1, t, LANES), lambda bi, hp: (bi, 0, hp))],
        out_specs=pl.BlockSpec((1, t, LANES), lambda bi, hp: (bi, 0, hp)),
        out_shape=jax.ShapeDtypeStruct((b, t, FOX_WIDTH), bf16),
        scratch_shapes=[pltpu.VMEM((2, ATTN_TQ, ATTN_TQ), f32), pltpu.VMEM((2, ATTN_TQ, ATTN_TQ), f32),
                        pltpu.VMEM((nq, 2, ATTN_TQ, LANES), f32), pltpu.VMEM((nq, 2, ATTN_TQ, LANES), f32)],
        compiler_params=_cparams(("arbitrary", "arbitrary")),
        name="fox_prompt",
    )(qa, ka, va, gate)


def _fox_sample_kernel(q_ref, kn_ref, vn_ref, ln_ref, gate_ref, kc_ref, vc_ref, lc_ref, triu_ref, o_ref,
                       kall_ref, vall_ref, *, past, tnew):
    nkeys = kall_ref.shape[1]
    nq = N_FOX_HEADS * tnew

    def new_cols(ref):
        z = jnp.concatenate([ref[...], jnp.zeros((LANES - tnew, ref.shape[1]), f32)], axis=0)
        return z.T

    kall_ref[:, 0:past] = kc_ref[0, 0].astype(bf16)
    vall_ref[:, 0:past] = vc_ref[0, 0].astype(bf16)
    kall_ref[:, past:nkeys] = new_cols(kn_ref).astype(bf16)
    vall_ref[:, past:nkeys] = new_cols(vn_ref).astype(bf16)

    l_all = jnp.concatenate([lc_ref[0, 0], new_cols(ln_ref)[0:N_FOX_HEADS, :]], axis=1)
    pieces = jnp.concatenate(_split3(l_all), axis=0).astype(bf16)
    c3 = jnp.dot(pieces, triu_ref[...], preferred_element_type=f32)
    c = c3[0:8] + c3[8:16] + c3[16:24]
    crow = jnp.concatenate([jnp.broadcast_to(c[hd:hd + 1, :], (tnew, nkeys)) for hd in range(N_FOX_HEADS)],
                           axis=0)
    rq = _row_iota((nq, LANES))
    cq = jnp.sum(jnp.where(_lane_iota((nq, LANES)) == rq % tnew, crow[:, past:nkeys], 0.0),
                 axis=1, keepdims=True)

    qrep = jnp.concatenate([q_ref[...]] * N_FOX_HEADS, axis=0)
    own = _row_iota((nq, FOX_WIDTH)) // tnew == _lane_iota((nq, FOX_WIDTH)) // HEAD_DIM
    qbd = jnp.where(own, qrep, jnp.zeros_like(qrep))
    s = jnp.dot(qbd, kall_ref[...], preferred_element_type=f32) + cq - crow
    s = jnp.where(_lane_iota((nq, nkeys)) <= past + _row_iota((nq, nkeys)) % tnew, s, -jnp.inf)
    p = jnp.exp(s - jnp.max(s, axis=1, keepdims=True))
    o2 = lax.dot_general(p.astype(bf16), vall_ref[...], (((1,), (1,)), ((), ())), preferred_element_type=f32)
    o2 = o2 / jnp.sum(p, axis=1, keepdims=True)
    olane = _lane_iota((tnew, FOX_WIDTH))
    out = jnp.zeros((tnew, FOX_WIDTH), f32)
    for hd in range(N_FOX_HEADS):
        out = out + jnp.where(olane // HEAD_DIM == hd, o2[hd * tnew:(hd + 1) * tnew, :], 0.0)
    o_ref[...] = (out * gate_ref[...]).astype(bf16)


def _fox_sample(q, k_new, v_new, logf_new, gate, cache_kt, cache_vt, cache_lt, layer, *, batch):
    n = q.shape[0]
    tnew = n // batch
    past = cache_kt.shape[3]
    nkeys = past + LANES
    assert N_FOX_HEADS * tnew == LANES and past % LANES == 0
    triu = jnp.asarray(np.triu(np.ones((nkeys, nkeys), np.float32)), bf16)
    rows = lambda w: pl.BlockSpec((tnew, w), lambda bi: (bi, 0))
    cache = lambda h: pl.BlockSpec((1, 1, h, past), lambda bi: (layer, bi, 0, 0))
    return pl.pallas_call(
        functools.partial(_fox_sample_kernel, past=past, tnew=tnew),
        grid=(batch,),
        in_specs=[rows(FOX_WIDTH), rows(FOX_WIDTH), rows(FOX_WIDTH), rows(LANES), rows(FOX_WIDTH),
                  cache(FOX_WIDTH), cache(FOX_WIDTH), cache(N_FOX_HEADS),
                  pl.BlockSpec((nkeys, nkeys), lambda bi: (0, 0))],
        out_specs=rows(FOX_WIDTH),
        out_shape=jax.ShapeDtypeStruct((n, FOX_WIDTH), bf16),
        scratch_shapes=[pltpu.VMEM((FOX_WIDTH, nkeys), bf16), pltpu.VMEM((FOX_WIDTH, nkeys), bf16)],
        compiler_params=_cparams(("arbitrary",)),
        name="fox_sample",
    )(q, k_new, v_new, logf_new, gate, cache_kt, cache_vt, cache_lt, triu)


def _s5_prep_kernel(are_r, aim_r, ldt_r, are_c, aim_c, ldt_c, bxr_ref, bxi_ref, cxr_ref, cxi_ref,
                    bst_ref, cst_ref, kt_ref, apow_ref):
    L, W, M = S5_CHUNK, SSM_WIDTH, SSM_MODES

    def cmul(xr, xi, yr, yi):
        return xr * yr - xi * yi, xr * yi + xi * yr

    def powers(are, aim, ldt, n):
        dt = jnp.exp(ldt[...])
        mag = jnp.exp(dt * are[...])
        p1 = (mag * jnp.cos(dt * aim[...]), mag * jnp.sin(dt * aim[...]))
        out = [(jnp.ones_like(mag), jnp.zeros_like(mag)), p1]
        for _ in range(n - 1):
            out.append(cmul(*out[-1], *p1))
        return out

    prow = powers(are_r, aim_r, ldt_r, L)
    pcol = powers(are_c, aim_c, ldt_c, L)
    ar, ai = are_r[...], aim_r[...]
    abr, abi = prow[1]
    den = ar * ar + ai * ai
    zr = ((abr - 1.0) * ar + abi * ai) / den
    zi = (abi * ar - (abr - 1.0) * ai) / den
    bbr, bbi = cmul(zr, zi, bxr_ref[...], bxi_ref[...])

    def b_pow(k):
        return jnp.concatenate(cmul(*prow[k], bbr, bbi), axis=1)

    def c_pow(k):
        pr, pi = pcol[k]
        cr, ci = cxr_ref[...], cxi_ref[...]
        return jnp.concatenate([cr * pr - ci * pi, -(cr * pi + ci * pr)], axis=0)

    cpow = [c_pow(k) for k in range(L + 1)]
    for tl in range(L):
        bst_ref[tl * W:(tl + 1) * W, :] = b_pow(L - 1 - tl).astype(bf16)
        cst_ref[:, tl * W:(tl + 1) * W] = cpow[tl + 1].astype(bf16)

    def hi_lo(x):
        hi = x.astype(bf16)
        return hi, (x - hi.astype(f32)).astype(bf16)

    bh, bl = hi_lo(b_pow(0))
    ch, cl = hi_lo(jnp.concatenate(cpow[:L], axis=1))
    kd = (jnp.dot(bh, ch, preferred_element_type=f32) + jnp.dot(bh, cl, preferred_element_type=f32)
          + jnp.dot(bl, ch, preferred_element_type=f32)).astype(bf16)
    for tl in range(L):
        if tl:
            kt_ref[tl * W:(tl + 1) * W, 0:tl * W] = jnp.zeros((W, tl * W), bf16)
        kt_ref[tl * W:(tl + 1) * W, tl * W:] = kd[:, :(L - tl) * W]

    step = prow[L]
    seg = step
    for _ in range(int(math.log2(S5_SEG))):
        seg = cmul(*seg, *seg)
    apow_ref[...] = jnp.concatenate(
        [jnp.concatenate(step, axis=1), jnp.concatenate(seg, axis=1), jnp.zeros((6, 2 * M), f32)], axis=0)


def _s5_prep_inputs(a_re, a_im, log_dt, b_re, b_im, c_re, c_im):
    G, N = SSM_GROUPS, SSM_STATE
    W, M = SSM_WIDTH, SSM_MODES
    depth = a_re.shape[0]
    eye = jnp.eye(G, dtype=f32)
    bx = lambda b: (jnp.swapaxes(b, 2, 3)[:, :, :, None, :] * eye[None, :, None, :, None]).reshape(depth, W, M)
    cx = lambda c: (jnp.swapaxes(c, 2, 3)[:, :, :, None, :] * eye[None, :, None, :, None]).reshape(depth, M, W)
    ldt = jnp.repeat(log_dt, N, axis=1)
    return [a_re.reshape(depth, 1, M), a_im.reshape(depth, 1, M), ldt.reshape(depth, 1, M),
            a_re.reshape(depth, M, 1), a_im.reshape(depth, M, 1), ldt.reshape(depth, M, 1),
            bx(b_re), bx(b_im), cx(c_re), cx(c_im)]


def _s5_prep(ins, layer):
    L, W, M = S5_CHUNK, SSM_WIDTH, SSM_MODES
    shapes = [((L * W, 2 * M), bf16), ((2 * M, L * W), bf16), ((L * W, L * W), bf16), ((8, 2 * M), f32)]
    return pl.pallas_call(
        _s5_prep_kernel,
        grid=(1,),
        in_specs=[_wspec(a, layer) for a in ins],
        out_specs=[pl.BlockSpec(sh, lambda i: (0, 0)) for sh, _ in shapes],
        out_shape=[jax.ShapeDtypeStruct(sh, dt) for sh, dt in shapes],
        compiler_params=_cparams(("arbitrary",)),
        name="s5_prep",
    )(*ins)


def _s5_kernel(u_ref, gs_ref, h0_ref, bst_ref, cst_ref, kt_ref, apow_ref, d_ref, wg_ref, bg_ref,
               o_ref, hT_ref, ds_ref, sp_ref, carry_ref, *, chained, seglen):
    L, W, M = S5_CHUNK, SSM_WIDTH, SSM_MODES
    nr = 8 * seglen
    nlb = M // LANES

    def chunked(ref, tl):
        return jnp.concatenate([ref[0, pl.ds(tl, nr, stride=L), :], ref[1, pl.ds(tl, nr, stride=L), :]], axis=1)

    uf = [chunked(u_ref, tl) for tl in range(L)]
    ub = jnp.concatenate(uf, axis=1).astype(bf16)
    ds = jnp.dot(ub, bst_ref[...], preferred_element_type=f32)
    for lb in range(2 * nlb):
        ds_ref[lb] = ds[:, lb * LANES:(lb + 1) * LANES]
    def blocks(row):
        return ([row[:, lb * LANES:(lb + 1) * LANES] for lb in range(nlb)],
                [row[:, M + lb * LANES:M + (lb + 1) * LANES] for lb in range(nlb)])

    def bcast8(parts):
        return [jnp.broadcast_to(p, (8, LANES)) for p in parts]

    ar, ai = (bcast8(p) for p in blocks(apow_ref[0:1, :]))

    def scan(vr, vi, store):
        for i in range(seglen):
            rows = pl.ds(i, 8, stride=seglen)
            for lb in range(nlb):
                if store:
                    sp_ref[lb, rows, :] = vr[lb]
                    sp_ref[nlb + lb, rows, :] = vi[lb]
                dr, di = ds_ref[lb, rows, :], ds_ref[nlb + lb, rows, :]
                vr[lb], vi[lb] = (ar[lb] * vr[lb] - ai[lb] * vi[lb] + dr,
                                  ar[lb] * vi[lb] + ai[lb] * vr[lb] + di)
        return vr, vi

    if chained:
        ti = pl.program_id(1)

        @pl.when(ti == 0)
        def _():
            carry_ref[0:1, :] = h0_ref[0]

        zero = [jnp.zeros((8, LANES), f32) for _ in range(nlb)]
        er, ei = scan(list(zero), list(zero), store=False)
        pr, pi = blocks(apow_ref[1:2, :])
        cr, ci = blocks(carry_ref[0:1, :])
        sr, si = [[] for _ in range(nlb)], [[] for _ in range(nlb)]
        for j in range(8):
            for lb in range(nlb):
                sr[lb].append(cr[lb])
                si[lb].append(ci[lb])
                cr[lb], ci[lb] = (pr[lb] * cr[lb] - pi[lb] * ci[lb] + er[lb][j:j + 1, :],
                                  pr[lb] * ci[lb] + pi[lb] * cr[lb] + ei[lb][j:j + 1, :])
        final = jnp.concatenate(cr + ci, axis=1)
        carry_ref[0:1, :] = final
        hT_ref[0] = final
        scan([jnp.concatenate(x, axis=0) for x in sr], [jnp.concatenate(x, axis=0) for x in si], store=True)
    else:
        vr, vi = blocks(h0_ref[...])
        vr, vi = scan(vr, vi, store=True)
        hT_ref[...] = jnp.concatenate(vr + vi, axis=1)

    spb = jnp.concatenate([sp_ref[lb] for lb in range(2 * nlb)], axis=1).astype(bf16)

    def y_pair(pair):
        kk = (2 * pair + 2) * W
        cols = slice(2 * pair * W, kk)
        return (jnp.dot(ub[:, :kk], kt_ref[0:kk, cols], preferred_element_type=f32)
                + jnp.dot(spb, cst_ref[:, cols], preferred_element_type=f32))

    def finish(pair, ypair):
        for tl in (2 * pair, 2 * pair + 1):
            y = ypair[:, (tl % 2) * W:(tl % 2 + 1) * W] + d_ref[...] * uf[tl]
            g = _gelu_tanh(y)
            gate = jax.nn.sigmoid(jnp.dot(g.astype(bf16), wg_ref[...], preferred_element_type=f32) + bg_ref[...])
            out = g * gate * chunked(gs_ref, tl)
            o_ref[0, pl.ds(tl, nr, stride=L), :] = out[:, :LANES]
            o_ref[1, pl.ds(tl, nr, stride=L), :] = out[:, LANES:]

    ycur = y_pair(0)
    for pair in range(L // 2):
        ynext = y_pair(pair + 1) if pair + 1 < L // 2 else None
        finish(pair, ycur)
        ycur = ynext


def _s5(u, gs, h0, ops, d_row, w_glu, b_glu, *, batch, chained, layer):
    L, W, M = S5_CHUNK, SSM_WIDTH, SSM_MODES
    n = u.shape[1]
    t = n // batch
    assert t % L == 0
    chunks = t // L
    bst, cst, kt, apow = ops
    if chained:
        seglen = S5_SEG
        assert chunks % (8 * seglen) == 0
        grid = (batch, chunks // (8 * seglen))
        state = pl.BlockSpec((1, 1, 2 * M), lambda bi, ti: (bi, 0, 0))
    else:
        seglen = chunks
        assert batch == 8
        grid = (1, 1)
        state = pl.BlockSpec((8, 2 * M), lambda bi, ti: (0, 0))
    tiles = grid[1]
    nr = 8 * seglen
    blk = pl.BlockSpec((2, nr * L, LANES), lambda bi, ti: (0, bi * tiles + ti, 0))
    full = lambda a: _wspec(a, layer)
    once = lambda a: pl.BlockSpec(a.shape, lambda bi, ti: (0,) * a.ndim, pipeline_mode=pl.Buffered(1))
    return pl.pallas_call(
        functools.partial(_s5_kernel, chained=chained, seglen=seglen),
        grid=grid,
        in_specs=[blk, blk, state, once(bst), once(cst), once(kt), full(apow), full(d_row),
                  full(w_glu), full(b_glu)],
        out_specs=[blk, state],
        out_shape=[jax.ShapeDtypeStruct((2, n, LANES), f32), jax.ShapeDtypeStruct(h0.shape, f32)],
        scratch_shapes=[pltpu.VMEM((2 * M // LANES, nr, LANES), f32), pltpu.VMEM((2 * M // LANES, nr, LANES), f32),
                        pltpu.VMEM((8, 2 * M), f32)],
        compiler_params=_cparams(("arbitrary", "arbitrary")),
        name="s5_prompt" if chained else "s5_sample",
    )(u, gs, h0, bst, cst, kt, apow, d_row, w_glu, b_glu)


def _memory_kv_kernel(mem_ref, g_ref, w_ref, kg_ref, hm_ref, mkt_ref, mvt_ref):
    batch, _, nmem = mkt_ref.shape
    x = mem_ref[...]
    ms = jnp.mean(x * x, axis=-1, keepdims=True)
    h = (x * lax.rsqrt(ms + EPS) * g_ref[...]).astype(bf16)
    mk = jnp.dot(h, w_ref[:, 0:MEM_WIDTH], preferred_element_type=f32)
    mk = _head_norm(mk, hm_ref.at[0:MEM_WIDTH, 0:MEM_WIDTH], kg_ref)
    mv = jnp.dot(h, w_ref[:, MEM_WIDTH:2 * MEM_WIDTH], preferred_element_type=f32)
    for b in range(batch):
        mkt_ref[b] = mk[b * nmem:(b + 1) * nmem, :].T
        mvt_ref[b] = mv[b * nmem:(b + 1) * nmem, :].T


def _memory_kv(mem, lw, *, batch, layer):
    n = mem.shape[0]
    ins = [mem, lw["mem_norm"], lw["w_mem_kv"], lw["mem_k_norm"], lw["head_mean"]]
    out = pl.BlockSpec((batch, MEM_WIDTH, n // batch), lambda i: (0, 0, 0))
    return pl.pallas_call(
        _memory_kv_kernel,
        grid=(1,),
        in_specs=[_wspec(a, layer) for a in ins],
        out_specs=[out, out],
        out_shape=[jax.ShapeDtypeStruct((batch, MEM_WIDTH, n // batch), f32)] * 2,
        compiler_params=_cparams(("arbitrary",)),
        name="memory_kv",
    )(*ins)


def _mix_out_kernel(x_ref, fox_ref, ssm_ref, mq_ref, gm_ref, mkt_ref, mvt_ref, w_ref, o_ref):
    tm = x_ref.shape[0]
    nmem = mkt_ref.shape[2]
    mq = mq_ref[...]
    mkt = mkt_ref[0].astype(bf16)
    mvt = mvt_ref[0].astype(bf16)
    khead = _row_iota((MEM_WIDTH, nmem)) // HEAD_DIM
    zero = jnp.zeros_like(mkt)
    mk_heads = jnp.concatenate([jnp.where(khead == hd, mkt, zero) for hd in range(N_MEM_HEADS)], axis=1)
    s_all = jnp.dot(mq, mk_heads, preferred_element_type=f32)
    ssm = jnp.concatenate([ssm_ref[0], ssm_ref[1]], axis=1).astype(bf16)
    y = x_ref[...]
    y = y + jnp.dot(fox_ref[...], w_ref[0:FOX_WIDTH, :], preferred_element_type=f32)
    y = y + jnp.dot(ssm, w_ref[FOX_WIDTH:FOX_WIDTH + SSM_WIDTH, :], preferred_element_type=f32)
    mem = jnp.zeros((tm, MEM_WIDTH), f32)
    for hd in range(N_MEM_HEADS):
        s = s_all[:, hd * nmem:(hd + 1) * nmem]
        p = jnp.exp(s - jnp.max(s, axis=-1, keepdims=True))
        p = p / jnp.sum(p, axis=-1, keepdims=True)
        vh = jnp.where(khead == hd, mvt, zero)
        mem = mem + lax.dot_general(p.astype(bf16), vh, (((1,), (1,)), ((), ())), preferred_element_type=f32)
    memg = (mem * gm_ref[...]).astype(bf16)
    o_ref[...] = y + jnp.dot(memg, w_ref[FOX_WIDTH + SSM_WIDTH:, :], preferred_element_type=f32)


def _mix_out(x, fox, ssm, mq, gm, mkt, mvt, w_out, *, batch, tm, layer):
    n, d = x.shape
    t = n // batch
    assert t % tm == 0
    tiles_per_batch = t // tm
    row = lambda w: pl.BlockSpec((tm, w), lambda i: (i, 0))
    if mkt.ndim == 4:
        memb = pl.BlockSpec((None, 1) + mkt.shape[2:], lambda i: (layer, i // tiles_per_batch, 0, 0))
    else:
        memb = pl.BlockSpec((1,) + mkt.shape[1:], lambda i: (i // tiles_per_batch, 0, 0))
    return pl.pallas_call(
        _mix_out_kernel,
        grid=(n // tm,),
        in_specs=[row(d), row(FOX_WIDTH), pl.BlockSpec((2, tm, LANES), lambda i: (0, i, 0)),
                  row(MEM_WIDTH), row(MEM_WIDTH), memb, memb, _wspec(w_out, layer)],
        out_specs=row(d),
        out_shape=jax.ShapeDtypeStruct((n, d), f32),
        compiler_params=_cparams(("arbitrary",)),
        name="mix_out_prompt" if tiles_per_batch > 1 else "mix_out_sample",
    )(x, fox, ssm, mq, gm, mkt, mvt, w_out)


def _repack_w_in_kernel(wt_ref, o_ref):
    nf = 4 * FOX_WIDTH
    dblk = wt_ref.shape[1]

    def put(lo, hi, rows):
        o_ref[:, lo:hi] = rows.T.astype(bf16)

    put(_C_Q, _C_F, wt_ref[0:FOX_WIDTH, :])
    put(_C_F, _C_K, jnp.concatenate([wt_ref[nf:nf + N_FOX_HEADS, :],
                                     jnp.zeros((LANES - N_FOX_HEADS, dblk), f32)], axis=0))
    put(_C_K, _C_SU, wt_ref[FOX_WIDTH:nf, :])
    put(_C_SU, _W_COLS, wt_ref[nf + N_FOX_HEADS:, :])


def _repack_w_in(w_in):
    depth, d, cols = w_in.shape
    assert cols == 4 * FOX_WIDTH + N_FOX_HEADS + 2 * SSM_WIDTH + 2 * MEM_WIDTH
    dblk = 256
    return pl.pallas_call(
        _repack_w_in_kernel,
        grid=(depth, d // dblk),
        in_specs=[pl.BlockSpec((None, cols, dblk), lambda l, i: (l, 0, i))],
        out_specs=pl.BlockSpec((None, dblk, _W_COLS), lambda l, i: (l, i, 0)),
        out_shape=jax.ShapeDtypeStruct((depth, d, _W_COLS), bf16),
        compiler_params=_cparams(("arbitrary", "arbitrary")),
        name="repack_w_in",
    )(jnp.swapaxes(w_in, 1, 2))


def _stacked_weights(norm_g, w_in, b_forget, fox_q_norm, fox_k_norm, mem_q_norm, mem_norm, w_mem_kv,
                     mem_k_norm, w_out, w_glu, b_glu, ssm_d):
    depth, d, _ = w_in.shape
    w_packed = _repack_w_in(w_in)
    row = lambda a: a.reshape(depth, 1, -1)
    per_head = lambda g, heads: jnp.tile(g, (1, heads)).reshape(depth, 1, heads * HEAD_DIM)
    return dict(
        norm_g=row(norm_g), w_in=w_packed,
        b_forget=row(jnp.pad(b_forget, ((0, 0), (0, LANES - N_FOX_HEADS)))),
        fox_q_norm=per_head(fox_q_norm, N_FOX_HEADS), fox_k_norm=per_head(fox_k_norm, N_FOX_HEADS),
        mem_q_norm=per_head(mem_q_norm, N_MEM_HEADS), mem_k_norm=per_head(mem_k_norm, N_MEM_HEADS),
        mem_norm=row(mem_norm), w_mem_kv=w_mem_kv.astype(bf16),
        head_mean=_head_mean_matrix(), bias_sel=_bias_selector(),
        w_out=w_out.astype(bf16), w_glu=w_glu.astype(bf16), b_glu=row(b_glu), ssm_d=row(ssm_d))


def _split_state(hT, batch):
    hT = hT.reshape(batch, 2, SSM_GROUPS, SSM_STATE)
    return hT[:, 0], hT[:, 1]


def kernel(x_prompt, x_sample, mem_prompt, cache_fox_k, cache_fox_v, cache_fox_logf, state_ssm_re, state_ssm_im, cache_mem_k, cache_mem_v, norm_g, w_in, b_forget, fox_q_norm, fox_k_norm, ssm_a_re, ssm_a_im, ssm_log_dt, ssm_b_re, ssm_b_im, ssm_c_re, ssm_c_im, ssm_d, w_glu, b_glu, mem_norm, w_mem_kv, mem_q_norm, mem_k_norm, w_out):
    B, T, D = x_prompt.shape
    Bs, Ts, _ = x_sample.shape
    depth = w_in.shape[0]
    past = cache_fox_k.shape[2]
    nmem = mem_prompt.shape[1]
    M = SSM_MODES

    xp = x_prompt.reshape(B * T, D)
    xs = x_sample.reshape(Bs * Ts, D)
    mem = mem_prompt.reshape(B * nmem, D)
    cache_kt = jnp.transpose(cache_fox_k, (0, 1, 3, 4, 2)).reshape(depth, Bs, FOX_WIDTH, past)
    cache_vt = jnp.transpose(cache_fox_v, (0, 1, 3, 4, 2)).reshape(depth, Bs, FOX_WIDTH, past)
    cache_lt = jnp.transpose(cache_fox_logf, (0, 1, 3, 2)).astype(f32)
    cache_mkt = jnp.transpose(cache_mem_k, (0, 1, 3, 4, 2)).reshape(depth, Bs, MEM_WIDTH, nmem)
    cache_mvt = jnp.transpose(cache_mem_v, (0, 1, 3, 4, 2)).reshape(depth, Bs, MEM_WIDTH, nmem)

    outs = {k: [] for k in ("pf", "pre", "pim", "pmk", "pmv", "sk", "sv", "sf", "sre", "sim")}
    kv_leaves = None
    lw = _stacked_weights(norm_g, w_in, b_forget, fox_q_norm, fox_k_norm, mem_q_norm, mem_norm,
                          w_mem_kv, mem_k_norm, w_out, w_glu, b_glu, ssm_d)
    s5_params = _s5_prep_inputs(ssm_a_re, ssm_a_im, ssm_log_dt, ssm_b_re, ssm_b_im, ssm_c_re, ssm_c_im)
    for l in range(depth):
        ops = _s5_prep(s5_params, l)
        mkt, mvt = _memory_kv(mem, lw, batch=B, layer=l)

        pr = _in_proj(xp, lw, batch=B, prompt=True, layer=l, depth=depth, kv_leaves=kv_leaves)
        kv_leaves = (pr["kt"], pr["vt"])
        fox = _fox_prompt(pr["qa"], pr["ka"], pr["va"], pr["gf"].reshape(B, T, FOX_WIDTH)).reshape(B * T, FOX_WIDTH)
        ssm, hT = _s5(pr["u"], pr["gs"], jnp.zeros((B, 1, 2 * M), f32), ops, lw["ssm_d"], lw["w_glu"],
                      lw["b_glu"], batch=B, chained=True, layer=l)
        xp = _mix_out(xp, fox, ssm, pr["mq"], pr["gm"], mkt, mvt,
                      lw["w_out"], batch=B, tm=MIX_TILE, layer=l)
        hre, him = _split_state(hT, B)
        outs["pf"].append(pr["logft"])
        outs["pre"].append(hre)
        outs["pim"].append(him)
        outs["pmk"].append(mkt.reshape(B, N_MEM_HEADS, HEAD_DIM, nmem))
        outs["pmv"].append(mvt.reshape(B, N_MEM_HEADS, HEAD_DIM, nmem))

        sr = _in_proj(xs, lw, batch=Bs, prompt=False, layer=l)
        fox_s = _fox_sample(sr["q"], sr["k"], sr["v"], sr["logf"], sr["gf"], cache_kt, cache_vt, cache_lt, l,
                            batch=Bs)
        h0 = jnp.concatenate([state_ssm_re[l].reshape(Bs, M), state_ssm_im[l].reshape(Bs, M)],
                             axis=-1).astype(f32)
        ssm_s, hT_s = _s5(sr["u"], sr["gs"], h0, ops, lw["ssm_d"], lw["w_glu"], lw["b_glu"], batch=Bs,
                          chained=False, layer=l)
        xs = _mix_out(xs, fox_s, ssm_s, sr["mq"], sr["gm"], cache_mkt, cache_mvt,
                      lw["w_out"], batch=Bs, tm=Ts, layer=l)
        sre, sim = _split_state(hT_s, Bs)
        outs["sk"].append(sr["k"].reshape(Bs, Ts, N_FOX_HEADS, HEAD_DIM))
        outs["sv"].append(sr["v"].reshape(Bs, Ts, N_FOX_HEADS, HEAD_DIM))
        outs["sf"].append(sr["logf"][:, :N_FOX_HEADS].reshape(Bs, Ts, N_FOX_HEADS))
        outs["sre"].append(sre)
        outs["sim"].append(sim)

    st = {k: jnp.stack(v) for k, v in outs.items()}
    fox_k_prompt = jnp.transpose(kv_leaves[0], (0, 1, 4, 2, 3))
    fox_v_prompt = jnp.transpose(kv_leaves[1], (0, 1, 4, 2, 3))
    fox_logf_prompt = jnp.transpose(st["pf"], (0, 1, 3, 2))
    mem_k_prompt = jnp.transpose(st["pmk"], (0, 1, 4, 2, 3))
    mem_v_prompt = jnp.transpose(st["pmv"], (0, 1, 4, 2, 3))
    return (xp.reshape(B, T, D), xs.reshape(Bs, Ts, D), fox_k_prompt, fox_v_prompt, fox_logf_prompt,
            st["pre"], st["pim"], mem_k_prompt, mem_v_prompt, st["sk"], st["sv"], st["sf"], st["sre"], st["sim"])
```

```python
import functools
import math

import numpy as np
import jax
import jax.numpy as jnp
from jax import lax
from jax.experimental import pallas as pl
from jax.experimental.pallas import tpu as pltpu

f32 = jnp.float32
bf16 = jnp.bfloat16

HEAD_DIM = 64
N_FOX_HEADS = 8
FOX_WIDTH = N_FOX_HEADS * HEAD_DIM
SSM_GROUPS = 16
SSM_CH = 16
SSM_STATE = 64
SSM_WIDTH = SSM_GROUPS * SSM_CH
SSM_MODES = SSM_GROUPS * SSM_STATE
N_MEM_HEADS = 4
MEM_WIDTH = N_MEM_HEADS * HEAD_DIM
EPS = 1e-6
QK_SCALE = HEAD_DIM ** -0.5
LOG2E = math.log2(math.e)

LANES = 128
S5_CHUNK = 8
S5_SEG = 16
ROW_TILE = 512
MIX_TILE = 1024
ATTN_TQ = 512
ATTN_UNROLL = 28
VMEM_LIMIT = 56 * 1024 * 1024

_C_Q, _C_F, _C_K, _C_V, _C_G = 0, 512, 640, 1152, 1664
_C_SU, _C_SG, _C_MQ, _C_MG = 2176, 2432, 2688, 2944
_W_COLS = 3200
_N_PIECES = 3


def _cparams(sem):
    return pltpu.CompilerParams(dimension_semantics=sem, vmem_limit_bytes=VMEM_LIMIT)


def _wspec(a, layer):
    if a.ndim == 3:
        return pl.BlockSpec((None,) + a.shape[1:], lambda *_: (layer, 0, 0))
    return pl.BlockSpec(a.shape, lambda *_: (0,) * a.ndim)


def _silu(x):
    return x * jax.nn.sigmoid(x)


def _log_sigmoid(x):
    return -(jnp.maximum(-x, 0.0) + jnp.log1p(jnp.exp(-jnp.abs(x))))


def _gelu_tanh(x):
    return 0.5 * x * (1.0 + jnp.tanh(math.sqrt(2.0 / math.pi) * (x + 0.044715 * (x * x * x))))


def _split3(x):
    hi = x.astype(bf16).astype(f32)
    r = x - hi
    mid = r.astype(bf16).astype(f32)
    lo = (r - mid).astype(bf16).astype(f32)
    return hi, mid, lo


def _pack3(x):
    hi, mid, lo = _split3(x)
    return (hi + pltpu.roll(mid, 8, 1) + pltpu.roll(lo, 16, 1)).astype(bf16)


def _unpack3(y):
    return y + pltpu.roll(y, LANES - 8, 1) + pltpu.roll(y, LANES - 16, 1)


def _lane_iota(shape):
    return lax.broadcasted_iota(jnp.int32, shape, len(shape) - 1)


def _row_iota(shape):
    return lax.broadcasted_iota(jnp.int32, shape, len(shape) - 2)


def _head_norm(z, hm_ref, g_ref):
    ms = jnp.dot((z * z).astype(bf16), hm_ref[...], preferred_element_type=f32)
    return z * lax.rsqrt(ms + EPS) * g_ref[...]


def _in_proj_kernel(*refs, prompt, tiles_per_batch, aliased):
    if prompt:
        if aliased:
            refs = refs[:10] + refs[12:]
        (x_ref, ng_ref, w_ref, bf_ref, qg_ref, kg_ref, mqg_ref, hm_ref, tril_ref, eb_ref,
         kt_ref, vt_ref, logft_ref, gf_ref, u_ref, gs_ref, mq_ref, gm_ref, qa_ref, ka_ref, va_ref,
         carry_ref) = refs
    else:
        (x_ref, ng_ref, w_ref, bf_ref, qg_ref, kg_ref, mqg_ref, hm_ref,
         k_ref, v_ref, logf_ref, gf_ref, u_ref, gs_ref, mq_ref, gm_ref, q_ref) = refs

    x = x_ref[...]
    tm = x.shape[0]
    lane = _lane_iota((tm, LANES))
    ms = jnp.mean(x * x, axis=-1, keepdims=True)
    h = (x * lax.rsqrt(ms + EPS) * ng_ref[...]).astype(bf16)

    def seg(lo, width):
        return jnp.dot(h, w_ref[:, lo:lo + width], preferred_element_type=f32)

    def halves(ref, z):
        ref[0] = z[:, :LANES]
        ref[1] = z[:, LANES:]

    zqf = seg(_C_Q, FOX_WIDTH + LANES)
    logf = _log_sigmoid(zqf[:, FOX_WIDTH:] + bf_ref[...])
    lf = jnp.where(lane < N_FOX_HEADS, logf, 0.0)
    zk = seg(_C_K, FOX_WIDTH)
    if prompt:
        i = pl.program_id(0)

        @pl.when(i % tiles_per_batch == 0)
        def _():
            carry_ref[...] = jnp.zeros_like(carry_ref)

        csum = jnp.dot(tril_ref[...], _pack3(lf), preferred_element_type=f32)
    v = seg(_C_V, FOX_WIDTH)
    if prompt:
        c = jnp.where(lane < N_FOX_HEADS, _unpack3(csum) + carry_ref[0:1, :], 0.0)
        carry_ref[0:1, :] = c[tm - 1:tm, :]
        pieces = (_pack3(c * LOG2E).astype(f32) + jnp.where(lane == 3 * N_FOX_HEADS, 1.0, 0.0)).astype(bf16)
    if prompt:
        aug = jnp.dot(pieces, eb_ref[...], preferred_element_type=f32)
    q = _head_norm(zqf[:, :FOX_WIDTH], hm_ref, qg_ref) * QK_SCALE
    k = _head_norm(zk, hm_ref, kg_ref)

    def gate_proj():
        gf_ref[...] = _silu(seg(_C_G, FOX_WIDTH))

    def ssm_proj():
        zs = seg(_C_SU, 2 * SSM_WIDTH)
        halves(u_ref, zs[:, :SSM_WIDTH])
        halves(gs_ref, _silu(zs[:, SSM_WIDTH:]))

    def mem_proj():
        zm = seg(_C_MQ, 2 * MEM_WIDTH)
        mq = _head_norm(zm[:, :MEM_WIDTH], hm_ref.at[0:MEM_WIDTH, 0:MEM_WIDTH], mqg_ref) * QK_SCALE
        mq_ref[...] = mq.astype(bf16)
        gm_ref[...] = _silu(zm[:, MEM_WIDTH:])

    late = [gate_proj, ssm_proj, mem_proj]
    if not prompt:
        for proj in late:
            proj()
        k_ref[...] = k
        v_ref[...] = v
        logf_ref[...] = lf
        q_ref[...] = q.astype(bf16)
        return

    kt = k.T
    vt = v.T
    for hd in range(N_FOX_HEADS):
        kt_ref[0, 0, hd] = kt[hd * HEAD_DIM:(hd + 1) * HEAD_DIM, :]
        vt_ref[0, 0, hd] = vt[hd * HEAD_DIM:(hd + 1) * HEAD_DIM, :]
    for later in range(1, kt_ref.shape[0]):
        kt_ref[later] = jnp.zeros(kt_ref.shape[1:], f32)
        vt_ref[later] = jnp.zeros(vt_ref.shape[1:], f32)
    logft_ref[0] = lf.T[0:N_FOX_HEADS, :]

    q = q * LOG2E
    qaug, kaug = aug[:, :LANES], aug[:, LANES:]
    for hd in range(N_FOX_HEADS):
        if hd % 3 == 0 and late:
            late.pop(0)()
        pair = (hd // 2) * LANES
        own = (lane < HEAD_DIM) if hd % 2 == 0 else (lane >= HEAD_DIM)
        ones_lane = HEAD_DIM if hd % 2 == 0 else 0
        g0 = _bias_group(hd)
        bias = (lane >= g0) & (lane < g0 + 2 * _N_PIECES)
        qa_ref[0, hd] = jnp.where(own, q[:, pair:pair + LANES], jnp.where(bias, qaug, 0.0)).astype(bf16)
        ka_ref[0, hd] = jnp.where(own, k[:, pair:pair + LANES], kaug).astype(bf16)
        va_ref[0, hd] = jnp.where(own, v[:, pair:pair + LANES],
                                  jnp.where(lane == ones_lane, 1.0, 0.0)).astype(bf16)


def _bias_group(hd):
    return (HEAD_DIM if hd % 2 == 0 else 0) + 16 * (hd // 2)


def _bias_selector():
    eb = np.zeros((LANES, 2 * LANES), np.float32)
    one = 3 * N_FOX_HEADS
    for hd in range(N_FOX_HEADS):
        g0 = _bias_group(hd)
        for p in range(_N_PIECES):
            eb[p * N_FOX_HEADS + hd, g0 + p] = 1.0
            eb[one, g0 + _N_PIECES + p] = 1.0
            eb[one, LANES + g0 + p] = 1.0
            eb[p * N_FOX_HEADS + hd, LANES + g0 + _N_PIECES + p] = -1.0
    return jnp.asarray(eb, bf16)


def _head_mean_matrix():
    idx = np.arange(FOX_WIDTH) // HEAD_DIM
    return jnp.asarray((idx[:, None] == idx[None, :]).astype(np.float32) / HEAD_DIM, bf16)


def _in_proj(x, lw, *, batch, prompt, layer=0, depth=1, kv_leaves=None):
    n, d = x.shape
    t = n // batch
    tm = ROW_TILE if prompt else n
    assert n % tm == 0 and (t % tm == 0 or not prompt)
    tiles_per_batch = t // tm if prompt else 1
    grid = (n // tm,)
    row = lambda w: pl.BlockSpec((tm, w), lambda i: (i, 0))
    full = lambda a: _wspec(a, layer)
    split = pl.BlockSpec((2, tm, LANES), lambda i: (0, i, 0))
    by_batch = lambda i: (i // tiles_per_batch, 0, i % tiles_per_batch, 0)

    ins = [x, lw["norm_g"], lw["w_in"], lw["b_forget"], lw["fox_q_norm"], lw["fox_k_norm"],
           lw["mem_q_norm"], lw["head_mean"]]
    in_specs = [row(d)] + [full(a) for a in ins[1:]]
    common = [((n, FOX_WIDTH), f32, row(FOX_WIDTH)), ((2, n, LANES), f32, split), ((2, n, LANES), f32, split),
              ((n, MEM_WIDTH), bf16, row(MEM_WIDTH)), ((n, MEM_WIDTH), f32, row(MEM_WIDTH))]
    scratch = []
    if prompt:
        tril = jnp.asarray(np.tril(np.ones((tm, tm), np.float32)), bf16)
        extra = [tril, lw["bias_sel"]]
        ins += extra
        in_specs += [full(a) for a in extra]
        aliases = {}
        if kv_leaves is None:
            tspec = pl.BlockSpec((depth, 1, N_FOX_HEADS, HEAD_DIM, tm),
                                 lambda i: (0, i // tiles_per_batch, 0, 0, i % tiles_per_batch))
        else:
            aliases = {len(ins): 0, len(ins) + 1: 1}
            ins += list(kv_leaves)
            in_specs += [pl.BlockSpec(memory_space=pl.ANY)] * 2
            tspec = pl.BlockSpec((1, 1, N_FOX_HEADS, HEAD_DIM, tm),
                                 lambda i: (layer, i // tiles_per_batch, 0, 0, i % tiles_per_batch))
        lspec = pl.BlockSpec((1, N_FOX_HEADS, tm), lambda i: (i // tiles_per_batch, 0, i % tiles_per_batch))
        hspec = pl.BlockSpec((1, N_FOX_HEADS, tm, LANES), by_batch)
        tshape = (depth, batch, N_FOX_HEADS, HEAD_DIM, t)
        outs = ([(tshape, f32, tspec), (tshape, f32, tspec), ((batch, N_FOX_HEADS, t), f32, lspec)] + common
                + [((batch, N_FOX_HEADS, t, LANES), bf16, hspec)] * 3)
        names = ["kt", "vt", "logft", "gf", "u", "gs", "mq", "gm", "qa", "ka", "va"]
        scratch = [pltpu.VMEM((8, LANES), f32)]
    else:
        outs = ([((n, FOX_WIDTH), f32, row(FOX_WIDTH)), ((n, FOX_WIDTH), f32, row(FOX_WIDTH)),
                 ((n, LANES), f32, row(LANES))] + common + [((n, FOX_WIDTH), bf16, row(FOX_WIDTH))])
        names = ["k", "v", "logf", "gf", "u", "gs", "mq", "gm", "q"]
        aliases = {}

    res = pl.pallas_call(
        functools.partial(_in_proj_kernel, prompt=prompt, tiles_per_batch=tiles_per_batch, aliased=bool(aliases)),
        grid=grid, in_specs=in_specs, out_specs=[o[2] for o in outs],
        out_shape=[jax.ShapeDtypeStruct(o[0], o[1]) for o in outs],
        scratch_shapes=scratch, input_output_aliases=aliases,
        compiler_params=_cparams(("arbitrary",)),
        name="in_proj_prompt" if prompt else "in_proj_sample",
    )(*ins)
    return dict(zip(names, res))


def _fox_prompt_kernel(qa_ref, ka_ref, va_ref, gate_ref, o_ref, s0_ref, s1_ref, m_ref, acc_ref, *, nq):
    tq = tk = ATTN_TQ
    s_refs = (s0_ref, s1_ref)
    m_ref[...] = jnp.full(m_ref.shape, -jnp.inf, f32)
    acc_ref[...] = jnp.zeros(acc_ref.shape, f32)

    def scores(qb, kb, slot):
        qoff = pl.multiple_of(qb * tq, tq)
        koff = pl.multiple_of(kb * tk, tk)
        for j in range(2):
            s_refs[slot][j] = lax.dot_general(qa_ref[0, j, pl.ds(qoff, tq), :], ka_ref[0, j, pl.ds(koff, tk), :],
                                              (((1,), (1,)), ((), ())), preferred_element_type=f32)

    def softmax_pv(qb, kb, slot, masked):
        koff = pl.multiple_of(kb * tk, tk)
        for j in range(2):
            s = s_refs[slot][j]
            if masked:
                s = jnp.where(_row_iota((tq, tk)) >= _lane_iota((tq, tk)), s, -jnp.inf)
            m = m_ref[qb, j]
            m_new = jnp.maximum(m, jnp.max(s, axis=-1, keepdims=True))
            p = jnp.exp2(s - jnp.concatenate([m_new] * (tk // LANES), axis=1))
            vv = va_ref[0, j, pl.ds(koff, tk), :]
            acc_ref[qb, j] = (jnp.exp2(m - m_new) * acc_ref[qb, j]
                              + jnp.dot(p.astype(bf16), vv, preferred_element_type=f32))
            m_ref[qb, j] = m_new

    def finalize(qb):
        h0 = acc_ref[qb, 0] / acc_ref[qb, 0, :, HEAD_DIM:HEAD_DIM + 1]
        h1 = acc_ref[qb, 1] / acc_ref[qb, 1, :, 0:1]
        rows = pl.ds(pl.multiple_of(qb * tq, tq), tq)
        out = jnp.where(_lane_iota((tq, LANES)) < HEAD_DIM, h0, h1) * gate_ref[0, rows, :]
        o_ref[0, rows, :] = out.astype(bf16)

    def after(qb, kb):
        row_end = kb + 1 >= qb
        last = jnp.logical_and(row_end, qb + 1 >= nq)
        nqb = jnp.where(last, 0, jnp.where(row_end, qb + 1, qb))
        nkb = jnp.where(row_end, 0, kb + 1)
        return nqb, nkb

    def lower_step(qb, kb, slot):
        nqb, nkb = after(qb, kb)
        scores(nqb, nkb, 1 - slot)
        softmax_pv(qb, kb, slot, False)
        return nqb, nkb

    def diag_step(qb, slot):
        nxt = jnp.minimum(qb + 1, nq - 1)
        scores(nxt, nxt, 1 - slot)
        softmax_pv(qb, qb, slot, True)
        finalize(qb)

    def steps_per_trip(n):
        return max(d for d in range(2, ATTN_UNROLL + 1, 2) if n % d == 0)

    n_lower = nq * (nq - 1) // 2
    if n_lower:
        scores(1, 0, 0)
        per = steps_per_trip(n_lower)

        def lower_body(i, carry):
            for k in range(per):
                carry = lower_step(*carry, k % 2)
            return carry

        lax.fori_loop(0, n_lower // per, lower_body, (jnp.int32(1), jnp.int32(0)))
    else:
        scores(0, 0, 0)

    per_diag = steps_per_trip(nq)

    def diag_body(i, carry):
        for k in range(per_diag):
            diag_step(per_diag * i + k, k % 2)
        return carry

    lax.fori_loop(0, nq // per_diag, diag_body, 0)


def _fox_prompt(qa, ka, va, gate):
    b, nh, t, _ = qa.shape
    nq = t // ATTN_TQ
    assert t % ATTN_TQ == 0 and nq % 2 == 0 and (nq * (nq - 1) // 2) % 2 == 0
    whole = pl.BlockSpec((1, 2, t, LANES), lambda bi, hp: (bi, hp, 0, 0))
    return pl.pallas_call(
        functools.partial(_fox_prompt_kernel, nq=nq),
        grid=(b, nh // 2),
        in_specs=[whole, whole, whole, pl.BlockSpec((1, t, LANES), lambda bi, hp: (bi, 0, hp))],
        out_specs=pl.BlockSpec((1, t, LANES), lambda bi, hp: (bi, 0, hp)),
        out_shape=jax.ShapeDtypeStruct((b, t, FOX_WIDTH), bf16),
        scratch_shapes=[pltpu.VMEM((2, ATTN_TQ, ATTN_TQ), f32), pltpu.VMEM((2, ATTN_TQ, ATTN_TQ), f32),
                        pltpu.VMEM((nq, 2, ATTN_TQ, LANES), f32), pltpu.VMEM((nq, 2, ATTN_TQ, LANES), f32)],
        compiler_params=_cparams(("arbitrary", "arbitrary")),
        name="fox_prompt",
    )(qa, ka, va, gate)


def _fox_sample_kernel(q_ref, kn_ref, vn_ref, ln_ref, gate_ref, kc_ref, vc_ref, lc_ref, triu_ref, o_ref,
                       kall_ref, vall_ref, *, past, tnew):
    nkeys = kall_ref.shape[1]
    nq = N_FOX_HEADS * tnew

    def new_cols(ref):
        z = jnp.concatenate([ref[...], jnp.zeros((LANES - tnew, ref.shape[1]), f32)], axis=0)
        return z.T

    kall_ref[:, 0:past] = kc_ref[0, 0].astype(bf16)
    vall_ref[:, 0:past] = vc_ref[0, 0].astype(bf16)
    kall_ref[:, past:nkeys] = new_cols(kn_ref).astype(bf16)
    vall_ref[:, past:nkeys] = new_cols(vn_ref).astype(bf16)

    l_all = jnp.concatenate([lc_ref[0, 0], new_cols(ln_ref)[0:N_FOX_HEADS, :]], axis=1)
    pieces = jnp.concatenate(_split3(l_all), axis=0).astype(bf16)
    c3 = jnp.dot(pieces, triu_ref[...], preferred_element_type=f32)
    c = c3[0:8] + c3[8:16] + c3[16:24]
    crow = jnp.concatenate([jnp.broadcast_to(c[hd:hd + 1, :], (tnew, nkeys)) for hd in range(N_FOX_HEADS)],
                           axis=0)
    rq = _row_iota((nq, LANES))
    cq = jnp.sum(jnp.where(_lane_iota((nq, LANES)) == rq % tnew, crow[:, past:nkeys], 0.0),
                 axis=1, keepdims=True)

    qrep = jnp.concatenate([q_ref[...]] * N_FOX_HEADS, axis=0)
    own = _row_iota((nq, FOX_WIDTH)) // tnew == _lane_iota((nq, FOX_WIDTH)) // HEAD_DIM
    qbd = jnp.where(own, qrep, jnp.zeros_like(qrep))
    s = jnp.dot(qbd, kall_ref[...], preferred_element_type=f32) + cq - crow
    s = jnp.where(_lane_iota((nq, nkeys)) <= past + _row_iota((nq, nkeys)) % tnew, s, -jnp.inf)
    p = jnp.exp(s - jnp.max(s, axis=1, keepdims=True))
    o2 = lax.dot_general(p.astype(bf16), vall_ref[...], (((1,), (1,)), ((), ())), preferred_element_type=f32)
    o2 = o2 / jnp.sum(p, axis=1, keepdims=True)
    olane = _lane_iota((tnew, FOX_WIDTH))
    out = jnp.zeros((tnew, FOX_WIDTH), f32)
    for hd in range(N_FOX_HEADS):
        out = out + jnp.where(olane // HEAD_DIM == hd, o2[hd * tnew:(hd + 1) * tnew, :], 0.0)
    o_ref[...] = (out * gate_ref[...]).astype(bf16)


def _fox_sample(q, k_new, v_new, logf_new, gate, cache_kt, cache_vt, cache_lt, layer, *, batch):
    n = q.shape[0]
    tnew = n // batch
    past = cache_kt.shape[3]
    nkeys = past + LANES
    assert N_FOX_HEADS * tnew == LANES and past % LANES == 0
    triu = jnp.asarray(np.triu(np.ones((nkeys, nkeys), np.float32)), bf16)
    rows = lambda w: pl.BlockSpec((tnew, w), lambda bi: (bi, 0))
    cache = lambda h: pl.BlockSpec((1, 1, h, past), lambda bi: (layer, bi, 0, 0))
    return pl.pallas_call(
        functools.partial(_fox_sample_kernel, past=past, tnew=tnew),
        grid=(batch,),
        in_specs=[rows(FOX_WIDTH), rows(FOX_WIDTH), rows(FOX_WIDTH), rows(LANES), rows(FOX_WIDTH),
                  cache(FOX_WIDTH), cache(FOX_WIDTH), cache(N_FOX_HEADS),
                  pl.BlockSpec((nkeys, nkeys), lambda bi: (0, 0))],
        out_specs=rows(FOX_WIDTH),
        out_shape=jax.ShapeDtypeStruct((n, FOX_WIDTH), bf16),
        scratch_shapes=[pltpu.VMEM((FOX_WIDTH, nkeys), bf16), pltpu.VMEM((FOX_WIDTH, nkeys), bf16)],
        compiler_params=_cparams(("arbitrary",)),
        name="fox_sample",
    )(q, k_new, v_new, logf_new, gate, cache_kt, cache_vt, cache_lt, triu)


def _s5_prep_kernel(are_r, aim_r, ldt_r, are_c, aim_c, ldt_c, bxr_ref, bxi_ref, cxr_ref, cxi_ref,
                    bst_ref, cst_ref, kt_ref, apow_ref):
    L, W, M = S5_CHUNK, SSM_WIDTH, SSM_MODES

    def cmul(xr, xi, yr, yi):
        return xr * yr - xi * yi, xr * yi + xi * yr

    def powers(are, aim, ldt, n):
        dt = jnp.exp(ldt[...])
        mag = jnp.exp(dt * are[...])
        p1 = (mag * jnp.cos(dt * aim[...]), mag * jnp.sin(dt * aim[...]))
        out = [(jnp.ones_like(mag), jnp.zeros_like(mag)), p1]
        for _ in range(n - 1):
            out.append(cmul(*out[-1], *p1))
        return out

    prow = powers(are_r, aim_r, ldt_r, L)
    pcol = powers(are_c, aim_c, ldt_c, L)
    ar, ai = are_r[...], aim_r[...]
    abr, abi = prow[1]
    den = ar * ar + ai * ai
    zr = ((abr - 1.0) * ar + abi * ai) / den
    zi = (abi * ar - (abr - 1.0) * ai) / den
    bbr, bbi = cmul(zr, zi, bxr_ref[...], bxi_ref[...])

    def b_pow(k):
        return jnp.concatenate(cmul(*prow[k], bbr, bbi), axis=1)

    def c_pow(k):
        pr, pi = pcol[k]
        cr, ci = cxr_ref[...], cxi_ref[...]
        return jnp.concatenate([cr * pr - ci * pi, -(cr * pi + ci * pr)], axis=0)

    cpow = [c_pow(k) for k in range(L + 1)]
    for tl in range(L):
        bst_ref[tl * W:(tl + 1) * W, :] = b_pow(L - 1 - tl).astype(bf16)
        cst_ref[:, tl * W:(tl + 1) * W] = cpow[tl + 1].astype(bf16)

    def hi_lo(x):
        hi = x.astype(bf16)
        return hi, (x - hi.astype(f32)).astype(bf16)

    bh, bl = hi_lo(b_pow(0))
    ch, cl = hi_lo(jnp.concatenate(cpow[:L], axis=1))
    kd = (jnp.dot(bh, ch, preferred_element_type=f32) + jnp.dot(bh, cl, preferred_element_type=f32)
          + jnp.dot(bl, ch, preferred_element_type=f32)).astype(bf16)
    for tl in range(L):
        if tl:
            kt_ref[tl * W:(tl + 1) * W, 0:tl * W] = jnp.zeros((W, tl * W), bf16)
        kt_ref[tl * W:(tl + 1) * W, tl * W:] = kd[:, :(L - tl) * W]

    step = prow[L]
    seg = step
    for _ in range(int(math.log2(S5_SEG))):
        seg = cmul(*seg, *seg)
    apow_ref[...] = jnp.concatenate(
        [jnp.concatenate(step, axis=1), jnp.concatenate(seg, axis=1), jnp.zeros((6, 2 * M), f32)], axis=0)


def _s5_prep_inputs(a_re, a_im, log_dt, b_re, b_im, c_re, c_im):
    G, N = SSM_GROUPS, SSM_STATE
    W, M = SSM_WIDTH, SSM_MODES
    depth = a_re.shape[0]
    eye = jnp.eye(G, dtype=f32)
    bx = lambda b: (jnp.swapaxes(b, 2, 3)[:, :, :, None, :] * eye[None, :, None, :, None]).reshape(depth, W, M)
    cx = lambda c: (jnp.swapaxes(c, 2, 3)[:, :, :, None, :] * eye[None, :, None, :, None]).reshape(depth, M, W)
    ldt = jnp.repeat(log_dt, N, axis=1)
    return [a_re.reshape(depth, 1, M), a_im.reshape(depth, 1, M), ldt.reshape(depth, 1, M),
            a_re.reshape(depth, M, 1), a_im.reshape(depth, M, 1), ldt.reshape(depth, M, 1),
            bx(b_re), bx(b_im), cx(c_re), cx(c_im)]


def _s5_prep(ins, layer):
    L, W, M = S5_CHUNK, SSM_WIDTH, SSM_MODES
    shapes = [((L * W, 2 * M), bf16), ((2 * M, L * W), bf16), ((L * W, L * W), bf16), ((8, 2 * M), f32)]
    return pl.pallas_call(
        _s5_prep_kernel,
        grid=(1,),
        in_specs=[_wspec(a, layer) for a in ins],
        out_specs=[pl.BlockSpec(sh, lambda i: (0, 0)) for sh, _ in shapes],
        out_shape=[jax.ShapeDtypeStruct(sh, dt) for sh, dt in shapes],
        compiler_params=_cparams(("arbitrary",)),
        name="s5_prep",
    )(*ins)


def _s5_kernel(u_ref, gs_ref, h0_ref, bst_ref, cst_ref, kt_ref, apow_ref, d_ref, wg_ref, bg_ref,
               o_ref, hT_ref, ds_ref, sp_ref, carry_ref, *, chained, seglen):
    L, W, M = S5_CHUNK, SSM_WIDTH, SSM_MODES
    nr = 8 * seglen
    nlb = M // LANES

    def chunked(ref, tl):
        return jnp.concatenate([ref[0, pl.ds(tl, nr, stride=L), :], ref[1, pl.ds(tl, nr, stride=L), :]], axis=1)

    uf = [chunked(u_ref, tl) for tl in range(L)]
    ub = jnp.concatenate(uf, axis=1).astype(bf16)
    ds = jnp.dot(ub, bst_ref[...], preferred_element_type=f32)
    for lb in range(2 * nlb):
        ds_ref[lb] = ds[:, lb * LANES:(lb + 1) * LANES]
    def blocks(row):
        return ([row[:, lb * LANES:(lb + 1) * LANES] for lb in range(nlb)],
                [row[:, M + lb * LANES:M + (lb + 1) * LANES] for lb in range(nlb)])

    def bcast8(parts):
        return [jnp.broadcast_to(p, (8, LANES)) for p in parts]

    ar, ai = (bcast8(p) for p in blocks(apow_ref[0:1, :]))

    def scan(vr, vi, store):
        for i in range(seglen):
            rows = pl.ds(i, 8, stride=seglen)
            for lb in range(nlb):
                if store:
                    sp_ref[lb, rows, :] = vr[lb]
                    sp_ref[nlb + lb, rows, :] = vi[lb]
                dr, di = ds_ref[lb, rows, :], ds_ref[nlb + lb, rows, :]
                vr[lb], vi[lb] = (ar[lb] * vr[lb] - ai[lb] * vi[lb] + dr,
                                  ar[lb] * vi[lb] + ai[lb] * vr[lb] + di)
        return vr, vi

    if chained:
        ti = pl.program_id(1)

        @pl.when(ti == 0)
        def _():
            carry_ref[0:1, :] = h0_ref[0]

        zero = [jnp.zeros((8, LANES), f32) for _ in range(nlb)]
        er, ei = scan(list(zero), list(zero), store=False)
        pr, pi = blocks(apow_ref[1:2, :])
        cr, ci = blocks(carry_ref[0:1, :])
        sr, si = [[] for _ in range(nlb)], [[] for _ in range(nlb)]
        for j in range(8):
            for lb in range(nlb):
                sr[lb].append(cr[lb])
                si[lb].append(ci[lb])
                cr[lb], ci[lb] = (pr[lb] * cr[lb] - pi[lb] * ci[lb] + er[lb][j:j + 1, :],
                                  pr[lb] * ci[lb] + pi[lb] * cr[lb] + ei[lb][j:j + 1, :])
        final = jnp.concatenate(cr + ci, axis=1)
        carry_ref[0:1, :] = final
        hT_ref[0] = final
        scan([jnp.concatenate(x, axis=0) for x in sr], [jnp.concatenate(x, axis=0) for x in si], store=True)
    else:
        vr, vi = blocks(h0_ref[...])
        vr, vi = scan(vr, vi, store=True)
        hT_ref[...] = jnp.concatenate(vr + vi, axis=1)

    spb = jnp.concatenate([sp_ref[lb] for lb in range(2 * nlb)], axis=1).astype(bf16)

    def y_pair(pair):
        kk = (2 * pair + 2) * W
        cols = slice(2 * pair * W, kk)
        return (jnp.dot(ub[:, :kk], kt_ref[0:kk, cols], preferred_element_type=f32)
                + jnp.dot(spb, cst_ref[:, cols], preferred_element_type=f32))

    def finish(pair, ypair):
        for tl in (2 * pair, 2 * pair + 1):
            y = ypair[:, (tl % 2) * W:(tl % 2 + 1) * W] + d_ref[...] * uf[tl]
            g = _gelu_tanh(y)
            gate = jax.nn.sigmoid(jnp.dot(g.astype(bf16), wg_ref[...], preferred_element_type=f32) + bg_ref[...])
            out = g * gate * chunked(gs_ref, tl)
            o_ref[0, pl.ds(tl, nr, stride=L), :] = out[:, :LANES]
            o_ref[1, pl.ds(tl, nr, stride=L), :] = out[:, LANES:]

    ycur = y_pair(0)
    for pair in range(L // 2):
        ynext = y_pair(pair + 1) if pair + 1 < L // 2 else None
        finish(pair, ycur)
        ycur = ynext


def _s5(u, gs, h0, ops, d_row, w_glu, b_glu, *, batch, chained, layer):
    L, W, M = S5_CHUNK, SSM_WIDTH, SSM_MODES
    n = u.shape[1]
    t = n // batch
    assert t % L == 0
    chunks = t // L
    bst, cst, kt, apow = ops
    if chained:
        seglen = S5_SEG
        assert chunks % (8 * seglen) == 0
        grid = (batch, chunks // (8 * seglen))
        state = pl.BlockSpec((1, 1, 2 * M), lambda bi, ti: (bi, 0, 0))
    else:
        seglen = chunks
        assert batch == 8
        grid = (1, 1)
        state = pl.BlockSpec((8, 2 * M), lambda bi, ti: (0, 0))
    tiles = grid[1]
    nr = 8 * seglen
    blk = pl.BlockSpec((2, nr * L, LANES), lambda bi, ti: (0, bi * tiles + ti, 0))
    full = lambda a: _wspec(a, layer)
    once = lambda a: pl.BlockSpec(a.shape, lambda bi, ti: (0,) * a.ndim, pipeline_mode=pl.Buffered(1))
    return pl.pallas_call(
        functools.partial(_s5_kernel, chained=chained, seglen=seglen),
        grid=grid,
        in_specs=[blk, blk, state, once(bst), once(cst), once(kt), full(apow), full(d_row),
                  full(w_glu), full(b_glu)],
        out_specs=[blk, state],
        out_shape=[jax.ShapeDtypeStruct((2, n, LANES), f32), jax.ShapeDtypeStruct(h0.shape, f32)],
        scratch_shapes=[pltpu.VMEM((2 * M // LANES, nr, LANES), f32), pltpu.VMEM((2 * M // LANES, nr, LANES), f32),
                        pltpu.VMEM((8, 2 * M), f32)],
        compiler_params=_cparams(("arbitrary", "arbitrary")),
        name="s5_prompt" if chained else "s5_sample",
    )(u, gs, h0, bst, cst, kt, apow, d_row, w_glu, b_glu)


def _memory_kv_kernel(mem_ref, g_ref, w_ref, kg_ref, hm_ref, mkt_ref, mvt_ref):
    batch, _, nmem = mkt_ref.shape
    x = mem_ref[...]
    ms = jnp.mean(x * x, axis=-1, keepdims=True)
    h = (x * lax.rsqrt(ms + EPS) * g_ref[...]).astype(bf16)
    mk = jnp.dot(h, w_ref[:, 0:MEM_WIDTH], preferred_element_type=f32)
    mk = _head_norm(mk, hm_ref.at[0:MEM_WIDTH, 0:MEM_WIDTH], kg_ref)
    mv = jnp.dot(h, w_ref[:, MEM_WIDTH:2 * MEM_WIDTH], preferred_element_type=f32)
    for b in range(batch):
        mkt_ref[b] = mk[b * nmem:(b + 1) * nmem, :].T
        mvt_ref[b] = mv[b * nmem:(b + 1) * nmem, :].T


def _memory_kv(mem, lw, *, batch, layer):
    n = mem.shape[0]
    ins = [mem, lw["mem_norm"], lw["w_mem_kv"], lw["mem_k_norm"], lw["head_mean"]]
    out = pl.BlockSpec((batch, MEM_WIDTH, n // batch), lambda i: (0, 0, 0))
    return pl.pallas_call(
        _memory_kv_kernel,
        grid=(1,),
        in_specs=[_wspec(a, layer) for a in ins],
        out_specs=[out, out],
        out_shape=[jax.ShapeDtypeStruct((batch, MEM_WIDTH, n // batch), f32)] * 2,
        compiler_params=_cparams(("arbitrary",)),
        name="memory_kv",
    )(*ins)


def _mix_out_kernel(x_ref, fox_ref, ssm_ref, mq_ref, gm_ref, mkt_ref, mvt_ref, w_ref, o_ref):
    tm = x_ref.shape[0]
    nmem = mkt_ref.shape[2]
    mq = mq_ref[...]
    mkt = mkt_ref[0].astype(bf16)
    mvt = mvt_ref[0].astype(bf16)
    khead = _row_iota((MEM_WIDTH, nmem)) // HEAD_DIM
    zero = jnp.zeros_like(mkt)
    mk_heads = jnp.concatenate([jnp.where(khead == hd, mkt, zero) for hd in range(N_MEM_HEADS)], axis=1)
    s_all = jnp.dot(mq, mk_heads, preferred_element_type=f32)
    ssm = jnp.concatenate([ssm_ref[0], ssm_ref[1]], axis=1).astype(bf16)
    y = x_ref[...]
    y = y + jnp.dot(fox_ref[...], w_ref[0:FOX_WIDTH, :], preferred_element_type=f32)
    y = y + jnp.dot(ssm, w_ref[FOX_WIDTH:FOX_WIDTH + SSM_WIDTH, :], preferred_element_type=f32)
    mem = jnp.zeros((tm, MEM_WIDTH), f32)
    for hd in range(N_MEM_HEADS):
        s = s_all[:, hd * nmem:(hd + 1) * nmem]
        p = jnp.exp(s - jnp.max(s, axis=-1, keepdims=True))
        p = p / jnp.sum(p, axis=-1, keepdims=True)
        vh = jnp.where(khead == hd, mvt, zero)
        mem = mem + lax.dot_general(p.astype(bf16), vh, (((1,), (1,)), ((), ())), preferred_element_type=f32)
    memg = (mem * gm_ref[...]).astype(bf16)
    o_ref[...] = y + jnp.dot(memg, w_ref[FOX_WIDTH + SSM_WIDTH:, :], preferred_element_type=f32)


def _mix_out(x, fox, ssm, mq, gm, mkt, mvt, w_out, *, batch, tm, layer):
    n, d = x.shape
    t = n // batch
    assert t % tm == 0
    tiles_per_batch = t // tm
    row = lambda w: pl.BlockSpec((tm, w), lambda i: (i, 0))
    if mkt.ndim == 4:
        memb = pl.BlockSpec((None, 1) + mkt.shape[2:], lambda i: (layer, i // tiles_per_batch, 0, 0))
    else:
        memb = pl.BlockSpec((1,) + mkt.shape[1:], lambda i: (i // tiles_per_batch, 0, 0))
    return pl.pallas_call(
        _mix_out_kernel,
        grid=(n // tm,),
        in_specs=[row(d), row(FOX_WIDTH), pl.BlockSpec((2, tm, LANES), lambda i: (0, i, 0)),
                  row(MEM_WIDTH), row(MEM_WIDTH), memb, memb, _wspec(w_out, layer)],
        out_specs=row(d),
        out_shape=jax.ShapeDtypeStruct((n, d), f32),
        compiler_params=_cparams(("arbitrary",)),
        name="mix_out_prompt" if tiles_per_batch > 1 else "mix_out_sample",
    )(x, fox, ssm, mq, gm, mkt, mvt, w_out)


def _repack_w_in_kernel(wt_ref, o_ref):
    nf = 4 * FOX_WIDTH
    dblk = wt_ref.shape[1]

    def put(lo, hi, rows):
        o_ref[:, lo:hi] = rows.T.astype(bf16)

    put(_C_Q, _C_F, wt_ref[0:FOX_WIDTH, :])
    put(_C_F, _C_K, jnp.concatenate([wt_ref[nf:nf + N_FOX_HEADS, :],
                                     jnp.zeros((LANES - N_FOX_HEADS, dblk), f32)], axis=0))
    put(_C_K, _C_SU, wt_ref[FOX_WIDTH:nf, :])
    put(_C_SU, _W_COLS, wt_ref[nf + N_FOX_HEADS:, :])


def _repack_w_in(w_in):
    depth, d, cols = w_in.shape
    assert cols == 4 * FOX_WIDTH + N_FOX_HEADS + 2 * SSM_WIDTH + 2 * MEM_WIDTH
    dblk = 256
    return pl.pallas_call(
        _repack_w_in_kernel,
        grid=(depth, d // dblk),
        in_specs=[pl.BlockSpec((None, cols, dblk), lambda l, i: (l, 0, i))],
        out_specs=pl.BlockSpec((None, dblk, _W_COLS), lambda l, i: (l, i, 0)),
        out_shape=jax.ShapeDtypeStruct((depth, d, _W_COLS), bf16),
        compiler_params=_cparams(("arbitrary", "arbitrary")),
        name="repack_w_in",
    )(jnp.swapaxes(w_in, 1, 2))


def _stacked_weights(norm_g, w_in, b_forget, fox_q_norm, fox_k_norm, mem_q_norm, mem_norm, w_mem_kv,
                     mem_k_norm, w_out, w_glu, b_glu, ssm_d):
    depth, d, _ = w_in.shape
    w_packed = _repack_w_in(w_in)
    row = lambda a: a.reshape(depth, 1, -1)
    per_head = lambda g, heads: jnp.tile(g, (1, heads)).reshape(depth, 1, heads * HEAD_DIM)
    return dict(
        norm_g=row(norm_g), w_in=w_packed,
        b_forget=row(jnp.pad(b_forget, ((0, 0), (0, LANES - N_FOX_HEADS)))),
        fox_q_norm=per_head(fox_q_norm, N_FOX_HEADS), fox_k_norm=per_head(fox_k_norm, N_FOX_HEADS),
        mem_q_norm=per_head(mem_q_norm, N_MEM_HEADS), mem_k_norm=per_head(mem_k_norm, N_MEM_HEADS),
        mem_norm=row(mem_norm), w_mem_kv=w_mem_kv.astype(bf16),
        head_mean=_head_mean_matrix(), bias_sel=_bias_selector(),
        w_out=w_out.astype(bf16), w_glu=w_glu.astype(bf16), b_glu=row(b_glu), ssm_d=row(ssm_d))


def _split_state(hT, batch):
    hT = hT.reshape(batch, 2, SSM_GROUPS, SSM_STATE)
    return hT[:, 0], hT[:, 1]


def kernel(x_prompt, x_sample, mem_prompt, cache_fox_k, cache_fox_v, cache_fox_logf, state_ssm_re, state_ssm_im, cache_mem_k, cache_mem_v, norm_g, w_in, b_forget, fox_q_norm, fox_k_norm, ssm_a_re, ssm_a_im, ssm_log_dt, ssm_b_re, ssm_b_im, ssm_c_re, ssm_c_im, ssm_d, w_glu, b_glu, mem_norm, w_mem_kv, mem_q_norm, mem_k_norm, w_out):
    B, T, D = x_prompt.shape
    Bs, Ts, _ = x_sample.shape
    depth = w_in.shape[0]
    past = cache_fox_k.shape[2]
    nmem = mem_prompt.shape[1]
    M = SSM_MODES

    xp = x_prompt.reshape(B * T, D)
    xs = x_sample.reshape(Bs * Ts, D)
    mem = mem_prompt.reshape(B * nmem, D)
    cache_kt = jnp.transpose(cache_fox_k, (0, 1, 3, 4, 2)).reshape(depth, Bs, FOX_WIDTH, past)
    cache_vt = jnp.transpose(cache_fox_v, (0, 1, 3, 4, 2)).reshape(depth, Bs, FOX_WIDTH, past)
    cache_lt = jnp.transpose(cache_fox_logf, (0, 1, 3, 2)).astype(f32)
    cache_mkt = jnp.transpose(cache_mem_k, (0, 1, 3, 4, 2)).reshape(depth, Bs, MEM_WIDTH, nmem)
    cache_mvt = jnp.transpose(cache_mem_v, (0, 1, 3, 4, 2)).reshape(depth, Bs, MEM_WIDTH, nmem)

    outs = {k: [] for k in ("pf", "pre", "pim", "pmk", "pmv", "sk", "sv", "sf", "sre", "sim")}
    kv_leaves = None
    lw = _stacked_weights(norm_g, w_in, b_forget, fox_q_norm, fox_k_norm, mem_q_norm, mem_norm,
                          w_mem_kv, mem_k_norm, w_out, w_glu, b_glu, ssm_d)
    s5_params = _s5_prep_inputs(ssm_a_re, ssm_a_im, ssm_log_dt, ssm_b_re, ssm_b_im, ssm_c_re, ssm_c_im)
    for l in range(depth):
        ops = _s5_prep(s5_params, l)
        mkt, mvt = _memory_kv(mem, lw, batch=B, layer=l)

        pr = _in_proj(xp, lw, batch=B, prompt=True, layer=l, depth=depth, kv_leaves=kv_leaves)
        kv_leaves = (pr["kt"], pr["vt"])
        fox = _fox_prompt(pr["qa"], pr["ka"], pr["va"], pr["gf"].reshape(B, T, FOX_WIDTH)).reshape(B * T, FOX_WIDTH)
        ssm, hT = _s5(pr["u"], pr["gs"], jnp.zeros((B, 1, 2 * M), f32), ops, lw["ssm_d"], lw["w_glu"],
                      lw["b_glu"], batch=B, chained=True, layer=l)
        xp = _mix_out(xp, fox, ssm, pr["mq"], pr["gm"], mkt, mvt,
                      lw["w_out"], batch=B, tm=MIX_TILE, layer=l)
        hre, him = _split_state(hT, B)
        outs["pf"].append(pr["logft"])
        outs["pre"].append(hre)
        outs["pim"].append(him)
        outs["pmk"].append(mkt.reshape(B, N_MEM_HEADS, HEAD_DIM, nmem))
        outs["pmv"].append(mvt.reshape(B, N_MEM_HEADS, HEAD_DIM, nmem))

        sr = _in_proj(xs, lw, batch=Bs, prompt=False, layer=l)
        fox_s = _fox_sample(sr["q"], sr["k"], sr["v"], sr["logf"], sr["gf"], cache_kt, cache_vt, cache_lt, l,
                            batch=Bs)
        h0 = jnp.concatenate([state_ssm_re[l].reshape(Bs, M), state_ssm_im[l].reshape(Bs, M)],
                             axis=-1).astype(f32)
        ssm_s, hT_s = _s5(sr["u"], sr["gs"], h0, ops, lw["ssm_d"], lw["w_glu"], lw["b_glu"], batch=Bs,
                          chained=False, layer=l)
        xs = _mix_out(xs, fox_s, ssm_s, sr["mq"], sr["gm"], cache_mkt, cache_mvt,
                      lw["w_out"], batch=Bs, tm=Ts, layer=l)
        sre, sim = _split_state(hT_s, Bs)
        outs["sk"].append(sr["k"].reshape(Bs, Ts, N_FOX_HEADS, HEAD_DIM))
        outs["sv"].append(sr["v"].reshape(Bs, Ts, N_FOX_HEADS, HEAD_DIM))
        outs["sf"].append(sr["logf"][:, :N_FOX_HEADS].reshape(Bs, Ts, N_FOX_HEADS))
        outs["sre"].append(sre)
        outs["sim"].append(sim)

    st = {k: jnp.stack(v) for k, v in outs.items()}
    fox_k_prompt = jnp.transpose(kv_leaves[0], (0, 1, 4, 2, 3))
    fox_v_prompt = jnp.transpose(kv_leaves[1], (0, 1, 4, 2, 3))
    fox_logf_prompt = jnp.transpose(st["pf"], (0, 1, 3, 2))
    mem_k_prompt = jnp.transpose(st["pmk"], (0, 1, 4, 2, 3))
    mem_v_prompt = jnp.transpose(st["pmv"], (0, 1, 4, 2, 3))
    return (xp.reshape(B, T, D), xs.reshape(Bs, Ts, D), fox_k_prompt, fox_v_prompt, fox_logf_prompt,
            st["pre"], st["pim"], mem_k_prompt, mem_v_prompt, st["sk"], st["sv"], st["sf"], st["sre"], st["sim"])
```

```python
import functools
import math

import numpy as np
import jax
import jax.numpy as jnp
from jax import lax
from jax.experimental import pallas as pl
from jax.experimental.pallas import tpu as pltpu

f32 = jnp.float32
bf16 = jnp.bfloat16

HEAD_DIM = 64
N_FOX_HEADS = 8
FOX_WIDTH = N_FOX_HEADS * HEAD_DIM
SSM_GROUPS = 16
SSM_CH = 16
SSM_STATE = 64
SSM_WIDTH = SSM_GROUPS * SSM_CH
SSM_MODES = SSM_GROUPS * SSM_STATE
N_MEM_HEADS = 4
MEM_WIDTH = N_MEM_HEADS * HEAD_DIM
EPS = 1e-6
QK_SCALE = HEAD_DIM ** -0.5
LOG2E = math.log2(math.e)

LANES = 128
S5_CHUNK = 8
S5_SEG = 16
ROW_TILE = 512
MIX_TILE = 1024
ATTN_TQ = 512
ATTN_UNROLL = 28
VMEM_LIMIT = 56 * 1024 * 1024

_C_Q, _C_F, _C_K, _C_V, _C_G = 0, 512, 640, 1152, 1664
_C_SU, _C_SG, _C_MQ, _C_MG = 2176, 2432, 2688, 2944
_W_COLS = 3200
_N_PIECES = 3


def _cparams(sem):
    return pltpu.CompilerParams(dimension_semantics=sem, vmem_limit_bytes=VMEM_LIMIT)


def _wspec(a, layer):
    if a.ndim == 3:
        return pl.BlockSpec((None,) + a.shape[1:], lambda *_: (layer, 0, 0))
    return pl.BlockSpec(a.shape, lambda *_: (0,) * a.ndim)


def _silu(x):
    return x * jax.nn.sigmoid(x)


def _log_sigmoid(x):
    return -(jnp.maximum(-x, 0.0) + jnp.log1p(jnp.exp(-jnp.abs(x))))


def _gelu_tanh(x):
    return 0.5 * x * (1.0 + jnp.tanh(math.sqrt(2.0 / math.pi) * (x + 0.044715 * (x * x * x))))


def _split3(x):
    hi = x.astype(bf16).astype(f32)
    r = x - hi
    mid = r.astype(bf16).astype(f32)
    lo = (r - mid).astype(bf16).astype(f32)
    return hi, mid, lo


def _pack3(x):
    hi, mid, lo = _split3(x)
    return (hi + pltpu.roll(mid, 8, 1) + pltpu.roll(lo, 16, 1)).astype(bf16)


def _unpack3(y):
    return y + pltpu.roll(y, LANES - 8, 1) + pltpu.roll(y, LANES - 16, 1)


def _lane_iota(shape):
    return lax.broadcasted_iota(jnp.int32, shape, len(shape) - 1)


def _row_iota(shape):
    return lax.broadcasted_iota(jnp.int32, shape, len(shape) - 2)


def _head_norm(z, hm_ref, g_ref):
    ms = jnp.dot((z * z).astype(bf16), hm_ref[...], preferred_element_type=f32)
    return z * lax.rsqrt(ms + EPS) * g_ref[...]


def _in_proj_kernel(*refs, prompt, tiles_per_batch, aliased):
    if prompt:
        if aliased:
            refs = refs[:10] + refs[12:]
        (x_ref, ng_ref, w_ref, bf_ref, qg_ref, kg_ref, mqg_ref, hm_ref, tril_ref, eb_ref,
         kt_ref, vt_ref, logft_ref, gf_ref, u_ref, gs_ref, mq_ref, gm_ref, qa_ref, ka_ref, va_ref,
         carry_ref) = refs
    else:
        (x_ref, ng_ref, w_ref, bf_ref, qg_ref, kg_ref, mqg_ref, hm_ref,
         k_ref, v_ref, logf_ref, gf_ref, u_ref, gs_ref, mq_ref, gm_ref, q_ref) = refs

    x = x_ref[...]
    tm = x.shape[0]
    lane = _lane_iota((tm, LANES))
    ms = jnp.mean(x * x, axis=-1, keepdims=True)
    h = (x * lax.rsqrt(ms + EPS) * ng_ref[...]).astype(bf16)

    def seg(lo, width):
        return jnp.dot(h, w_ref[:, lo:lo + width], preferred_element_type=f32)

    def halves(ref, z):
        ref[0] = z[:, :LANES]
        ref[1] = z[:, LANES:]

    zqf = seg(_C_Q, FOX_WIDTH + LANES)
    logf = _log_sigmoid(zqf[:, FOX_WIDTH:] + bf_ref[...])
    lf = jnp.where(lane < N_FOX_HEADS, logf, 0.0)
    zk = seg(_C_K, FOX_WIDTH)
    if prompt:
        i = pl.program_id(0)

        @pl.when(i % tiles_per_batch == 0)
        def _():
            carry_ref[...] = jnp.zeros_like(carry_ref)

        csum = jnp.dot(tril_ref[...], _pack3(lf), preferred_element_type=f32)
    v = seg(_C_V, FOX_WIDTH)
    if prompt:
        c = jnp.where(lane < N_FOX_HEADS, _unpack3(csum) + carry_ref[0:1, :], 0.0)
        carry_ref[0:1, :] = c[tm - 1:tm, :]
        pieces = (_pack3(c * LOG2E).astype(f32) + jnp.where(lane == 3 * N_FOX_HEADS, 1.0, 0.0)).astype(bf16)
    if prompt:
        aug = jnp.dot(pieces, eb_ref[...], preferred_element_type=f32)
    q = _head_norm(zqf[:, :FOX_WIDTH], hm_ref, qg_ref) * QK_SCALE
    k = _head_norm(zk, hm_ref, kg_ref)

    def gate_proj():
        gf_ref[...] = _silu(seg(_C_G, FOX_WIDTH))

    def ssm_proj():
        zs = seg(_C_SU, 2 * SSM_WIDTH)
        halves(u_ref, zs[:, :SSM_WIDTH])
        halves(gs_ref, _silu(zs[:, SSM_WIDTH:]))

    def mem_proj():
        zm = seg(_C_MQ, 2 * MEM_WIDTH)
        mq = _head_norm(zm[:, :MEM_WIDTH], hm_ref.at[0:MEM_WIDTH, 0:MEM_WIDTH], mqg_ref) * QK_SCALE
        mq_ref[...] = mq.astype(bf16)
        gm_ref[...] = _silu(zm[:, MEM_WIDTH:])

    late = [gate_proj, ssm_proj, mem_proj]
    if not prompt:
        for proj in late:
            proj()
        k_ref[...] = k
        v_ref[...] = v
        logf_ref[...] = lf
        q_ref[...] = q.astype(bf16)
        return

    kt = k.T
    vt = v.T
    for hd in range(N_FOX_HEADS):
        kt_ref[0, 0, hd] = kt[hd * HEAD_DIM:(hd + 1) * HEAD_DIM, :]
        vt_ref[0, 0, hd] = vt[hd * HEAD_DIM:(hd + 1) * HEAD_DIM, :]
    for later in range(1, kt_ref.shape[0]):
        kt_ref[later] = jnp.zeros(kt_ref.shape[1:], f32)
        vt_ref[later] = jnp.zeros(vt_ref.shape[1:], f32)
    logft_ref[0] = lf.T[0:N_FOX_HEADS, :]

    q = q * LOG2E
    qaug, kaug = aug[:, :LANES], aug[:, LANES:]
    for hd in range(N_FOX_HEADS):
        if hd % 3 == 0 and late:
            late.pop(0)()
        pair = (hd // 2) * LANES
        own = (lane < HEAD_DIM) if hd % 2 == 0 else (lane >= HEAD_DIM)
        ones_lane = HEAD_DIM if hd % 2 == 0 else 0
        g0 = _bias_group(hd)
        bias = (lane >= g0) & (lane < g0 + 2 * _N_PIECES)
        qa_ref[0, hd] = jnp.where(own, q[:, pair:pair + LANES], jnp.where(bias, qaug, 0.0)).astype(bf16)
        ka_ref[0, hd] = jnp.where(own, k[:, pair:pair + LANES], kaug).astype(bf16)
        va_ref[0, hd] = jnp.where(own, v[:, pair:pair + LANES],
                                  jnp.where(lane == ones_lane, 1.0, 0.0)).astype(bf16)


def _bias_group(hd):
    return (HEAD_DIM if hd % 2 == 0 else 0) + 16 * (hd // 2)


def _bias_selector():
    eb = np.zeros((LANES, 2 * LANES), np.float32)
    one = 3 * N_FOX_HEADS
    for hd in range(N_FOX_HEADS):
        g0 = _bias_group(hd)
        for p in range(_N_PIECES):
            eb[p * N_FOX_HEADS + hd, g0 + p] = 1.0
            eb[one, g0 + _N_PIECES + p] = 1.0
            eb[one, LANES + g0 + p] = 1.0
            eb[p * N_FOX_HEADS + hd, LANES + g0 + _N_PIECES + p] = -1.0
    return jnp.asarray(eb, bf16)


def _head_mean_matrix():
    idx = np.arange(FOX_WIDTH) // HEAD_DIM
    return jnp.asarray((idx[:, None] == idx[None, :]).astype(np.float32) / HEAD_DIM, bf16)


def _in_proj(x, lw, *, batch, prompt, layer=0, depth=1, kv_leaves=None):
    n, d = x.shape
    t = n // batch
    tm = ROW_TILE if prompt else n
    assert n % tm == 0 and (t % tm == 0 or not prompt)
    tiles_per_batch = t // tm if prompt else 1
    grid = (n // tm,)
    row = lambda w: pl.BlockSpec((tm, w), lambda i: (i, 0))
    full = lambda a: _wspec(a, layer)
    split = pl.BlockSpec((2, tm, LANES), lambda i: (0, i, 0))
    by_batch = lambda i: (i // tiles_per_batch, 0, i % tiles_per_batch, 0)

    ins = [x, lw["norm_g"], lw["w_in"], lw["b_forget"], lw["fox_q_norm"], lw["fox_k_norm"],
           lw["mem_q_norm"], lw["head_mean"]]
    in_specs = [row(d)] + [full(a) for a in ins[1:]]
    common = [((n, FOX_WIDTH), f32, row(FOX_WIDTH)), ((2, n, LANES), f32, split), ((2, n, LANES), f32, split),
              ((n, MEM_WIDTH), bf16, row(MEM_WIDTH)), ((n, MEM_WIDTH), f32, row(MEM_WIDTH))]
    scratch = []
    if prompt:
        tril = jnp.asarray(np.tril(np.ones((tm, tm), np.float32)), bf16)
        extra = [tril, lw["bias_sel"]]
        ins += extra
        in_specs += [full(a) for a in extra]
        aliases = {}
        if kv_leaves is None:
            tspec = pl.BlockSpec((depth, 1, N_FOX_HEADS, HEAD_DIM, tm),
                                 lambda i: (0, i // tiles_per_batch, 0, 0, i % tiles_per_batch))
        else:
            aliases = {len(ins): 0, len(ins) + 1: 1}
            ins += list(kv_leaves)
            in_specs += [pl.BlockSpec(memory_space=pl.ANY)] * 2
            tspec = pl.BlockSpec((1, 1, N_FOX_HEADS, HEAD_DIM, tm),
                                 lambda i: (layer, i // tiles_per_batch, 0, 0, i % tiles_per_batch))
        lspec = pl.BlockSpec((1, N_FOX_HEADS, tm), lambda i: (i // tiles_per_batch, 0, i % tiles_per_batch))
        hspec = pl.BlockSpec((1, N_FOX_HEADS, tm, LANES), by_batch)
        tshape = (depth, batch, N_FOX_HEADS, HEAD_DIM, t)
        outs = ([(tshape, f32, tspec), (tshape, f32, tspec), ((batch, N_FOX_HEADS, t), f32, lspec)] + common
                + [((batch, N_FOX_HEADS, t, LANES), bf16, hspec)] * 3)
        names = ["kt", "vt", "logft", "gf", "u", "gs", "mq", "gm", "qa", "ka", "va"]
        scratch = [pltpu.VMEM((8, LANES), f32)]
    else:
        outs = ([((n, FOX_WIDTH), f32, row(FOX_WIDTH)), ((n, FOX_WIDTH), f32, row(FOX_WIDTH)),
                 ((n, LANES), f32, row(LANES))] + common + [((n, FOX_WIDTH), bf16, row(FOX_WIDTH))])
        names = ["k", "v", "logf", "gf", "u", "gs", "mq", "gm", "q"]
        aliases = {}

    res = pl.pallas_call(
        functools.partial(_in_proj_kernel, prompt=prompt, tiles_per_batch=tiles_per_batch, aliased=bool(aliases)),
        grid=grid, in_specs=in_specs, out_specs=[o[2] for o in outs],
        out_shape=[jax.ShapeDtypeStruct(o[0], o[1]) for o in outs],
        scratch_shapes=scratch, input_output_aliases=aliases,
        compiler_params=_cparams(("arbitrary",)),
        name="in_proj_prompt" if prompt else "in_proj_sample",
    )(*ins)
    return dict(zip(names, res))


def _fox_prompt_kernel(qa_ref, ka_ref, va_ref, gate_ref, o_ref, s0_ref, s1_ref, m_ref, acc_ref, *, nq):
    tq = tk = ATTN_TQ
    s_refs = (s0_ref, s1_ref)
    static = nq * (nq - 1) // 2 <= ATTN_UNROLL
    if not static:
        m_ref[...] = jnp.full(m_ref.shape, -jnp.inf, f32)
        acc_ref[...] = jnp.zeros(acc_ref.shape, f32)

    def offset(block):
        return block * tq if isinstance(block, int) else pl.multiple_of(block * tq, tq)

    def scores(qb, kb, slot):
        qoff = offset(qb)
        koff = offset(kb)
        for j in range(2):
            s_refs[slot][j] = lax.dot_general(qa_ref[0, j, pl.ds(qoff, tq), :], ka_ref[0, j, pl.ds(koff, tk), :],
                                              (((1,), (1,)), ((), ())), preferred_element_type=f32)

    def softmax_pv(qb, kb, slot, masked, first=False):
        koff = offset(kb)
        for j in range(2):
            s = s_refs[slot][j]
            if masked:
                s = jnp.where(_row_iota((tq, tk)) >= _lane_iota((tq, tk)), s, -jnp.inf)
            smax = jnp.max(s, axis=-1, keepdims=True)
            if first:
                m_new = jnp.broadcast_to(smax, (tq, LANES))
            else:
                m = m_ref[qb, j]
                m_new = jnp.maximum(m, smax)
            p = jnp.exp2(s - jnp.concatenate([m_new] * (tk // LANES), axis=1))
            pv = jnp.dot(p.astype(bf16), va_ref[0, j, pl.ds(koff, tk), :], preferred_element_type=f32)
            acc_ref[qb, j] = pv if first else jnp.exp2(m - m_new) * acc_ref[qb, j] + pv
            m_ref[qb, j] = m_new

    def finalize(qb):
        h0 = acc_ref[qb, 0] / acc_ref[qb, 0, :, HEAD_DIM:HEAD_DIM + 1]
        h1 = acc_ref[qb, 1] / acc_ref[qb, 1, :, 0:1]
        rows = pl.ds(offset(qb), tq)
        out = jnp.where(_lane_iota((tq, LANES)) < HEAD_DIM, h0, h1) * gate_ref[0, rows, :]
        o_ref[0, rows, :] = out.astype(bf16)

    def after(qb, kb):
        row_end = kb + 1 >= qb
        last = jnp.logical_and(row_end, qb + 1 >= nq)
        nqb = jnp.where(last, 0, jnp.where(row_end, qb + 1, qb))
        nkb = jnp.where(row_end, 0, kb + 1)
        return nqb, nkb

    def lower_step(qb, kb, slot):
        nqb, nkb = after(qb, kb)
        scores(nqb, nkb, 1 - slot)
        softmax_pv(qb, kb, slot, False)
        return nqb, nkb

    def diag_step(qb, slot):
        nxt = jnp.minimum(qb + 1, nq - 1)
        scores(nxt, nxt, 1 - slot)
        softmax_pv(qb, qb, slot, True)
        finalize(qb)

    def steps_per_trip(n):
        return max(d for d in range(2, ATTN_UNROLL + 1, 2) if n % d == 0)

    n_lower = nq * (nq - 1) // 2
    if static:
        steps = [(qb, kb) for qb in range(1, nq) for kb in range(qb)] + [(qb, qb) for qb in range(nq)]
        scores(*steps[0], 0)
        for idx, (qb, kb) in enumerate(steps):
            if idx + 1 < len(steps):
                scores(*steps[idx + 1], 1 - idx % 2)
            softmax_pv(qb, kb, idx % 2, kb == qb, first=(kb == 0))
            if kb == qb:
                finalize(qb)
        return

    if n_lower:
        scores(1, 0, 0)
        per = steps_per_trip(n_lower)

        def lower_body(i, carry):
            for k in range(per):
                carry = lower_step(*carry, k % 2)
            return carry

        lax.fori_loop(0, n_lower // per, lower_body, (jnp.int32(1), jnp.int32(0)))
    else:
        scores(0, 0, 0)

    per_diag = steps_per_trip(nq)

    def diag_body(i, carry):
        for k in range(per_diag):
            diag_step(per_diag * i + k, k % 2)
        return carry

    lax.fori_loop(0, nq // per_diag, diag_body, 0)


def _fox_prompt(qa, ka, va, gate):
    b, nh, t, _ = qa.shape
    nq = t // ATTN_TQ
    assert t % ATTN_TQ == 0 and nq % 2 == 0 and (nq * (nq - 1) // 2) % 2 == 0
    whole = pl.BlockSpec((1, 2, t, LANES), lambda bi, hp: (bi, hp, 0, 0))
    return pl.pallas_call(
        functools.partial(_fox_prompt_kernel, nq=nq),
        grid=(b, nh // 2),
        in_specs=[whole, whole, whole, pl.BlockSpec((1, t, LANES), lambda bi, hp: (bi, 0, hp))],
        out_specs=pl.BlockSpec((1, t, LANES), lambda bi, hp: (bi, 0, hp)),
        out_shape=jax.ShapeDtypeStruct((b, t, FOX_WIDTH), bf16),
        scratch_shapes=[pltpu.VMEM((2, ATTN_TQ, ATTN_TQ), f32), pltpu.VMEM((2, ATTN_TQ, ATTN_TQ), f32),
                        pltpu.VMEM((nq, 2, ATTN_TQ, LANES), f32), pltpu.VMEM((nq, 2, ATTN_TQ, LANES), f32)],
        compiler_params=_cparams(("arbitrary", "arbitrary")),
        name="fox_prompt",
    )(qa, ka, va, gate)


def _fox_sample_kernel(q_ref, kn_ref, vn_ref, ln_ref, gate_ref, kc_ref, vc_ref, lc_ref, triu_ref, o_ref,
                       kall_ref, vall_ref, *, past, tnew):
    nkeys = kall_ref.shape[1]
    nq = N_FOX_HEADS * tnew

    def new_cols(ref):
        z = jnp.concatenate([ref[...], jnp.zeros((LANES - tnew, ref.shape[1]), f32)], axis=0)
        return z.T

    kall_ref[:, 0:past] = kc_ref[0, 0].astype(bf16)
    vall_ref[:, 0:past] = vc_ref[0, 0].astype(bf16)
    kall_ref[:, past:nkeys] = new_cols(kn_ref).astype(bf16)
    vall_ref[:, past:nkeys] = new_cols(vn_ref).astype(bf16)

    l_all = jnp.concatenate([lc_ref[0, 0], new_cols(ln_ref)[0:N_FOX_HEADS, :]], axis=1)
    pieces = jnp.concatenate(_split3(l_all), axis=0).astype(bf16)
    c3 = jnp.dot(pieces, triu_ref[...], preferred_element_type=f32)
    c = c3[0:8] + c3[8:16] + c3[16:24]
    crow = jnp.concatenate([jnp.broadcast_to(c[hd:hd + 1, :], (tnew, nkeys)) for hd in range(N_FOX_HEADS)],
                           axis=0)
    rq = _row_iota((nq, LANES))
    cq = jnp.sum(jnp.where(_lane_iota((nq, LANES)) == rq % tnew, crow[:, past:nkeys], 0.0),
                 axis=1, keepdims=True)

    qrep = jnp.concatenate([q_ref[...]] * N_FOX_HEADS, axis=0)
    own = _row_iota((nq, FOX_WIDTH)) // tnew == _lane_iota((nq, FOX_WIDTH)) // HEAD_DIM
    qbd = jnp.where(own, qrep, jnp.zeros_like(qrep))
    s = jnp.dot(qbd, kall_ref[...], preferred_element_type=f32) + cq - crow
    s = jnp.where(_lane_iota((nq, nkeys)) <= past + _row_iota((nq, nkeys)) % tnew, s, -jnp.inf)
    p = jnp.exp(s - jnp.max(s, axis=1, keepdims=True))
    o2 = lax.dot_general(p.astype(bf16), vall_ref[...], (((1,), (1,)), ((), ())), preferred_element_type=f32)
    o2 = o2 / jnp.sum(p, axis=1, keepdims=True)
    olane = _lane_iota((tnew, FOX_WIDTH))
    out = jnp.zeros((tnew, FOX_WIDTH), f32)
    for hd in range(N_FOX_HEADS):
        out = out + jnp.where(olane // HEAD_DIM == hd, o2[hd * tnew:(hd + 1) * tnew, :], 0.0)
    o_ref[...] = (out * gate_ref[...]).astype(bf16)


def _fox_sample(q, k_new, v_new, logf_new, gate, cache_kt, cache_vt, cache_lt, layer, *, batch):
    n = q.shape[0]
    tnew = n // batch
    past = cache_kt.shape[3]
    nkeys = past + LANES
    assert N_FOX_HEADS * tnew == LANES and past % LANES == 0
    triu = jnp.asarray(np.triu(np.ones((nkeys, nkeys), np.float32)), bf16)
    rows = lambda w: pl.BlockSpec((tnew, w), lambda bi: (bi, 0))
    cache = lambda h: pl.BlockSpec((1, 1, h, past), lambda bi: (layer, bi, 0, 0))
    return pl.pallas_call(
        functools.partial(_fox_sample_kernel, past=past, tnew=tnew),
        grid=(batch,),
        in_specs=[rows(FOX_WIDTH), rows(FOX_WIDTH), rows(FOX_WIDTH), rows(LANES), rows(FOX_WIDTH),
                  cache(FOX_WIDTH), cache(FOX_WIDTH), cache(N_FOX_HEADS),
                  pl.BlockSpec((nkeys, nkeys), lambda bi: (0, 0))],
        out_specs=rows(FOX_WIDTH),
        out_shape=jax.ShapeDtypeStruct((n, FOX_WIDTH), bf16),
        scratch_shapes=[pltpu.VMEM((FOX_WIDTH, nkeys), bf16), pltpu.VMEM((FOX_WIDTH, nkeys), bf16)],
        compiler_params=_cparams(("arbitrary",)),
        name="fox_sample",
    )(q, k_new, v_new, logf_new, gate, cache_kt, cache_vt, cache_lt, triu)


def _s5_prep_kernel(are_r, aim_r, ldt_r, are_c, aim_c, ldt_c, bxr_ref, bxi_ref, cxr_ref, cxi_ref,
                    bst_ref, cst_ref, kt_ref, apow_ref):
    L, W, M = S5_CHUNK, SSM_WIDTH, SSM_MODES

    def cmul(xr, xi, yr, yi):
        return xr * yr - xi * yi, xr * yi + xi * yr

    def powers(are, aim, ldt, n):
        dt = jnp.exp(ldt[...])
        mag = jnp.exp(dt * are[...])
        p1 = (mag * jnp.cos(dt * aim[...]), mag * jnp.sin(dt * aim[...]))
        out = [(jnp.ones_like(mag), jnp.zeros_like(mag)), p1]
        for _ in range(n - 1):
            out.append(cmul(*out[-1], *p1))
        return out

    prow = powers(are_r, aim_r, ldt_r, L)
    pcol = powers(are_c, aim_c, ldt_c, L)
    ar, ai = are_r[...], aim_r[...]
    abr, abi = prow[1]
    den = ar * ar + ai * ai
    zr = ((abr - 1.0) * ar + abi * ai) / den
    zi = (abi * ar - (abr - 1.0) * ai) / den
    bbr, bbi = cmul(zr, zi, bxr_ref[...], bxi_ref[...])

    def b_pow(k):
        return jnp.concatenate(cmul(*prow[k], bbr, bbi), axis=1)

    def c_pow(k):
        pr, pi = pcol[k]
        cr, ci = cxr_ref[...], cxi_ref[...]
        return jnp.concatenate([cr * pr - ci * pi, -(cr * pi + ci * pr)], axis=0)

    cpow = [c_pow(k) for k in range(L + 1)]
    for tl in range(L):
        bst_ref[tl * W:(tl + 1) * W, :] = b_pow(L - 1 - tl).astype(bf16)
        cst_ref[:, tl * W:(tl + 1) * W] = cpow[tl + 1].astype(bf16)

    def hi_lo(x):
        hi = x.astype(bf16)
        return hi, (x - hi.astype(f32)).astype(bf16)

    bh, bl = hi_lo(b_pow(0))
    ch, cl = hi_lo(jnp.concatenate(cpow[:L], axis=1))
    kd = (jnp.dot(bh, ch, preferred_element_type=f32) + jnp.dot(bh, cl, preferred_element_type=f32)
          + jnp.dot(bl, ch, preferred_element_type=f32)).astype(bf16)
    for tl in range(L):
        if tl:
            kt_ref[tl * W:(tl + 1) * W, 0:tl * W] = jnp.zeros((W, tl * W), bf16)
        kt_ref[tl * W:(tl + 1) * W, tl * W:] = kd[:, :(L - tl) * W]

    step = prow[L]
    seg = step
    for _ in range(int(math.log2(S5_SEG))):
        seg = cmul(*seg, *seg)
    apow_ref[...] = jnp.concatenate(
        [jnp.concatenate(step, axis=1), jnp.concatenate(seg, axis=1), jnp.zeros((6, 2 * M), f32)], axis=0)


def _s5_prep_inputs(a_re, a_im, log_dt, b_re, b_im, c_re, c_im):
    G, N = SSM_GROUPS, SSM_STATE
    W, M = SSM_WIDTH, SSM_MODES
    depth = a_re.shape[0]
    eye = jnp.eye(G, dtype=f32)
    bx = lambda b: (jnp.swapaxes(b, 2, 3)[:, :, :, None, :] * eye[None, :, None, :, None]).reshape(depth, W, M)
    cx = lambda c: (jnp.swapaxes(c, 2, 3)[:, :, :, None, :] * eye[None, :, None, :, None]).reshape(depth, M, W)
    ldt = jnp.repeat(log_dt, N, axis=1)
    return [a_re.reshape(depth, 1, M), a_im.reshape(depth, 1, M), ldt.reshape(depth, 1, M),
            a_re.reshape(depth, M, 1), a_im.reshape(depth, M, 1), ldt.reshape(depth, M, 1),
            bx(b_re), bx(b_im), cx(c_re), cx(c_im)]


def _s5_prep(ins, layer):
    L, W, M = S5_CHUNK, SSM_WIDTH, SSM_MODES
    shapes = [((L * W, 2 * M), bf16), ((2 * M, L * W), bf16), ((L * W, L * W), bf16), ((8, 2 * M), f32)]
    return pl.pallas_call(
        _s5_prep_kernel,
        grid=(1,),
        in_specs=[_wspec(a, layer) for a in ins],
        out_specs=[pl.BlockSpec(sh, lambda i: (0, 0)) for sh, _ in shapes],
        out_shape=[jax.ShapeDtypeStruct(sh, dt) for sh, dt in shapes],
        compiler_params=_cparams(("arbitrary",)),
        name="s5_prep",
    )(*ins)


def _s5_kernel(u_ref, gs_ref, h0_ref, bst_ref, cst_ref, kt_ref, apow_ref, d_ref, wg_ref, bg_ref,
               o_ref, hT_ref, ds_ref, sp_ref, carry_ref, *, chained, seglen):
    L, W, M = S5_CHUNK, SSM_WIDTH, SSM_MODES
    nr = 8 * seglen
    nlb = M // LANES

    def chunked(ref, tl):
        return jnp.concatenate([ref[0, pl.ds(tl, nr, stride=L), :], ref[1, pl.ds(tl, nr, stride=L), :]], axis=1)

    uf = [chunked(u_ref, tl) for tl in range(L)]
    ub = jnp.concatenate(uf, axis=1).astype(bf16)
    ds = jnp.dot(ub, bst_ref[...], preferred_element_type=f32)
    for lb in range(2 * nlb):
        ds_ref[lb] = ds[:, lb * LANES:(lb + 1) * LANES]
    def blocks(row):
        return ([row[:, lb * LANES:(lb + 1) * LANES] for lb in range(nlb)],
                [row[:, M + lb * LANES:M + (lb + 1) * LANES] for lb in range(nlb)])

    def bcast8(parts):
        return [jnp.broadcast_to(p, (8, LANES)) for p in parts]

    ar, ai = (bcast8(p) for p in blocks(apow_ref[0:1, :]))

    def scan(vr, vi, store):
        for i in range(seglen):
            rows = pl.ds(i, 8, stride=seglen)
            for lb in range(nlb):
                if store:
                    sp_ref[lb, rows, :] = vr[lb]
                    sp_ref[nlb + lb, rows, :] = vi[lb]
                dr, di = ds_ref[lb, rows, :], ds_ref[nlb + lb, rows, :]
                vr[lb], vi[lb] = (ar[lb] * vr[lb] - ai[lb] * vi[lb] + dr,
                                  ar[lb] * vi[lb] + ai[lb] * vr[lb] + di)
        return vr, vi

    if chained:
        ti = pl.program_id(1)

        @pl.when(ti == 0)
        def _():
            carry_ref[0:1, :] = h0_ref[0]

        zero = [jnp.zeros((8, LANES), f32) for _ in range(nlb)]
        er, ei = scan(list(zero), list(zero), store=False)
        pr, pi = blocks(apow_ref[1:2, :])
        cr, ci = blocks(carry_ref[0:1, :])
        sr, si = [[] for _ in range(nlb)], [[] for _ in range(nlb)]
        for j in range(8):
            for lb in range(nlb):
                sr[lb].append(cr[lb])
                si[lb].append(ci[lb])
                cr[lb], ci[lb] = (pr[lb] * cr[lb] - pi[lb] * ci[lb] + er[lb][j:j + 1, :],
                                  pr[lb] * ci[lb] + pi[lb] * cr[lb] + ei[lb][j:j + 1, :])
        final = jnp.concatenate(cr + ci, axis=1)
        carry_ref[0:1, :] = final
        hT_ref[0] = final
        scan([jnp.concatenate(x, axis=0) for x in sr], [jnp.concatenate(x, axis=0) for x in si], store=True)
    else:
        vr, vi = blocks(h0_ref[...])
        vr, vi = scan(vr, vi, store=True)
        hT_ref[...] = jnp.concatenate(vr + vi, axis=1)

    spb = jnp.concatenate([sp_ref[lb] for lb in range(2 * nlb)], axis=1).astype(bf16)

    def y_pair(pair):
        kk = (2 * pair + 2) * W
        cols = slice(2 * pair * W, kk)
        return (jnp.dot(ub[:, :kk], kt_ref[0:kk, cols], preferred_element_type=f32)
                + jnp.dot(spb, cst_ref[:, cols], preferred_element_type=f32))

    def finish(pair, ypair):
        for tl in (2 * pair, 2 * pair + 1):
            y = ypair[:, (tl % 2) * W:(tl % 2 + 1) * W] + d_ref[...] * uf[tl]
            g = _gelu_tanh(y)
            gate = jax.nn.sigmoid(jnp.dot(g.astype(bf16), wg_ref[...], preferred_element_type=f32) + bg_ref[...])
            out = g * gate * chunked(gs_ref, tl)
            o_ref[0, pl.ds(tl, nr, stride=L), :] = out[:, :LANES]
            o_ref[1, pl.ds(tl, nr, stride=L), :] = out[:, LANES:]

    ycur = y_pair(0)
    for pair in range(L // 2):
        ynext = y_pair(pair + 1) if pair + 1 < L // 2 else None
        finish(pair, ycur)
        ycur = ynext


def _s5(u, gs, h0, ops, d_row, w_glu, b_glu, *, batch, chained, layer):
    L, W, M = S5_CHUNK, SSM_WIDTH, SSM_MODES
    n = u.shape[1]
    t = n // batch
    assert t % L == 0
    chunks = t // L
    bst, cst, kt, apow = ops
    if chained:
        seglen = S5_SEG
        assert chunks % (8 * seglen) == 0
        grid = (batch, chunks // (8 * seglen))
        state = pl.BlockSpec((1, 1, 2 * M), lambda bi, ti: (bi, 0, 0))
    else:
        seglen = chunks
        assert batch == 8
        grid = (1, 1)
        state = pl.BlockSpec((8, 2 * M), lambda bi, ti: (0, 0))
    tiles = grid[1]
    nr = 8 * seglen
    blk = pl.BlockSpec((2, nr * L, LANES), lambda bi, ti: (0, bi * tiles + ti, 0))
    full = lambda a: _wspec(a, layer)
    once = lambda a: pl.BlockSpec(a.shape, lambda bi, ti: (0,) * a.ndim, pipeline_mode=pl.Buffered(1))
    return pl.pallas_call(
        functools.partial(_s5_kernel, chained=chained, seglen=seglen),
        grid=grid,
        in_specs=[blk, blk, state, once(bst), once(cst), once(kt), full(apow), full(d_row),
                  full(w_glu), full(b_glu)],
        out_specs=[blk, state],
        out_shape=[jax.ShapeDtypeStruct((2, n, LANES), f32), jax.ShapeDtypeStruct(h0.shape, f32)],
        scratch_shapes=[pltpu.VMEM((2 * M // LANES, nr, LANES), f32), pltpu.VMEM((2 * M // LANES, nr, LANES), f32),
                        pltpu.VMEM((8, 2 * M), f32)],
        compiler_params=_cparams(("arbitrary", "arbitrary")),
        name="s5_prompt" if chained else "s5_sample",
    )(u, gs, h0, bst, cst, kt, apow, d_row, w_glu, b_glu)


def _memory_kv_kernel(mem_ref, g_ref, w_ref, kg_ref, hm_ref, mkt_ref, mvt_ref):
    batch, _, nmem = mkt_ref.shape
    x = mem_ref[...]
    ms = jnp.mean(x * x, axis=-1, keepdims=True)
    h = (x * lax.rsqrt(ms + EPS) * g_ref[...]).astype(bf16)
    mk = jnp.dot(h, w_ref[:, 0:MEM_WIDTH], preferred_element_type=f32)
    mk = _head_norm(mk, hm_ref.at[0:MEM_WIDTH, 0:MEM_WIDTH], kg_ref)
    mv = jnp.dot(h, w_ref[:, MEM_WIDTH:2 * MEM_WIDTH], preferred_element_type=f32)
    for b in range(batch):
        mkt_ref[b] = mk[b * nmem:(b + 1) * nmem, :].T
        mvt_ref[b] = mv[b * nmem:(b + 1) * nmem, :].T


def _memory_kv(mem, lw, *, batch, layer):
    n = mem.shape[0]
    ins = [mem, lw["mem_norm"], lw["w_mem_kv"], lw["mem_k_norm"], lw["head_mean"]]
    out = pl.BlockSpec((batch, MEM_WIDTH, n // batch), lambda i: (0, 0, 0))
    return pl.pallas_call(
        _memory_kv_kernel,
        grid=(1,),
        in_specs=[_wspec(a, layer) for a in ins],
        out_specs=[out, out],
        out_shape=[jax.ShapeDtypeStruct((batch, MEM_WIDTH, n // batch), f32)] * 2,
        compiler_params=_cparams(("arbitrary",)),
        name="memory_kv",
    )(*ins)


def _mix_out_kernel(x_ref, fox_ref, ssm_ref, mq_ref, gm_ref, mkt_ref, mvt_ref, w_ref, o_ref):
    tm = x_ref.shape[0]
    nmem = mkt_ref.shape[2]
    mq = mq_ref[...]
    mkt = mkt_ref[0].astype(bf16)
    mvt = mvt_ref[0].astype(bf16)
    khead = _row_iota((MEM_WIDTH, nmem)) // HEAD_DIM
    zero = jnp.zeros_like(mkt)
    mk_heads = jnp.concatenate([jnp.where(khead == hd, mkt, zero) for hd in range(N_MEM_HEADS)], axis=1)
    s_all = jnp.dot(mq, mk_heads, preferred_element_type=f32)
    ssm = jnp.concatenate([ssm_ref[0], ssm_ref[1]], axis=1).astype(bf16)
    y = x_ref[...]
    y = y + jnp.dot(fox_ref[...], w_ref[0:FOX_WIDTH, :], preferred_element_type=f32)
    y = y + jnp.dot(ssm, w_ref[FOX_WIDTH:FOX_WIDTH + SSM_WIDTH, :], preferred_element_type=f32)
    mem = jnp.zeros((tm, MEM_WIDTH), f32)
    for hd in range(N_MEM_HEADS):
        s = s_all[:, hd * nmem:(hd + 1) * nmem]
        p = jnp.exp(s - jnp.max(s, axis=-1, keepdims=True))
        p = p / jnp.sum(p, axis=-1, keepdims=True)
        vh = jnp.where(khead == hd, mvt, zero)
        mem = mem + lax.dot_general(p.astype(bf16), vh, (((1,), (1,)), ((), ())), preferred_element_type=f32)
    memg = (mem * gm_ref[...]).astype(bf16)
    o_ref[...] = y + jnp.dot(memg, w_ref[FOX_WIDTH + SSM_WIDTH:, :], preferred_element_type=f32)


def _mix_out(x, fox, ssm, mq, gm, mkt, mvt, w_out, *, batch, tm, layer):
    n, d = x.shape
    t = n // batch
    assert t % tm == 0
    tiles_per_batch = t // tm
    row = lambda w: pl.BlockSpec((tm, w), lambda i: (i, 0))
    if mkt.ndim == 4:
        memb = pl.BlockSpec((None, 1) + mkt.shape[2:], lambda i: (layer, i // tiles_per_batch, 0, 0))
    else:
        memb = pl.BlockSpec((1,) + mkt.shape[1:], lambda i: (i // tiles_per_batch, 0, 0))
    return pl.pallas_call(
        _mix_out_kernel,
        grid=(n // tm,),
        in_specs=[row(d), row(FOX_WIDTH), pl.BlockSpec((2, tm, LANES), lambda i: (0, i, 0)),
                  row(MEM_WIDTH), row(MEM_WIDTH), memb, memb, _wspec(w_out, layer)],
        out_specs=row(d),
        out_shape=jax.ShapeDtypeStruct((n, d), f32),
        compiler_params=_cparams(("arbitrary",)),
        name="mix_out_prompt" if tiles_per_batch > 1 else "mix_out_sample",
    )(x, fox, ssm, mq, gm, mkt, mvt, w_out)


def _repack_w_in_kernel(wt_ref, o_ref):
    nf = 4 * FOX_WIDTH
    dblk = wt_ref.shape[1]

    def put(lo, hi, rows):
        o_ref[:, lo:hi] = rows.T.astype(bf16)

    put(_C_Q, _C_F, wt_ref[0:FOX_WIDTH, :])
    put(_C_F, _C_K, jnp.concatenate([wt_ref[nf:nf + N_FOX_HEADS, :],
                                     jnp.zeros((LANES - N_FOX_HEADS, dblk), f32)], axis=0))
    put(_C_K, _C_SU, wt_ref[FOX_WIDTH:nf, :])
    put(_C_SU, _W_COLS, wt_ref[nf + N_FOX_HEADS:, :])


def _repack_w_in(w_in):
    depth, d, cols = w_in.shape
    assert cols == 4 * FOX_WIDTH + N_FOX_HEADS + 2 * SSM_WIDTH + 2 * MEM_WIDTH
    dblk = 256
    return pl.pallas_call(
        _repack_w_in_kernel,
        grid=(depth, d // dblk),
        in_specs=[pl.BlockSpec((None, cols, dblk), lambda l, i: (l, 0, i))],
        out_specs=pl.BlockSpec((None, dblk, _W_COLS), lambda l, i: (l, i, 0)),
        out_shape=jax.ShapeDtypeStruct((depth, d, _W_COLS), bf16),
        compiler_params=_cparams(("arbitrary", "arbitrary")),
        name="repack_w_in",
    )(jnp.swapaxes(w_in, 1, 2))


def _stacked_weights(norm_g, w_in, b_forget, fox_q_norm, fox_k_norm, mem_q_norm, mem_norm, w_mem_kv,
                     mem_k_norm, w_out, w_glu, b_glu, ssm_d):
    depth, d, _ = w_in.shape
    w_packed = _repack_w_in(w_in)
    row = lambda a: a.reshape(depth, 1, -1)
    per_head = lambda g, heads: jnp.tile(g, (1, heads)).reshape(depth, 1, heads * HEAD_DIM)
    return dict(
        norm_g=row(norm_g), w_in=w_packed,
        b_forget=row(jnp.pad(b_forget, ((0, 0), (0, LANES - N_FOX_HEADS)))),
        fox_q_norm=per_head(fox_q_norm, N_FOX_HEADS), fox_k_norm=per_head(fox_k_norm, N_FOX_HEADS),
        mem_q_norm=per_head(mem_q_norm, N_MEM_HEADS), mem_k_norm=per_head(mem_k_norm, N_MEM_HEADS),
        mem_norm=row(mem_norm), w_mem_kv=w_mem_kv.astype(bf16),
        head_mean=_head_mean_matrix(), bias_sel=_bias_selector(),
        w_out=w_out.astype(bf16), w_glu=w_glu.astype(bf16), b_glu=row(b_glu), ssm_d=row(ssm_d))


def _split_state(hT, batch):
    hT = hT.reshape(batch, 2, SSM_GROUPS, SSM_STATE)
    return hT[:, 0], hT[:, 1]


def kernel(x_prompt, x_sample, mem_prompt, cache_fox_k, cache_fox_v, cache_fox_logf, state_ssm_re, state_ssm_im, cache_mem_k, cache_mem_v, norm_g, w_in, b_forget, fox_q_norm, fox_k_norm, ssm_a_re, ssm_a_im, ssm_log_dt, ssm_b_re, ssm_b_im, ssm_c_re, ssm_c_im, ssm_d, w_glu, b_glu, mem_norm, w_mem_kv, mem_q_norm, mem_k_norm, w_out):
    B, T, D = x_prompt.shape
    Bs, Ts, _ = x_sample.shape
    depth = w_in.shape[0]
    past = cache_fox_k.shape[2]
    nmem = mem_prompt.shape[1]
    M = SSM_MODES

    xp = x_prompt.reshape(B * T, D)
    xs = x_sample.reshape(Bs * Ts, D)
    mem = mem_prompt.reshape(B * nmem, D)
    cache_kt = jnp.transpose(cache_fox_k, (0, 1, 3, 4, 2)).reshape(depth, Bs, FOX_WIDTH, past)
    cache_vt = jnp.transpose(cache_fox_v, (0, 1, 3, 4, 2)).reshape(depth, Bs, FOX_WIDTH, past)
    cache_lt = jnp.transpose(cache_fox_logf, (0, 1, 3, 2)).astype(f32)
    cache_mkt = jnp.transpose(cache_mem_k, (0, 1, 3, 4, 2)).reshape(depth, Bs, MEM_WIDTH, nmem)
    cache_mvt = jnp.transpose(cache_mem_v, (0, 1, 3, 4, 2)).reshape(depth, Bs, MEM_WIDTH, nmem)

    outs = {k: [] for k in ("pf", "pre", "pim", "pmk", "pmv", "sk", "sv", "sf", "sre", "sim")}
    kv_leaves = None
    lw = _stacked_weights(norm_g, w_in, b_forget, fox_q_norm, fox_k_norm, mem_q_norm, mem_norm,
                          w_mem_kv, mem_k_norm, w_out, w_glu, b_glu, ssm_d)
    s5_params = _s5_prep_inputs(ssm_a_re, ssm_a_im, ssm_log_dt, ssm_b_re, ssm_b_im, ssm_c_re, ssm_c_im)
    for l in range(depth):
        ops = _s5_prep(s5_params, l)
        mkt, mvt = _memory_kv(mem, lw, batch=B, layer=l)

        pr = _in_proj(xp, lw, batch=B, prompt=True, layer=l, depth=depth, kv_leaves=kv_leaves)
        kv_leaves = (pr["kt"], pr["vt"])
        fox = _fox_prompt(pr["qa"], pr["ka"], pr["va"], pr["gf"].reshape(B, T, FOX_WIDTH)).reshape(B * T, FOX_WIDTH)
        ssm, hT = _s5(pr["u"], pr["gs"], jnp.zeros((B, 1, 2 * M), f32), ops, lw["ssm_d"], lw["w_glu"],
                      lw["b_glu"], batch=B, chained=True, layer=l)
        xp = _mix_out(xp, fox, ssm, pr["mq"], pr["gm"], mkt, mvt,
                      lw["w_out"], batch=B, tm=MIX_TILE, layer=l)
        hre, him = _split_state(hT, B)
        outs["pf"].append(pr["logft"])
        outs["pre"].append(hre)
        outs["pim"].append(him)
        outs["pmk"].append(mkt.reshape(B, N_MEM_HEADS, HEAD_DIM, nmem))
        outs["pmv"].append(mvt.reshape(B, N_MEM_HEADS, HEAD_DIM, nmem))

        sr = _in_proj(xs, lw, batch=Bs, prompt=False, layer=l)
        fox_s = _fox_sample(sr["q"], sr["k"], sr["v"], sr["logf"], sr["gf"], cache_kt, cache_vt, cache_lt, l,
                            batch=Bs)
        h0 = jnp.concatenate([state_ssm_re[l].reshape(Bs, M), state_ssm_im[l].reshape(Bs, M)],
                             axis=-1).astype(f32)
        ssm_s, hT_s = _s5(sr["u"], sr["gs"], h0, ops, lw["ssm_d"], lw["w_glu"], lw["b_glu"], batch=Bs,
                          chained=False, layer=l)
        xs = _mix_out(xs, fox_s, ssm_s, sr["mq"], sr["gm"], cache_mkt, cache_mvt,
                      lw["w_out"], batch=Bs, tm=Ts, layer=l)
        sre, sim = _split_state(hT_s, Bs)
        outs["sk"].append(sr["k"].reshape(Bs, Ts, N_FOX_HEADS, HEAD_DIM))
        outs["sv"].append(sr["v"].reshape(Bs, Ts, N_FOX_HEADS, HEAD_DIM))
        outs["sf"].append(sr["logf"][:, :N_FOX_HEADS].reshape(Bs, Ts, N_FOX_HEADS))
        outs["sre"].append(sre)
        outs["sim"].append(sim)

    st = {k: jnp.stack(v) for k, v in outs.items()}
    fox_k_prompt = jnp.transpose(kv_leaves[0], (0, 1, 4, 2, 3))
    fox_v_prompt = jnp.transpose(kv_leaves[1], (0, 1, 4, 2, 3))
    fox_logf_prompt = jnp.transpose(st["pf"], (0, 1, 3, 2))
    mem_k_prompt = jnp.transpose(st["pmk"], (0, 1, 4, 2, 3))
    mem_v_prompt = jnp.transpose(st["pmv"], (0, 1, 4, 2, 3))
    return (xp.reshape(B, T, D), xs.reshape(Bs, Ts, D), fox_k_prompt, fox_v_prompt, fox_logf_prompt,
            st["pre"], st["pim"], mem_k_prompt, mem_v_prompt, st["sk"], st["sv"], st["sf"], st["sre"], st["sim"])
```
